```python
import jax, jax.numpy as jnp
from jax import lax
import numpy as np

D_MODEL = 1024
BATCH = 2
SEQ = 8192
DEPTH = 1
DEC_BATCH = 32
DEC_SEQ = 16
PAST_LEN = 2048

CHUNK = 64
D_MIX = D_MODEL
RET_WIDTH = D_MIX // 2
RET_HEADS = 4
RET_HEAD_DIM = RET_WIDTH // RET_HEADS
GM_WIDTH = D_MIX - RET_WIDTH
GM_HEADS = 4
GM_HEAD_DIM = GM_WIDTH // GM_HEADS
GM_CHUNK = 128
IN_WIDTH = 4 * RET_WIDTH + 2 * GM_WIDTH
SPLITS = [RET_WIDTH, 2 * RET_WIDTH, 3 * RET_WIDTH, 4 * RET_WIDTH, 4 * RET_WIDTH + GM_WIDTH]
N_EXPERTS = 32
TOP_K = 4
D_FF = D_MODEL
SWIGLU_LIMIT = 7.0
SWIGLU_ALPHA = 1.702
EXPERT_BLOCK = 128
ROPE_BASE = 10000.0
EPS = 1e-6

kernel_name = "hybrid_retention_gmlp_moe_stream_step"

F32 = jnp.float32


def rmsnorm(x, g):
    xf = x.astype(F32)
    y = xf * lax.rsqrt(jnp.mean(xf * xf, axis=-1, keepdims=True) + EPS)
    return (y * g.astype(F32)).astype(x.dtype)


def rotary(x, pos):
    d = x.shape[-1]
    half = d // 2
    inv = ROPE_BASE ** (-jnp.arange(half, dtype=F32) / half)
    ang = pos.astype(F32)[:, None] * inv[None, :]
    cos = jnp.cos(ang)[None, :, None, :]
    sin = jnp.sin(ang)[None, :, None, :]
    xf = x.astype(F32)
    x1, x2 = xf[..., :half], xf[..., half:]
    return jnp.concatenate([x1 * cos - x2 * sin, x1 * sin + x2 * cos], axis=-1)


def retention(q, k, v, s0, block):
    B, T, H, dk = q.shape
    dv = v.shape[-1]
    n = T // block
    log_g = jnp.log1p(-(2.0 ** (-5.0 - jnp.arange(H, dtype=F32))))
    idx = jnp.arange(block, dtype=F32)
    diff = idx[:, None] - idx[None, :]
    dmat = jnp.where(diff[None] >= 0, jnp.exp(jnp.maximum(diff, 0.0)[None] * log_g[:, None, None]), 0.0)
    q_dec = jnp.exp((idx + 1.0)[:, None] * log_g[None, :])
    k_dec = jnp.exp((block - 1.0 - idx)[:, None] * log_g[None, :])
    s_dec = jnp.exp(block * log_g)

    def to_blocks(a):
        return a.astype(F32).reshape(B, n, block, H, a.shape[-1]).transpose(1, 0, 2, 3, 4)

    def step(S, blk):
        qb, kb, vb = blk
        scores = jnp.einsum('bihd,bjhd->bhij', qb, kb) * dmat[None]
        o_in = jnp.einsum('bhij,bjhe->bihe', scores, vb)
        o_cross = jnp.einsum('bihd,bhde->bihe', qb * q_dec[None, :, :, None], S)
        S_new = s_dec[None, :, None, None] * S + jnp.einsum('bjhd,bjhe->bhde', kb * k_dec[None, :, :, None], vb)
        return S_new, o_in + o_cross

    S_fin, o = lax.scan(step, s0.astype(F32), (to_blocks(q), to_blocks(k), to_blocks(v)))
    o = o.transpose(1, 0, 2, 3, 4).reshape(B, T, H, dv)
    return o, S_fin


def head_norm(o):
    mu = jnp.mean(o, axis=-1, keepdims=True)
    var = jnp.mean(jnp.square(o - mu), axis=-1, keepdims=True)
    return (o - mu) * lax.rsqrt(var + EPS)


def chunk_gmlp(u, vg, ln_g, ln_b, w_s, b_s):
    B, T, _ = u.shape
    L = min(T, GM_CHUNK)
    n = T // L
    uf = jax.nn.gelu(u.astype(F32))
    vf = jax.nn.gelu(vg.astype(F32))
    mu = jnp.mean(vf, axis=-1, keepdims=True)
    var = jnp.mean(jnp.square(vf - mu), axis=-1, keepdims=True)
    vn = (vf - mu) * lax.rsqrt(var + EPS) * ln_g.astype(F32) + ln_b.astype(F32)
    w = jnp.tril(w_s[:, :L, :L].astype(F32))
    vb = vn.reshape(B, n, L, GM_HEADS, GM_HEAD_DIM)
    mixed = jnp.einsum('hts,bcshd->bcthd', w, vb) + b_s[:, :L].astype(F32).T[None, None, :, :, None]
    out = uf * mixed.reshape(B, T, GM_WIDTH)
    return out.astype(u.dtype), vn.astype(u.dtype)


def moe(h, w_router, b_router, w_gate, b_gate, w_up, b_up, w_down, b_down):
    N = h.shape[0]
    NK = N * TOP_K
    logits = h.astype(F32) @ w_router.astype(F32) + b_router.astype(F32)
    top_v, top_i = lax.top_k(logits, TOP_K)
    wts = jax.nn.softmax(top_v, axis=-1)
    flat_e = top_i.reshape(-1).astype(jnp.int32)
    flat_w = wts.reshape(-1)
    flat_tok = jnp.arange(NK, dtype=jnp.int32) // TOP_K
    order = jnp.argsort(flat_e, stable=True)
    se = flat_e[order]
    counts = jnp.zeros((N_EXPERTS,), jnp.int32).at[flat_e].add(1)
    starts = jnp.cumsum(counts) - counts
    pcounts = (counts + EXPERT_BLOCK - 1) // EXPERT_BLOCK * EXPERT_BLOCK
    pends = jnp.cumsum(pcounts)
    pstarts = pends - pcounts
    dest = pstarts[se] + (jnp.arange(NK, dtype=jnp.int32) - starts[se])
    n_blocks = -(-(NK + N_EXPERTS * (EXPERT_BLOCK - 1)) // EXPERT_BLOCK)
    slot_tok = jnp.zeros((n_blocks * EXPERT_BLOCK,), jnp.int32).at[dest].set(flat_tok[order])
    slot_w = jnp.zeros((n_blocks * EXPERT_BLOCK,), F32).at[dest].set(flat_w[order])
    block_e = jnp.minimum(jnp.searchsorted(pends, jnp.arange(n_blocks, dtype=jnp.int32) * EXPERT_BLOCK, side='right'),
                          N_EXPERTS - 1).astype(jnp.int32)

    def run(args):
        tok, wt, e = args
        xb = h[tok]
        gt = xb @ w_gate[e] + b_gate[e]
        up = xb @ w_up[e] + b_up[e]
        gt = jnp.minimum(gt, SWIGLU_LIMIT)
        up = jnp.clip(up, -SWIGLU_LIMIT, SWIGLU_LIMIT)
        act = gt * jax.nn.sigmoid(gt * SWIGLU_ALPHA) * (up + 1.0)
        y = act @ w_down[e] + b_down[e]
        return y.astype(F32) * wt[:, None]

    ys = lax.map(run, (slot_tok.reshape(n_blocks, EXPERT_BLOCK), slot_w.reshape(n_blocks, EXPERT_BLOCK), block_e))
    out = jnp.zeros((N, h.shape[1]), F32).at[slot_tok].add(ys.reshape(-1, h.shape[1]))
    return out.astype(h.dtype)


def layer(x, pos, s0, p):
    B, T, _ = x.shape
    h = rmsnorm(x, p['norm1_g'])
    proj = h @ p['w_in']
    q, k, v, g, u, vg = jnp.split(proj, SPLITS, axis=-1)
    q = rotary(q.reshape(B, T, RET_HEADS, RET_HEAD_DIM), pos)
    k = rotary(k.reshape(B, T, RET_HEADS, RET_HEAD_DIM), pos) * (RET_HEAD_DIM ** -0.5)
    o, s_new = retention(q, k, v.reshape(B, T, RET_HEADS, RET_HEAD_DIM), s0, min(T, CHUNK))
    ret_y = jax.nn.silu(g.astype(F32)) * head_norm(o).reshape(B, T, RET_WIDTH)
    gm_y, v_rows = chunk_gmlp(u, vg, p['ln_v_g'], p['ln_v_b'], p['w_s'], p['b_s'])
    mix = jnp.concatenate([ret_y.astype(x.dtype), gm_y], axis=-1) @ p['w_out']
    x = x + mix
    h2 = rmsnorm(x, p['norm2_g'])
    x = x + moe(h2.reshape(B * T, D_MODEL), p['w_router'], p['b_router'], p['w_gate'], p['b_gate'],
                p['w_up'], p['b_up'], p['w_down'], p['b_down']).reshape(B, T, D_MODEL)
    return x, s_new.astype(x.dtype), v_rows


def setup_inputs(seed: int = 0) -> dict:
    key = jax.random.key(seed)
    ks = jax.random.split(key, 24)
    nrm = jax.random.normal
    d = {}
    d['x_prompt'] = nrm(ks[0], (BATCH, SEQ, D_MODEL), F32)
    d['x_sample'] = nrm(ks[1], (DEC_BATCH, DEC_SEQ, D_MODEL), F32)
    d['state_ret'] = 0.1 * nrm(ks[2], (DEPTH, DEC_BATCH, RET_HEADS, RET_HEAD_DIM, RET_HEAD_DIM), F32)
    d['norm1_g'] = 1.0 + 0.02 * nrm(ks[3], (DEPTH, D_MODEL), F32)
    d['w_in'] = nrm(ks[4], (DEPTH, D_MODEL, IN_WIDTH), F32) * D_MODEL ** -0.5
    d['ln_v_g'] = 1.0 + 0.02 * nrm(ks[5], (DEPTH, GM_WIDTH), F32)
    d['ln_v_b'] = 0.02 * nrm(ks[6], (DEPTH, GM_WIDTH), F32)
    d['w_s'] = nrm(ks[7], (DEPTH, GM_HEADS, GM_CHUNK, GM_CHUNK), F32) * GM_CHUNK ** -0.5
    d['b_s'] = 1.0 + 0.02 * nrm(ks[8], (DEPTH, GM_HEADS, GM_CHUNK), F32)
    d['w_out'] = nrm(ks[9], (DEPTH, D_MIX, D_MODEL), F32) * D_MIX ** -0.5
    d['norm2_g'] = 1.0 + 0.02 * nrm(ks[10], (DEPTH, D_MODEL), F32)
    d['w_router'] = nrm(ks[11], (DEPTH, D_MODEL, N_EXPERTS), F32) * D_MODEL ** -0.5
    d['b_router'] = 0.01 * nrm(ks[12], (DEPTH, N_EXPERTS), F32)
    d['w_gate'] = nrm(ks[13], (DEPTH, N_EXPERTS, D_MODEL, D_FF), F32) * D_MODEL ** -0.5
    d['b_gate'] = 0.02 * nrm(ks[14], (DEPTH, N_EXPERTS, D_FF), F32)
    d['w_up'] = nrm(ks[15], (DEPTH, N_EXPERTS, D_MODEL, D_FF), F32) * D_MODEL ** -0.5
    d['b_up'] = 0.02 * nrm(ks[16], (DEPTH, N_EXPERTS, D_FF), F32)
    d['w_down'] = nrm(ks[17], (DEPTH, N_EXPERTS, D_FF, D_MODEL), F32) * D_FF ** -0.5
    d['b_down'] = 0.02 * nrm(ks[18], (DEPTH, N_EXPERTS, D_MODEL), F32)
    d['norm_f_g'] = 1.0 + 0.02 * nrm(ks[19], (D_MODEL,), F32)
    return d


def reference(x_prompt, x_sample, state_ret, norm1_g, w_in, ln_v_g, ln_v_b, w_s, b_s, w_out, norm2_g,
              w_router, b_router, w_gate, b_gate, w_up, b_up, w_down, b_down, norm_f_g):
    pos_p = jnp.arange(SEQ, dtype=jnp.int32)
    pos_s = PAST_LEN + jnp.arange(x_sample.shape[1], dtype=jnp.int32)
    xp, xs = x_prompt, x_sample
    sp_list, ss_list, v_list = [], [], []
    for l in range(DEPTH):
        p = dict(norm1_g=norm1_g[l], w_in=w_in[l], ln_v_g=ln_v_g[l], ln_v_b=ln_v_b[l], w_s=w_s[l], b_s=b_s[l],
                 w_out=w_out[l], norm2_g=norm2_g[l], w_router=w_router[l], b_router=b_router[l],
                 w_gate=w_gate[l], b_gate=b_gate[l], w_up=w_up[l], b_up=b_up[l], w_down=w_down[l], b_down=b_down[l])
        s0 = jnp.zeros((xp.shape[0], RET_HEADS, RET_HEAD_DIM, RET_HEAD_DIM), F32)
        xp, sp, _ = layer(xp, pos_p, s0, p)
        xs, ss, vrows = layer(xs, pos_s, state_ret[l], p)
        sp_list.append(sp)
        ss_list.append(ss)
        v_list.append(vrows)
    y_prompt = rmsnorm(xp, norm_f_g)
    y_sample = rmsnorm(xs, norm_f_g)
    return (y_prompt, y_sample, jnp.stack(sp_list), jnp.stack(ss_list), jnp.stack(v_list))
```

```python
import functools

import numpy as np
import jax
import jax.numpy as jnp
from jax import lax
from jax.experimental import pallas as pl
from jax.experimental.pallas import tpu as pltpu

F32 = jnp.float32
BF16 = jnp.bfloat16
I32 = jnp.int32

D_MODEL = 1024
RET_WIDTH = 512
N_HEADS = 4
HEAD_DIM = 128
GM_WIDTH = 512
GM_CHUNK = 128
RET_CHUNK = 64
N_EXPERTS = 32
TOP_K = 4
SWIGLU_LIMIT = 7.0
SWIGLU_ALPHA = 1.702
ROPE_BASE = 10000.0
EPS = 1e-6

PROMPT_TILE = 512
PROMPT_RET_BLOCK = 256
EXPERT_ROWS = 256
DISPATCH_TILE = 512
COMBINE_TILE = 256
VMEM_LIMIT = 56 * 1024 * 1024


def _rmsnorm(x, g):
    ms = jnp.mean(x * x, axis=-1, keepdims=True)
    return x * lax.rsqrt(ms + EPS) * g


def _gelu(x):
    c = np.float32(np.sqrt(2.0 / np.pi))
    return x * (0.5 * (1.0 + jnp.tanh(c * (x + 0.044715 * (x * x * x)))))


def _sigmoid(x):
    return 1.0 / (1.0 + jnp.exp(-x))


def _dot(a, b):
    return jnp.dot(a, b, preferred_element_type=F32)


def _dot_nt(a, b):
    return lax.dot_general(a, b, (((1,), (1,)), ((), ())), preferred_element_type=F32)


def _dot_tn(a, b):
    return lax.dot_general(a, b, (((0,), (0,)), ((), ())), preferred_element_type=F32)


def _front_kernel(*refs, tile, ret_block, gm_block, emit_vn):
    (x_ref, s0_ref, cnt0_ref, g1_ref, win_ref, cq_ref, sq_ref, ck_ref, sk_ref, dmat_ref, qdec_ref, kdec_ref,
     sdec_ref, lng_ref, lnb_ref, ws_ref, bst_ref, wout_ref, g2_ref, wrh_ref, wrl_ref, br_ref, tri_ref) = refs[:23]
    outs = refs[23:]
    x2_ref, h2_ref, ti_ref, tw_ref, rk_ref, st_ref, cnt_ref = outs[:7]
    pos = 7
    vn_ref = None
    if emit_vn:
        vn_ref = outs[pos]
        pos += 1
    s_scr, cnt_scr, mix_scr = outs[pos:pos + 3]

    b = pl.program_id(0)
    t = pl.program_id(1)

    @pl.when(t == 0)
    def _():
        s_scr[...] = s0_ref[0]

    @pl.when((b == 0) & (t == 0))
    def _():
        cnt_scr[...] = cnt0_ref[...]

    x = x_ref[0]
    h = _rmsnorm(x, g1_ref[...]).astype(BF16)

    def proj(i):
        return _dot(h, win_ref[:, i * RET_WIDTH:(i + 1) * RET_WIDTH])

    q = proj(0)
    k = proj(1)
    v = proj(2)
    gate = proj(3)
    u = proj(4)
    vg = proj(5)

    cq = cq_ref[...]
    sq = sq_ref[...]
    ck = ck_ref[...]
    sk = sk_ref[...]

    n_blk = tile // ret_block
    for hd in range(N_HEADS):
        cols = slice(hd * HEAD_DIM, (hd + 1) * HEAD_DIM)
        qh = q[:, cols]
        kh = k[:, cols]
        qr = qh * cq + pltpu.roll(qh, HEAD_DIM // 2, axis=1) * sq
        kr = kh * ck + pltpu.roll(kh, HEAD_DIM // 2, axis=1) * sk
        vh = v[:, cols].astype(BF16)
        dm = dmat_ref[hd]
        qd = qdec_ref[hd]
        kd = kdec_ref[hd]
        sd = sdec_ref[hd]
        for c in range(n_blk):
            rows = slice(c * ret_block, (c + 1) * ret_block)
            qb = qr[rows].astype(BF16)
            kb = kr[rows]
            vb = vh[rows]
            state = s_scr[hd]
            scores = _dot_nt(qb, kb.astype(BF16)) * dm
            o = _dot(scores.astype(BF16), vb) + qd * _dot(qb, state.astype(BF16))
            s_scr[hd] = sd * state + _dot_tn((kb * kd).astype(BF16), vb)
            mu = jnp.mean(o, axis=-1, keepdims=True)
            oc = o - mu
            var = jnp.mean(oc * oc, axis=-1, keepdims=True)
            on = oc * lax.rsqrt(var + EPS)
            gg = gate[rows, cols]
            mix_scr[rows, cols] = ((gg * _sigmoid(gg)) * on).astype(BF16)

    uf = _gelu(u)
    vf = _gelu(vg)
    mu = jnp.mean(vf, axis=-1, keepdims=True)
    vc = vf - mu
    var = jnp.mean(vc * vc, axis=-1, keepdims=True)
    vn = vc * lax.rsqrt(var + EPS) * lng_ref[...] + lnb_ref[...]
    if emit_vn:
        vn_ref[...] = vn
    vnb = vn.astype(BF16)
    r_i = lax.broadcasted_iota(I32, (gm_block, gm_block), 0)
    c_i = lax.broadcasted_iota(I32, (gm_block, gm_block), 1)
    for hd in range(N_HEADS):
        cols = slice(hd * HEAD_DIM, (hd + 1) * HEAD_DIM)
        w_tril = jnp.where(r_i >= c_i, ws_ref[hd], 0.0).astype(BF16)
        bias = bst_ref[:, hd:hd + 1]
        for c in range(tile // gm_block):
            rows = slice(c * gm_block, (c + 1) * gm_block)
            mixed = _dot(w_tril, vnb[rows, cols]) + bias
            mix_scr[rows, RET_WIDTH + hd * HEAD_DIM:RET_WIDTH + (hd + 1) * HEAD_DIM] = (
                uf[rows, cols] * mixed).astype(BF16)

    x2 = x + _dot(mix_scr[...], wout_ref[...])
    x2_ref[...] = x2
    h2 = _rmsnorm(x2, g2_ref[...])
    h2_ref[...] = h2

    h_hi = h2.astype(BF16)
    h_lo = (h2 - h_hi.astype(F32)).astype(BF16)
    logits = (_dot(h_hi, wrh_ref[...]) + _dot(h_lo, wrh_ref[...]) + _dot(h_hi, wrl_ref[...])) + br_ref[...]

    iota_e = lax.broadcasted_iota(I32, (tile, N_EXPERTS), 1)
    iota_ef = iota_e.astype(F32)
    iota_k = lax.broadcasted_iota(I32, (tile, TOP_K), 1)
    lg = logits
    vals = []
    idxs = []
    for _ in range(TOP_K):
        m = jnp.max(lg, axis=-1, keepdims=True)
        idx = jnp.min(jnp.where(lg == m, iota_ef, float(N_EXPERTS)), axis=-1, keepdims=True).astype(I32)
        vals.append(m)
        idxs.append(idx)
        lg = jnp.where(iota_e == idx, -jnp.inf, lg)
    exps = [jnp.exp(vk - vals[0]) for vk in vals]
    denom = exps[0] + exps[1] + exps[2] + exps[3]

    onehot = jnp.zeros((tile, N_EXPERTS), F32)
    for idx in idxs:
        onehot = onehot + (iota_e == idx).astype(F32)
    cnt = cnt_scr[...]
    cum = _dot(tri_ref[...], onehot.astype(BF16)) + cnt
    ti = jnp.zeros((tile, TOP_K), I32)
    tw = jnp.zeros((tile, TOP_K), F32)
    rk = jnp.zeros((tile, TOP_K), I32)
    for kk in range(TOP_K):
        sel = iota_e == idxs[kk]
        rank = jnp.sum(jnp.where(sel, cum, 0.0), axis=-1, keepdims=True).astype(I32)
        ti = jnp.where(iota_k == kk, idxs[kk], ti)
        tw = jnp.where(iota_k == kk, exps[kk] / denom, tw)
        rk = jnp.where(iota_k == kk, rank, rk)
    ti_ref[...] = ti
    tw_ref[...] = tw
    rk_ref[...] = rk
    cnt_new = cnt + jnp.sum(onehot, axis=0, keepdims=True)
    cnt_scr[...] = cnt_new
    cnt_ref[...] = cnt_new

    @pl.when(t == pl.num_programs(1) - 1)
    def _():
        st_ref[0] = s_scr[...]


def _rope_tables(pos):
    half = HEAD_DIM // 2
    inv = ROPE_BASE ** (-jnp.arange(half, dtype=F32) / half)
    ang = pos.astype(F32)[:, None] * inv[None, :]
    cos = jnp.cos(ang)
    sin = jnp.sin(ang)
    cq = jnp.concatenate([cos, cos], axis=-1)
    sq = jnp.concatenate([-sin, sin], axis=-1)
    scale = HEAD_DIM ** -0.5
    return cq, sq, cq * scale, sq * scale


def _decay_tables(block):
    log_g = jnp.log1p(-(2.0 ** (-5.0 - jnp.arange(N_HEADS, dtype=F32))))
    idx = jnp.arange(block, dtype=F32)
    diff = idx[:, None] - idx[None, :]
    dmat = jnp.where(diff[None] >= 0, jnp.exp(jnp.maximum(diff, 0.0)[None] * log_g[:, None, None]), 0.0)
    q_dec = jnp.exp((idx + 1.0)[None, :] * log_g[:, None])
    k_dec = jnp.exp((block - 1.0 - idx)[None, :] * log_g[:, None])
    s_dec = jnp.exp(block * log_g)
    bc = lambda a: jnp.broadcast_to(a[:, :, None], (N_HEADS, block, HEAD_DIM))
    return dmat, bc(q_dec), bc(k_dec), jnp.broadcast_to(s_dec[:, None, None], (N_HEADS, 1, HEAD_DIM))


def _front_call(x, pos, s0, cnt0, p, *, tile, ret_block, emit_vn):
    bsz, seq, _ = x.shape
    n_t = seq // tile
    n_total = bsz * seq
    gm_block = min(seq, GM_CHUNK)
    cq, sq, ck, sk = _rope_tables(pos)
    dmat, qdec, kdec, sdec = _decay_tables(ret_block)
    tri = (jnp.arange(tile)[:, None] > jnp.arange(tile)[None, :]).astype(BF16)
    ws = p['w_s'][:, :gm_block, :gm_block]
    bst = p['b_s'][:, :gm_block].T

    const = lambda shape: pl.BlockSpec(shape, lambda b, t: (0,) * len(shape))
    in_specs = [
        pl.BlockSpec((1, tile, D_MODEL), lambda b, t: (b, t, 0)),
        pl.BlockSpec((1, N_HEADS, HEAD_DIM, HEAD_DIM), lambda b, t: (b, 0, 0, 0)),
        const((1, N_EXPERTS)),
        const((1, D_MODEL)),
        const((D_MODEL, 6 * RET_WIDTH)),
        pl.BlockSpec((tile, HEAD_DIM), lambda b, t: (t, 0)),
        pl.BlockSpec((tile, HEAD_DIM), lambda b, t: (t, 0)),
        pl.BlockSpec((tile, HEAD_DIM), lambda b, t: (t, 0)),
        pl.BlockSpec((tile, HEAD_DIM), lambda b, t: (t, 0)),
        const((N_HEADS, ret_block, ret_block)),
        const((N_HEADS, ret_block, HEAD_DIM)),
        const((N_HEADS, ret_block, HEAD_DIM)),
        const((N_HEADS, 1, HEAD_DIM)),
        const((1, GM_WIDTH)),
        const((1, GM_WIDTH)),
        const((N_HEADS, gm_block, gm_block)),
        const((gm_block, N_HEADS)),
        const((D_MODEL, D_MODEL)),
        const((1, D_MODEL)),
        const((D_MODEL, N_EXPERTS)),
        const((D_MODEL, N_EXPERTS)),
        const((1, N_EXPERTS)),
        const((tile, tile)),
    ]

    row = lambda b, t: (b * n_t + t, 0)
    out_shape = [
        jax.ShapeDtypeStruct((n_total, D_MODEL), F32),
        jax.ShapeDtypeStruct((n_total, D_MODEL), F32),
        jax.ShapeDtypeStruct((n_total, TOP_K), I32),
        jax.ShapeDtypeStruct((n_total, TOP_K), F32),
        jax.ShapeDtypeStruct((n_total, TOP_K), I32),
        jax.ShapeDtypeStruct((bsz, N_HEADS, HEAD_DIM, HEAD_DIM), F32),
        jax.ShapeDtypeStruct((1, N_EXPERTS), F32),
    ]
    out_specs = [
        pl.BlockSpec((tile, D_MODEL), row),
        pl.BlockSpec((tile, D_MODEL), row),
        pl.BlockSpec((tile, TOP_K), row),
        pl.BlockSpec((tile, TOP_K), row),
        pl.BlockSpec((tile, TOP_K), row),
        pl.BlockSpec((1, N_HEADS, HEAD_DIM, HEAD_DIM), lambda b, t: (b, 0, 0, 0)),
        pl.BlockSpec((1, N_EXPERTS), lambda b, t: (0, 0)),
    ]
    if emit_vn:
        out_shape.append(jax.ShapeDtypeStruct((bsz * seq, GM_WIDTH), F32))
        out_specs.append(pl.BlockSpec((tile, GM_WIDTH), lambda b, t: (b * n_t + t, 0)))

    kern = functools.partial(_front_kernel, tile=tile, ret_block=ret_block, gm_block=gm_block, emit_vn=emit_vn)
    return pl.pallas_call(
        kern,
        grid=(bsz, n_t),
        in_specs=in_specs,
        out_specs=out_specs,
        out_shape=out_shape,
        scratch_shapes=[
            pltpu.VMEM((N_HEADS, HEAD_DIM, HEAD_DIM), F32),
            pltpu.VMEM((1, N_EXPERTS), F32),
            pltpu.VMEM((tile, D_MODEL), BF16),
        ],
        compiler_params=pltpu.CompilerParams(
            dimension_semantics=("arbitrary", "arbitrary"), vmem_limit_bytes=VMEM_LIMIT),
        name="front_t%d" % tile,
    )(x, s0, cnt0, p['norm1_g'][None, :], p['w_in'].astype(BF16), cq, sq, ck, sk, dmat, qdec, kdec, sdec,
      p['ln_v_g'][None, :], p['ln_v_b'][None, :], ws, bst, p['w_out'].astype(BF16), p['norm2_g'][None, :],
      p['wr_hi'], p['wr_lo'], p['b_router'][None, :], tri)


def _plan_kernel(cnt_ref, ps_ref, nb_ref, be_ref, nbt_ref, *, n_blocks):
    cnt = cnt_ref[...]
    nb = jnp.floor((cnt + (EXPERT_ROWS - 1)) * (1.0 / EXPERT_ROWS))
    r_i = lax.broadcasted_iota(I32, (N_EXPERTS, N_EXPERTS), 0)
    c_i = lax.broadcasted_iota(I32, (N_EXPERTS, N_EXPERTS), 1)
    upper = (r_i < c_i).astype(BF16)
    nb8 = jnp.broadcast_to(nb, (8, N_EXPERTS)).astype(BF16)
    bstart = _dot(nb8, upper)[0:1]
    bend = bstart + nb
    ps_ref[...] = (bstart * EXPERT_ROWS).astype(I32)
    nb_ref[...] = nb.astype(I32)
    blk = lax.broadcasted_iota(I32, (n_blocks, N_EXPERTS), 0).astype(F32)
    be = jnp.sum((bend <= blk).astype(F32), axis=-1, keepdims=True)
    be_ref[...] = jnp.minimum(be, N_EXPERTS - 1.0).astype(I32)
    nbt_ref[...] = jnp.sum(nb, axis=-1, keepdims=True).astype(I32)


def _plan_call(cnt, n_blocks):
    return pl.pallas_call(
        functools.partial(_plan_kernel, n_blocks=n_blocks),
        out_shape=[
            jax.ShapeDtypeStruct((1, N_EXPERTS), I32),
            jax.ShapeDtypeStruct((1, N_EXPERTS), I32),
            jax.ShapeDtypeStruct((n_blocks, 1), I32),
            jax.ShapeDtypeStruct((1, 1), I32),
        ],
        name="plan",
    )(cnt)


def _dispatch_kernel(ps_ref, nb_ref, nbt_ref, h2p_ref, h2s_ref, ti_ref, rk_ref, xs_ref, zbuf, sem, zsem,
                     *, n_prompt_tiles, n_blocks):
    i = pl.program_id(0)

    def zero_copy(block_row):
        dst = xs_ref.at[pl.ds(pl.multiple_of(block_row * EXPERT_ROWS, EXPERT_ROWS), EXPERT_ROWS)]
        return pltpu.make_async_copy(zbuf, dst, zsem)

    @pl.when(i == 0)
    def _():
        zbuf[...] = jnp.zeros_like(zbuf)

        def last_block(e):
            return lax.shift_right_logical(ps_ref[e], EXPERT_ROWS.bit_length() - 1) + nb_ref[e] - 1

        def zstart(e, c):
            @pl.when(nb_ref[e] > 0)
            def _():
                zero_copy(last_block(e)).start()
            return c

        def zwait(e, c):
            @pl.when(nb_ref[e] > 0)
            def _():
                zero_copy(last_block(e)).wait()
            return c

        def tstart(blk, c):
            zero_copy(blk).start()
            return c

        def twait(blk, c):
            zero_copy(blk).wait()
            return c

        lax.fori_loop(0, N_EXPERTS, zstart, 0)
        lax.fori_loop(nbt_ref[0], n_blocks, tstart, 0)
        lax.fori_loop(0, N_EXPERTS, zwait, 0)
        lax.fori_loop(nbt_ref[0], n_blocks, twait, 0)

    def scatter_rows(src_ref):
        def row_copy(j):
            e = ti_ref[0, 0, j]
            dst = ps_ref[e] + rk_ref[0, 0, j]
            tl = lax.shift_right_logical(j, 2)
            return pltpu.make_async_copy(src_ref.at[pl.ds(tl, 1)], xs_ref.at[pl.ds(dst, 1)], sem)

        def start(j, c):
            row_copy(j).start()
            return c

        def wait(j, c):
            row_copy(j).wait()
            return c

        n = DISPATCH_TILE * TOP_K
        lax.fori_loop(0, n, start, 0, unroll=8)
        lax.fori_loop(0, n, wait, 0, unroll=8)

    @pl.when(i < n_prompt_tiles)
    def _():
        scatter_rows(h2p_ref)

    @pl.when(i >= n_prompt_tiles)
    def _():
        scatter_rows(h2s_ref)


def _dispatch_call(ps, nb, nbt, h2_p, h2_s, ti_flat, rk_flat, n_blocks):
    n_p_tiles = h2_p.shape[0] // DISPATCH_TILE
    n_s_tiles = h2_s.shape[0] // DISPATCH_TILE
    slots = DISPATCH_TILE * TOP_K
    return pl.pallas_call(
        functools.partial(_dispatch_kernel, n_prompt_tiles=n_p_tiles, n_blocks=n_blocks),
        grid_spec=pltpu.PrefetchScalarGridSpec(
            num_scalar_prefetch=3,
            grid=(n_p_tiles + n_s_tiles,),
            in_specs=[
                pl.BlockSpec((DISPATCH_TILE, D_MODEL), lambda i, *_: (jnp.minimum(i, n_p_tiles - 1), 0)),
                pl.BlockSpec((DISPATCH_TILE, D_MODEL), lambda i, *_: (jnp.maximum(i - n_p_tiles, 0), 0)),
                pl.BlockSpec((1, 1, slots), lambda i, *_: (i, 0, 0), memory_space=pltpu.SMEM),
                pl.BlockSpec((1, 1, slots), lambda i, *_: (i, 0, 0), memory_space=pltpu.SMEM),
            ],
            out_specs=pl.BlockSpec(memory_space=pl.ANY),
            scratch_shapes=[
                pltpu.VMEM((EXPERT_ROWS, D_MODEL), F32),
                pltpu.SemaphoreType.DMA(()),
                pltpu.SemaphoreType.DMA(()),
            ],
        ),
        out_shape=jax.ShapeDtypeStruct((n_blocks * EXPERT_ROWS, D_MODEL), F32),
        compiler_params=pltpu.CompilerParams(dimension_semantics=("arbitrary",)),
        name="dispatch",
    )(ps, nb, nbt, h2_p, h2_s, ti_flat, rk_flat)


def _expert_kernel(be_ref, nbt_ref, xs_ref, wg_ref, wu_ref, wd_ref, bg_ref, bu_ref, bd_ref, ys_ref, wbf):
    b = pl.program_id(0)
    live = b < nbt_ref[0]
    prev = be_ref[jnp.maximum(b - 1, 0)]
    fresh = (b == 0) | (be_ref[b] != prev)

    @pl.when(live & fresh)
    def _():
        wbf[0] = wg_ref[0].astype(BF16)
        wbf[1] = wu_ref[0].astype(BF16)
        wbf[2] = wd_ref[0].astype(BF16)

    @pl.when(live)
    def _():
        x = xs_ref[...].astype(BF16)
        gt = _dot(x, wbf[0]) + bg_ref[0]
        up = _dot(x, wbf[1]) + bu_ref[0]
        gt = jnp.minimum(gt, SWIGLU_LIMIT)
        up = jnp.clip(up, -SWIGLU_LIMIT, SWIGLU_LIMIT)
        act = gt * _sigmoid(gt * SWIGLU_ALPHA) * (up + 1.0)
        ys_ref[...] = _dot(act.astype(BF16), wbf[2]) + bd_ref[0]

    @pl.when(jnp.logical_not(live))
    def _():
        ys_ref[...] = jnp.zeros_like(ys_ref)


def _expert_call(be, nbt, xs, p, n_blocks):
    def xrow(b, be, nbt):
        return (jnp.minimum(b, nbt[0] - 1), 0)

    wspec = pl.BlockSpec((1, D_MODEL, D_MODEL), lambda b, be, nbt: (be[b], 0, 0))
    bspec = pl.BlockSpec((1, 1, D_MODEL), lambda b, be, nbt: (be[b], 0, 0))
    return pl.pallas_call(
        _expert_kernel,
        grid_spec=pltpu.PrefetchScalarGridSpec(
            num_scalar_prefetch=2,
            grid=(n_blocks,),
            in_specs=[pl.BlockSpec((EXPERT_ROWS, D_MODEL), xrow), wspec, wspec, wspec, bspec, bspec, bspec],
            out_specs=pl.BlockSpec((EXPERT_ROWS, D_MODEL), lambda b, be, nbt: (b, 0)),
            scratch_shapes=[pltpu.VMEM((3, D_MODEL, D_MODEL), BF16)],
        ),
        out_shape=jax.ShapeDtypeStruct((n_blocks * EXPERT_ROWS, D_MODEL), F32),
        compiler_params=pltpu.CompilerParams(
            dimension_semantics=("arbitrary",), vmem_limit_bytes=VMEM_LIMIT),
        name="experts",
    )(be, nbt, xs, p['w_gate'], p['w_up'], p['w_down'],
      p['b_gate'][:, None, :], p['b_up'][:, None, :], p['b_down'][:, None, :])


def _combine_kernel(ps_ref, x2_ref, tw_ref, ti_ref, rk_ref, ys_ref, gf_ref, y_ref, buf, sem):
    def row_copy(j):
        e = ti_ref[0, 0, j]
        src = ps_ref[e] + rk_ref[0, 0, j]
        tl = lax.shift_right_logical(j, 2)
        kk = lax.bitwise_and(j, TOP_K - 1)
        return pltpu.make_async_copy(ys_ref.at[pl.ds(src, 1)], buf.at[kk, pl.ds(tl, 1)], sem)

    def start(j, c):
        row_copy(j).start()
        return c

    def wait(j, c):
        row_copy(j).wait()
        return c

    n = COMBINE_TILE * TOP_K
    lax.fori_loop(0, n, start, 0, unroll=8)
    lax.fori_loop(0, n, wait, 0, unroll=8)

    tw = tw_ref[...]
    acc = x2_ref[...]
    moe = tw[:, 0:1] * buf[0]
    for kk in range(1, TOP_K):
        moe = moe + tw[:, kk:kk + 1] * buf[kk]
    y_ref[...] = _rmsnorm(acc + moe, gf_ref[...])


def _combine_call(ps, x2, tw, ti_flat, rk_flat, ys, gf, tile_off, n_tok):
    n_tiles = n_tok // COMBINE_TILE
    slots = COMBINE_TILE * TOP_K
    return pl.pallas_call(
        _combine_kernel,
        grid_spec=pltpu.PrefetchScalarGridSpec(
            num_scalar_prefetch=1,
            grid=(n_tiles,),
            in_specs=[
                pl.BlockSpec((COMBINE_TILE, D_MODEL), lambda i, ps: (i, 0)),
                pl.BlockSpec((COMBINE_TILE, TOP_K), lambda i, ps: (tile_off + i, 0)),
                pl.BlockSpec((1, 1, slots), lambda i, ps: (tile_off + i, 0, 0), memory_space=pltpu.SMEM),
                pl.BlockSpec((1, 1, slots), lambda i, ps: (tile_off + i, 0, 0), memory_space=pltpu.SMEM),
                pl.BlockSpec(memory_space=pl.ANY),
                pl.BlockSpec((1, D_MODEL), lambda i, ps: (0, 0)),
            ],
            out_specs=pl.BlockSpec((COMBINE_TILE, D_MODEL), lambda i, ps: (i, 0)),
            scratch_shapes=[
                pltpu.VMEM((TOP_K, COMBINE_TILE, D_MODEL), F32),
                pltpu.SemaphoreType.DMA(()),
            ],
        ),
        out_shape=jax.ShapeDtypeStruct((n_tok, D_MODEL), F32),
        compiler_params=pltpu.CompilerParams(dimension_semantics=("arbitrary",)),
        name="combine",
    )(ps, x2, tw, ti_flat, rk_flat, ys, gf)


def kernel(x_prompt, x_sample, state_ret, norm1_g, w_in, ln_v_g, ln_v_b, w_s, b_s, w_out, norm2_g, w_router, b_router, w_gate, b_gate, w_up, b_up, w_down, b_down, norm_f_g):
    bsz, seq, _ = x_prompt.shape
    dbsz, dseq, _ = x_sample.shape
    n_p = bsz * seq
    n_s = dbsz * dseq
    n_total = n_p + n_s
    past_len = 2048
    assert w_in.shape[0] == 1, "single layer"

    wr = w_router[0]
    wr_hi = wr.astype(BF16)
    wr_lo = (wr - wr_hi.astype(F32)).astype(BF16)
    p = dict(norm1_g=norm1_g[0], w_in=w_in[0], ln_v_g=ln_v_g[0], ln_v_b=ln_v_b[0], w_s=w_s[0], b_s=b_s[0],
             w_out=w_out[0], norm2_g=norm2_g[0], wr_hi=wr_hi, wr_lo=wr_lo, b_router=b_router[0],
             w_gate=w_gate[0], b_gate=b_gate[0], w_up=w_up[0], b_up=b_up[0], w_down=w_down[0], b_down=b_down[0])

    pos_p = jnp.arange(seq, dtype=jnp.int32)
    pos_s = past_len + jnp.arange(dseq, dtype=jnp.int32)
    s0_p = jnp.zeros((bsz, N_HEADS, HEAD_DIM, HEAD_DIM), F32)
    cnt0 = jnp.zeros((1, N_EXPERTS), F32)

    x2_p, h2_p, ti_p, tw_p, rk_p, st_p, cnt_p = _front_call(
        x_prompt, pos_p, s0_p, cnt0, p, tile=PROMPT_TILE, ret_block=PROMPT_RET_BLOCK, emit_vn=False)
    x2_s, h2_s, ti_s, tw_s, rk_s, st_s, cnt, vn_s = _front_call(
        x_sample, pos_s, state_ret[0], cnt_p, p, tile=dseq, ret_block=min(dseq, RET_CHUNK), emit_vn=True)
    ti = jnp.concatenate([ti_p, ti_s], axis=0)
    tw = jnp.concatenate([tw_p, tw_s], axis=0)
    rk = jnp.concatenate([rk_p, rk_s], axis=0)

    n_blocks = -(-(n_total * TOP_K + N_EXPERTS * (EXPERT_ROWS - 1)) // EXPERT_ROWS)
    ps, nb, be, nbt = _plan_call(cnt, n_blocks)
    ps = ps.reshape(N_EXPERTS)
    nb = nb.reshape(N_EXPERTS)
    be = be.reshape(n_blocks)
    nbt = nbt.reshape(1)

    ti_d = ti.reshape(n_total // DISPATCH_TILE, 1, DISPATCH_TILE * TOP_K)
    rk_d = rk.reshape(n_total // DISPATCH_TILE, 1, DISPATCH_TILE * TOP_K)
    xs = _dispatch_call(ps, nb, nbt, h2_p, h2_s, ti_d, rk_d, n_blocks)
    ys = _expert_call(be, nbt, xs, p, n_blocks)

    ti_c = ti.reshape(n_total // COMBINE_TILE, 1, COMBINE_TILE * TOP_K)
    rk_c = rk.reshape(n_total // COMBINE_TILE, 1, COMBINE_TILE * TOP_K)
    gf = norm_f_g[None, :]
    y_p = _combine_call(ps, x2_p, tw, ti_c, rk_c, ys, gf, 0, n_p)
    y_s = _combine_call(ps, x2_s, tw, ti_c, rk_c, ys, gf, n_p // COMBINE_TILE, n_s)

    return (y_p.reshape(bsz, seq, D_MODEL), y_s.reshape(dbsz, dseq, D_MODEL),
            st_p[None], st_s[None], vn_s.reshape(1, dbsz, dseq, GM_WIDTH))
```

```python
import functools

import numpy as np
import jax
import jax.numpy as jnp
from jax import lax
from jax.experimental import pallas as pl
from jax.experimental.pallas import tpu as pltpu

F32 = jnp.float32
BF16 = jnp.bfloat16
I32 = jnp.int32

D_MODEL = 1024
RET_WIDTH = 512
N_HEADS = 4
HEAD_DIM = 128
GM_WIDTH = 512
GM_CHUNK = 128
RET_CHUNK = 64
N_EXPERTS = 32
TOP_K = 4
SWIGLU_LIMIT = 7.0
SWIGLU_ALPHA = 1.702
ROPE_BASE = 10000.0
EPS = 1e-6

TOKEN_TILE = 512
PROMPT_RET_BLOCK = 256
EXPERT_ROWS = 512
EXPERT_SUB_ROWS = 256
TILE_SLOTS = TOKEN_TILE * TOP_K
VMEM_LIMIT = 56 * 1024 * 1024
LANES = 128
ROW_CHUNKS = D_MODEL // LANES
SPLIT = 16.0


def _load_rows(ref, first_row, n_rows):
    base = first_row * ROW_CHUNKS
    return jnp.concatenate(
        [ref[pl.ds(base + c, n_rows, stride=ROW_CHUNKS), :] for c in range(ROW_CHUNKS)], axis=1)


def _store_rows(ref, first_row, val):
    base = first_row * ROW_CHUNKS
    for c in range(ROW_CHUNKS):
        ref[pl.ds(base + c, val.shape[0], stride=ROW_CHUNKS), :] = val[:, c * LANES:(c + 1) * LANES]


def _row_tile(row):
    return pl.ds(pl.multiple_of(row * ROW_CHUNKS, ROW_CHUNKS), ROW_CHUNKS)


def _rmsnorm(x, g):
    ms = jnp.mean(x * x, axis=-1, keepdims=True)
    return x * lax.rsqrt(ms + EPS) * g


def _gelu(x):
    c = np.float32(np.sqrt(2.0 / np.pi))
    return x * (0.5 * (1.0 + jnp.tanh(c * (x + 0.044715 * (x * x * x)))))


def _sigmoid(x):
    return 1.0 / (1.0 + jnp.exp(-x))


def _dot(a, b):
    return jnp.dot(a, b, preferred_element_type=F32)


def _dot_nt(a, b):
    return lax.dot_general(a, b, (((1,), (1,)), ((), ())), preferred_element_type=F32)


def _dot_tn(a, b):
    return lax.dot_general(a, b, (((0,), (0,)), ((), ())), preferred_element_type=F32)


def _exact_count_dot(a, b, *, counts_on_left):
    cnt = a if counts_on_left else b
    hi = jnp.floor(cnt * (1.0 / SPLIT))
    lo = cnt - SPLIT * hi
    if counts_on_left:
        return SPLIT * _dot(hi.astype(BF16), b) + _dot(lo.astype(BF16), b)
    return SPLIT * _dot(a, hi.astype(BF16)) + _dot(a, lo.astype(BF16))


def _rotary(xh, cos2, sin2):
    return xh * cos2 + pltpu.roll(xh, HEAD_DIM // 2, axis=1) * sin2


def _gated_head_norm(o, gate):
    mu = jnp.mean(o, axis=-1, keepdims=True)
    oc = o - mu
    var = jnp.mean(oc * oc, axis=-1, keepdims=True)
    return ((gate * _sigmoid(gate)) * (oc * lax.rsqrt(var + EPS))).astype(BF16)


def _gmlp_inputs(u, vg, lng, lnb):
    uf = _gelu(u)
    vf = _gelu(vg)
    mu = jnp.mean(vf, axis=-1, keepdims=True)
    vc = vf - mu
    var = jnp.mean(vc * vc, axis=-1, keepdims=True)
    return uf, vc * lax.rsqrt(var + EPS) * lng + lnb


def _route_and_store(x, mix_scr, wout_ref, g2_ref, wrh_ref, wrl_ref, br_ref, tri_ref, upper_ref,
                     x2_ref, h2_ref, lp_ref, tw_ref, ct_ref, of_ref):
    tile = x.shape[0]
    x2 = x + _dot(mix_scr[...], wout_ref[...])
    x2_ref[...] = x2
    h2 = _rmsnorm(x2, g2_ref[...])
    _store_rows(h2_ref, 0, h2)

    h_hi = h2.astype(BF16)
    h_lo = (h2 - h_hi.astype(F32)).astype(BF16)
    logits = (_dot(h_hi, wrh_ref[...]) + _dot(h_lo, wrh_ref[...]) + _dot(h_hi, wrl_ref[...])) + br_ref[...]

    iota_e = lax.broadcasted_iota(I32, (tile, N_EXPERTS), 1)
    iota_ef = iota_e.astype(F32)
    iota_k = lax.broadcasted_iota(I32, (tile, TOP_K), 1)
    lg = logits
    vals = []
    idxs = []
    for _ in range(TOP_K):
        m = jnp.max(lg, axis=-1, keepdims=True)
        idx = jnp.min(jnp.where(lg == m, iota_ef, float(N_EXPERTS)), axis=-1, keepdims=True).astype(I32)
        vals.append(m)
        idxs.append(idx)
        lg = jnp.where(iota_e == idx, -jnp.inf, lg)
    exps = [jnp.exp(vk - vals[0]) for vk in vals]
    denom = exps[0] + exps[1] + exps[2] + exps[3]

    onehot = jnp.zeros((tile, N_EXPERTS), F32)
    for idx in idxs:
        onehot = onehot + (iota_e == idx).astype(F32)
    rank = _dot(tri_ref[...], onehot.astype(BF16))
    count = jnp.sum(onehot, axis=0, keepdims=True)
    offset = _exact_count_dot(jnp.broadcast_to(count, (8, N_EXPERTS)), upper_ref[...], counts_on_left=True)[0:1]
    where_to = rank + offset
    lp = jnp.zeros((tile, TOP_K), I32)
    tw = jnp.zeros((tile, TOP_K), F32)
    for kk in range(TOP_K):
        pos = jnp.sum(jnp.where(iota_e == idxs[kk], where_to, 0.0), axis=-1, keepdims=True).astype(I32)
        lp = jnp.where(iota_k == kk, pos, lp)
        tw = jnp.where(iota_k == kk, exps[kk] / denom, tw)
    lp_ref[...] = lp
    tw_ref[...] = tw
    ct_ref[0] = count
    of_ref[0] = offset


def _prompt_kernel(x_ref, g1_ref, win_ref, cq_ref, sq_ref, ck_ref, sk_ref, dmat_ref, qdec_ref, kdec_ref,
                   sdec_ref, lng_ref, lnb_ref, ws_ref, bst_ref, wout_ref, g2_ref, wrh_ref, wrl_ref, br_ref,
                   tri_ref, upper_ref,
                   x2_ref, h2_ref, lp_ref, tw_ref, ct_ref, of_ref, st_ref,
                   s_scr, mix_scr, *, ret_block):
    tile = TOKEN_TILE
    t = pl.program_id(1)

    @pl.when(t == 0)
    def _():
        s_scr[...] = jnp.zeros_like(s_scr)

    x = x_ref[0]
    h = _rmsnorm(x, g1_ref[...]).astype(BF16)

    def proj(i):
        return _dot(h, win_ref[:, i * RET_WIDTH:(i + 1) * RET_WIDTH])

    q = proj(0)
    k = proj(1)
    v = proj(2)
    gate = proj(3)
    u = proj(4)
    vg = proj(5)
    cq = cq_ref[...]
    sq = sq_ref[...]
    ck = ck_ref[...]
    sk = sk_ref[...]

    for hd in range(N_HEADS):
        cols = slice(hd * HEAD_DIM, (hd + 1) * HEAD_DIM)
        qr = _rotary(q[:, cols], cq, sq)
        kr = _rotary(k[:, cols], ck, sk)
        vh = v[:, cols].astype(BF16)
        dm = dmat_ref[hd]
        qd = qdec_ref[hd]
        kd = kdec_ref[hd]
        sd = sdec_ref[hd]
        for c in range(tile // ret_block):
            rows = slice(c * ret_block, (c + 1) * ret_block)
            qb = qr[rows].astype(BF16)
            kb = kr[rows]
            vb = vh[rows]
            state = s_scr[hd]
            scores = _dot_nt(qb, kb.astype(BF16)) * dm
            o = _dot(scores.astype(BF16), vb) + qd * _dot(qb, state.astype(BF16))
            s_scr[hd] = sd * state + _dot_tn((kb * kd).astype(BF16), vb)
            mix_scr[rows, cols] = _gated_head_norm(o, gate[rows, cols])

    uf, vn = _gmlp_inputs(u, vg, lng_ref[...], lnb_ref[...])
    vnb = vn.astype(BF16)
    r_i = lax.broadcasted_iota(I32, (GM_CHUNK, GM_CHUNK), 0)
    c_i = lax.broadcasted_iota(I32, (GM_CHUNK, GM_CHUNK), 1)
    for hd in range(N_HEADS):
        cols = slice(hd * HEAD_DIM, (hd + 1) * HEAD_DIM)
        w_tril = jnp.where(r_i >= c_i, ws_ref[hd], 0.0).astype(BF16)
        bias = bst_ref[:, hd:hd + 1]
        for c in range(tile // GM_CHUNK):
            rows = slice(c * GM_CHUNK, (c + 1) * GM_CHUNK)
            mixed = _dot(w_tril, vnb[rows, cols]) + bias
            mix_scr[rows, RET_WIDTH + hd * HEAD_DIM:RET_WIDTH + (hd + 1) * HEAD_DIM] = (
                uf[rows, cols] * mixed).astype(BF16)

    _route_and_store(x, mix_scr, wout_ref, g2_ref, wrh_ref, wrl_ref, br_ref, tri_ref, upper_ref,
                     x2_ref, h2_ref, lp_ref, tw_ref, ct_ref, of_ref)

    @pl.when(t == pl.num_programs(1) - 1)
    def _():
        st_ref[0] = s_scr[...]


def _decode_kernel(x_ref, s0_ref, g1_ref, win_ref, cq_ref, sq_ref, ck_ref, sk_ref, dblk_ref, qdec_ref, kdec_ref,
                   sdec_ref, lng_ref, lnb_ref, wtile_ref, bst_ref, wout_ref, g2_ref, wrh_ref, wrl_ref, br_ref,
                   tri_ref, upper_ref,
                   x2_ref, h2_ref, lp_ref, tw_ref, ct_ref, of_ref, st_ref, vn_ref,
                   mix_scr, q_scr, k_scr, v_scr, oc_scr, st_stage, st_sem, *, n_streams, frames):
    tile = n_streams * frames
    x = x_ref[...]
    h = _rmsnorm(x, g1_ref[...]).astype(BF16)

    def proj(i):
        return _dot(h, win_ref[:, i * RET_WIDTH:(i + 1) * RET_WIDTH])

    q = proj(0)
    k = proj(1)
    v = proj(2)
    gate = proj(3)
    u = proj(4)
    vg = proj(5)
    cq = cq_ref[...]
    sq = sq_ref[...]
    ck = ck_ref[...]
    sk = sk_ref[...]

    o_in = []
    for hd in range(N_HEADS):
        cols = slice(hd * HEAD_DIM, (hd + 1) * HEAD_DIM)
        qr = _rotary(q[:, cols], cq, sq)
        kr = _rotary(k[:, cols], ck, sk)
        vh = v[:, cols].astype(BF16)
        qb = qr.astype(BF16)
        scores = _dot_nt(qb, kr.astype(BF16)) * dblk_ref[hd]
        o_in.append(_dot(scores.astype(BF16), vh))
        q_scr[hd] = qb
        k_scr[hd] = (kr * kdec_ref[hd]).astype(BF16)
        v_scr[hd] = vh

    def state_copy(b):
        slot = lax.rem(b, 2)
        return pltpu.make_async_copy(st_stage.at[slot], st_ref.at[b], st_sem.at[slot])

    def stream_body(b, carry):
        rows = pl.ds(pl.multiple_of(b * frames, frames), frames)
        slot = lax.rem(b, 2)

        @pl.when(b >= 2)
        def _():
            state_copy(b - 2).wait()

        for hd in range(N_HEADS):
            state = s0_ref[b, hd]
            oc_scr[hd, rows, :] = _dot(q_scr[hd, rows, :], state.astype(BF16))
            st_stage[slot, hd] = sdec_ref[hd] * state + _dot_tn(k_scr[hd, rows, :], v_scr[hd, rows, :])
        state_copy(b).start()
        return carry

    lax.fori_loop(0, n_streams, stream_body, 0)
    state_copy(n_streams - 2).wait()
    state_copy(n_streams - 1).wait()

    for hd in range(N_HEADS):
        cols = slice(hd * HEAD_DIM, (hd + 1) * HEAD_DIM)
        o = o_in[hd] + qdec_ref[hd] * oc_scr[hd]
        mix_scr[:, cols] = _gated_head_norm(o, gate[:, cols])

    uf, vn = _gmlp_inputs(u, vg, lng_ref[...], lnb_ref[...])
    vn_ref[...] = vn
    vnb = vn.astype(BF16)
    r_i = lax.broadcasted_iota(I32, (tile, tile), 0)
    c_i = lax.broadcasted_iota(I32, (tile, tile), 1)
    shift = frames.bit_length() - 1
    same_stream = lax.shift_right_logical(r_i, shift) == lax.shift_right_logical(c_i, shift)
    keep = same_stream & (lax.bitwise_and(r_i, frames - 1) >= lax.bitwise_and(c_i, frames - 1))
    for hd in range(N_HEADS):
        cols = slice(hd * HEAD_DIM, (hd + 1) * HEAD_DIM)
        w_blk = jnp.where(keep, wtile_ref[hd], 0.0).astype(BF16)
        mixed = _dot(w_blk, vnb[:, cols]) + bst_ref[:, hd:hd + 1]
        mix_scr[:, RET_WIDTH + hd * HEAD_DIM:RET_WIDTH + (hd + 1) * HEAD_DIM] = (uf[:, cols] * mixed).astype(BF16)

    _route_and_store(x, mix_scr, wout_ref, g2_ref, wrh_ref, wrl_ref, br_ref, tri_ref, upper_ref,
                     x2_ref, h2_ref, lp_ref, tw_ref, ct_ref, of_ref)


def _rope_tables(pos):
    half = HEAD_DIM // 2
    inv = ROPE_BASE ** (-jnp.arange(half, dtype=F32) / half)
    ang = pos.astype(F32)[:, None] * inv[None, :]
    cos = jnp.cos(ang)
    sin = jnp.sin(ang)
    cq = jnp.concatenate([cos, cos], axis=-1)
    sq = jnp.concatenate([-sin, sin], axis=-1)
    scale = HEAD_DIM ** -0.5
    return cq, sq, cq * scale, sq * scale


def _decay_tables(block):
    log_g = jnp.log1p(-(2.0 ** (-5.0 - jnp.arange(N_HEADS, dtype=F32))))
    idx = jnp.arange(block, dtype=F32)
    diff = idx[:, None] - idx[None, :]
    dmat = jnp.where(diff[None] >= 0, jnp.exp(jnp.maximum(diff, 0.0)[None] * log_g[:, None, None]), 0.0)
    q_dec = jnp.exp((idx + 1.0)[None, :] * log_g[:, None])
    k_dec = jnp.exp((block - 1.0 - idx)[None, :] * log_g[:, None])
    s_dec = jnp.exp(block * log_g)
    bc = lambda a: jnp.broadcast_to(a[:, :, None], (N_HEADS, block, HEAD_DIM))
    return dmat, bc(q_dec), bc(k_dec), jnp.broadcast_to(s_dec[:, None, None], (N_HEADS, 1, HEAD_DIM))


def _routing_constants(tile):
    tri = (jnp.arange(tile)[:, None] > jnp.arange(tile)[None, :]).astype(BF16)
    upper = (jnp.arange(N_EXPERTS)[:, None] < jnp.arange(N_EXPERTS)[None, :]).astype(BF16)
    return tri, upper


def _front_out(n_tok, n_tiles):
    return [
        jax.ShapeDtypeStruct((n_tok, D_MODEL), F32),
        jax.ShapeDtypeStruct((n_tok * ROW_CHUNKS, LANES), F32),
        jax.ShapeDtypeStruct((n_tok, TOP_K), I32),
        jax.ShapeDtypeStruct((n_tok, TOP_K), F32),
        jax.ShapeDtypeStruct((n_tiles, 1, N_EXPERTS), F32),
        jax.ShapeDtypeStruct((n_tiles, 1, N_EXPERTS), F32),
    ]


def _prompt_call(x, p):
    bsz, seq, _ = x.shape
    tile = TOKEN_TILE
    n_t = seq // tile
    cq, sq, ck, sk = _rope_tables(jnp.arange(seq, dtype=I32))
    dmat, qdec, kdec, sdec = _decay_tables(PROMPT_RET_BLOCK)
    tri, upper = _routing_constants(tile)
    ws = p['w_s'][:, :GM_CHUNK, :GM_CHUNK]
    bst = p['b_s'][:, :GM_CHUNK].T

    const = lambda shape: pl.BlockSpec(shape, lambda b, t: (0,) * len(shape))
    pos = lambda: pl.BlockSpec((tile, HEAD_DIM), lambda b, t: (t, 0))
    in_specs = [
        pl.BlockSpec((1, tile, D_MODEL), lambda b, t: (b, t, 0)),
        const((1, D_MODEL)), const((D_MODEL, 6 * RET_WIDTH)),
        pos(), pos(), pos(), pos(),
        const((N_HEADS, PROMPT_RET_BLOCK, PROMPT_RET_BLOCK)),
        const((N_HEADS, PROMPT_RET_BLOCK, HEAD_DIM)), const((N_HEADS, PROMPT_RET_BLOCK, HEAD_DIM)),
        const((N_HEADS, 1, HEAD_DIM)),
        const((1, GM_WIDTH)), const((1, GM_WIDTH)),
        const((N_HEADS, GM_CHUNK, GM_CHUNK)), const((GM_CHUNK, N_HEADS)),
        const((D_MODEL, D_MODEL)), const((1, D_MODEL)),
        const((D_MODEL, N_EXPERTS)), const((D_MODEL, N_EXPERTS)), const((1, N_EXPERTS)),
        const((tile, tile)), const((N_EXPERTS, N_EXPERTS)),
    ]
    row = lambda b, t: (b * n_t + t, 0)
    row3 = lambda b, t: (b * n_t + t, 0, 0)
    out_specs = [
        pl.BlockSpec((tile, D_MODEL), row),
        pl.BlockSpec((tile * ROW_CHUNKS, LANES), row),
        pl.BlockSpec((tile, TOP_K), row),
        pl.BlockSpec((tile, TOP_K), row),
        pl.BlockSpec((1, 1, N_EXPERTS), row3),
        pl.BlockSpec((1, 1, N_EXPERTS), row3),
        pl.BlockSpec((1, N_HEADS, HEAD_DIM, HEAD_DIM), lambda b, t: (b, 0, 0, 0)),
    ]
    out_shape = _front_out(bsz * seq, bsz * n_t) + [jax.ShapeDtypeStruct((bsz, N_HEADS, HEAD_DIM, HEAD_DIM), F32)]
    return pl.pallas_call(
        functools.partial(_prompt_kernel, ret_block=PROMPT_RET_BLOCK),
        grid=(bsz, n_t),
        in_specs=in_specs,
        out_specs=out_specs,
        out_shape=out_shape,
        scratch_shapes=[
            pltpu.VMEM((N_HEADS, HEAD_DIM, HEAD_DIM), F32),
            pltpu.VMEM((tile, D_MODEL), BF16),
        ],
        compiler_params=pltpu.CompilerParams(
            dimension_semantics=("arbitrary", "arbitrary"), vmem_limit_bytes=VMEM_LIMIT),
        name="front_prompt",
    )(x, p['norm1_g'], p['w_in'], cq, sq, ck, sk, dmat, qdec, kdec, sdec, p['ln_v_g'], p['ln_v_b'], ws, bst,
      p['w_out'], p['norm2_g'], p['wr_hi'], p['wr_lo'], p['b_router'], tri, upper)


def _decode_call(x, state, past_len, p):
    n_streams, frames, _ = x.shape
    tile = n_streams * frames
    assert tile == TOKEN_TILE and frames <= RET_CHUNK and frames & (frames - 1) == 0
    rope = _rope_tables(past_len + jnp.arange(frames, dtype=I32))
    cq, sq, ck, sk = [jnp.tile(a, (n_streams, 1)) for a in rope]
    dmat, qdec, kdec, sdec = _decay_tables(frames)
    eye = jnp.eye(n_streams, dtype=F32)
    dblk = jnp.einsum('ab,hij->haibj', eye, dmat).reshape(N_HEADS, tile, tile)
    qdec = jnp.tile(qdec, (1, n_streams, 1))
    kdec = jnp.tile(kdec, (1, n_streams, 1))
    tri, upper = _routing_constants(tile)
    wtile = jnp.tile(p['w_s'][:, :frames, :frames], (1, n_streams, n_streams))
    bst = jnp.tile(p['b_s'][:, :frames].T, (n_streams, 1))

    def whole(a):
        return pl.BlockSpec(a.shape, lambda i, n=a.ndim: (0,) * n, pipeline_mode=pl.Buffered(1))

    args = (x.reshape(tile, D_MODEL), state, p['norm1_g'], p['w_in'], cq, sq, ck, sk, dblk, qdec, kdec, sdec,
            p['ln_v_g'], p['ln_v_b'], wtile, bst, p['w_out'], p['norm2_g'], p['wr_hi'], p['wr_lo'], p['b_router'],
            tri, upper)
    out_shape = _front_out(tile, 1) + [
        jax.ShapeDtypeStruct(state.shape, F32),
        jax.ShapeDtypeStruct((tile, GM_WIDTH), F32),
    ]
    out_specs = [pl.BlockSpec(s.shape, lambda i, n=len(s.shape): (0,) * n) for s in out_shape]
    out_specs[6] = pl.BlockSpec(memory_space=pl.ANY)
    return pl.pallas_call(
        functools.partial(_decode_kernel, n_streams=n_streams, frames=frames),
        grid=(1,),
        in_specs=[whole(a) for a in args],
        out_specs=out_specs,
        out_shape=out_shape,
        scratch_shapes=[
            pltpu.VMEM((tile, D_MODEL), BF16),
            pltpu.VMEM((N_HEADS, tile, HEAD_DIM), BF16),
            pltpu.VMEM((N_HEADS, tile, HEAD_DIM), BF16),
            pltpu.VMEM((N_HEADS, tile, HEAD_DIM), BF16),
            pltpu.VMEM((N_HEADS, tile, HEAD_DIM), F32),
            pltpu.VMEM((2, N_HEADS, HEAD_DIM, HEAD_DIM), F32),
            pltpu.SemaphoreType.DMA((2,)),
        ],
        compiler_params=pltpu.CompilerParams(
            dimension_semantics=("arbitrary",), vmem_limit_bytes=VMEM_LIMIT),
        name="front_decode",
    )(*args)


def _plan_kernel(ct_ref, lower_ref, upper_ref, cum_ref, nb_ref, bs_ref, be_ref, nbt_ref, *, n_blocks):
    counts = ct_ref[...]
    cum = _exact_count_dot(lower_ref[...], counts, counts_on_left=False)
    cum_ref[...] = cum.astype(I32)
    n_tiles = counts.shape[0]
    total = cum[n_tiles:n_tiles + 1]
    nb = jnp.floor((total + (EXPERT_ROWS - 1)) * (1.0 / EXPERT_ROWS))
    bstart = _dot(jnp.broadcast_to(nb, (8, N_EXPERTS)).astype(BF16), upper_ref[...])[0:1]
    bend = bstart + nb
    nb_ref[...] = nb.astype(I32)
    bs_ref[...] = bstart.astype(I32)
    blk = lax.broadcasted_iota(I32, (n_blocks, N_EXPERTS), 0).astype(F32)
    be = jnp.sum((bend <= blk).astype(F32), axis=-1, keepdims=True)
    be_ref[...] = jnp.minimum(be, N_EXPERTS - 1.0).astype(I32)
    nbt_ref[...] = jnp.sum(nb, axis=-1, keepdims=True).astype(I32)


def _plan_call(counts, n_blocks):
    n_tiles = counts.shape[0]
    lower = (jnp.arange(n_tiles + 1)[:, None] > jnp.arange(n_tiles)[None, :]).astype(BF16)
    upper = (jnp.arange(N_EXPERTS)[:, None] < jnp.arange(N_EXPERTS)[None, :]).astype(BF16)
    return pl.pallas_call(
        functools.partial(_plan_kernel, n_blocks=n_blocks),
        out_shape=[
            jax.ShapeDtypeStruct((n_tiles + 1, N_EXPERTS), I32),
            jax.ShapeDtypeStruct((1, N_EXPERTS), I32),
            jax.ShapeDtypeStruct((1, N_EXPERTS), I32),
            jax.ShapeDtypeStruct((n_blocks, 1), I32),
            jax.ShapeDtypeStruct((1, 1), I32),
        ],
        name="plan",
    )(counts, lower, upper)


def _dispatch_kernel(h2p_ref, h2s_ref, lp_ref, out_ref, *, n_prompt_tiles):
    i = pl.program_id(0)

    def permute(src_ref):
        def body(t, c):
            row = src_ref[_row_tile(t), :]
            for kk in range(TOP_K):
                out_ref[_row_tile(lp_ref[0, 0, t * TOP_K + kk]), :] = row
            return c

        lax.fori_loop(0, TOKEN_TILE, body, 0, unroll=8)

    @pl.when(i < n_prompt_tiles)
    def _():
        permute(h2p_ref)

    @pl.when(i >= n_prompt_tiles)
    def _():
        permute(h2s_ref)


def _dispatch_call(h2_p, h2_s, lp_tiles):
    rows = TOKEN_TILE * ROW_CHUNKS
    n_p_tiles = h2_p.shape[0] // rows
    n_s_tiles = h2_s.shape[0] // rows
    n_tiles = n_p_tiles + n_s_tiles
    return pl.pallas_call(
        functools.partial(_dispatch_kernel, n_prompt_tiles=n_p_tiles),
        grid=(n_tiles,),
        in_specs=[
            pl.BlockSpec((rows, LANES), lambda i: (jnp.minimum(i, n_p_tiles - 1), 0)),
            pl.BlockSpec((rows, LANES), lambda i: (jnp.maximum(i - n_p_tiles, 0), 0)),
            pl.BlockSpec((1, 1, TILE_SLOTS), lambda i: (i, 0, 0), memory_space=pltpu.SMEM),
        ],
        out_specs=pl.BlockSpec((TILE_SLOTS * ROW_CHUNKS, LANES), lambda i: (i, 0)),
        out_shape=jax.ShapeDtypeStruct((n_tiles * TILE_SLOTS * ROW_CHUNKS, LANES), F32),
        compiler_params=pltpu.CompilerParams(
            dimension_semantics=("arbitrary",), vmem_limit_bytes=VMEM_LIMIT),
        name="dispatch",
    )(h2_p, h2_s, lp_tiles)


def _expert_kernel(be_ref, bs_ref, nbt_ref, cum_ref, off_ref, xb_ref, wg_ref, wu_ref, wd_ref, bg_ref, bu_ref, bd_ref,
                   yb_ref, xbuf, ybuf, wbf, in_sem, out_sem, *, n_tiles, n_blocks):
    b = pl.program_id(0)
    nbt = nbt_ref[0]

    def segments(blk, slot, fn):
        e = be_ref[blk]
        r0 = (blk - bs_ref[e]) * EXPERT_ROWS

        def body(i, c):
            seg_lo = cum_ref[i * N_EXPERTS + e]
            seg_hi = cum_ref[(i + 1) * N_EXPERTS + e]
            lo = jnp.maximum(seg_lo, r0)
            hi = jnp.minimum(seg_hi, r0 + EXPERT_ROWS)

            @pl.when(hi > lo)
            def _():
                bucket_row = i * TILE_SLOTS + off_ref[i * N_EXPERTS + e] + (lo - seg_lo)
                fn(slot, bucket_row, slot * EXPERT_ROWS + (lo - r0), hi - lo)
            return c

        lax.fori_loop(0, n_tiles, body, 0)

    def words(row, n):
        return pl.ds(pl.multiple_of(row * ROW_CHUNKS, ROW_CHUNKS), n * ROW_CHUNKS)

    def gather(slot, bucket_row, buffer_row, n):
        return pltpu.make_async_copy(
            xb_ref.at[words(bucket_row, n)], xbuf.at[words(buffer_row, n)], in_sem.at[slot])

    def scatter(slot, bucket_row, buffer_row, n):
        return pltpu.make_async_copy(
            ybuf.at[words(buffer_row, n)], yb_ref.at[words(bucket_row, n)], out_sem.at[slot])

    def start_gather(blk):
        @pl.when(blk < nbt)
        def _():
            segments(blk, lax.rem(blk, 2), lambda *a: gather(*a).start())

    def wait_gather(blk):
        @pl.when(blk < nbt)
        def _():
            segments(blk, lax.rem(blk, 2), lambda *a: gather(*a).wait())

    def start_scatter(blk):
        @pl.when(blk < nbt)
        def _():
            segments(blk, lax.rem(blk, 2), lambda *a: scatter(*a).start())

    def wait_scatter(blk):
        @pl.when((blk >= 0) & (blk < nbt))
        def _():
            segments(blk, lax.rem(blk, 2), lambda *a: scatter(*a).wait())

    @pl.when(b == 0)
    def _():
        xbuf[...] = jnp.zeros_like(xbuf)
        start_gather(b)

    wait_gather(b)
    start_gather(b + 1)
    wait_scatter(b - 2)

    e = be_ref[b]
    live = b < nbt
    prev = be_ref[jnp.maximum(b - 1, 0)]
    fresh = (b == 0) | (e != prev)
    total = cum_ref[n_tiles * N_EXPERTS + e]
    rows_left = jnp.where(live, total - (b - bs_ref[e]) * EXPERT_ROWS, 0)
    base = lax.rem(b, 2) * EXPERT_ROWS

    @pl.when(live & fresh)
    def _():
        wbf[0] = wg_ref[0].astype(BF16)
        wbf[1] = wu_ref[0].astype(BF16)
        wbf[2] = wd_ref[0].astype(BF16)

    for s in range(EXPERT_ROWS // EXPERT_SUB_ROWS):
        first = s * EXPERT_SUB_ROWS

        @pl.when(rows_left > first)
        def _():
            x = _load_rows(xbuf, base + first, EXPERT_SUB_ROWS)
            valid = lax.broadcasted_iota(I32, (EXPERT_SUB_ROWS, 1), 0) < rows_left - first
            x = jnp.where(valid, x, 0.0).astype(BF16)
            gt = _dot(x, wbf[0]) + bg_ref[0]
            up = _dot(x, wbf[1]) + bu_ref[0]
            gt = jnp.minimum(gt, SWIGLU_LIMIT)
            up = jnp.clip(up, -SWIGLU_LIMIT, SWIGLU_LIMIT)
            act = gt * _sigmoid(gt * SWIGLU_ALPHA) * (up + 1.0)
            _store_rows(ybuf, base + first, _dot(act.astype(BF16), wbf[2]) + bd_ref[0])

    start_scatter(b)

    @pl.when(b == n_blocks - 1)
    def _():
        wait_scatter(b - 1)
        wait_scatter(b)


def _expert_call(be, bstart, nbt, cum_flat, off_flat, buckets, p, n_tiles, n_blocks):
    wspec = pl.BlockSpec((1, D_MODEL, D_MODEL), lambda b, be, *_: (be[b], 0, 0))
    bspec = pl.BlockSpec((1, 1, D_MODEL), lambda b, be, *_: (be[b], 0, 0))
    any_spec = pl.BlockSpec(memory_space=pl.ANY)
    return pl.pallas_call(
        functools.partial(_expert_kernel, n_tiles=n_tiles, n_blocks=n_blocks),
        grid_spec=pltpu.PrefetchScalarGridSpec(
            num_scalar_prefetch=5,
            grid=(n_blocks,),
            in_specs=[any_spec, wspec, wspec, wspec, bspec, bspec, bspec],
            out_specs=any_spec,
            scratch_shapes=[
                pltpu.VMEM((2 * EXPERT_ROWS * ROW_CHUNKS, LANES), F32),
                pltpu.VMEM((2 * EXPERT_ROWS * ROW_CHUNKS, LANES), F32),
                pltpu.VMEM((3, D_MODEL, D_MODEL), BF16),
                pltpu.SemaphoreType.DMA((2,)),
                pltpu.SemaphoreType.DMA((2,)),
            ],
        ),
        out_shape=jax.ShapeDtypeStruct(buckets.shape, F32),
        compiler_params=pltpu.CompilerParams(
            dimension_semantics=("arbitrary",), vmem_limit_bytes=VMEM_LIMIT),
        name="experts",
    )(be, bstart, nbt, cum_flat, off_flat, buckets, p['w_gate'], p['w_up'], p['w_down'],
      p['b_gate'][:, None, :], p['b_up'][:, None, :], p['b_down'][:, None, :])


def _combine_kernel(x2_ref, yb_ref, lp_ref, tw_ref, gf_ref, y_ref, acc):
    def body(t, c):
        j = t * TOP_K
        row = tw_ref[0, 0, j] * yb_ref[_row_tile(lp_ref[0, 0, j]), :]
        for kk in range(1, TOP_K):
            row = row + tw_ref[0, 0, j + kk] * yb_ref[_row_tile(lp_ref[0, 0, j + kk]), :]
        acc[_row_tile(t), :] = row
        return c

    lax.fori_loop(0, TOKEN_TILE, body, 0, unroll=4)
    y_ref[...] = _rmsnorm(x2_ref[...] + _load_rows(acc, 0, TOKEN_TILE), gf_ref[...])


def _combine_call(x2, ybuckets, lp_tiles, tw_tiles, gf, tile_off):
    n_tok = x2.shape[0]
    n_tiles = n_tok // TOKEN_TILE
    return pl.pallas_call(
        _combine_kernel,
        grid=(n_tiles,),
        in_specs=[
            pl.BlockSpec((TOKEN_TILE, D_MODEL), lambda i: (i, 0)),
            pl.BlockSpec((TILE_SLOTS * ROW_CHUNKS, LANES), lambda i: (tile_off + i, 0)),
            pl.BlockSpec((1, 1, TILE_SLOTS), lambda i: (tile_off + i, 0, 0), memory_space=pltpu.SMEM),
            pl.BlockSpec((1, 1, TILE_SLOTS), lambda i: (tile_off + i, 0, 0), memory_space=pltpu.SMEM),
            pl.BlockSpec((1, D_MODEL), lambda i: (0, 0)),
        ],
        out_specs=pl.BlockSpec((TOKEN_TILE, D_MODEL), lambda i: (i, 0)),
        scratch_shapes=[pltpu.VMEM((TOKEN_TILE * ROW_CHUNKS, LANES), F32)],
        out_shape=jax.ShapeDtypeStruct((n_tok, D_MODEL), F32),
        compiler_params=pltpu.CompilerParams(
            dimension_semantics=("arbitrary",), vmem_limit_bytes=VMEM_LIMIT),
        name="combine",
    )(x2, ybuckets, lp_tiles, tw_tiles, gf)


def kernel(x_prompt, x_sample, state_ret, norm1_g, w_in, ln_v_g, ln_v_b, w_s, b_s, w_out, norm2_g, w_router, b_router, w_gate, b_gate, w_up, b_up, w_down, b_down, norm_f_g):
    bsz, seq, _ = x_prompt.shape
    dbsz, dseq, _ = x_sample.shape
    n_p = bsz * seq
    n_s = dbsz * dseq
    n_total = n_p + n_s
    past_len = 2048
    assert w_in.shape[0] == 1, "single layer"

    wr = w_router[0]
    wr_hi = wr.astype(BF16)
    wr_lo = (wr - wr_hi.astype(F32)).astype(BF16)
    p = dict(norm1_g=norm1_g, w_in=w_in[0].astype(BF16), ln_v_g=ln_v_g, ln_v_b=ln_v_b, w_s=w_s[0], b_s=b_s[0],
             w_out=w_out[0].astype(BF16), norm2_g=norm2_g, wr_hi=wr_hi, wr_lo=wr_lo, b_router=b_router,
             w_gate=w_gate[0], b_gate=b_gate[0], w_up=w_up[0], b_up=b_up[0], w_down=w_down[0], b_down=b_down[0])

    x2_p, h2_p, lp_p, tw_p, ct_p, of_p, st_p = _prompt_call(x_prompt, p)
    x2_s, h2_s, lp_s, tw_s, ct_s, of_s, st_s, vn_s = _decode_call(x_sample, state_ret[0], past_len, p)

    n_tiles = n_total // TOKEN_TILE
    lp = jnp.concatenate([lp_p, lp_s], axis=0).reshape(n_tiles, 1, TILE_SLOTS)
    tw = jnp.concatenate([tw_p, tw_s], axis=0).reshape(n_tiles, 1, TILE_SLOTS)
    counts = jnp.concatenate([ct_p, ct_s], axis=0).reshape(n_tiles, N_EXPERTS)
    offsets = jnp.concatenate([of_p, of_s], axis=0).reshape(n_tiles * N_EXPERTS).astype(I32)

    n_blocks = -(-(n_total * TOP_K + N_EXPERTS * (EXPERT_ROWS - 1)) // EXPERT_ROWS)
    cum, nb, bstart, be, nbt = _plan_call(counts, n_blocks)

    buckets = _dispatch_call(h2_p, h2_s, lp)
    ybuckets = _expert_call(be.reshape(n_blocks), bstart.reshape(N_EXPERTS), nbt.reshape(1),
                            cum.reshape((n_tiles + 1) * N_EXPERTS), offsets, buckets, p, n_tiles, n_blocks)

    y_p = _combine_call(x2_p, ybuckets, lp, tw, norm_f_g[None, :], 0)
    y_s = _combine_call(x2_s, ybuckets, lp, tw, norm_f_g[None, :], n_p // TOKEN_TILE)

    return (y_p.reshape(bsz, seq, D_MODEL), y_s.reshape(dbsz, dseq, D_MODEL),
            st_p[None], st_s[None], vn_s.reshape(1, dbsz, dseq, GM_WIDTH))
```

```python
import functools

import numpy as np
import jax
import jax.numpy as jnp
from jax import lax
from jax.experimental import pallas as pl
from jax.experimental.pallas import tpu as pltpu

F32 = jnp.float32
BF16 = jnp.bfloat16
I32 = jnp.int32

D_MODEL = 1024
RET_WIDTH = 512
N_HEADS = 4
HEAD_DIM = 128
GM_WIDTH = 512
GM_CHUNK = 128
RET_CHUNK = 64
N_EXPERTS = 32
TOP_K = 4
SWIGLU_LIMIT = 7.0
SWIGLU_ALPHA = 1.702
ROPE_BASE = 10000.0
EPS = 1e-6

TOKEN_TILE = 512
PROMPT_RET_BLOCK = 256
EXPERT_ROWS = 512
EXPERT_SUB_ROWS = 256
TILE_SLOTS = TOKEN_TILE * TOP_K
VMEM_LIMIT = 56 * 1024 * 1024
LANES = 128
ROW_CHUNKS = D_MODEL // LANES
SPLIT = 16.0
_LOG_GAMMA = [float(np.log1p(-np.float32(2.0) ** np.float32(-5.0 - hd)).astype(np.float32)) for hd in range(N_HEADS)]


def _load_rows(ref, first_row, n_rows):
    base = first_row * ROW_CHUNKS
    return jnp.concatenate(
        [ref[pl.ds(base + c, n_rows, stride=ROW_CHUNKS), :] for c in range(ROW_CHUNKS)], axis=1)


def _store_rows(ref, first_row, val):
    base = first_row * ROW_CHUNKS
    for c in range(ROW_CHUNKS):
        ref[pl.ds(base + c, val.shape[0], stride=ROW_CHUNKS), :] = val[:, c * LANES:(c + 1) * LANES]


def _row_tile(row):
    return pl.ds(pl.multiple_of(row * ROW_CHUNKS, ROW_CHUNKS), ROW_CHUNKS)


def _rmsnorm(x, g):
    ms = jnp.mean(x * x, axis=-1, keepdims=True)
    return x * lax.rsqrt(ms + EPS) * g


def _gelu(x):
    c = np.float32(np.sqrt(2.0 / np.pi))
    return x * (0.5 * (1.0 + jnp.tanh(c * (x + 0.044715 * (x * x * x)))))


def _sigmoid(x):
    return 1.0 / (1.0 + jnp.exp(-x))


def _dot(a, b):
    return jnp.dot(a, b, preferred_element_type=F32)


def _dot_nt(a, b):
    return lax.dot_general(a, b, (((1,), (1,)), ((), ())), preferred_element_type=F32)


def _dot_tn(a, b):
    return lax.dot_general(a, b, (((0,), (0,)), ((), ())), preferred_element_type=F32)


def _exact_count_dot(a, b, *, counts_on_left):
    cnt = a if counts_on_left else b
    hi = jnp.floor(cnt * (1.0 / SPLIT))
    lo = cnt - SPLIT * hi
    if counts_on_left:
        return SPLIT * _dot(hi.astype(BF16), b) + _dot(lo.astype(BF16), b)
    return SPLIT * _dot(a, hi.astype(BF16)) + _dot(a, lo.astype(BF16))


def _rotary(xh, cos2, sin2):
    return xh * cos2 + pltpu.roll(xh, HEAD_DIM // 2, axis=1) * sin2


def _gated_head_norm(o, gate):
    mu = jnp.mean(o, axis=-1, keepdims=True)
    oc = o - mu
    var = jnp.mean(oc * oc, axis=-1, keepdims=True)
    return ((gate * _sigmoid(gate)) * (oc * lax.rsqrt(var + EPS))).astype(BF16)


def _gmlp_inputs(u, vg, lng, lnb):
    uf = _gelu(u)
    vf = _gelu(vg)
    mu = jnp.mean(vf, axis=-1, keepdims=True)
    vc = vf - mu
    var = jnp.mean(vc * vc, axis=-1, keepdims=True)
    return uf, vc * lax.rsqrt(var + EPS) * lng + lnb


def _route_and_store(x, mix_scr, wout_ref, g2_ref, wrh_ref, wrl_ref, br_ref, tri_ref, upper_ref,
                     x2_ref, h2_ref, lp_ref, tw_ref, ct_ref, of_ref):
    tile = x.shape[0]
    x2 = x + _dot(mix_scr[...], wout_ref[...])
    x2_ref[...] = x2
    h2 = _rmsnorm(x2, g2_ref[...])
    _store_rows(h2_ref, 0, h2)

    h_hi = h2.astype(BF16)
    h_lo = (h2 - h_hi.astype(F32)).astype(BF16)
    logits = (_dot(h_hi, wrh_ref[...]) + _dot(h_lo, wrh_ref[...]) + _dot(h_hi, wrl_ref[...])) + br_ref[...]

    iota_e = lax.broadcasted_iota(I32, (tile, N_EXPERTS), 1)
    iota_ef = iota_e.astype(F32)
    iota_k = lax.broadcasted_iota(I32, (tile, TOP_K), 1)
    lg = logits
    vals = []
    idxs = []
    for _ in range(TOP_K):
        m = jnp.max(lg, axis=-1, keepdims=True)
        idx = jnp.min(jnp.where(lg == m, iota_ef, float(N_EXPERTS)), axis=-1, keepdims=True).astype(I32)
        vals.append(m)
        idxs.append(idx)
        lg = jnp.where(iota_e == idx, -jnp.inf, lg)
    exps = [jnp.exp(vk - vals[0]) for vk in vals]
    denom = exps[0] + exps[1] + exps[2] + exps[3]

    onehot = jnp.zeros((tile, N_EXPERTS), F32)
    for idx in idxs:
        onehot = onehot + (iota_e == idx).astype(F32)
    rank = _dot(tri_ref[...], onehot.astype(BF16))
    count = jnp.sum(onehot, axis=0, keepdims=True)
    offset = _exact_count_dot(jnp.broadcast_to(count, (8, N_EXPERTS)), upper_ref[...], counts_on_left=True)[0:1]
    where_to = rank + offset
    lp = jnp.zeros((tile, TOP_K), I32)
    tw = jnp.zeros((tile, TOP_K), F32)
    for kk in range(TOP_K):
        pos = jnp.sum(jnp.where(iota_e == idxs[kk], where_to, 0.0), axis=-1, keepdims=True).astype(I32)
        lp = jnp.where(iota_k == kk, pos, lp)
        tw = jnp.where(iota_k == kk, exps[kk] / denom, tw)
    lp_ref[...] = lp
    tw_ref[...] = tw
    ct_ref[0] = count
    of_ref[0] = offset


def _prompt_kernel(x_ref, g1_ref, win_ref, cq_ref, sq_ref, ck_ref, sk_ref, dmat_ref, qdec_ref, kdec_ref,
                   sdec_ref, lng_ref, lnb_ref, ws_ref, bst_ref, wout_ref, g2_ref, wrh_ref, wrl_ref, br_ref,
                   tri_ref, upper_ref,
                   x2_ref, h2_ref, lp_ref, tw_ref, ct_ref, of_ref, st_ref,
                   s_scr, mix_scr, *, ret_block):
    tile = TOKEN_TILE
    t = pl.program_id(1)

    @pl.when(t == 0)
    def _():
        s_scr[...] = jnp.zeros_like(s_scr)

    x = x_ref[0]
    h = _rmsnorm(x, g1_ref[...]).astype(BF16)

    def proj(i):
        return _dot(h, win_ref[:, i * RET_WIDTH:(i + 1) * RET_WIDTH])

    q = proj(0)
    k = proj(1)
    v = proj(2)
    gate = proj(3)
    u = proj(4)
    vg = proj(5)
    cq = cq_ref[...]
    sq = sq_ref[...]
    ck = ck_ref[...]
    sk = sk_ref[...]

    for hd in range(N_HEADS):
        cols = slice(hd * HEAD_DIM, (hd + 1) * HEAD_DIM)
        qr = _rotary(q[:, cols], cq, sq)
        kr = _rotary(k[:, cols], ck, sk)
        vh = v[:, cols].astype(BF16)
        dm = dmat_ref[hd]
        qd = qdec_ref[hd]
        kd = kdec_ref[hd]
        sd = sdec_ref[hd]
        for c in range(tile // ret_block):
            rows = slice(c * ret_block, (c + 1) * ret_block)
            qb = qr[rows].astype(BF16)
            kb = kr[rows]
            vb = vh[rows]
            state = s_scr[hd]
            scores = _dot_nt(qb, kb.astype(BF16)) * dm
            o = _dot(scores.astype(BF16), vb) + qd * _dot(qb, state.astype(BF16))
            s_scr[hd] = sd * state + _dot_tn((kb * kd).astype(BF16), vb)
            mix_scr[rows, cols] = _gated_head_norm(o, gate[rows, cols])

    uf, vn = _gmlp_inputs(u, vg, lng_ref[...], lnb_ref[...])
    vnb = vn.astype(BF16)
    r_i = lax.broadcasted_iota(I32, (GM_CHUNK, GM_CHUNK), 0)
    c_i = lax.broadcasted_iota(I32, (GM_CHUNK, GM_CHUNK), 1)
    for hd in range(N_HEADS):
        cols = slice(hd * HEAD_DIM, (hd + 1) * HEAD_DIM)
        w_tril = jnp.where(r_i >= c_i, ws_ref[hd], 0.0).astype(BF16)
        bias = bst_ref[:, hd:hd + 1]
        for c in range(tile // GM_CHUNK):
            rows = slice(c * GM_CHUNK, (c + 1) * GM_CHUNK)
            mixed = _dot(w_tril, vnb[rows, cols]) + bias
            mix_scr[rows, RET_WIDTH + hd * HEAD_DIM:RET_WIDTH + (hd + 1) * HEAD_DIM] = (
                uf[rows, cols] * mixed).astype(BF16)

    _route_and_store(x, mix_scr, wout_ref, g2_ref, wrh_ref, wrl_ref, br_ref, tri_ref, upper_ref,
                     x2_ref, h2_ref, lp_ref, tw_ref, ct_ref, of_ref)

    @pl.when(t == pl.num_programs(1) - 1)
    def _():
        st_ref[0] = s_scr[...]


def _decode_kernel(x_ref, s0_ref, g1_ref, win_ref, cq_ref, sq_ref, ck_ref, sk_ref, qdec_ref, kdec_ref,
                   sdec_ref, lng_ref, lnb_ref, ws_ref, bst_ref, wout_ref, g2_ref, wrh_ref, wrl_ref, br_ref,
                   tri_ref, upper_ref,
                   x2_ref, h2_ref, lp_ref, tw_ref, ct_ref, of_ref, st_ref, vn_ref,
                   mix_scr, q_scr, k_scr, v_scr, oc_scr, st_stage, st_sem, *, n_streams, frames):
    tile = n_streams * frames
    x = x_ref[...]
    h = _rmsnorm(x, g1_ref[...]).astype(BF16)

    def proj(i):
        return _dot(h, win_ref[:, i * RET_WIDTH:(i + 1) * RET_WIDTH])

    q = proj(0)
    k = proj(1)
    v = proj(2)
    gate = proj(3)
    u = proj(4)
    vg = proj(5)
    cq = cq_ref[...]
    sq = sq_ref[...]
    ck = ck_ref[...]
    sk = sk_ref[...]

    r_i = lax.broadcasted_iota(I32, (tile, tile), 0)
    c_i = lax.broadcasted_iota(I32, (tile, tile), 1)
    shift = frames.bit_length() - 1
    r_frame = lax.bitwise_and(r_i, frames - 1)
    c_frame = lax.bitwise_and(c_i, frames - 1)
    keep = (lax.shift_right_logical(r_i, shift) == lax.shift_right_logical(c_i, shift)) & (r_frame >= c_frame)
    frame_gap = (r_frame - c_frame).astype(F32)

    o_in = []
    for hd in range(N_HEADS):
        cols = slice(hd * HEAD_DIM, (hd + 1) * HEAD_DIM)
        qr = _rotary(q[:, cols], cq, sq)
        kr = _rotary(k[:, cols], ck, sk)
        vh = v[:, cols].astype(BF16)
        qb = qr.astype(BF16)
        decay = jnp.where(keep, jnp.exp(frame_gap * _LOG_GAMMA[hd]), 0.0)
        scores = _dot_nt(qb, kr.astype(BF16)) * decay
        o_in.append(_dot(scores.astype(BF16), vh))
        q_scr[hd] = qb
        k_scr[hd] = (kr * kdec_ref[hd]).astype(BF16)
        v_scr[hd] = vh

    def state_copy(b):
        slot = lax.rem(b, 2)
        return pltpu.make_async_copy(st_stage.at[slot], st_ref.at[b], st_sem.at[slot])

    def stream_body(b, carry):
        rows = pl.ds(pl.multiple_of(b * frames, frames), frames)
        slot = lax.rem(b, 2)

        @pl.when(b >= 2)
        def _():
            state_copy(b - 2).wait()

        for hd in range(N_HEADS):
            state = s0_ref[b, hd]
            oc_scr[hd, rows, :] = _dot(q_scr[hd, rows, :], state.astype(BF16))
            st_stage[slot, hd] = sdec_ref[hd] * state + _dot_tn(k_scr[hd, rows, :], v_scr[hd, rows, :])
        state_copy(b).start()
        return carry

    lax.fori_loop(0, n_streams, stream_body, 0)
    state_copy(n_streams - 2).wait()
    state_copy(n_streams - 1).wait()

    for hd in range(N_HEADS):
        cols = slice(hd * HEAD_DIM, (hd + 1) * HEAD_DIM)
        o = o_in[hd] + qdec_ref[hd] * oc_scr[hd]
        mix_scr[:, cols] = _gated_head_norm(o, gate[:, cols])

    uf, vn = _gmlp_inputs(u, vg, lng_ref[...], lnb_ref[...])
    vn_ref[...] = vn
    vnb = vn.astype(BF16)
    sel = (lax.bitwise_and(lax.broadcasted_iota(I32, (tile, frames), 0), frames - 1)
           == lax.broadcasted_iota(I32, (tile, frames), 1)).astype(BF16)
    for hd in range(N_HEADS):
        cols = slice(hd * HEAD_DIM, (hd + 1) * HEAD_DIM)
        w_rows = _dot(sel, ws_ref[hd].astype(BF16)).astype(BF16)
        w_blk = jnp.where(keep, _dot_nt(w_rows, sel), 0.0).astype(BF16)
        mixed = _dot(w_blk, vnb[:, cols]) + bst_ref[:, hd:hd + 1]
        mix_scr[:, RET_WIDTH + hd * HEAD_DIM:RET_WIDTH + (hd + 1) * HEAD_DIM] = (uf[:, cols] * mixed).astype(BF16)

    _route_and_store(x, mix_scr, wout_ref, g2_ref, wrh_ref, wrl_ref, br_ref, tri_ref, upper_ref,
                     x2_ref, h2_ref, lp_ref, tw_ref, ct_ref, of_ref)


def _rope_tables(pos):
    half = HEAD_DIM // 2
    inv = ROPE_BASE ** (-jnp.arange(half, dtype=F32) / half)
    ang = pos.astype(F32)[:, None] * inv[None, :]
    cos = jnp.cos(ang)
    sin = jnp.sin(ang)
    cq = jnp.concatenate([cos, cos], axis=-1)
    sq = jnp.concatenate([-sin, sin], axis=-1)
    scale = HEAD_DIM ** -0.5
    return cq, sq, cq * scale, sq * scale


def _decay_tables(block):
    log_g = jnp.log1p(-(2.0 ** (-5.0 - jnp.arange(N_HEADS, dtype=F32))))
    idx = jnp.arange(block, dtype=F32)
    diff = idx[:, None] - idx[None, :]
    dmat = jnp.where(diff[None] >= 0, jnp.exp(jnp.maximum(diff, 0.0)[None] * log_g[:, None, None]), 0.0)
    q_dec = jnp.exp((idx + 1.0)[None, :] * log_g[:, None])
    k_dec = jnp.exp((block - 1.0 - idx)[None, :] * log_g[:, None])
    s_dec = jnp.exp(block * log_g)
    bc = lambda a: jnp.broadcast_to(a[:, :, None], (N_HEADS, block, HEAD_DIM))
    return dmat, bc(q_dec), bc(k_dec), jnp.broadcast_to(s_dec[:, None, None], (N_HEADS, 1, HEAD_DIM))


def _routing_constants(tile):
    tri = (jnp.arange(tile)[:, None] > jnp.arange(tile)[None, :]).astype(BF16)
    upper = (jnp.arange(N_EXPERTS)[:, None] < jnp.arange(N_EXPERTS)[None, :]).astype(BF16)
    return tri, upper


def _front_out(n_tok, n_tiles):
    return [
        jax.ShapeDtypeStruct((n_tok, D_MODEL), F32),
        jax.ShapeDtypeStruct((n_tok * ROW_CHUNKS, LANES), F32),
        jax.ShapeDtypeStruct((n_tok, TOP_K), I32),
        jax.ShapeDtypeStruct((n_tok, TOP_K), F32),
        jax.ShapeDtypeStruct((n_tiles, 1, N_EXPERTS), F32),
        jax.ShapeDtypeStruct((n_tiles, 1, N_EXPERTS), F32),
    ]


def _prompt_call(x, p):
    bsz, seq, _ = x.shape
    tile = TOKEN_TILE
    n_t = seq // tile
    cq, sq, ck, sk = _rope_tables(jnp.arange(seq, dtype=I32))
    dmat, qdec, kdec, sdec = _decay_tables(PROMPT_RET_BLOCK)
    tri, upper = _routing_constants(tile)
    ws = p['w_s'][:, :GM_CHUNK, :GM_CHUNK]
    bst = p['b_s'][:, :GM_CHUNK].T

    const = lambda shape: pl.BlockSpec(shape, lambda b, t: (0,) * len(shape))
    pos = lambda: pl.BlockSpec((tile, HEAD_DIM), lambda b, t: (t, 0))
    in_specs = [
        pl.BlockSpec((1, tile, D_MODEL), lambda b, t: (b, t, 0)),
        const((1, D_MODEL)), const((D_MODEL, 6 * RET_WIDTH)),
        pos(), pos(), pos(), pos(),
        const((N_HEADS, PROMPT_RET_BLOCK, PROMPT_RET_BLOCK)),
        const((N_HEADS, PROMPT_RET_BLOCK, HEAD_DIM)), const((N_HEADS, PROMPT_RET_BLOCK, HEAD_DIM)),
        const((N_HEADS, 1, HEAD_DIM)),
        const((1, GM_WIDTH)), const((1, GM_WIDTH)),
        const((N_HEADS, GM_CHUNK, GM_CHUNK)), const((GM_CHUNK, N_HEADS)),
        const((D_MODEL, D_MODEL)), const((1, D_MODEL)),
        const((D_MODEL, N_EXPERTS)), const((D_MODEL, N_EXPERTS)), const((1, N_EXPERTS)),
        const((tile, tile)), const((N_EXPERTS, N_EXPERTS)),
    ]
    row = lambda b, t: (b * n_t + t, 0)
    row3 = lambda b, t: (b * n_t + t, 0, 0)
    out_specs = [
        pl.BlockSpec((tile, D_MODEL), row),
        pl.BlockSpec((tile * ROW_CHUNKS, LANES), row),
        pl.BlockSpec((tile, TOP_K), row),
        pl.BlockSpec((tile, TOP_K), row),
        pl.BlockSpec((1, 1, N_EXPERTS), row3),
        pl.BlockSpec((1, 1, N_EXPERTS), row3),
        pl.BlockSpec((1, N_HEADS, HEAD_DIM, HEAD_DIM), lambda b, t: (b, 0, 0, 0)),
    ]
    out_shape = _front_out(bsz * seq, bsz * n_t) + [jax.ShapeDtypeStruct((bsz, N_HEADS, HEAD_DIM, HEAD_DIM), F32)]
    return pl.pallas_call(
        functools.partial(_prompt_kernel, ret_block=PROMPT_RET_BLOCK),
        grid=(bsz, n_t),
        in_specs=in_specs,
        out_specs=out_specs,
        out_shape=out_shape,
        scratch_shapes=[
            pltpu.VMEM((N_HEADS, HEAD_DIM, HEAD_DIM), F32),
            pltpu.VMEM((tile, D_MODEL), BF16),
        ],
        compiler_params=pltpu.CompilerParams(
            dimension_semantics=("arbitrary", "arbitrary"), vmem_limit_bytes=VMEM_LIMIT),
        name="front_prompt",
    )(x, p['norm1_g'], p['w_in'], cq, sq, ck, sk, dmat, qdec, kdec, sdec, p['ln_v_g'], p['ln_v_b'], ws, bst,
      p['w_out'], p['norm2_g'], p['wr_hi'], p['wr_lo'], p['b_router'], tri, upper)


def _decode_call(x, state, past_len, p):
    n_streams, frames, _ = x.shape
    tile = n_streams * frames
    assert tile == TOKEN_TILE and frames <= RET_CHUNK and frames & (frames - 1) == 0
    rope = _rope_tables(past_len + jnp.arange(frames, dtype=I32))
    cq, sq, ck, sk = [jnp.tile(a, (n_streams, 1)) for a in rope]
    _, qdec, kdec, sdec = _decay_tables(frames)
    qdec = jnp.tile(qdec, (1, n_streams, 1))
    kdec = jnp.tile(kdec, (1, n_streams, 1))
    tri, upper = _routing_constants(tile)
    ws = p['w_s'][:, :frames, :frames]
    bst = jnp.tile(p['b_s'][:, :frames].T, (n_streams, 1))

    def whole(a):
        return pl.BlockSpec(a.shape, lambda i, n=a.ndim: (0,) * n, pipeline_mode=pl.Buffered(1))

    args = (x.reshape(tile, D_MODEL), state, p['norm1_g'], p['w_in'], cq, sq, ck, sk, qdec, kdec, sdec,
            p['ln_v_g'], p['ln_v_b'], ws, bst, p['w_out'], p['norm2_g'], p['wr_hi'], p['wr_lo'], p['b_router'],
            tri, upper)
    out_shape = _front_out(tile, 1) + [
        jax.ShapeDtypeStruct(state.shape, F32),
        jax.ShapeDtypeStruct((tile, GM_WIDTH), F32),
    ]
    out_specs = [pl.BlockSpec(s.shape, lambda i, n=len(s.shape): (0,) * n) for s in out_shape]
    out_specs[6] = pl.BlockSpec(memory_space=pl.ANY)
    return pl.pallas_call(
        functools.partial(_decode_kernel, n_streams=n_streams, frames=frames),
        grid=(1,),
        in_specs=[whole(a) for a in args],
        out_specs=out_specs,
        out_shape=out_shape,
        scratch_shapes=[
            pltpu.VMEM((tile, D_MODEL), BF16),
            pltpu.VMEM((N_HEADS, tile, HEAD_DIM), BF16),
            pltpu.VMEM((N_HEADS, tile, HEAD_DIM), BF16),
            pltpu.VMEM((N_HEADS, tile, HEAD_DIM), BF16),
            pltpu.VMEM((N_HEADS, tile, HEAD_DIM), F32),
            pltpu.VMEM((2, N_HEADS, HEAD_DIM, HEAD_DIM), F32),
            pltpu.SemaphoreType.DMA((2,)),
        ],
        compiler_params=pltpu.CompilerParams(
            dimension_semantics=("arbitrary",), vmem_limit_bytes=VMEM_LIMIT),
        name="front_decode",
    )(*args)


def _plan_kernel(ct_ref, lower_ref, upper_ref, cum_ref, nb_ref, bs_ref, be_ref, nbt_ref, tlo_ref, thi_ref,
                 *, n_blocks):
    counts = ct_ref[...]
    cum = _exact_count_dot(lower_ref[...], counts, counts_on_left=False)
    cum_ref[...] = cum.astype(I32)
    n_tiles = counts.shape[0]
    total = cum[n_tiles:n_tiles + 1]
    nb = jnp.floor((total + (EXPERT_ROWS - 1)) * (1.0 / EXPERT_ROWS))
    bstart = _dot(jnp.broadcast_to(nb, (8, N_EXPERTS)).astype(BF16), upper_ref[...])[0:1]
    bend = bstart + nb
    nb_ref[...] = nb.astype(I32)
    bs_ref[...] = bstart.astype(I32)
    blk = lax.broadcasted_iota(I32, (n_blocks, N_EXPERTS), 0).astype(F32)
    be = jnp.minimum(jnp.sum((bend <= blk).astype(F32), axis=-1, keepdims=True), N_EXPERTS - 1.0)
    be_ref[...] = be.astype(I32)
    nbt_ref[...] = jnp.sum(nb, axis=-1, keepdims=True).astype(I32)
    mine = lax.broadcasted_iota(I32, (n_blocks, N_EXPERTS), 1).astype(F32) == be
    pick = lambda row: jnp.sum(jnp.where(mine, row, 0.0), axis=-1, keepdims=True)
    r0 = (blk[:, 0:1] - pick(bstart)) * EXPERT_ROWS
    tlo = jnp.zeros((n_blocks, 1), F32)
    thi = jnp.zeros((n_blocks, 1), F32)
    seg_lo = pick(cum[0:1])
    for i in range(n_tiles):
        seg_hi = pick(cum[i + 1:i + 2])
        tlo = tlo + (seg_hi <= r0).astype(F32)
        thi = thi + (seg_lo < r0 + EXPERT_ROWS).astype(F32)
        seg_lo = seg_hi
    tlo_ref[...] = tlo.astype(I32)
    thi_ref[...] = thi.astype(I32)


def _plan_call(counts, n_blocks):
    n_tiles = counts.shape[0]
    lower = (jnp.arange(n_tiles + 1)[:, None] > jnp.arange(n_tiles)[None, :]).astype(BF16)
    upper = (jnp.arange(N_EXPERTS)[:, None] < jnp.arange(N_EXPERTS)[None, :]).astype(BF16)
    return pl.pallas_call(
        functools.partial(_plan_kernel, n_blocks=n_blocks),
        out_shape=[
            jax.ShapeDtypeStruct((n_tiles + 1, N_EXPERTS), I32),
            jax.ShapeDtypeStruct((1, N_EXPERTS), I32),
            jax.ShapeDtypeStruct((1, N_EXPERTS), I32),
            jax.ShapeDtypeStruct((n_blocks, 1), I32),
            jax.ShapeDtypeStruct((1, 1), I32),
            jax.ShapeDtypeStruct((n_blocks, 1), I32),
            jax.ShapeDtypeStruct((n_blocks, 1), I32),
        ],
        name="plan",
    )(counts, lower, upper)


def _dispatch_kernel(h2p_ref, h2s_ref, lp_ref, out_ref, *, n_prompt_tiles):
    i = pl.program_id(0)

    def permute(src_ref):
        def body(t, c):
            row = src_ref[_row_tile(t), :]
            for kk in range(TOP_K):
                out_ref[_row_tile(lp_ref[0, 0, t * TOP_K + kk]), :] = row
            return c

        lax.fori_loop(0, TOKEN_TILE, body, 0, unroll=8)

    @pl.when(i < n_prompt_tiles)
    def _():
        permute(h2p_ref)

    @pl.when(i >= n_prompt_tiles)
    def _():
        permute(h2s_ref)


def _dispatch_call(h2_p, h2_s, lp_tiles):
    rows = TOKEN_TILE * ROW_CHUNKS
    n_p_tiles = h2_p.shape[0] // rows
    n_s_tiles = h2_s.shape[0] // rows
    n_tiles = n_p_tiles + n_s_tiles
    return pl.pallas_call(
        functools.partial(_dispatch_kernel, n_prompt_tiles=n_p_tiles),
        grid=(n_tiles,),
        in_specs=[
            pl.BlockSpec((rows, LANES), lambda i: (jnp.minimum(i, n_p_tiles - 1), 0)),
            pl.BlockSpec((rows, LANES), lambda i: (jnp.maximum(i - n_p_tiles, 0), 0)),
            pl.BlockSpec((1, 1, TILE_SLOTS), lambda i: (i, 0, 0), memory_space=pltpu.SMEM),
        ],
        out_specs=pl.BlockSpec((TILE_SLOTS * ROW_CHUNKS, LANES), lambda i: (i, 0)),
        out_shape=jax.ShapeDtypeStruct((n_tiles * TILE_SLOTS * ROW_CHUNKS, LANES), F32),
        compiler_params=pltpu.CompilerParams(
            dimension_semantics=("arbitrary",), vmem_limit_bytes=VMEM_LIMIT),
        name="dispatch",
    )(h2_p, h2_s, lp_tiles)


def _expert_kernel(be_ref, bs_ref, nbt_ref, cum_ref, off_ref, tlo_ref, thi_ref,
                   xb_ref, wg_ref, wu_ref, wd_ref, bg_ref, bu_ref, bd_ref,
                   yb_ref, xbuf, ybuf, wbf, in_sem, out_sem, *, n_tiles, n_blocks):
    b = pl.program_id(0)
    nbt = nbt_ref[0]

    def block_rows(blk):
        e = be_ref[blk]
        r0 = (blk - bs_ref[e]) * EXPERT_ROWS
        return r0, jnp.minimum(cum_ref[n_tiles * N_EXPERTS + e] - r0, EXPERT_ROWS)

    def segments(blk, slot, fn):
        e = be_ref[blk]
        r0 = (blk - bs_ref[e]) * EXPERT_ROWS

        def body(i, c):
            seg_lo = cum_ref[i * N_EXPERTS + e]
            seg_hi = cum_ref[(i + 1) * N_EXPERTS + e]
            lo = jnp.maximum(seg_lo, r0)
            hi = jnp.minimum(seg_hi, r0 + EXPERT_ROWS)

            @pl.when(hi > lo)
            def _():
                bucket_row = i * TILE_SLOTS + off_ref[i * N_EXPERTS + e] + (lo - seg_lo)
                fn(slot, bucket_row, slot * EXPERT_ROWS + (lo - r0), hi - lo)
            return c

        lax.fori_loop(tlo_ref[blk], thi_ref[blk], body, 0)

    def words(row, n):
        return pl.ds(pl.multiple_of(row * ROW_CHUNKS, ROW_CHUNKS), n * ROW_CHUNKS)

    def gather(slot, bucket_row, buffer_row, n):
        return pltpu.make_async_copy(
            xb_ref.at[words(bucket_row, n)], xbuf.at[words(buffer_row, n)], in_sem.at[slot])

    def scatter(slot, bucket_row, buffer_row, n):
        return pltpu.make_async_copy(
            ybuf.at[words(buffer_row, n)], yb_ref.at[words(bucket_row, n)], out_sem.at[slot])

    def start_gather(blk):
        @pl.when(blk < nbt)
        def _():
            segments(blk, lax.rem(blk, 2), lambda *a: gather(*a).start())

    def wait_gather(blk):
        @pl.when(blk < nbt)
        def _():
            slot = lax.rem(blk, 2)
            gather(slot, 0, slot * EXPERT_ROWS, block_rows(blk)[1]).wait()

    def start_scatter(blk):
        @pl.when(blk < nbt)
        def _():
            segments(blk, lax.rem(blk, 2), lambda *a: scatter(*a).start())

    def wait_scatter(blk):
        @pl.when((blk >= 0) & (blk < nbt))
        def _():
            slot = lax.rem(blk, 2)
            scatter(slot, 0, slot * EXPERT_ROWS, block_rows(blk)[1]).wait()

    @pl.when(b == 0)
    def _():
        xbuf[...] = jnp.zeros_like(xbuf)
        start_gather(b)

    wait_gather(b)
    start_gather(b + 1)
    wait_scatter(b - 2)

    e = be_ref[b]
    live = b < nbt
    prev = be_ref[jnp.maximum(b - 1, 0)]
    fresh = (b == 0) | (e != prev)
    rows_left = jnp.where(live, block_rows(b)[1], 0)
    base = lax.rem(b, 2) * EXPERT_ROWS

    @pl.when(live & fresh)
    def _():
        wbf[0] = wg_ref[0].astype(BF16)
        wbf[1] = wu_ref[0].astype(BF16)
        wbf[2] = wd_ref[0].astype(BF16)

    for s in range(EXPERT_ROWS // EXPERT_SUB_ROWS):
        first = s * EXPERT_SUB_ROWS

        @pl.when(rows_left > first)
        def _():
            x = _load_rows(xbuf, base + first, EXPERT_SUB_ROWS)
            valid = lax.broadcasted_iota(I32, (EXPERT_SUB_ROWS, 1), 0) < rows_left - first
            x = jnp.where(valid, x, 0.0).astype(BF16)
            gt = _dot(x, wbf[0]) + bg_ref[0]
            up = _dot(x, wbf[1]) + bu_ref[0]
            gt = jnp.minimum(gt, SWIGLU_LIMIT)
            up = jnp.clip(up, -SWIGLU_LIMIT, SWIGLU_LIMIT)
            act = gt * _sigmoid(gt * SWIGLU_ALPHA) * (up + 1.0)
            _store_rows(ybuf, base + first, _dot(act.astype(BF16), wbf[2]) + bd_ref[0])

    start_scatter(b)

    @pl.when(b == n_blocks - 1)
    def _():
        wait_scatter(b - 1)
        wait_scatter(b)


def _expert_call(be, bstart, nbt, cum_flat, off_flat, tlo, thi, buckets, p, n_tiles, n_blocks):
    wspec = pl.BlockSpec((1, D_MODEL, D_MODEL), lambda b, be, *_: (be[b], 0, 0))
    bspec = pl.BlockSpec((1, 1, D_MODEL), lambda b, be, *_: (be[b], 0, 0))
    any_spec = pl.BlockSpec(memory_space=pl.ANY)
    return pl.pallas_call(
        functools.partial(_expert_kernel, n_tiles=n_tiles, n_blocks=n_blocks),
        grid_spec=pltpu.PrefetchScalarGridSpec(
            num_scalar_prefetch=7,
            grid=(n_blocks,),
            in_specs=[any_spec, wspec, wspec, wspec, bspec, bspec, bspec],
            out_specs=any_spec,
            scratch_shapes=[
                pltpu.VMEM((2 * EXPERT_ROWS * ROW_CHUNKS, LANES), F32),
                pltpu.VMEM((2 * EXPERT_ROWS * ROW_CHUNKS, LANES), F32),
                pltpu.VMEM((3, D_MODEL, D_MODEL), BF16),
                pltpu.SemaphoreType.DMA((2,)),
                pltpu.SemaphoreType.DMA((2,)),
            ],
        ),
        out_shape=jax.ShapeDtypeStruct(buckets.shape, F32),
        compiler_params=pltpu.CompilerParams(
            dimension_semantics=("arbitrary",), vmem_limit_bytes=VMEM_LIMIT),
        name="experts",
    )(be, bstart, nbt, cum_flat, off_flat, tlo, thi, buckets, p['w_gate'], p['w_up'], p['w_down'],
      p['b_gate'][:, None, :], p['b_up'][:, None, :], p['b_down'][:, None, :])


def _combine_kernel(x2_ref, yb_ref, lp_ref, tw_ref, gf_ref, y_ref, acc):
    def body(t, c):
        j = t * TOP_K
        row = tw_ref[0, 0, j] * yb_ref[_row_tile(lp_ref[0, 0, j]), :]
        for kk in range(1, TOP_K):
            row = row + tw_ref[0, 0, j + kk] * yb_ref[_row_tile(lp_ref[0, 0, j + kk]), :]
        acc[_row_tile(t), :] = row
        return c

    lax.fori_loop(0, TOKEN_TILE, body, 0, unroll=4)
    y_ref[...] = _rmsnorm(x2_ref[...] + _load_rows(acc, 0, TOKEN_TILE), gf_ref[...])


def _combine_call(x2, ybuckets, lp_tiles, tw_tiles, gf, tile_off):
    n_tok = x2.shape[0]
    n_tiles = n_tok // TOKEN_TILE
    return pl.pallas_call(
        _combine_kernel,
        grid=(n_tiles,),
        in_specs=[
            pl.BlockSpec((TOKEN_TILE, D_MODEL), lambda i: (i, 0)),
            pl.BlockSpec((TILE_SLOTS * ROW_CHUNKS, LANES), lambda i: (tile_off + i, 0)),
            pl.BlockSpec((1, 1, TILE_SLOTS), lambda i: (tile_off + i, 0, 0), memory_space=pltpu.SMEM),
            pl.BlockSpec((1, 1, TILE_SLOTS), lambda i: (tile_off + i, 0, 0), memory_space=pltpu.SMEM),
            pl.BlockSpec((1, D_MODEL), lambda i: (0, 0)),
        ],
        out_specs=pl.BlockSpec((TOKEN_TILE, D_MODEL), lambda i: (i, 0)),
        scratch_shapes=[pltpu.VMEM((TOKEN_TILE * ROW_CHUNKS, LANES), F32)],
        out_shape=jax.ShapeDtypeStruct((n_tok, D_MODEL), F32),
        compiler_params=pltpu.CompilerParams(
            dimension_semantics=("arbitrary",), vmem_limit_bytes=VMEM_LIMIT),
        name="combine",
    )(x2, ybuckets, lp_tiles, tw_tiles, gf)


def kernel(x_prompt, x_sample, state_ret, norm1_g, w_in, ln_v_g, ln_v_b, w_s, b_s, w_out, norm2_g, w_router, b_router, w_gate, b_gate, w_up, b_up, w_down, b_down, norm_f_g):
    bsz, seq, _ = x_prompt.shape
    dbsz, dseq, _ = x_sample.shape
    n_p = bsz * seq
    n_s = dbsz * dseq
    n_total = n_p + n_s
    past_len = 2048
    assert w_in.shape[0] == 1, "single layer"

    wr = w_router[0]
    wr_hi = wr.astype(BF16)
    wr_lo = (wr - wr_hi.astype(F32)).astype(BF16)
    p = dict(norm1_g=norm1_g, w_in=w_in[0].astype(BF16), ln_v_g=ln_v_g, ln_v_b=ln_v_b, w_s=w_s[0], b_s=b_s[0],
             w_out=w_out[0].astype(BF16), norm2_g=norm2_g, wr_hi=wr_hi, wr_lo=wr_lo, b_router=b_router,
             w_gate=w_gate[0], b_gate=b_gate[0], w_up=w_up[0], b_up=b_up[0], w_down=w_down[0], b_down=b_down[0])

    x2_p, h2_p, lp_p, tw_p, ct_p, of_p, st_p = _prompt_call(x_prompt, p)
    x2_s, h2_s, lp_s, tw_s, ct_s, of_s, st_s, vn_s = _decode_call(x_sample, state_ret[0], past_len, p)

    n_tiles = n_total // TOKEN_TILE
    lp = jnp.concatenate([lp_p, lp_s], axis=0).reshape(n_tiles, 1, TILE_SLOTS)
    tw = jnp.concatenate([tw_p, tw_s], axis=0).reshape(n_tiles, 1, TILE_SLOTS)
    counts = jnp.concatenate([ct_p, ct_s], axis=0).reshape(n_tiles, N_EXPERTS)
    offsets = jnp.concatenate([of_p, of_s], axis=0).reshape(n_tiles * N_EXPERTS).astype(I32)

    n_blocks = -(-(n_total * TOP_K + N_EXPERTS * (EXPERT_ROWS - 1)) // EXPERT_ROWS)
    cum, nb, bstart, be, nbt, tlo, thi = _plan_call(counts, n_blocks)

    buckets = _dispatch_call(h2_p, h2_s, lp)
    ybuckets = _expert_call(be.reshape(n_blocks), bstart.reshape(N_EXPERTS), nbt.reshape(1),
                            cum.reshape((n_tiles + 1) * N_EXPERTS), offsets, tlo.reshape(n_blocks),
                            thi.reshape(n_blocks), buckets, p, n_tiles, n_blocks)

    y_p = _combine_call(x2_p, ybuckets, lp, tw, norm_f_g[None, :], 0)
    y_s = _combine_call(x2_s, ybuckets, lp, tw, norm_f_g[None, :], n_p // TOKEN_TILE)

    return (y_p.reshape(bsz, seq, D_MODEL), y_s.reshape(dbsz, dseq, D_MODEL),
            st_p[None], st_s[None], vn_s.reshape(1, dbsz, dseq, GM_WIDTH))
```

```python
import functools

import numpy as np
import jax
import jax.numpy as jnp
from jax import lax
from jax.experimental import pallas as pl
from jax.experimental.pallas import tpu as pltpu

F32 = jnp.float32
BF16 = jnp.bfloat16
I32 = jnp.int32

D_MODEL = 1024
RET_WIDTH = 512
N_HEADS = 4
HEAD_DIM = 128
GM_WIDTH = 512
GM_CHUNK = 128
RET_CHUNK = 64
N_EXPERTS = 32
TOP_K = 4
SWIGLU_LIMIT = 7.0
SWIGLU_ALPHA = 1.702
ROPE_BASE = 10000.0
EPS = 1e-6

TOKEN_TILE = 512
PROMPT_RET_BLOCK = 256
EXPERT_ROWS = 512
EXPERT_SUB_ROWS = 256
TILE_SLOTS = TOKEN_TILE * TOP_K
VMEM_LIMIT = 56 * 1024 * 1024
LANES = 128
ROW_CHUNKS = D_MODEL // LANES
SPLIT = 16.0
_LOG_GAMMA = [float(np.log1p(-np.float32(2.0) ** np.float32(-5.0 - hd)).astype(np.float32)) for hd in range(N_HEADS)]


def _load_rows(ref, first_row, n_rows):
    base = first_row * ROW_CHUNKS
    return jnp.concatenate(
        [ref[pl.ds(base + c, n_rows, stride=ROW_CHUNKS), :] for c in range(ROW_CHUNKS)], axis=1)


def _store_rows(ref, first_row, val):
    base = first_row * ROW_CHUNKS
    for c in range(ROW_CHUNKS):
        ref[pl.ds(base + c, val.shape[0], stride=ROW_CHUNKS), :] = val[:, c * LANES:(c + 1) * LANES]


def _row_tile(row):
    return pl.ds(pl.multiple_of(row * ROW_CHUNKS, ROW_CHUNKS), ROW_CHUNKS)


def _rmsnorm(x, g):
    ms = jnp.mean(x * x, axis=-1, keepdims=True)
    return x * lax.rsqrt(ms + EPS) * g


def _gelu(x):
    c = np.float32(np.sqrt(2.0 / np.pi))
    return x * (0.5 * (1.0 + jnp.tanh(c * (x + 0.044715 * (x * x * x)))))


def _sigmoid(x):
    return 1.0 / (1.0 + jnp.exp(-x))


def _dot(a, b):
    return jnp.dot(a, b, preferred_element_type=F32)


def _dot_nt(a, b):
    return lax.dot_general(a, b, (((1,), (1,)), ((), ())), preferred_element_type=F32)


def _dot_tn(a, b):
    return lax.dot_general(a, b, (((0,), (0,)), ((), ())), preferred_element_type=F32)


def _exact_count_dot(a, b, *, counts_on_left):
    cnt = a if counts_on_left else b
    hi = jnp.floor(cnt * (1.0 / SPLIT))
    lo = cnt - SPLIT * hi
    if counts_on_left:
        return SPLIT * _dot(hi.astype(BF16), b) + _dot(lo.astype(BF16), b)
    return SPLIT * _dot(a, hi.astype(BF16)) + _dot(a, lo.astype(BF16))


def _rotary(xh, cos2, sin2):
    return xh * cos2 + pltpu.roll(xh, HEAD_DIM // 2, axis=1) * sin2


def _gated_head_norm(o, gate):
    mu = jnp.mean(o, axis=-1, keepdims=True)
    oc = o - mu
    var = jnp.mean(oc * oc, axis=-1, keepdims=True)
    return ((gate * _sigmoid(gate)) * (oc * lax.rsqrt(var + EPS))).astype(BF16)


def _gmlp_inputs(u, vg, lng, lnb):
    uf = _gelu(u)
    vf = _gelu(vg)
    mu = jnp.mean(vf, axis=-1, keepdims=True)
    vc = vf - mu
    var = jnp.mean(vc * vc, axis=-1, keepdims=True)
    return uf, vc * lax.rsqrt(var + EPS) * lng + lnb


def _route_and_store(x, mix_scr, wout_ref, g2_ref, wrh_ref, wrl_ref, br_ref, tri_ref, upper_ref,
                     x2_ref, h2_ref, lp_ref, tw_ref, ct_ref, of_ref):
    tile = x.shape[0]
    x2 = x + _dot(mix_scr[...], wout_ref[...])
    x2_ref[...] = x2
    h2 = _rmsnorm(x2, g2_ref[...])
    _store_rows(h2_ref, 0, h2)

    h_hi = h2.astype(BF16)
    h_lo = (h2 - h_hi.astype(F32)).astype(BF16)
    logits = (_dot(h_hi, wrh_ref[...]) + _dot(h_lo, wrh_ref[...]) + _dot(h_hi, wrl_ref[...])) + br_ref[...]

    iota_e = lax.broadcasted_iota(I32, (tile, N_EXPERTS), 1)
    iota_ef = iota_e.astype(F32)
    iota_k = lax.broadcasted_iota(I32, (tile, TOP_K), 1)
    lg = logits
    vals = []
    idxs = []
    for _ in range(TOP_K):
        m = jnp.max(lg, axis=-1, keepdims=True)
        idx = jnp.min(jnp.where(lg == m, iota_ef, float(N_EXPERTS)), axis=-1, keepdims=True).astype(I32)
        vals.append(m)
        idxs.append(idx)
        lg = jnp.where(iota_e == idx, -jnp.inf, lg)
    exps = [jnp.exp(vk - vals[0]) for vk in vals]
    denom = exps[0] + exps[1] + exps[2] + exps[3]

    onehot = jnp.zeros((tile, N_EXPERTS), F32)
    for idx in idxs:
        onehot = onehot + (iota_e == idx).astype(F32)
    rank = _dot(tri_ref[...], onehot.astype(BF16))
    count = jnp.sum(onehot, axis=0, keepdims=True)
    offset = _exact_count_dot(jnp.broadcast_to(count, (8, N_EXPERTS)), upper_ref[...], counts_on_left=True)[0:1]
    where_to = rank + offset
    lp = jnp.zeros((tile, TOP_K), I32)
    tw = jnp.zeros((tile, TOP_K), F32)
    for kk in range(TOP_K):
        pos = jnp.sum(jnp.where(iota_e == idxs[kk], where_to, 0.0), axis=-1, keepdims=True).astype(I32)
        lp = jnp.where(iota_k == kk, pos, lp)
        tw = jnp.where(iota_k == kk, exps[kk] / denom, tw)
    lp_ref[...] = lp
    tw_ref[...] = tw
    ct_ref[0] = count
    of_ref[0] = offset


def _prompt_kernel(x_ref, g1_ref, win_ref, cq_ref, sq_ref, ck_ref, sk_ref, dmat_ref, qdec_ref, kdec_ref,
                   sdec_ref, lng_ref, lnb_ref, ws_ref, bst_ref, wout_ref, g2_ref, wrh_ref, wrl_ref, br_ref,
                   tri_ref, upper_ref,
                   x2_ref, h2_ref, lp_ref, tw_ref, ct_ref, of_ref, st_ref,
                   s_scr, mix_scr, *, ret_block):
    tile = TOKEN_TILE
    t = pl.program_id(1)

    @pl.when(t == 0)
    def _():
        s_scr[...] = jnp.zeros_like(s_scr)

    x = x_ref[0]
    h = _rmsnorm(x, g1_ref[...]).astype(BF16)

    def proj(i):
        return _dot(h, win_ref[:, i * RET_WIDTH:(i + 1) * RET_WIDTH])

    q = proj(0)
    k = proj(1)
    v = proj(2)
    gate = proj(3)
    u = proj(4)
    vg = proj(5)
    cq = cq_ref[...]
    sq = sq_ref[...]
    ck = ck_ref[...]
    sk = sk_ref[...]

    for hd in range(N_HEADS):
        cols = slice(hd * HEAD_DIM, (hd + 1) * HEAD_DIM)
        qr = _rotary(q[:, cols], cq, sq)
        kr = _rotary(k[:, cols], ck, sk)
        vh = v[:, cols].astype(BF16)
        dm = dmat_ref[hd]
        qd = qdec_ref[hd]
        kd = kdec_ref[hd]
        sd = sdec_ref[hd]
        for c in range(tile // ret_block):
            rows = slice(c * ret_block, (c + 1) * ret_block)
            qb = qr[rows].astype(BF16)
            kb = kr[rows]
            vb = vh[rows]
            state = s_scr[hd]
            scores = _dot_nt(qb, kb.astype(BF16)) * dm
            o = _dot(scores.astype(BF16), vb) + qd * _dot(qb, state.astype(BF16))
            s_scr[hd] = sd * state + _dot_tn((kb * kd).astype(BF16), vb)
            mix_scr[rows, cols] = _gated_head_norm(o, gate[rows, cols])

    uf, vn = _gmlp_inputs(u, vg, lng_ref[...], lnb_ref[...])
    vnb = vn.astype(BF16)
    r_i = lax.broadcasted_iota(I32, (GM_CHUNK, GM_CHUNK), 0)
    c_i = lax.broadcasted_iota(I32, (GM_CHUNK, GM_CHUNK), 1)
    for hd in range(N_HEADS):
        cols = slice(hd * HEAD_DIM, (hd + 1) * HEAD_DIM)
        w_tril = jnp.where(r_i >= c_i, ws_ref[hd], 0.0).astype(BF16)
        bias = bst_ref[:, hd:hd + 1]
        for c in range(tile // GM_CHUNK):
            rows = slice(c * GM_CHUNK, (c + 1) * GM_CHUNK)
            mixed = _dot(w_tril, vnb[rows, cols]) + bias
            mix_scr[rows, RET_WIDTH + hd * HEAD_DIM:RET_WIDTH + (hd + 1) * HEAD_DIM] = (
                uf[rows, cols] * mixed).astype(BF16)

    _route_and_store(x, mix_scr, wout_ref, g2_ref, wrh_ref, wrl_ref, br_ref, tri_ref, upper_ref,
                     x2_ref, h2_ref, lp_ref, tw_ref, ct_ref, of_ref)

    @pl.when(t == pl.num_programs(1) - 1)
    def _():
        st_ref[0] = s_scr[...]


def _decode_kernel(x_ref, s0_ref, g1_ref, win_ref, cq_ref, sq_ref, ck_ref, sk_ref, qdec_ref, kdec_ref,
                   sdec_ref, lng_ref, lnb_ref, ws_ref, bst_ref, wout_ref, g2_ref, wrh_ref, wrl_ref, br_ref,
                   tri_ref, upper_ref,
                   x2_ref, h2_ref, lp_ref, tw_ref, ct_ref, of_ref, st_ref, vn_ref,
                   mix_scr, q_scr, k_scr, v_scr, oc_scr, st_stage, st_sem, *, n_streams, frames):
    tile = n_streams * frames
    x = x_ref[...]
    h = _rmsnorm(x, g1_ref[...]).astype(BF16)

    def proj(i):
        return _dot(h, win_ref[:, i * RET_WIDTH:(i + 1) * RET_WIDTH])

    q = proj(0)
    k = proj(1)
    v = proj(2)
    gate = proj(3)
    u = proj(4)
    vg = proj(5)
    cq = cq_ref[...]
    sq = sq_ref[...]
    ck = ck_ref[...]
    sk = sk_ref[...]

    r_i = lax.broadcasted_iota(I32, (tile, tile), 0)
    c_i = lax.broadcasted_iota(I32, (tile, tile), 1)
    shift = frames.bit_length() - 1
    r_frame = lax.bitwise_and(r_i, frames - 1)
    c_frame = lax.bitwise_and(c_i, frames - 1)
    keep = (lax.shift_right_logical(r_i, shift) == lax.shift_right_logical(c_i, shift)) & (r_frame >= c_frame)
    frame_gap = (r_frame - c_frame).astype(F32)

    o_in = []
    for hd in range(N_HEADS):
        cols = slice(hd * HEAD_DIM, (hd + 1) * HEAD_DIM)
        qr = _rotary(q[:, cols], cq, sq)
        kr = _rotary(k[:, cols], ck, sk)
        vh = v[:, cols].astype(BF16)
        qb = qr.astype(BF16)
        decay = jnp.where(keep, jnp.exp(frame_gap * _LOG_GAMMA[hd]), 0.0)
        scores = _dot_nt(qb, kr.astype(BF16)) * decay
        o_in.append(_dot(scores.astype(BF16), vh))
        q_scr[hd] = qb
        k_scr[hd] = (kr * kdec_ref[hd]).astype(BF16)
        v_scr[hd] = vh

    def state_copy(b):
        slot = lax.rem(b, 2)
        return pltpu.make_async_copy(st_stage.at[slot], st_ref.at[b], st_sem.at[slot])

    def stream_body(b, carry):
        rows = pl.ds(pl.multiple_of(b * frames, frames), frames)
        slot = lax.rem(b, 2)

        @pl.when(b >= 2)
        def _():
            state_copy(b - 2).wait()

        for hd in range(N_HEADS):
            state = s0_ref[b, hd]
            oc_scr[hd, rows, :] = _dot(q_scr[hd, rows, :], state.astype(BF16))
            st_stage[slot, hd] = sdec_ref[hd] * state + _dot_tn(k_scr[hd, rows, :], v_scr[hd, rows, :])
        state_copy(b).start()
        return carry

    lax.fori_loop(0, n_streams, stream_body, 0)
    state_copy(n_streams - 2).wait()
    state_copy(n_streams - 1).wait()

    for hd in range(N_HEADS):
        cols = slice(hd * HEAD_DIM, (hd + 1) * HEAD_DIM)
        o = o_in[hd] + qdec_ref[hd] * oc_scr[hd]
        mix_scr[:, cols] = _gated_head_norm(o, gate[:, cols])

    uf, vn = _gmlp_inputs(u, vg, lng_ref[...], lnb_ref[...])
    vn_ref[...] = vn
    vnb = vn.astype(BF16)
    sel = (lax.bitwise_and(lax.broadcasted_iota(I32, (tile, frames), 0), frames - 1)
           == lax.broadcasted_iota(I32, (tile, frames), 1)).astype(BF16)
    for hd in range(N_HEADS):
        cols = slice(hd * HEAD_DIM, (hd + 1) * HEAD_DIM)
        w_rows = _dot(sel, ws_ref[hd].astype(BF16)).astype(BF16)
        w_blk = jnp.where(keep, _dot_nt(w_rows, sel), 0.0).astype(BF16)
        mixed = _dot(w_blk, vnb[:, cols]) + bst_ref[:, hd:hd + 1]
        mix_scr[:, RET_WIDTH + hd * HEAD_DIM:RET_WIDTH + (hd + 1) * HEAD_DIM] = (uf[:, cols] * mixed).astype(BF16)

    _route_and_store(x, mix_scr, wout_ref, g2_ref, wrh_ref, wrl_ref, br_ref, tri_ref, upper_ref,
                     x2_ref, h2_ref, lp_ref, tw_ref, ct_ref, of_ref)


def _rope_tables(pos):
    half = HEAD_DIM // 2
    inv = ROPE_BASE ** (-jnp.arange(half, dtype=F32) / half)
    ang = pos.astype(F32)[:, None] * inv[None, :]
    cos = jnp.cos(ang)
    sin = jnp.sin(ang)
    cq = jnp.concatenate([cos, cos], axis=-1)
    sq = jnp.concatenate([-sin, sin], axis=-1)
    scale = HEAD_DIM ** -0.5
    return cq, sq, cq * scale, sq * scale


def _decay_tables(block):
    log_g = jnp.log1p(-(2.0 ** (-5.0 - jnp.arange(N_HEADS, dtype=F32))))
    idx = jnp.arange(block, dtype=F32)
    diff = idx[:, None] - idx[None, :]
    dmat = jnp.where(diff[None] >= 0, jnp.exp(jnp.maximum(diff, 0.0)[None] * log_g[:, None, None]), 0.0)
    q_dec = jnp.exp((idx + 1.0)[None, :] * log_g[:, None])
    k_dec = jnp.exp((block - 1.0 - idx)[None, :] * log_g[:, None])
    s_dec = jnp.exp(block * log_g)
    bc = lambda a: jnp.broadcast_to(a[:, :, None], (N_HEADS, block, HEAD_DIM))
    return dmat, bc(q_dec), bc(k_dec), jnp.broadcast_to(s_dec[:, None, None], (N_HEADS, 1, HEAD_DIM))


def _routing_constants(tile):
    tri = (jnp.arange(tile)[:, None] > jnp.arange(tile)[None, :]).astype(BF16)
    upper = (jnp.arange(N_EXPERTS)[:, None] < jnp.arange(N_EXPERTS)[None, :]).astype(BF16)
    return tri, upper


def _front_out(n_tok, n_tiles):
    return [
        jax.ShapeDtypeStruct((n_tok, D_MODEL), F32),
        jax.ShapeDtypeStruct((n_tok * ROW_CHUNKS, LANES), F32),
        jax.ShapeDtypeStruct((n_tok, TOP_K), I32),
        jax.ShapeDtypeStruct((n_tok, TOP_K), F32),
        jax.ShapeDtypeStruct((n_tiles, 1, N_EXPERTS), F32),
        jax.ShapeDtypeStruct((n_tiles, 1, N_EXPERTS), F32),
    ]


def _prompt_call(x, p):
    bsz, seq, _ = x.shape
    tile = TOKEN_TILE
    n_t = seq // tile
    cq, sq, ck, sk = _rope_tables(jnp.arange(seq, dtype=I32))
    dmat, qdec, kdec, sdec = _decay_tables(PROMPT_RET_BLOCK)
    tri, upper = _routing_constants(tile)
    ws = p['w_s'][:, :GM_CHUNK, :GM_CHUNK]
    bst = p['b_s'][:, :GM_CHUNK].T

    const = lambda shape: pl.BlockSpec(shape, lambda b, t: (0,) * len(shape))
    pos = lambda: pl.BlockSpec((tile, HEAD_DIM), lambda b, t: (t, 0))
    in_specs = [
        pl.BlockSpec((1, tile, D_MODEL), lambda b, t: (b, t, 0)),
        const((1, D_MODEL)), const((D_MODEL, 6 * RET_WIDTH)),
        pos(), pos(), pos(), pos(),
        const((N_HEADS, PROMPT_RET_BLOCK, PROMPT_RET_BLOCK)),
        const((N_HEADS, PROMPT_RET_BLOCK, HEAD_DIM)), const((N_HEADS, PROMPT_RET_BLOCK, HEAD_DIM)),
        const((N_HEADS, 1, HEAD_DIM)),
        const((1, GM_WIDTH)), const((1, GM_WIDTH)),
        const((N_HEADS, GM_CHUNK, GM_CHUNK)), const((GM_CHUNK, N_HEADS)),
        const((D_MODEL, D_MODEL)), const((1, D_MODEL)),
        const((D_MODEL, N_EXPERTS)), const((D_MODEL, N_EXPERTS)), const((1, N_EXPERTS)),
        const((tile, tile)), const((N_EXPERTS, N_EXPERTS)),
    ]
    row = lambda b, t: (b * n_t + t, 0)
    row3 = lambda b, t: (b * n_t + t, 0, 0)
    out_specs = [
        pl.BlockSpec((tile, D_MODEL), row),
        pl.BlockSpec((tile * ROW_CHUNKS, LANES), row),
        pl.BlockSpec((tile, TOP_K), row),
        pl.BlockSpec((tile, TOP_K), row),
        pl.BlockSpec((1, 1, N_EXPERTS), row3),
        pl.BlockSpec((1, 1, N_EXPERTS), row3),
        pl.BlockSpec((1, N_HEADS, HEAD_DIM, HEAD_DIM), lambda b, t: (b, 0, 0, 0)),
    ]
    out_shape = _front_out(bsz * seq, bsz * n_t) + [jax.ShapeDtypeStruct((bsz, N_HEADS, HEAD_DIM, HEAD_DIM), F32)]
    return pl.pallas_call(
        functools.partial(_prompt_kernel, ret_block=PROMPT_RET_BLOCK),
        grid=(bsz, n_t),
        in_specs=in_specs,
        out_specs=out_specs,
        out_shape=out_shape,
        scratch_shapes=[
            pltpu.VMEM((N_HEADS, HEAD_DIM, HEAD_DIM), F32),
            pltpu.VMEM((tile, D_MODEL), BF16),
        ],
        compiler_params=pltpu.CompilerParams(
            dimension_semantics=("arbitrary", "arbitrary"), vmem_limit_bytes=VMEM_LIMIT),
        name="front_prompt",
    )(x, p['norm1_g'], p['w_in'], cq, sq, ck, sk, dmat, qdec, kdec, sdec, p['ln_v_g'], p['ln_v_b'], ws, bst,
      p['w_out'], p['norm2_g'], p['wr_hi'], p['wr_lo'], p['b_router'], tri, upper)


def _decode_call(x, state, past_len, p):
    n_streams, frames, _ = x.shape
    tile = n_streams * frames
    assert tile == TOKEN_TILE and frames <= RET_CHUNK and frames & (frames - 1) == 0
    rope = _rope_tables(past_len + jnp.arange(frames, dtype=I32))
    cq, sq, ck, sk = [jnp.tile(a, (n_streams, 1)) for a in rope]
    _, qdec, kdec, sdec = _decay_tables(frames)
    qdec = jnp.tile(qdec, (1, n_streams, 1))
    kdec = jnp.tile(kdec, (1, n_streams, 1))
    tri, upper = _routing_constants(tile)
    ws = p['w_s'][:, :frames, :frames]
    bst = jnp.tile(p['b_s'][:, :frames].T, (n_streams, 1))

    def whole(a):
        return pl.BlockSpec(a.shape, lambda i, n=a.ndim: (0,) * n, pipeline_mode=pl.Buffered(1))

    args = (x.reshape(tile, D_MODEL), state, p['norm1_g'], p['w_in'], cq, sq, ck, sk, qdec, kdec, sdec,
            p['ln_v_g'], p['ln_v_b'], ws, bst, p['w_out'], p['norm2_g'], p['wr_hi'], p['wr_lo'], p['b_router'],
            tri, upper)
    out_shape = _front_out(tile, 1) + [
        jax.ShapeDtypeStruct(state.shape, F32),
        jax.ShapeDtypeStruct((tile, GM_WIDTH), F32),
    ]
    out_specs = [pl.BlockSpec(s.shape, lambda i, n=len(s.shape): (0,) * n) for s in out_shape]
    out_specs[6] = pl.BlockSpec(memory_space=pl.ANY)
    return pl.pallas_call(
        functools.partial(_decode_kernel, n_streams=n_streams, frames=frames),
        grid=(1,),
        in_specs=[whole(a) for a in args],
        out_specs=out_specs,
        out_shape=out_shape,
        scratch_shapes=[
            pltpu.VMEM((tile, D_MODEL), BF16),
            pltpu.VMEM((N_HEADS, tile, HEAD_DIM), BF16),
            pltpu.VMEM((N_HEADS, tile, HEAD_DIM), BF16),
            pltpu.VMEM((N_HEADS, tile, HEAD_DIM), BF16),
            pltpu.VMEM((N_HEADS, tile, HEAD_DIM), F32),
            pltpu.VMEM((2, N_HEADS, HEAD_DIM, HEAD_DIM), F32),
            pltpu.SemaphoreType.DMA((2,)),
        ],
        compiler_params=pltpu.CompilerParams(
            dimension_semantics=("arbitrary",), vmem_limit_bytes=VMEM_LIMIT),
        name="front_decode",
    )(*args)


def _plan_kernel(ct_ref, lower_ref, upper_ref, cum_ref, nb_ref, bs_ref, be_ref, nbt_ref, tlo_ref, thi_ref,
                 *, n_blocks):
    counts = ct_ref[...]
    cum = _exact_count_dot(lower_ref[...], counts, counts_on_left=False)
    cum_ref[...] = cum.astype(I32)
    n_tiles = counts.shape[0]
    total = cum[n_tiles:n_tiles + 1]
    nb = jnp.floor((total + (EXPERT_ROWS - 1)) * (1.0 / EXPERT_ROWS))
    bstart = _dot(jnp.broadcast_to(nb, (8, N_EXPERTS)).astype(BF16), upper_ref[...])[0:1]
    bend = bstart + nb
    nb_ref[...] = nb.astype(I32)
    bs_ref[...] = bstart.astype(I32)
    blk = lax.broadcasted_iota(I32, (n_blocks, N_EXPERTS), 0).astype(F32)
    be = jnp.minimum(jnp.sum((bend <= blk).astype(F32), axis=-1, keepdims=True), N_EXPERTS - 1.0)
    be_ref[...] = be.astype(I32)
    nbt_ref[...] = jnp.sum(nb, axis=-1, keepdims=True).astype(I32)
    mine = lax.broadcasted_iota(I32, (n_blocks, N_EXPERTS), 1).astype(F32) == be
    pick = lambda row: jnp.sum(jnp.where(mine, row, 0.0), axis=-1, keepdims=True)
    r0 = (blk[:, 0:1] - pick(bstart)) * EXPERT_ROWS
    tlo = jnp.zeros((n_blocks, 1), F32)
    thi = jnp.zeros((n_blocks, 1), F32)
    seg_lo = pick(cum[0:1])
    for i in range(n_tiles):
        seg_hi = pick(cum[i + 1:i + 2])
        tlo = tlo + (seg_hi <= r0).astype(F32)
        thi = thi + (seg_lo < r0 + EXPERT_ROWS).astype(F32)
        seg_lo = seg_hi
    tlo_ref[...] = tlo.astype(I32)
    thi_ref[...] = thi.astype(I32)


def _plan_call(counts, n_blocks):
    n_tiles = counts.shape[0]
    lower = (jnp.arange(n_tiles + 1)[:, None] > jnp.arange(n_tiles)[None, :]).astype(BF16)
    upper = (jnp.arange(N_EXPERTS)[:, None] < jnp.arange(N_EXPERTS)[None, :]).astype(BF16)
    return pl.pallas_call(
        functools.partial(_plan_kernel, n_blocks=n_blocks),
        out_shape=[
            jax.ShapeDtypeStruct((n_tiles + 1, N_EXPERTS), I32),
            jax.ShapeDtypeStruct((1, N_EXPERTS), I32),
            jax.ShapeDtypeStruct((1, N_EXPERTS), I32),
            jax.ShapeDtypeStruct((n_blocks, 1), I32),
            jax.ShapeDtypeStruct((1, 1), I32),
            jax.ShapeDtypeStruct((n_blocks, 1), I32),
            jax.ShapeDtypeStruct((n_blocks, 1), I32),
        ],
        name="plan",
    )(counts, lower, upper)


def _dispatch_kernel(h2p_ref, h2s_ref, lp_ref, out_ref, *, n_prompt_tiles):
    i = pl.program_id(0)

    def permute(src_ref):
        def body(t, c):
            row = src_ref[_row_tile(t), :]
            for kk in range(TOP_K):
                out_ref[_row_tile(lp_ref[0, 0, t * TOP_K + kk]), :] = row
            return c

        lax.fori_loop(0, TOKEN_TILE, body, 0, unroll=8)

    @pl.when(i < n_prompt_tiles)
    def _():
        permute(h2p_ref)

    @pl.when(i >= n_prompt_tiles)
    def _():
        permute(h2s_ref)


def _dispatch_call(h2_p, h2_s, lp_tiles):
    rows = TOKEN_TILE * ROW_CHUNKS
    n_p_tiles = h2_p.shape[0] // rows
    n_s_tiles = h2_s.shape[0] // rows
    n_tiles = n_p_tiles + n_s_tiles
    return pl.pallas_call(
        functools.partial(_dispatch_kernel, n_prompt_tiles=n_p_tiles),
        grid=(n_tiles,),
        in_specs=[
            pl.BlockSpec((rows, LANES), lambda i: (jnp.minimum(i, n_p_tiles - 1), 0)),
            pl.BlockSpec((rows, LANES), lambda i: (jnp.maximum(i - n_p_tiles, 0), 0)),
            pl.BlockSpec((1, 1, TILE_SLOTS), lambda i: (i, 0, 0), memory_space=pltpu.SMEM),
        ],
        out_specs=pl.BlockSpec((TILE_SLOTS * ROW_CHUNKS, LANES), lambda i: (i, 0)),
        out_shape=jax.ShapeDtypeStruct((n_tiles * TILE_SLOTS * ROW_CHUNKS, LANES), F32),
        compiler_params=pltpu.CompilerParams(
            dimension_semantics=("arbitrary",), vmem_limit_bytes=VMEM_LIMIT),
        name="dispatch",
    )(h2_p, h2_s, lp_tiles)


def _expert_kernel(be_ref, bs_ref, nb_ref, nbt_ref, cum_ref, off_ref, tlo_ref, thi_ref,
                   xb_ref, wg_ref, wu_ref, wd_ref, bg_ref, bu_ref, bd_ref,
                   yb_ref, xbuf, ybuf, wbf, in_sem, out_sem, *, n_tiles):
    b = pl.program_id(0)
    nbt = nbt_ref[0]

    def block_rows(blk):
        e = be_ref[blk]
        r0 = (blk - bs_ref[e]) * EXPERT_ROWS
        return r0, jnp.minimum(cum_ref[n_tiles * N_EXPERTS + e] - r0, EXPERT_ROWS)

    def segments(blk, slot, fn):
        e = be_ref[blk]
        r0 = (blk - bs_ref[e]) * EXPERT_ROWS

        def body(i, c):
            seg_lo = cum_ref[i * N_EXPERTS + e]
            seg_hi = cum_ref[(i + 1) * N_EXPERTS + e]
            lo = jnp.maximum(seg_lo, r0)
            hi = jnp.minimum(seg_hi, r0 + EXPERT_ROWS)

            @pl.when(hi > lo)
            def _():
                bucket_row = i * TILE_SLOTS + off_ref[i * N_EXPERTS + e] + (lo - seg_lo)
                fn(slot, bucket_row, slot * EXPERT_ROWS + (lo - r0), hi - lo)
            return c

        lax.fori_loop(tlo_ref[blk], thi_ref[blk], body, 0)

    def words(row, n):
        return pl.ds(pl.multiple_of(row * ROW_CHUNKS, ROW_CHUNKS), n * ROW_CHUNKS)

    def gather(slot, bucket_row, buffer_row, n):
        return pltpu.make_async_copy(
            xb_ref.at[words(bucket_row, n)], xbuf.at[words(buffer_row, n)], in_sem.at[slot])

    def scatter(slot, bucket_row, buffer_row, n):
        return pltpu.make_async_copy(
            ybuf.at[words(buffer_row, n)], yb_ref.at[words(bucket_row, n)], out_sem.at[slot])

    def start_gather(blk):
        @pl.when(blk < nbt)
        def _():
            segments(blk, lax.rem(blk, 2), lambda *a: gather(*a).start())

    def wait_gather(blk):
        @pl.when(blk < nbt)
        def _():
            slot = lax.rem(blk, 2)
            gather(slot, 0, slot * EXPERT_ROWS, block_rows(blk)[1]).wait()

    def start_scatter(blk):
        @pl.when(blk < nbt)
        def _():
            segments(blk, lax.rem(blk, 2), lambda *a: scatter(*a).start())

    def wait_scatter(blk):
        @pl.when((blk >= 0) & (blk < nbt))
        def _():
            slot = lax.rem(blk, 2)
            scatter(slot, 0, slot * EXPERT_ROWS, block_rows(blk)[1]).wait()

    def mlp(base, n_rows, rows_left):
        x = _load_rows(xbuf, base, n_rows)
        valid = lax.broadcasted_iota(I32, (n_rows, 1), 0) < rows_left
        x = jnp.where(valid, x, 0.0).astype(BF16)
        gt = _dot(x, wbf[0]) + bg_ref[0]
        up = _dot(x, wbf[1]) + bu_ref[0]
        gt = jnp.minimum(gt, SWIGLU_LIMIT)
        up = jnp.clip(up, -SWIGLU_LIMIT, SWIGLU_LIMIT)
        act = gt * _sigmoid(gt * SWIGLU_ALPHA) * (up + 1.0)
        _store_rows(ybuf, base, _dot(act.astype(BF16), wbf[2]) + bd_ref[0])

    def block_body(blk, carry):
        wait_gather(blk)
        start_gather(blk + 1)
        wait_scatter(blk - 2)
        rows_left = block_rows(blk)[1]
        base = lax.rem(blk, 2) * EXPERT_ROWS

        @pl.when(rows_left > EXPERT_SUB_ROWS)
        def _():
            mlp(base, EXPERT_ROWS, rows_left)

        @pl.when(rows_left <= EXPERT_SUB_ROWS)
        def _():
            mlp(base, EXPERT_SUB_ROWS, rows_left)

        start_scatter(blk)
        return carry

    first_block = bs_ref[b]
    n_mine = nb_ref[b]

    @pl.when(b == 0)
    def _():
        xbuf[...] = jnp.zeros_like(xbuf)
        start_gather(0)

    @pl.when(n_mine > 0)
    def _():
        wbf[0] = wg_ref[0].astype(BF16)
        wbf[1] = wu_ref[0].astype(BF16)
        wbf[2] = wd_ref[0].astype(BF16)

    lax.fori_loop(first_block, first_block + n_mine, block_body, 0)

    @pl.when(b == N_EXPERTS - 1)
    def _():
        wait_scatter(nbt - 2)
        wait_scatter(nbt - 1)


def _expert_call(be, bstart, nb, nbt, cum_flat, off_flat, tlo, thi, buckets, p, n_tiles):
    wspec = pl.BlockSpec((1, D_MODEL, D_MODEL), lambda e, *_: (e, 0, 0))
    bspec = pl.BlockSpec((1, 1, D_MODEL), lambda e, *_: (e, 0, 0))
    any_spec = pl.BlockSpec(memory_space=pl.ANY)
    return pl.pallas_call(
        functools.partial(_expert_kernel, n_tiles=n_tiles),
        grid_spec=pltpu.PrefetchScalarGridSpec(
            num_scalar_prefetch=8,
            grid=(N_EXPERTS,),
            in_specs=[any_spec, wspec, wspec, wspec, bspec, bspec, bspec],
            out_specs=any_spec,
            scratch_shapes=[
                pltpu.VMEM((2 * EXPERT_ROWS * ROW_CHUNKS, LANES), F32),
                pltpu.VMEM((2 * EXPERT_ROWS * ROW_CHUNKS, LANES), F32),
                pltpu.VMEM((3, D_MODEL, D_MODEL), BF16),
                pltpu.SemaphoreType.DMA((2,)),
                pltpu.SemaphoreType.DMA((2,)),
            ],
        ),
        out_shape=jax.ShapeDtypeStruct(buckets.shape, F32),
        compiler_params=pltpu.CompilerParams(
            dimension_semantics=("arbitrary",), vmem_limit_bytes=VMEM_LIMIT),
        name="experts",
    )(be, bstart, nb, nbt, cum_flat, off_flat, tlo, thi, buckets, p['w_gate'], p['w_up'], p['w_down'],
      p['b_gate'][:, None, :], p['b_up'][:, None, :], p['b_down'][:, None, :])


def _combine_kernel(x2_ref, yb_ref, lp_ref, tw_ref, gf_ref, y_ref, acc):
    def body(t, c):
        j = t * TOP_K
        row = tw_ref[0, 0, j] * yb_ref[_row_tile(lp_ref[0, 0, j]), :]
        for kk in range(1, TOP_K):
            row = row + tw_ref[0, 0, j + kk] * yb_ref[_row_tile(lp_ref[0, 0, j + kk]), :]
        acc[_row_tile(t), :] = row
        return c

    lax.fori_loop(0, TOKEN_TILE, body, 0, unroll=4)
    y_ref[...] = _rmsnorm(x2_ref[...] + _load_rows(acc, 0, TOKEN_TILE), gf_ref[...])


def _combine_call(x2, ybuckets, lp_tiles, tw_tiles, gf, tile_off):
    n_tok = x2.shape[0]
    n_tiles = n_tok // TOKEN_TILE
    return pl.pallas_call(
        _combine_kernel,
        grid=(n_tiles,),
        in_specs=[
            pl.BlockSpec((TOKEN_TILE, D_MODEL), lambda i: (i, 0)),
            pl.BlockSpec((TILE_SLOTS * ROW_CHUNKS, LANES), lambda i: (tile_off + i, 0)),
            pl.BlockSpec((1, 1, TILE_SLOTS), lambda i: (tile_off + i, 0, 0), memory_space=pltpu.SMEM),
            pl.BlockSpec((1, 1, TILE_SLOTS), lambda i: (tile_off + i, 0, 0), memory_space=pltpu.SMEM),
            pl.BlockSpec((1, D_MODEL), lambda i: (0, 0)),
        ],
        out_specs=pl.BlockSpec((TOKEN_TILE, D_MODEL), lambda i: (i, 0)),
        scratch_shapes=[pltpu.VMEM((TOKEN_TILE * ROW_CHUNKS, LANES), F32)],
        out_shape=jax.ShapeDtypeStruct((n_tok, D_MODEL), F32),
        compiler_params=pltpu.CompilerParams(
            dimension_semantics=("arbitrary",), vmem_limit_bytes=VMEM_LIMIT),
        name="combine",
    )(x2, ybuckets, lp_tiles, tw_tiles, gf)


def kernel(x_prompt, x_sample, state_ret, norm1_g, w_in, ln_v_g, ln_v_b, w_s, b_s, w_out, norm2_g, w_router, b_router, w_gate, b_gate, w_up, b_up, w_down, b_down, norm_f_g):
    bsz, seq, _ = x_prompt.shape
    dbsz, dseq, _ = x_sample.shape
    n_p = bsz * seq
    n_s = dbsz * dseq
    n_total = n_p + n_s
    past_len = 2048
    assert w_in.shape[0] == 1, "single layer"

    wr = w_router[0]
    wr_hi = wr.astype(BF16)
    wr_lo = (wr - wr_hi.astype(F32)).astype(BF16)
    p = dict(norm1_g=norm1_g, w_in=w_in[0].astype(BF16), ln_v_g=ln_v_g, ln_v_b=ln_v_b, w_s=w_s[0], b_s=b_s[0],
             w_out=w_out[0].astype(BF16), norm2_g=norm2_g, wr_hi=wr_hi, wr_lo=wr_lo, b_router=b_router,
             w_gate=w_gate[0], b_gate=b_gate[0], w_up=w_up[0], b_up=b_up[0], w_down=w_down[0], b_down=b_down[0])

    x2_p, h2_p, lp_p, tw_p, ct_p, of_p, st_p = _prompt_call(x_prompt, p)
    x2_s, h2_s, lp_s, tw_s, ct_s, of_s, st_s, vn_s = _decode_call(x_sample, state_ret[0], past_len, p)

    n_tiles = n_total // TOKEN_TILE
    lp = jnp.concatenate([lp_p, lp_s], axis=0).reshape(n_tiles, 1, TILE_SLOTS)
    tw = jnp.concatenate([tw_p, tw_s], axis=0).reshape(n_tiles, 1, TILE_SLOTS)
    counts = jnp.concatenate([ct_p, ct_s], axis=0).reshape(n_tiles, N_EXPERTS)
    offsets = jnp.concatenate([of_p, of_s], axis=0).reshape(n_tiles * N_EXPERTS).astype(I32)

    n_blocks = -(-(n_total * TOP_K + N_EXPERTS * (EXPERT_ROWS - 1)) // EXPERT_ROWS)
    cum, nb, bstart, be, nbt, tlo, thi = _plan_call(counts, n_blocks)

    buckets = _dispatch_call(h2_p, h2_s, lp)
    ybuckets = _expert_call(be.reshape(n_blocks), bstart.reshape(N_EXPERTS), nb.reshape(N_EXPERTS), nbt.reshape(1),
                            cum.reshape((n_tiles + 1) * N_EXPERTS), offsets, tlo.reshape(n_blocks),
                            thi.reshape(n_blocks), buckets, p, n_tiles)

    y_p = _combine_call(x2_p, ybuckets, lp, tw, norm_f_g[None, :], 0)
    y_s = _combine_call(x2_s, ybuckets, lp, tw, norm_f_g[None, :], n_p // TOKEN_TILE)

    return (y_p.reshape(bsz, seq, D_MODEL), y_s.reshape(dbsz, dseq, D_MODEL),
            st_p[None], st_s[None], vn_s.reshape(1, dbsz, dseq, GM_WIDTH))
```

```python
import functools

import numpy as np
import jax
import jax.numpy as jnp
from jax import lax
from jax.experimental import pallas as pl
from jax.experimental.pallas import tpu as pltpu

F32 = jnp.float32
BF16 = jnp.bfloat16
I32 = jnp.int32

D_MODEL = 1024
RET_WIDTH = 512
N_HEADS = 4
HEAD_DIM = 128
GM_WIDTH = 512
GM_CHUNK = 128
RET_CHUNK = 64
N_EXPERTS = 32
TOP_K = 4
SWIGLU_LIMIT = 7.0
SWIGLU_ALPHA = 1.702
ROPE_BASE = 10000.0
EPS = 1e-6

TOKEN_TILE = 512
PROMPT_RET_BLOCK = 256
EXPERT_ROWS = 512
EXPERT_SUB_ROWS = 256
TILE_SLOTS = TOKEN_TILE * TOP_K
COMBINE_CHUNK = 256
VMEM_LIMIT = 56 * 1024 * 1024
LANES = 128
ROW_CHUNKS = D_MODEL // LANES
SPLIT = 16.0
_LOG_GAMMA = [float(np.log1p(-np.float32(2.0) ** np.float32(-5.0 - hd)).astype(np.float32)) for hd in range(N_HEADS)]


def _load_rows(ref, first_row, n_rows):
    base = first_row * ROW_CHUNKS
    return jnp.concatenate(
        [ref[pl.ds(base + c, n_rows, stride=ROW_CHUNKS), :] for c in range(ROW_CHUNKS)], axis=1)


def _store_rows(ref, first_row, val):
    base = first_row * ROW_CHUNKS
    for c in range(ROW_CHUNKS):
        ref[pl.ds(base + c, val.shape[0], stride=ROW_CHUNKS), :] = val[:, c * LANES:(c + 1) * LANES]


def _row_tile(row):
    return pl.ds(pl.multiple_of(row * ROW_CHUNKS, ROW_CHUNKS), ROW_CHUNKS)


def _rmsnorm(x, g):
    ms = jnp.mean(x * x, axis=-1, keepdims=True)
    return x * lax.rsqrt(ms + EPS) * g


def _gelu(x):
    c = np.float32(np.sqrt(2.0 / np.pi))
    return x * (0.5 * (1.0 + jnp.tanh(c * (x + 0.044715 * (x * x * x)))))


def _sigmoid(x):
    return 1.0 / (1.0 + jnp.exp(-x))


def _dot(a, b):
    return jnp.dot(a, b, preferred_element_type=F32)


def _dot_nt(a, b):
    return lax.dot_general(a, b, (((1,), (1,)), ((), ())), preferred_element_type=F32)


def _dot_tn(a, b):
    return lax.dot_general(a, b, (((0,), (0,)), ((), ())), preferred_element_type=F32)


def _exact_count_dot(a, b, *, counts_on_left):
    cnt = a if counts_on_left else b
    hi = jnp.floor(cnt * (1.0 / SPLIT))
    lo = cnt - SPLIT * hi
    if counts_on_left:
        return SPLIT * _dot(hi.astype(BF16), b) + _dot(lo.astype(BF16), b)
    return SPLIT * _dot(a, hi.astype(BF16)) + _dot(a, lo.astype(BF16))


def _rotary(xh, cos2, sin2):
    return xh * cos2 + pltpu.roll(xh, HEAD_DIM // 2, axis=1) * sin2


def _gated_head_norm(o, gate):
    mu = jnp.mean(o, axis=-1, keepdims=True)
    oc = o - mu
    var = jnp.mean(oc * oc, axis=-1, keepdims=True)
    return ((gate * _sigmoid(gate)) * (oc * lax.rsqrt(var + EPS))).astype(BF16)


def _gmlp_inputs(u, vg, lng, lnb):
    uf = _gelu(u)
    vf = _gelu(vg)
    mu = jnp.mean(vf, axis=-1, keepdims=True)
    vc = vf - mu
    var = jnp.mean(vc * vc, axis=-1, keepdims=True)
    return uf, vc * lax.rsqrt(var + EPS) * lng + lnb


def _route_and_store(x, mix_scr, wout_ref, g2_ref, wrh_ref, wrl_ref, br_ref, tri_ref, upper_ref,
                     x2_ref, h2_ref, lp_ref, tw_ref, ct_ref, of_ref):
    tile = x.shape[0]
    x2 = x + _dot(mix_scr[...], wout_ref[...])
    x2_ref[...] = x2
    h2 = _rmsnorm(x2, g2_ref[...])
    _store_rows(h2_ref, 0, h2)

    h_hi = h2.astype(BF16)
    h_lo = (h2 - h_hi.astype(F32)).astype(BF16)
    logits = (_dot(h_hi, wrh_ref[...]) + _dot(h_lo, wrh_ref[...]) + _dot(h_hi, wrl_ref[...])) + br_ref[...]

    iota_e = lax.broadcasted_iota(I32, (tile, N_EXPERTS), 1)
    iota_ef = iota_e.astype(F32)
    iota_k = lax.broadcasted_iota(I32, (tile, TOP_K), 1)
    lg = logits
    vals = []
    idxs = []
    for _ in range(TOP_K):
        m = jnp.max(lg, axis=-1, keepdims=True)
        idx = jnp.min(jnp.where(lg == m, iota_ef, float(N_EXPERTS)), axis=-1, keepdims=True).astype(I32)
        vals.append(m)
        idxs.append(idx)
        lg = jnp.where(iota_e == idx, -jnp.inf, lg)
    exps = [jnp.exp(vk - vals[0]) for vk in vals]
    denom = exps[0] + exps[1] + exps[2] + exps[3]

    onehot = jnp.zeros((tile, N_EXPERTS), F32)
    for idx in idxs:
        onehot = onehot + (iota_e == idx).astype(F32)
    rank = _dot(tri_ref[...], onehot.astype(BF16))
    count = jnp.sum(onehot, axis=0, keepdims=True)
    offset = _exact_count_dot(jnp.broadcast_to(count, (8, N_EXPERTS)), upper_ref[...], counts_on_left=True)[0:1]
    where_to = rank + offset
    lp = jnp.zeros((tile, TOP_K), I32)
    tw = jnp.zeros((tile, TOP_K), F32)
    for kk in range(TOP_K):
        pos = jnp.sum(jnp.where(iota_e == idxs[kk], where_to, 0.0), axis=-1, keepdims=True).astype(I32)
        lp = jnp.where(iota_k == kk, pos, lp)
        tw = jnp.where(iota_k == kk, exps[kk] / denom, tw)
    lp_ref[...] = lp
    tw_ref[...] = tw
    ct_ref[0] = count
    of_ref[0] = offset


def _prompt_kernel(x_ref, g1_ref, win_ref, cq_ref, sq_ref, ck_ref, sk_ref, dmat_ref, qdec_ref, kdec_ref,
                   sdec_ref, lng_ref, lnb_ref, ws_ref, bst_ref, wout_ref, g2_ref, wrh_ref, wrl_ref, br_ref,
                   tri_ref, upper_ref,
                   x2_ref, h2_ref, lp_ref, tw_ref, ct_ref, of_ref, st_ref,
                   s_scr, mix_scr, *, ret_block):
    tile = TOKEN_TILE
    t = pl.program_id(1)

    @pl.when(t == 0)
    def _():
        s_scr[...] = jnp.zeros_like(s_scr)

    x = x_ref[0]
    h = _rmsnorm(x, g1_ref[...]).astype(BF16)

    def proj(i):
        return _dot(h, win_ref[:, i * RET_WIDTH:(i + 1) * RET_WIDTH])

    q = proj(0)
    k = proj(1)
    v = proj(2)
    gate = proj(3)
    u = proj(4)
    vg = proj(5)
    cq = cq_ref[...]
    sq = sq_ref[...]
    ck = ck_ref[...]
    sk = sk_ref[...]

    for hd in range(N_HEADS):
        cols = slice(hd * HEAD_DIM, (hd + 1) * HEAD_DIM)
        qr = _rotary(q[:, cols], cq, sq)
        kr = _rotary(k[:, cols], ck, sk)
        vh = v[:, cols].astype(BF16)
        dm = dmat_ref[hd]
        qd = qdec_ref[hd]
        kd = kdec_ref[hd]
        sd = sdec_ref[hd]
        for c in range(tile // ret_block):
            rows = slice(c * ret_block, (c + 1) * ret_block)
            qb = qr[rows].astype(BF16)
            kb = kr[rows]
            vb = vh[rows]
            state = s_scr[hd]
            scores = _dot_nt(qb, kb.astype(BF16)) * dm
            o = _dot(scores.astype(BF16), vb) + qd * _dot(qb, state.astype(BF16))
            s_scr[hd] = sd * state + _dot_tn((kb * kd).astype(BF16), vb)
            mix_scr[rows, cols] = _gated_head_norm(o, gate[rows, cols])

    uf, vn = _gmlp_inputs(u, vg, lng_ref[...], lnb_ref[...])
    vnb = vn.astype(BF16)
    r_i = lax.broadcasted_iota(I32, (GM_CHUNK, GM_CHUNK), 0)
    c_i = lax.broadcasted_iota(I32, (GM_CHUNK, GM_CHUNK), 1)
    for hd in range(N_HEADS):
        cols = slice(hd * HEAD_DIM, (hd + 1) * HEAD_DIM)
        w_tril = jnp.where(r_i >= c_i, ws_ref[hd], 0.0).astype(BF16)
        bias = bst_ref[:, hd:hd + 1]
        for c in range(tile // GM_CHUNK):
            rows = slice(c * GM_CHUNK, (c + 1) * GM_CHUNK)
            mixed = _dot(w_tril, vnb[rows, cols]) + bias
            mix_scr[rows, RET_WIDTH + hd * HEAD_DIM:RET_WIDTH + (hd + 1) * HEAD_DIM] = (
                uf[rows, cols] * mixed).astype(BF16)

    _route_and_store(x, mix_scr, wout_ref, g2_ref, wrh_ref, wrl_ref, br_ref, tri_ref, upper_ref,
                     x2_ref, h2_ref, lp_ref, tw_ref, ct_ref, of_ref)

    @pl.when(t == pl.num_programs(1) - 1)
    def _():
        st_ref[0] = s_scr[...]


def _decode_kernel(x_ref, s0_ref, g1_ref, win_ref, cq_ref, sq_ref, ck_ref, sk_ref, qdec_ref, kdec_ref,
                   sdec_ref, lng_ref, lnb_ref, ws_ref, bst_ref, wout_ref, g2_ref, wrh_ref, wrl_ref, br_ref,
                   tri_ref, upper_ref,
                   x2_ref, h2_ref, lp_ref, tw_ref, ct_ref, of_ref, st_ref, vn_ref,
                   mix_scr, q_scr, k_scr, v_scr, oc_scr, st_stage, st_sem, *, n_streams, frames):
    tile = n_streams * frames
    x = x_ref[...]
    h = _rmsnorm(x, g1_ref[...]).astype(BF16)

    def proj(i):
        return _dot(h, win_ref[:, i * RET_WIDTH:(i + 1) * RET_WIDTH])

    q = proj(0)
    k = proj(1)
    v = proj(2)
    gate = proj(3)
    u = proj(4)
    vg = proj(5)
    cq = cq_ref[...]
    sq = sq_ref[...]
    ck = ck_ref[...]
    sk = sk_ref[...]

    r_i = lax.broadcasted_iota(I32, (tile, tile), 0)
    c_i = lax.broadcasted_iota(I32, (tile, tile), 1)
    shift = frames.bit_length() - 1
    r_frame = lax.bitwise_and(r_i, frames - 1)
    c_frame = lax.bitwise_and(c_i, frames - 1)
    keep = (lax.shift_right_logical(r_i, shift) == lax.shift_right_logical(c_i, shift)) & (r_frame >= c_frame)
    frame_gap = (r_frame - c_frame).astype(F32)

    o_in = []
    for hd in range(N_HEADS):
        cols = slice(hd * HEAD_DIM, (hd + 1) * HEAD_DIM)
        qr = _rotary(q[:, cols], cq, sq)
        kr = _rotary(k[:, cols], ck, sk)
        vh = v[:, cols].astype(BF16)
        qb = qr.astype(BF16)
        decay = jnp.where(keep, jnp.exp(frame_gap * _LOG_GAMMA[hd]), 0.0)
        scores = _dot_nt(qb, kr.astype(BF16)) * decay
        o_in.append(_dot(scores.astype(BF16), vh))
        q_scr[hd] = qb
        k_scr[hd] = (kr * kdec_ref[hd]).astype(BF16)
        v_scr[hd] = vh

    def state_copy(b):
        slot = lax.rem(b, 2)
        return pltpu.make_async_copy(st_stage.at[slot], st_ref.at[b], st_sem.at[slot])

    def stream_body(b, carry):
        rows = pl.ds(pl.multiple_of(b * frames, frames), frames)
        slot = lax.rem(b, 2)

        @pl.when(b >= 2)
        def _():
            state_copy(b - 2).wait()

        for hd in range(N_HEADS):
            state = s0_ref[b, hd]
            oc_scr[hd, rows, :] = _dot(q_scr[hd, rows, :], state.astype(BF16))
            st_stage[slot, hd] = sdec_ref[hd] * state + _dot_tn(k_scr[hd, rows, :], v_scr[hd, rows, :])
        state_copy(b).start()
        return carry

    lax.fori_loop(0, n_streams, stream_body, 0)
    state_copy(n_streams - 2).wait()
    state_copy(n_streams - 1).wait()

    for hd in range(N_HEADS):
        cols = slice(hd * HEAD_DIM, (hd + 1) * HEAD_DIM)
        o = o_in[hd] + qdec_ref[hd] * oc_scr[hd]
        mix_scr[:, cols] = _gated_head_norm(o, gate[:, cols])

    uf, vn = _gmlp_inputs(u, vg, lng_ref[...], lnb_ref[...])
    vn_ref[...] = vn
    vnb = vn.astype(BF16)
    sel = (lax.bitwise_and(lax.broadcasted_iota(I32, (tile, frames), 0), frames - 1)
           == lax.broadcasted_iota(I32, (tile, frames), 1)).astype(BF16)
    for hd in range(N_HEADS):
        cols = slice(hd * HEAD_DIM, (hd + 1) * HEAD_DIM)
        w_rows = _dot(sel, ws_ref[hd].astype(BF16)).astype(BF16)
        w_blk = jnp.where(keep, _dot_nt(w_rows, sel), 0.0).astype(BF16)
        mixed = _dot(w_blk, vnb[:, cols]) + bst_ref[:, hd:hd + 1]
        mix_scr[:, RET_WIDTH + hd * HEAD_DIM:RET_WIDTH + (hd + 1) * HEAD_DIM] = (uf[:, cols] * mixed).astype(BF16)

    _route_and_store(x, mix_scr, wout_ref, g2_ref, wrh_ref, wrl_ref, br_ref, tri_ref, upper_ref,
                     x2_ref, h2_ref, lp_ref, tw_ref, ct_ref, of_ref)


def _rope_tables(pos):
    half = HEAD_DIM // 2
    inv = ROPE_BASE ** (-jnp.arange(half, dtype=F32) / half)
    ang = pos.astype(F32)[:, None] * inv[None, :]
    cos = jnp.cos(ang)
    sin = jnp.sin(ang)
    cq = jnp.concatenate([cos, cos], axis=-1)
    sq = jnp.concatenate([-sin, sin], axis=-1)
    scale = HEAD_DIM ** -0.5
    return cq, sq, cq * scale, sq * scale


def _decay_tables(block):
    log_g = jnp.log1p(-(2.0 ** (-5.0 - jnp.arange(N_HEADS, dtype=F32))))
    idx = jnp.arange(block, dtype=F32)
    diff = idx[:, None] - idx[None, :]
    dmat = jnp.where(diff[None] >= 0, jnp.exp(jnp.maximum(diff, 0.0)[None] * log_g[:, None, None]), 0.0)
    q_dec = jnp.exp((idx + 1.0)[None, :] * log_g[:, None])
    k_dec = jnp.exp((block - 1.0 - idx)[None, :] * log_g[:, None])
    s_dec = jnp.exp(block * log_g)
    bc = lambda a: jnp.broadcast_to(a[:, :, None], (N_HEADS, block, HEAD_DIM))
    return dmat, bc(q_dec), bc(k_dec), jnp.broadcast_to(s_dec[:, None, None], (N_HEADS, 1, HEAD_DIM))


def _routing_constants(tile):
    tri = (jnp.arange(tile)[:, None] > jnp.arange(tile)[None, :]).astype(BF16)
    upper = (jnp.arange(N_EXPERTS)[:, None] < jnp.arange(N_EXPERTS)[None, :]).astype(BF16)
    return tri, upper


def _front_out(n_tok, n_tiles):
    return [
        jax.ShapeDtypeStruct((n_tok, D_MODEL), F32),
        jax.ShapeDtypeStruct((n_tok * ROW_CHUNKS, LANES), F32),
        jax.ShapeDtypeStruct((n_tok, TOP_K), I32),
        jax.ShapeDtypeStruct((n_tok, TOP_K), F32),
        jax.ShapeDtypeStruct((n_tiles, 1, N_EXPERTS), F32),
        jax.ShapeDtypeStruct((n_tiles, 1, N_EXPERTS), F32),
    ]


def _prompt_call(x, p):
    bsz, seq, _ = x.shape
    tile = TOKEN_TILE
    n_t = seq // tile
    cq, sq, ck, sk = _rope_tables(jnp.arange(seq, dtype=I32))
    dmat, qdec, kdec, sdec = _decay_tables(PROMPT_RET_BLOCK)
    tri, upper = _routing_constants(tile)
    ws = p['w_s'][:, :GM_CHUNK, :GM_CHUNK]
    bst = p['b_s'][:, :GM_CHUNK].T

    const = lambda shape: pl.BlockSpec(shape, lambda b, t: (0,) * len(shape))
    pos = lambda: pl.BlockSpec((tile, HEAD_DIM), lambda b, t: (t, 0))
    in_specs = [
        pl.BlockSpec((1, tile, D_MODEL), lambda b, t: (b, t, 0)),
        const((1, D_MODEL)), const((D_MODEL, 6 * RET_WIDTH)),
        pos(), pos(), pos(), pos(),
        const((N_HEADS, PROMPT_RET_BLOCK, PROMPT_RET_BLOCK)),
        const((N_HEADS, PROMPT_RET_BLOCK, HEAD_DIM)), const((N_HEADS, PROMPT_RET_BLOCK, HEAD_DIM)),
        const((N_HEADS, 1, HEAD_DIM)),
        const((1, GM_WIDTH)), const((1, GM_WIDTH)),
        const((N_HEADS, GM_CHUNK, GM_CHUNK)), const((GM_CHUNK, N_HEADS)),
        const((D_MODEL, D_MODEL)), const((1, D_MODEL)),
        const((D_MODEL, N_EXPERTS)), const((D_MODEL, N_EXPERTS)), const((1, N_EXPERTS)),
        const((tile, tile)), const((N_EXPERTS, N_EXPERTS)),
    ]
    row = lambda b, t: (b * n_t + t, 0)
    row3 = lambda b, t: (b * n_t + t, 0, 0)
    out_specs = [
        pl.BlockSpec((tile, D_MODEL), row),
        pl.BlockSpec((tile * ROW_CHUNKS, LANES), row),
        pl.BlockSpec((tile, TOP_K), row),
        pl.BlockSpec((tile, TOP_K), row),
        pl.BlockSpec((1, 1, N_EXPERTS), row3),
        pl.BlockSpec((1, 1, N_EXPERTS), row3),
        pl.BlockSpec((1, N_HEADS, HEAD_DIM, HEAD_DIM), lambda b, t: (b, 0, 0, 0)),
    ]
    out_shape = _front_out(bsz * seq, bsz * n_t) + [jax.ShapeDtypeStruct((bsz, N_HEADS, HEAD_DIM, HEAD_DIM), F32)]
    return pl.pallas_call(
        functools.partial(_prompt_kernel, ret_block=PROMPT_RET_BLOCK),
        grid=(bsz, n_t),
        in_specs=in_specs,
        out_specs=out_specs,
        out_shape=out_shape,
        scratch_shapes=[
            pltpu.VMEM((N_HEADS, HEAD_DIM, HEAD_DIM), F32),
            pltpu.VMEM((tile, D_MODEL), BF16),
        ],
        compiler_params=pltpu.CompilerParams(
            dimension_semantics=("arbitrary", "arbitrary"), vmem_limit_bytes=VMEM_LIMIT),
        name="front_prompt",
    )(x, p['norm1_g'], p['w_in'], cq, sq, ck, sk, dmat, qdec, kdec, sdec, p['ln_v_g'], p['ln_v_b'], ws, bst,
      p['w_out'], p['norm2_g'], p['wr_hi'], p['wr_lo'], p['b_router'], tri, upper)


def _decode_call(x, state, past_len, p):
    n_streams, frames, _ = x.shape
    tile = n_streams * frames
    assert tile == TOKEN_TILE and frames <= RET_CHUNK and frames & (frames - 1) == 0
    rope = _rope_tables(past_len + jnp.arange(frames, dtype=I32))
    cq, sq, ck, sk = [jnp.tile(a, (n_streams, 1)) for a in rope]
    _, qdec, kdec, sdec = _decay_tables(frames)
    qdec = jnp.tile(qdec, (1, n_streams, 1))
    kdec = jnp.tile(kdec, (1, n_streams, 1))
    tri, upper = _routing_constants(tile)
    ws = p['w_s'][:, :frames, :frames]
    bst = jnp.tile(p['b_s'][:, :frames].T, (n_streams, 1))

    def whole(a):
        return pl.BlockSpec(a.shape, lambda i, n=a.ndim: (0,) * n, pipeline_mode=pl.Buffered(1))

    args = (x.reshape(tile, D_MODEL), state, p['norm1_g'], p['w_in'], cq, sq, ck, sk, qdec, kdec, sdec,
            p['ln_v_g'], p['ln_v_b'], ws, bst, p['w_out'], p['norm2_g'], p['wr_hi'], p['wr_lo'], p['b_router'],
            tri, upper)
    out_shape = _front_out(tile, 1) + [
        jax.ShapeDtypeStruct(state.shape, F32),
        jax.ShapeDtypeStruct((tile, GM_WIDTH), F32),
    ]
    out_specs = [pl.BlockSpec(s.shape, lambda i, n=len(s.shape): (0,) * n) for s in out_shape]
    out_specs[6] = pl.BlockSpec(memory_space=pl.ANY)
    return pl.pallas_call(
        functools.partial(_decode_kernel, n_streams=n_streams, frames=frames),
        grid=(1,),
        in_specs=[whole(a) for a in args],
        out_specs=out_specs,
        out_shape=out_shape,
        scratch_shapes=[
            pltpu.VMEM((tile, D_MODEL), BF16),
            pltpu.VMEM((N_HEADS, tile, HEAD_DIM), BF16),
            pltpu.VMEM((N_HEADS, tile, HEAD_DIM), BF16),
            pltpu.VMEM((N_HEADS, tile, HEAD_DIM), BF16),
            pltpu.VMEM((N_HEADS, tile, HEAD_DIM), F32),
            pltpu.VMEM((2, N_HEADS, HEAD_DIM, HEAD_DIM), F32),
            pltpu.SemaphoreType.DMA((2,)),
        ],
        compiler_params=pltpu.CompilerParams(
            dimension_semantics=("arbitrary",), vmem_limit_bytes=VMEM_LIMIT),
        name="front_decode",
    )(*args)


def _plan_kernel(ct_ref, lower_ref, upper_ref, cum_ref, nb_ref, bs_ref, be_ref, nbt_ref, tlo_ref, thi_ref,
                 *, n_blocks):
    counts = ct_ref[...]
    cum = _exact_count_dot(lower_ref[...], counts, counts_on_left=False)
    cum_ref[...] = cum.astype(I32)
    n_tiles = counts.shape[0]
    total = cum[n_tiles:n_tiles + 1]
    nb = jnp.floor((total + (EXPERT_ROWS - 1)) * (1.0 / EXPERT_ROWS))
    bstart = _dot(jnp.broadcast_to(nb, (8, N_EXPERTS)).astype(BF16), upper_ref[...])[0:1]
    bend = bstart + nb
    nb_ref[...] = nb.astype(I32)
    bs_ref[...] = bstart.astype(I32)
    blk = lax.broadcasted_iota(I32, (n_blocks, N_EXPERTS), 0).astype(F32)
    be = jnp.minimum(jnp.sum((bend <= blk).astype(F32), axis=-1, keepdims=True), N_EXPERTS - 1.0)
    be_ref[...] = be.astype(I32)
    nbt_ref[...] = jnp.sum(nb, axis=-1, keepdims=True).astype(I32)
    mine = lax.broadcasted_iota(I32, (n_blocks, N_EXPERTS), 1).astype(F32) == be
    pick = lambda row: jnp.sum(jnp.where(mine, row, 0.0), axis=-1, keepdims=True)
    r0 = (blk[:, 0:1] - pick(bstart)) * EXPERT_ROWS
    tlo = jnp.zeros((n_blocks, 1), F32)
    thi = jnp.zeros((n_blocks, 1), F32)
    seg_lo = pick(cum[0:1])
    for i in range(n_tiles):
        seg_hi = pick(cum[i + 1:i + 2])
        tlo = tlo + (seg_hi <= r0).astype(F32)
        thi = thi + (seg_lo < r0 + EXPERT_ROWS).astype(F32)
        seg_lo = seg_hi
    tlo_ref[...] = tlo.astype(I32)
    thi_ref[...] = thi.astype(I32)


def _plan_call(counts, n_blocks):
    n_tiles = counts.shape[0]
    lower = (jnp.arange(n_tiles + 1)[:, None] > jnp.arange(n_tiles)[None, :]).astype(BF16)
    upper = (jnp.arange(N_EXPERTS)[:, None] < jnp.arange(N_EXPERTS)[None, :]).astype(BF16)
    return pl.pallas_call(
        functools.partial(_plan_kernel, n_blocks=n_blocks),
        out_shape=[
            jax.ShapeDtypeStruct((n_tiles + 1, N_EXPERTS), I32),
            jax.ShapeDtypeStruct((1, N_EXPERTS), I32),
            jax.ShapeDtypeStruct((1, N_EXPERTS), I32),
            jax.ShapeDtypeStruct((n_blocks, 1), I32),
            jax.ShapeDtypeStruct((1, 1), I32),
            jax.ShapeDtypeStruct((n_blocks, 1), I32),
            jax.ShapeDtypeStruct((n_blocks, 1), I32),
        ],
        name="plan",
    )(counts, lower, upper)


def _dispatch_kernel(h2p_ref, h2s_ref, lp_ref, out_ref, *, n_prompt_tiles):
    i = pl.program_id(0)

    def permute(src_ref):
        def body(t, c):
            row = src_ref[_row_tile(t), :]
            for kk in range(TOP_K):
                out_ref[_row_tile(lp_ref[0, 0, t * TOP_K + kk]), :] = row
            return c

        lax.fori_loop(0, TOKEN_TILE, body, 0, unroll=8)

    @pl.when(i < n_prompt_tiles)
    def _():
        permute(h2p_ref)

    @pl.when(i >= n_prompt_tiles)
    def _():
        permute(h2s_ref)


def _dispatch_call(h2_p, h2_s, lp_tiles):
    rows = TOKEN_TILE * ROW_CHUNKS
    n_p_tiles = h2_p.shape[0] // rows
    n_s_tiles = h2_s.shape[0] // rows
    n_tiles = n_p_tiles + n_s_tiles
    return pl.pallas_call(
        functools.partial(_dispatch_kernel, n_prompt_tiles=n_p_tiles),
        grid=(n_tiles,),
        in_specs=[
            pl.BlockSpec((rows, LANES), lambda i: (jnp.minimum(i, n_p_tiles - 1), 0)),
            pl.BlockSpec((rows, LANES), lambda i: (jnp.maximum(i - n_p_tiles, 0), 0)),
            pl.BlockSpec((1, 1, TILE_SLOTS), lambda i: (i, 0, 0), memory_space=pltpu.SMEM),
        ],
        out_specs=pl.BlockSpec((TILE_SLOTS * ROW_CHUNKS, LANES), lambda i: (i, 0)),
        out_shape=jax.ShapeDtypeStruct((n_tiles * TILE_SLOTS * ROW_CHUNKS, LANES), F32),
        compiler_params=pltpu.CompilerParams(
            dimension_semantics=("arbitrary",), vmem_limit_bytes=VMEM_LIMIT),
        name="dispatch",
    )(h2_p, h2_s, lp_tiles)


def _expert_kernel(be_ref, bs_ref, nb_ref, nbt_ref, cum_ref, off_ref, tlo_ref, thi_ref,
                   xb_ref, wg_ref, wu_ref, wd_ref, bg_ref, bu_ref, bd_ref,
                   yb_ref, xbuf, ybuf, wbf, in_sem, out_sem, *, n_tiles):
    b = pl.program_id(0)
    nbt = nbt_ref[0]

    def block_rows(blk):
        e = be_ref[blk]
        r0 = (blk - bs_ref[e]) * EXPERT_ROWS
        return r0, jnp.minimum(cum_ref[n_tiles * N_EXPERTS + e] - r0, EXPERT_ROWS)

    def segments(blk, slot, fn):
        e = be_ref[blk]
        r0 = (blk - bs_ref[e]) * EXPERT_ROWS

        def body(i, c):
            seg_lo = cum_ref[i * N_EXPERTS + e]
            seg_hi = cum_ref[(i + 1) * N_EXPERTS + e]
            lo = jnp.maximum(seg_lo, r0)
            hi = jnp.minimum(seg_hi, r0 + EXPERT_ROWS)

            @pl.when(hi > lo)
            def _():
                bucket_row = i * TILE_SLOTS + off_ref[i * N_EXPERTS + e] + (lo - seg_lo)
                fn(slot, bucket_row, slot * EXPERT_ROWS + (lo - r0), hi - lo)
            return c

        lax.fori_loop(tlo_ref[blk], thi_ref[blk], body, 0)

    def words(row, n):
        return pl.ds(pl.multiple_of(row * ROW_CHUNKS, ROW_CHUNKS), n * ROW_CHUNKS)

    def gather(slot, bucket_row, buffer_row, n):
        return pltpu.make_async_copy(
            xb_ref.at[words(bucket_row, n)], xbuf.at[words(buffer_row, n)], in_sem.at[slot])

    def scatter(slot, bucket_row, buffer_row, n):
        return pltpu.make_async_copy(
            ybuf.at[words(buffer_row, n)], yb_ref.at[words(bucket_row, n)], out_sem.at[slot])

    def start_gather(blk):
        @pl.when(blk < nbt)
        def _():
            segments(blk, lax.rem(blk, 2), lambda *a: gather(*a).start())

    def wait_gather(blk):
        @pl.when(blk < nbt)
        def _():
            slot = lax.rem(blk, 2)
            gather(slot, 0, slot * EXPERT_ROWS, block_rows(blk)[1]).wait()

    def start_scatter(blk):
        @pl.when(blk < nbt)
        def _():
            segments(blk, lax.rem(blk, 2), lambda *a: scatter(*a).start())

    def wait_scatter(blk):
        @pl.when((blk >= 0) & (blk < nbt))
        def _():
            slot = lax.rem(blk, 2)
            scatter(slot, 0, slot * EXPERT_ROWS, block_rows(blk)[1]).wait()

    def mlp(base, n_rows, rows_left):
        x = _load_rows(xbuf, base, n_rows)
        valid = lax.broadcasted_iota(I32, (n_rows, 1), 0) < rows_left
        x = jnp.where(valid, x, 0.0).astype(BF16)
        gt = _dot(x, wbf[0]) + bg_ref[0]
        up = _dot(x, wbf[1]) + bu_ref[0]
        gt = jnp.minimum(gt, SWIGLU_LIMIT)
        up = jnp.clip(up, -SWIGLU_LIMIT, SWIGLU_LIMIT)
        act = gt * _sigmoid(gt * SWIGLU_ALPHA) * (up + 1.0)
        _store_rows(ybuf, base, _dot(act.astype(BF16), wbf[2]) + bd_ref[0])

    def block_body(blk, carry):
        wait_gather(blk)
        start_gather(blk + 1)
        wait_scatter(blk - 2)
        rows_left = block_rows(blk)[1]
        base = lax.rem(blk, 2) * EXPERT_ROWS

        @pl.when(rows_left > EXPERT_SUB_ROWS)
        def _():
            mlp(base, EXPERT_ROWS, rows_left)

        @pl.when(rows_left <= EXPERT_SUB_ROWS)
        def _():
            mlp(base, EXPERT_SUB_ROWS, rows_left)

        start_scatter(blk)
        return carry

    first_block = bs_ref[b]
    n_mine = nb_ref[b]

    @pl.when(b == 0)
    def _():
        xbuf[...] = jnp.zeros_like(xbuf)
        start_gather(0)

    @pl.when(n_mine > 0)
    def _():
        wbf[0] = wg_ref[0].astype(BF16)
        wbf[1] = wu_ref[0].astype(BF16)
        wbf[2] = wd_ref[0].astype(BF16)

    lax.fori_loop(first_block, first_block + n_mine, block_body, 0)

    @pl.when(b == N_EXPERTS - 1)
    def _():
        wait_scatter(nbt - 2)
        wait_scatter(nbt - 1)


def _expert_call(be, bstart, nb, nbt, cum_flat, off_flat, tlo, thi, buckets, p, n_tiles):
    wspec = pl.BlockSpec((1, D_MODEL, D_MODEL), lambda e, *_: (e, 0, 0))
    bspec = pl.BlockSpec((1, 1, D_MODEL), lambda e, *_: (e, 0, 0))
    any_spec = pl.BlockSpec(memory_space=pl.ANY)
    return pl.pallas_call(
        functools.partial(_expert_kernel, n_tiles=n_tiles),
        grid_spec=pltpu.PrefetchScalarGridSpec(
            num_scalar_prefetch=8,
            grid=(N_EXPERTS,),
            in_specs=[any_spec, wspec, wspec, wspec, bspec, bspec, bspec],
            out_specs=any_spec,
            scratch_shapes=[
                pltpu.VMEM((2 * EXPERT_ROWS * ROW_CHUNKS, LANES), F32),
                pltpu.VMEM((2 * EXPERT_ROWS * ROW_CHUNKS, LANES), F32),
                pltpu.VMEM((3, D_MODEL, D_MODEL), BF16),
                pltpu.SemaphoreType.DMA((2,)),
                pltpu.SemaphoreType.DMA((2,)),
            ],
        ),
        out_shape=jax.ShapeDtypeStruct(buckets.shape, F32),
        compiler_params=pltpu.CompilerParams(
            dimension_semantics=("arbitrary",), vmem_limit_bytes=VMEM_LIMIT),
        name="experts",
    )(be, bstart, nb, nbt, cum_flat, off_flat, tlo, thi, buckets, p['w_gate'], p['w_up'], p['w_down'],
      p['b_gate'][:, None, :], p['b_up'][:, None, :], p['b_down'][:, None, :])


def _combine_kernel(x2_ref, yb_ref, lp_ref, tw_ref, gf_ref, y_ref, ybf, pick):
    for c in range(TILE_SLOTS // COMBINE_CHUNK):
        rows = slice(c * COMBINE_CHUNK, (c + 1) * COMBINE_CHUNK)
        ybf[rows, :] = _load_rows(yb_ref, c * COMBINE_CHUNK, COMBINE_CHUNK).astype(BF16)
    slot = lax.broadcasted_iota(I32, (TOKEN_TILE, TILE_SLOTS), 1)
    lp = lp_ref[...]
    tw = tw_ref[...]
    sel = jnp.zeros((TOKEN_TILE, TILE_SLOTS), F32)
    for kk in range(TOP_K):
        sel = jnp.where(slot == lp[:, kk:kk + 1], tw[:, kk:kk + 1], sel)
    pick[...] = sel.astype(BF16)
    y_ref[...] = _rmsnorm(x2_ref[...] + _dot(pick[...], ybf[...]), gf_ref[...])


def _combine_call(x2, ybuckets, lp, tw, gf, tile_off):
    n_tok = x2.shape[0]
    n_tiles = n_tok // TOKEN_TILE
    return pl.pallas_call(
        _combine_kernel,
        grid=(n_tiles,),
        in_specs=[
            pl.BlockSpec((TOKEN_TILE, D_MODEL), lambda i: (i, 0)),
            pl.BlockSpec((TILE_SLOTS * ROW_CHUNKS, LANES), lambda i: (tile_off + i, 0)),
            pl.BlockSpec((TOKEN_TILE, TOP_K), lambda i: (i, 0)),
            pl.BlockSpec((TOKEN_TILE, TOP_K), lambda i: (i, 0)),
            pl.BlockSpec((1, D_MODEL), lambda i: (0, 0)),
        ],
        out_specs=pl.BlockSpec((TOKEN_TILE, D_MODEL), lambda i: (i, 0)),
        scratch_shapes=[
            pltpu.VMEM((TILE_SLOTS, D_MODEL), BF16),
            pltpu.VMEM((TOKEN_TILE, TILE_SLOTS), BF16),
        ],
        out_shape=jax.ShapeDtypeStruct((n_tok, D_MODEL), F32),
        compiler_params=pltpu.CompilerParams(
            dimension_semantics=("arbitrary",), vmem_limit_bytes=VMEM_LIMIT),
        name="combine",
    )(x2, ybuckets, lp, tw, gf)


def kernel(x_prompt, x_sample, state_ret, norm1_g, w_in, ln_v_g, ln_v_b, w_s, b_s, w_out, norm2_g, w_router, b_router, w_gate, b_gate, w_up, b_up, w_down, b_down, norm_f_g):
    bsz, seq, _ = x_prompt.shape
    dbsz, dseq, _ = x_sample.shape
    n_p = bsz * seq
    n_s = dbsz * dseq
    n_total = n_p + n_s
    past_len = 2048
    assert w_in.shape[0] == 1, "single layer"

    wr = w_router[0]
    wr_hi = wr.astype(BF16)
    wr_lo = (wr - wr_hi.astype(F32)).astype(BF16)
    p = dict(norm1_g=norm1_g, w_in=w_in[0].astype(BF16), ln_v_g=ln_v_g, ln_v_b=ln_v_b, w_s=w_s[0], b_s=b_s[0],
             w_out=w_out[0].astype(BF16), norm2_g=norm2_g, wr_hi=wr_hi, wr_lo=wr_lo, b_router=b_router,
             w_gate=w_gate[0], b_gate=b_gate[0], w_up=w_up[0], b_up=b_up[0], w_down=w_down[0], b_down=b_down[0])

    x2_p, h2_p, lp_p, tw_p, ct_p, of_p, st_p = _prompt_call(x_prompt, p)
    x2_s, h2_s, lp_s, tw_s, ct_s, of_s, st_s, vn_s = _decode_call(x_sample, state_ret[0], past_len, p)

    n_tiles = n_total // TOKEN_TILE
    lp = jnp.concatenate([lp_p, lp_s], axis=0).reshape(n_tiles, 1, TILE_SLOTS)
    counts = jnp.concatenate([ct_p, ct_s], axis=0).reshape(n_tiles, N_EXPERTS)
    offsets = jnp.concatenate([of_p, of_s], axis=0).reshape(n_tiles * N_EXPERTS).astype(I32)

    n_blocks = -(-(n_total * TOP_K + N_EXPERTS * (EXPERT_ROWS - 1)) // EXPERT_ROWS)
    cum, nb, bstart, be, nbt, tlo, thi = _plan_call(counts, n_blocks)

    buckets = _dispatch_call(h2_p, h2_s, lp)
    ybuckets = _expert_call(be.reshape(n_blocks), bstart.reshape(N_EXPERTS), nb.reshape(N_EXPERTS), nbt.reshape(1),
                            cum.reshape((n_tiles + 1) * N_EXPERTS), offsets, tlo.reshape(n_blocks),
                            thi.reshape(n_blocks), buckets, p, n_tiles)

    y_p = _combine_call(x2_p, ybuckets, lp_p, tw_p, norm_f_g[None, :], 0)
    y_s = _combine_call(x2_s, ybuckets, lp_s, tw_s, norm_f_g[None, :], n_p // TOKEN_TILE)

    return (y_p.reshape(bsz, seq, D_MODEL), y_s.reshape(dbsz, dseq, D_MODEL),
            st_p[None], st_s[None], vn_s.reshape(1, dbsz, dseq, GM_WIDTH))
```

```python
import functools

import numpy as np
import jax
import jax.numpy as jnp
from jax import lax
from jax.experimental import pallas as pl
from jax.experimental.pallas import tpu as pltpu

F32 = jnp.float32
BF16 = jnp.bfloat16
I32 = jnp.int32

D_MODEL = 1024
RET_WIDTH = 512
N_HEADS = 4
HEAD_DIM = 128
GM_WIDTH = 512
GM_CHUNK = 128
RET_CHUNK = 64
N_EXPERTS = 32
TOP_K = 4
SWIGLU_LIMIT = 7.0
SWIGLU_ALPHA = 1.702
ROPE_BASE = 10000.0
EPS = 1e-6

TOKEN_TILE = 512
PROMPT_RET_BLOCK = 256
EXPERT_ROWS = 512
EXPERT_SUB_ROWS = 256
TILE_SLOTS = TOKEN_TILE * TOP_K
COMBINE_CHUNK = 256
VMEM_LIMIT = 56 * 1024 * 1024
LANES = 128
U32 = jnp.uint32
ROW_CHUNKS = D_MODEL // LANES // 2
SPLIT = 16.0
_LOG_GAMMA = [float(np.log1p(-np.float32(2.0) ** np.float32(-5.0 - hd)).astype(np.float32)) for hd in range(N_HEADS)]


def _load_rows(ref, first_row, n_rows):
    base = first_row * ROW_CHUNKS
    lo, hi = [], []
    for c in range(ROW_CHUNKS):
        w = ref[pl.ds(base + c, n_rows, stride=ROW_CHUNKS), :]
        lo.append(pltpu.bitcast(lax.shift_left(w, jnp.uint32(16)), F32))
        hi.append(pltpu.bitcast(lax.bitwise_and(w, jnp.uint32(0xFFFF0000)), F32))
    return jnp.concatenate(lo + hi, axis=1).astype(BF16)


def _store_rows(ref, first_row, val):
    base = first_row * ROW_CHUNKS
    bits = pltpu.bitcast(val.astype(BF16).astype(F32), U32)
    half = D_MODEL // 2
    for c in range(ROW_CHUNKS):
        low = lax.shift_right_logical(bits[:, c * LANES:(c + 1) * LANES], jnp.uint32(16))
        ref[pl.ds(base + c, val.shape[0], stride=ROW_CHUNKS), :] = lax.bitwise_or(
            bits[:, half + c * LANES:half + (c + 1) * LANES], low)


def _row_tile(row):
    return pl.ds(pl.multiple_of(row * ROW_CHUNKS, ROW_CHUNKS), ROW_CHUNKS)


def _rmsnorm(x, g):
    ms = jnp.mean(x * x, axis=-1, keepdims=True)
    return x * lax.rsqrt(ms + EPS) * g


def _gelu(x):
    c = np.float32(np.sqrt(2.0 / np.pi))
    return x * (0.5 * (1.0 + jnp.tanh(c * (x + 0.044715 * (x * x * x)))))


def _sigmoid(x):
    return 1.0 / (1.0 + jnp.exp(-x))


def _dot(a, b):
    return jnp.dot(a, b, preferred_element_type=F32)


def _dot_nt(a, b):
    return lax.dot_general(a, b, (((1,), (1,)), ((), ())), preferred_element_type=F32)


def _dot_tn(a, b):
    return lax.dot_general(a, b, (((0,), (0,)), ((), ())), preferred_element_type=F32)


def _exact_count_dot(a, b, *, counts_on_left):
    cnt = a if counts_on_left else b
    hi = jnp.floor(cnt * (1.0 / SPLIT))
    lo = cnt - SPLIT * hi
    if counts_on_left:
        return SPLIT * _dot(hi.astype(BF16), b) + _dot(lo.astype(BF16), b)
    return SPLIT * _dot(a, hi.astype(BF16)) + _dot(a, lo.astype(BF16))


def _rotary(xh, cos2, sin2):
    return xh * cos2 + pltpu.roll(xh, HEAD_DIM // 2, axis=1) * sin2


def _gated_head_norm(o, gate):
    mu = jnp.mean(o, axis=-1, keepdims=True)
    oc = o - mu
    var = jnp.mean(oc * oc, axis=-1, keepdims=True)
    return ((gate * _sigmoid(gate)) * (oc * lax.rsqrt(var + EPS))).astype(BF16)


def _gmlp_inputs(u, vg, lng, lnb):
    uf = _gelu(u)
    vf = _gelu(vg)
    mu = jnp.mean(vf, axis=-1, keepdims=True)
    vc = vf - mu
    var = jnp.mean(vc * vc, axis=-1, keepdims=True)
    return uf, vc * lax.rsqrt(var + EPS) * lng + lnb


def _route_and_store(x, mix_scr, wout_ref, g2_ref, wrh_ref, wrl_ref, br_ref, tri_ref, upper_ref,
                     x2_ref, h2_ref, lp_ref, tw_ref, ct_ref, of_ref):
    tile = x.shape[0]
    x2 = x + _dot(mix_scr[...], wout_ref[...])
    x2_ref[...] = x2
    h2 = _rmsnorm(x2, g2_ref[...])
    _store_rows(h2_ref, 0, h2)

    h_hi = h2.astype(BF16)
    h_lo = (h2 - h_hi.astype(F32)).astype(BF16)
    logits = (_dot(h_hi, wrh_ref[...]) + _dot(h_lo, wrh_ref[...]) + _dot(h_hi, wrl_ref[...])) + br_ref[...]

    iota_e = lax.broadcasted_iota(I32, (tile, N_EXPERTS), 1)
    iota_ef = iota_e.astype(F32)
    iota_k = lax.broadcasted_iota(I32, (tile, TOP_K), 1)
    lg = logits
    vals = []
    idxs = []
    for _ in range(TOP_K):
        m = jnp.max(lg, axis=-1, keepdims=True)
        idx = jnp.min(jnp.where(lg == m, iota_ef, float(N_EXPERTS)), axis=-1, keepdims=True).astype(I32)
        vals.append(m)
        idxs.append(idx)
        lg = jnp.where(iota_e == idx, -jnp.inf, lg)
    exps = [jnp.exp(vk - vals[0]) for vk in vals]
    denom = exps[0] + exps[1] + exps[2] + exps[3]

    onehot = jnp.zeros((tile, N_EXPERTS), F32)
    for idx in idxs:
        onehot = onehot + (iota_e == idx).astype(F32)
    rank = _dot(tri_ref[...], onehot.astype(BF16))
    count = jnp.sum(onehot, axis=0, keepdims=True)
    offset = _exact_count_dot(jnp.broadcast_to(count, (8, N_EXPERTS)), upper_ref[...], counts_on_left=True)[0:1]
    where_to = rank + offset
    lp = jnp.zeros((tile, TOP_K), I32)
    tw = jnp.zeros((tile, TOP_K), F32)
    for kk in range(TOP_K):
        pos = jnp.sum(jnp.where(iota_e == idxs[kk], where_to, 0.0), axis=-1, keepdims=True).astype(I32)
        lp = jnp.where(iota_k == kk, pos, lp)
        tw = jnp.where(iota_k == kk, exps[kk] / denom, tw)
    lp_ref[...] = lp
    tw_ref[...] = tw
    ct_ref[0] = count
    of_ref[0] = offset


def _prompt_kernel(x_ref, g1_ref, win_ref, cq_ref, sq_ref, ck_ref, sk_ref, dmat_ref, qdec_ref, kdec_ref,
                   sdec_ref, lng_ref, lnb_ref, ws_ref, bst_ref, wout_ref, g2_ref, wrh_ref, wrl_ref, br_ref,
                   tri_ref, upper_ref,
                   x2_ref, h2_ref, lp_ref, tw_ref, ct_ref, of_ref, st_ref,
                   s_scr, mix_scr, *, ret_block):
    tile = TOKEN_TILE
    t = pl.program_id(1)

    @pl.when(t == 0)
    def _():
        s_scr[...] = jnp.zeros_like(s_scr)

    x = x_ref[0]
    h = _rmsnorm(x, g1_ref[...]).astype(BF16)

    def proj(i):
        return _dot(h, win_ref[:, i * RET_WIDTH:(i + 1) * RET_WIDTH])

    q = proj(0)
    k = proj(1)
    v = proj(2)
    gate = proj(3)
    u = proj(4)
    vg = proj(5)
    cq = cq_ref[...]
    sq = sq_ref[...]
    ck = ck_ref[...]
    sk = sk_ref[...]

    for hd in range(N_HEADS):
        cols = slice(hd * HEAD_DIM, (hd + 1) * HEAD_DIM)
        qr = _rotary(q[:, cols], cq, sq)
        kr = _rotary(k[:, cols], ck, sk)
        vh = v[:, cols].astype(BF16)
        dm = dmat_ref[hd]
        qd = qdec_ref[hd]
        kd = kdec_ref[hd]
        sd = sdec_ref[hd]
        for c in range(tile // ret_block):
            rows = slice(c * ret_block, (c + 1) * ret_block)
            qb = qr[rows].astype(BF16)
            kb = kr[rows]
            vb = vh[rows]
            state = s_scr[hd]
            scores = _dot_nt(qb, kb.astype(BF16)) * dm
            o = _dot(scores.astype(BF16), vb) + qd * _dot(qb, state.astype(BF16))
            s_scr[hd] = sd * state + _dot_tn((kb * kd).astype(BF16), vb)
            mix_scr[rows, cols] = _gated_head_norm(o, gate[rows, cols])

    uf, vn = _gmlp_inputs(u, vg, lng_ref[...], lnb_ref[...])
    vnb = vn.astype(BF16)
    r_i = lax.broadcasted_iota(I32, (GM_CHUNK, GM_CHUNK), 0)
    c_i = lax.broadcasted_iota(I32, (GM_CHUNK, GM_CHUNK), 1)
    for hd in range(N_HEADS):
        cols = slice(hd * HEAD_DIM, (hd + 1) * HEAD_DIM)
        w_tril = jnp.where(r_i >= c_i, ws_ref[hd], 0.0).astype(BF16)
        bias = bst_ref[:, hd:hd + 1]
        for c in range(tile // GM_CHUNK):
            rows = slice(c * GM_CHUNK, (c + 1) * GM_CHUNK)
            mixed = _dot(w_tril, vnb[rows, cols]) + bias
            mix_scr[rows, RET_WIDTH + hd * HEAD_DIM:RET_WIDTH + (hd + 1) * HEAD_DIM] = (
                uf[rows, cols] * mixed).astype(BF16)

    _route_and_store(x, mix_scr, wout_ref, g2_ref, wrh_ref, wrl_ref, br_ref, tri_ref, upper_ref,
                     x2_ref, h2_ref, lp_ref, tw_ref, ct_ref, of_ref)

    @pl.when(t == pl.num_programs(1) - 1)
    def _():
        st_ref[0] = s_scr[...]


def _decode_kernel(x_ref, s0_ref, g1_ref, win_ref, cq_ref, sq_ref, ck_ref, sk_ref, qdec_ref, kdec_ref,
                   sdec_ref, lng_ref, lnb_ref, ws_ref, bst_ref, wout_ref, g2_ref, wrh_ref, wrl_ref, br_ref,
                   tri_ref, upper_ref,
                   x2_ref, h2_ref, lp_ref, tw_ref, ct_ref, of_ref, st_ref, vn_ref,
                   mix_scr, q_scr, k_scr, v_scr, oc_scr, st_stage, st_sem, *, n_streams, frames):
    tile = n_streams * frames
    x = x_ref[...]
    h = _rmsnorm(x, g1_ref[...]).astype(BF16)

    def proj(i):
        return _dot(h, win_ref[:, i * RET_WIDTH:(i + 1) * RET_WIDTH])

    q = proj(0)
    k = proj(1)
    v = proj(2)
    gate = proj(3)
    u = proj(4)
    vg = proj(5)
    cq = cq_ref[...]
    sq = sq_ref[...]
    ck = ck_ref[...]
    sk = sk_ref[...]

    r_i = lax.broadcasted_iota(I32, (tile, tile), 0)
    c_i = lax.broadcasted_iota(I32, (tile, tile), 1)
    shift = frames.bit_length() - 1
    r_frame = lax.bitwise_and(r_i, frames - 1)
    c_frame = lax.bitwise_and(c_i, frames - 1)
    keep = (lax.shift_right_logical(r_i, shift) == lax.shift_right_logical(c_i, shift)) & (r_frame >= c_frame)
    frame_gap = (r_frame - c_frame).astype(F32)

    o_in = []
    for hd in range(N_HEADS):
        cols = slice(hd * HEAD_DIM, (hd + 1) * HEAD_DIM)
        qr = _rotary(q[:, cols], cq, sq)
        kr = _rotary(k[:, cols], ck, sk)
        vh = v[:, cols].astype(BF16)
        qb = qr.astype(BF16)
        decay = jnp.where(keep, jnp.exp(frame_gap * _LOG_GAMMA[hd]), 0.0)
        scores = _dot_nt(qb, kr.astype(BF16)) * decay
        o_in.append(_dot(scores.astype(BF16), vh))
        q_scr[hd] = qb
        k_scr[hd] = (kr * kdec_ref[hd]).astype(BF16)
        v_scr[hd] = vh

    def state_copy(b):
        slot = lax.rem(b, 2)
        return pltpu.make_async_copy(st_stage.at[slot], st_ref.at[b], st_sem.at[slot])

    def stream_body(b, carry):
        rows = pl.ds(pl.multiple_of(b * frames, frames), frames)
        slot = lax.rem(b, 2)

        @pl.when(b >= 2)
        def _():
            state_copy(b - 2).wait()

        for hd in range(N_HEADS):
            state = s0_ref[b, hd]
            oc_scr[hd, rows, :] = _dot(q_scr[hd, rows, :], state.astype(BF16))
            st_stage[slot, hd] = sdec_ref[hd] * state + _dot_tn(k_scr[hd, rows, :], v_scr[hd, rows, :])
        state_copy(b).start()
        return carry

    lax.fori_loop(0, n_streams, stream_body, 0)
    state_copy(n_streams - 2).wait()
    state_copy(n_streams - 1).wait()

    for hd in range(N_HEADS):
        cols = slice(hd * HEAD_DIM, (hd + 1) * HEAD_DIM)
        o = o_in[hd] + qdec_ref[hd] * oc_scr[hd]
        mix_scr[:, cols] = _gated_head_norm(o, gate[:, cols])

    uf, vn = _gmlp_inputs(u, vg, lng_ref[...], lnb_ref[...])
    vn_ref[...] = vn
    vnb = vn.astype(BF16)
    sel = (lax.bitwise_and(lax.broadcasted_iota(I32, (tile, frames), 0), frames - 1)
           == lax.broadcasted_iota(I32, (tile, frames), 1)).astype(BF16)
    for hd in range(N_HEADS):
        cols = slice(hd * HEAD_DIM, (hd + 1) * HEAD_DIM)
        w_rows = _dot(sel, ws_ref[hd].astype(BF16)).astype(BF16)
        w_blk = jnp.where(keep, _dot_nt(w_rows, sel), 0.0).astype(BF16)
        mixed = _dot(w_blk, vnb[:, cols]) + bst_ref[:, hd:hd + 1]
        mix_scr[:, RET_WIDTH + hd * HEAD_DIM:RET_WIDTH + (hd + 1) * HEAD_DIM] = (uf[:, cols] * mixed).astype(BF16)

    _route_and_store(x, mix_scr, wout_ref, g2_ref, wrh_ref, wrl_ref, br_ref, tri_ref, upper_ref,
                     x2_ref, h2_ref, lp_ref, tw_ref, ct_ref, of_ref)


def _rope_tables(pos):
    half = HEAD_DIM // 2
    inv = ROPE_BASE ** (-jnp.arange(half, dtype=F32) / half)
    ang = pos.astype(F32)[:, None] * inv[None, :]
    cos = jnp.cos(ang)
    sin = jnp.sin(ang)
    cq = jnp.concatenate([cos, cos], axis=-1)
    sq = jnp.concatenate([-sin, sin], axis=-1)
    scale = HEAD_DIM ** -0.5
    return cq, sq, cq * scale, sq * scale


def _decay_tables(block):
    log_g = jnp.log1p(-(2.0 ** (-5.0 - jnp.arange(N_HEADS, dtype=F32))))
    idx = jnp.arange(block, dtype=F32)
    diff = idx[:, None] - idx[None, :]
    dmat = jnp.where(diff[None] >= 0, jnp.exp(jnp.maximum(diff, 0.0)[None] * log_g[:, None, None]), 0.0)
    q_dec = jnp.exp((idx + 1.0)[None, :] * log_g[:, None])
    k_dec = jnp.exp((block - 1.0 - idx)[None, :] * log_g[:, None])
    s_dec = jnp.exp(block * log_g)
    bc = lambda a: jnp.broadcast_to(a[:, :, None], (N_HEADS, block, HEAD_DIM))
    return dmat, bc(q_dec), bc(k_dec), jnp.broadcast_to(s_dec[:, None, None], (N_HEADS, 1, HEAD_DIM))


def _routing_constants(tile):
    tri = (jnp.arange(tile)[:, None] > jnp.arange(tile)[None, :]).astype(BF16)
    upper = (jnp.arange(N_EXPERTS)[:, None] < jnp.arange(N_EXPERTS)[None, :]).astype(BF16)
    return tri, upper


def _front_out(n_tok, n_tiles):
    return [
        jax.ShapeDtypeStruct((n_tok, D_MODEL), F32),
        jax.ShapeDtypeStruct((n_tok * ROW_CHUNKS, LANES), U32),
        jax.ShapeDtypeStruct((n_tok, TOP_K), I32),
        jax.ShapeDtypeStruct((n_tok, TOP_K), F32),
        jax.ShapeDtypeStruct((n_tiles, 1, N_EXPERTS), F32),
        jax.ShapeDtypeStruct((n_tiles, 1, N_EXPERTS), F32),
    ]


def _prompt_call(x, p):
    bsz, seq, _ = x.shape
    tile = TOKEN_TILE
    n_t = seq // tile
    cq, sq, ck, sk = _rope_tables(jnp.arange(seq, dtype=I32))
    dmat, qdec, kdec, sdec = _decay_tables(PROMPT_RET_BLOCK)
    tri, upper = _routing_constants(tile)
    ws = p['w_s'][:, :GM_CHUNK, :GM_CHUNK]
    bst = p['b_s'][:, :GM_CHUNK].T

    const = lambda shape: pl.BlockSpec(shape, lambda b, t: (0,) * len(shape))
    pos = lambda: pl.BlockSpec((tile, HEAD_DIM), lambda b, t: (t, 0))
    in_specs = [
        pl.BlockSpec((1, tile, D_MODEL), lambda b, t: (b, t, 0)),
        const((1, D_MODEL)), const((D_MODEL, 6 * RET_WIDTH)),
        pos(), pos(), pos(), pos(),
        const((N_HEADS, PROMPT_RET_BLOCK, PROMPT_RET_BLOCK)),
        const((N_HEADS, PROMPT_RET_BLOCK, HEAD_DIM)), const((N_HEADS, PROMPT_RET_BLOCK, HEAD_DIM)),
        const((N_HEADS, 1, HEAD_DIM)),
        const((1, GM_WIDTH)), const((1, GM_WIDTH)),
        const((N_HEADS, GM_CHUNK, GM_CHUNK)), const((GM_CHUNK, N_HEADS)),
        const((D_MODEL, D_MODEL)), const((1, D_MODEL)),
        const((D_MODEL, N_EXPERTS)), const((D_MODEL, N_EXPERTS)), const((1, N_EXPERTS)),
        const((tile, tile)), const((N_EXPERTS, N_EXPERTS)),
    ]
    row = lambda b, t: (b * n_t + t, 0)
    row3 = lambda b, t: (b * n_t + t, 0, 0)
    out_specs = [
        pl.BlockSpec((tile, D_MODEL), row),
        pl.BlockSpec((tile * ROW_CHUNKS, LANES), row),
        pl.BlockSpec((tile, TOP_K), row),
        pl.BlockSpec((tile, TOP_K), row),
        pl.BlockSpec((1, 1, N_EXPERTS), row3),
        pl.BlockSpec((1, 1, N_EXPERTS), row3),
        pl.BlockSpec((1, N_HEADS, HEAD_DIM, HEAD_DIM), lambda b, t: (b, 0, 0, 0)),
    ]
    out_shape = _front_out(bsz * seq, bsz * n_t) + [jax.ShapeDtypeStruct((bsz, N_HEADS, HEAD_DIM, HEAD_DIM), F32)]
    return pl.pallas_call(
        functools.partial(_prompt_kernel, ret_block=PROMPT_RET_BLOCK),
        grid=(bsz, n_t),
        in_specs=in_specs,
        out_specs=out_specs,
        out_shape=out_shape,
        scratch_shapes=[
            pltpu.VMEM((N_HEADS, HEAD_DIM, HEAD_DIM), F32),
            pltpu.VMEM((tile, D_MODEL), BF16),
        ],
        compiler_params=pltpu.CompilerParams(
            dimension_semantics=("arbitrary", "arbitrary"), vmem_limit_bytes=VMEM_LIMIT),
        name="front_prompt",
    )(x, p['norm1_g'], p['w_in'], cq, sq, ck, sk, dmat, qdec, kdec, sdec, p['ln_v_g'], p['ln_v_b'], ws, bst,
      p['w_out'], p['norm2_g'], p['wr_hi'], p['wr_lo'], p['b_router'], tri, upper)


def _decode_call(x, state, past_len, p):
    n_streams, frames, _ = x.shape
    tile = n_streams * frames
    assert tile == TOKEN_TILE and frames <= RET_CHUNK and frames & (frames - 1) == 0
    rope = _rope_tables(past_len + jnp.arange(frames, dtype=I32))
    cq, sq, ck, sk = [jnp.tile(a, (n_streams, 1)) for a in rope]
    _, qdec, kdec, sdec = _decay_tables(frames)
    qdec = jnp.tile(qdec, (1, n_streams, 1))
    kdec = jnp.tile(kdec, (1, n_streams, 1))
    tri, upper = _routing_constants(tile)
    ws = p['w_s'][:, :frames, :frames]
    bst = jnp.tile(p['b_s'][:, :frames].T, (n_streams, 1))

    def whole(a):
        return pl.BlockSpec(a.shape, lambda i, n=a.ndim: (0,) * n, pipeline_mode=pl.Buffered(1))

    args = (x.reshape(tile, D_MODEL), state, p['norm1_g'], p['w_in'], cq, sq, ck, sk, qdec, kdec, sdec,
            p['ln_v_g'], p['ln_v_b'], ws, bst, p['w_out'], p['norm2_g'], p['wr_hi'], p['wr_lo'], p['b_router'],
            tri, upper)
    out_shape = _front_out(tile, 1) + [
        jax.ShapeDtypeStruct(state.shape, F32),
        jax.ShapeDtypeStruct((tile, GM_WIDTH), F32),
    ]
    out_specs = [pl.BlockSpec(s.shape, lambda i, n=len(s.shape): (0,) * n) for s in out_shape]
    out_specs[6] = pl.BlockSpec(memory_space=pl.ANY)
    return pl.pallas_call(
        functools.partial(_decode_kernel, n_streams=n_streams, frames=frames),
        grid=(1,),
        in_specs=[whole(a) for a in args],
        out_specs=out_specs,
        out_shape=out_shape,
        scratch_shapes=[
            pltpu.VMEM((tile, D_MODEL), BF16),
            pltpu.VMEM((N_HEADS, tile, HEAD_DIM), BF16),
            pltpu.VMEM((N_HEADS, tile, HEAD_DIM), BF16),
            pltpu.VMEM((N_HEADS, tile, HEAD_DIM), BF16),
            pltpu.VMEM((N_HEADS, tile, HEAD_DIM), F32),
            pltpu.VMEM((2, N_HEADS, HEAD_DIM, HEAD_DIM), F32),
            pltpu.SemaphoreType.DMA((2,)),
        ],
        compiler_params=pltpu.CompilerParams(
            dimension_semantics=("arbitrary",), vmem_limit_bytes=VMEM_LIMIT),
        name="front_decode",
    )(*args)


def _plan_kernel(ct_ref, lower_ref, upper_ref, cum_ref, nb_ref, bs_ref, be_ref, nbt_ref, tlo_ref, thi_ref,
                 *, n_blocks):
    counts = ct_ref[...]
    cum = _exact_count_dot(lower_ref[...], counts, counts_on_left=False)
    cum_ref[...] = cum.astype(I32)
    n_tiles = counts.shape[0]
    total = cum[n_tiles:n_tiles + 1]
    nb = jnp.floor((total + (EXPERT_ROWS - 1)) * (1.0 / EXPERT_ROWS))
    bstart = _dot(jnp.broadcast_to(nb, (8, N_EXPERTS)).astype(BF16), upper_ref[...])[0:1]
    bend = bstart + nb
    nb_ref[...] = nb.astype(I32)
    bs_ref[...] = bstart.astype(I32)
    blk = lax.broadcasted_iota(I32, (n_blocks, N_EXPERTS), 0).astype(F32)
    be = jnp.minimum(jnp.sum((bend <= blk).astype(F32), axis=-1, keepdims=True), N_EXPERTS - 1.0)
    be_ref[...] = be.astype(I32)
    nbt_ref[...] = jnp.sum(nb, axis=-1, keepdims=True).astype(I32)
    mine = lax.broadcasted_iota(I32, (n_blocks, N_EXPERTS), 1).astype(F32) == be
    pick = lambda row: jnp.sum(jnp.where(mine, row, 0.0), axis=-1, keepdims=True)
    r0 = (blk[:, 0:1] - pick(bstart)) * EXPERT_ROWS
    tlo = jnp.zeros((n_blocks, 1), F32)
    thi = jnp.zeros((n_blocks, 1), F32)
    seg_lo = pick(cum[0:1])
    for i in range(n_tiles):
        seg_hi = pick(cum[i + 1:i + 2])
        tlo = tlo + (seg_hi <= r0).astype(F32)
        thi = thi + (seg_lo < r0 + EXPERT_ROWS).astype(F32)
        seg_lo = seg_hi
    tlo_ref[...] = tlo.astype(I32)
    thi_ref[...] = thi.astype(I32)


def _plan_call(counts, n_blocks):
    n_tiles = counts.shape[0]
    lower = (jnp.arange(n_tiles + 1)[:, None] > jnp.arange(n_tiles)[None, :]).astype(BF16)
    upper = (jnp.arange(N_EXPERTS)[:, None] < jnp.arange(N_EXPERTS)[None, :]).astype(BF16)
    return pl.pallas_call(
        functools.partial(_plan_kernel, n_blocks=n_blocks),
        out_shape=[
            jax.ShapeDtypeStruct((n_tiles + 1, N_EXPERTS), I32),
            jax.ShapeDtypeStruct((1, N_EXPERTS), I32),
            jax.ShapeDtypeStruct((1, N_EXPERTS), I32),
            jax.ShapeDtypeStruct((n_blocks, 1), I32),
            jax.ShapeDtypeStruct((1, 1), I32),
            jax.ShapeDtypeStruct((n_blocks, 1), I32),
            jax.ShapeDtypeStruct((n_blocks, 1), I32),
        ],
        name="plan",
    )(counts, lower, upper)


def _dispatch_kernel(h2p_ref, h2s_ref, lp_ref, out_ref, *, n_prompt_tiles):
    i = pl.program_id(0)

    def permute(src_ref):
        def body(t, c):
            row = src_ref[_row_tile(t), :]
            for kk in range(TOP_K):
                out_ref[_row_tile(lp_ref[0, 0, t * TOP_K + kk]), :] = row
            return c

        lax.fori_loop(0, TOKEN_TILE, body, 0, unroll=8)

    @pl.when(i < n_prompt_tiles)
    def _():
        permute(h2p_ref)

    @pl.when(i >= n_prompt_tiles)
    def _():
        permute(h2s_ref)


def _dispatch_call(h2_p, h2_s, lp_tiles):
    rows = TOKEN_TILE * ROW_CHUNKS
    n_p_tiles = h2_p.shape[0] // rows
    n_s_tiles = h2_s.shape[0] // rows
    n_tiles = n_p_tiles + n_s_tiles
    return pl.pallas_call(
        functools.partial(_dispatch_kernel, n_prompt_tiles=n_p_tiles),
        grid=(n_tiles,),
        in_specs=[
            pl.BlockSpec((rows, LANES), lambda i: (jnp.minimum(i, n_p_tiles - 1), 0)),
            pl.BlockSpec((rows, LANES), lambda i: (jnp.maximum(i - n_p_tiles, 0), 0)),
            pl.BlockSpec((1, 1, TILE_SLOTS), lambda i: (i, 0, 0), memory_space=pltpu.SMEM),
        ],
        out_specs=pl.BlockSpec((TILE_SLOTS * ROW_CHUNKS, LANES), lambda i: (i, 0)),
        out_shape=jax.ShapeDtypeStruct((n_tiles * TILE_SLOTS * ROW_CHUNKS, LANES), U32),
        compiler_params=pltpu.CompilerParams(
            dimension_semantics=("arbitrary",), vmem_limit_bytes=VMEM_LIMIT),
        name="dispatch",
    )(h2_p, h2_s, lp_tiles)


def _expert_kernel(be_ref, bs_ref, nb_ref, nbt_ref, cum_ref, off_ref, tlo_ref, thi_ref,
                   xb_ref, wg_ref, wu_ref, wd_ref, bg_ref, bu_ref, bd_ref,
                   yb_ref, xbuf, ybuf, wbf, in_sem, out_sem, *, n_tiles):
    b = pl.program_id(0)
    nbt = nbt_ref[0]

    def block_rows(blk):
        e = be_ref[blk]
        r0 = (blk - bs_ref[e]) * EXPERT_ROWS
        return r0, jnp.minimum(cum_ref[n_tiles * N_EXPERTS + e] - r0, EXPERT_ROWS)

    def segments(blk, slot, fn):
        e = be_ref[blk]
        r0 = (blk - bs_ref[e]) * EXPERT_ROWS

        def body(i, c):
            seg_lo = cum_ref[i * N_EXPERTS + e]
            seg_hi = cum_ref[(i + 1) * N_EXPERTS + e]
            lo = jnp.maximum(seg_lo, r0)
            hi = jnp.minimum(seg_hi, r0 + EXPERT_ROWS)

            @pl.when(hi > lo)
            def _():
                bucket_row = i * TILE_SLOTS + off_ref[i * N_EXPERTS + e] + (lo - seg_lo)
                fn(slot, bucket_row, slot * EXPERT_ROWS + (lo - r0), hi - lo)
            return c

        lax.fori_loop(tlo_ref[blk], thi_ref[blk], body, 0)

    def words(row, n):
        return pl.ds(pl.multiple_of(row * ROW_CHUNKS, ROW_CHUNKS), n * ROW_CHUNKS)

    def gather(slot, bucket_row, buffer_row, n):
        return pltpu.make_async_copy(
            xb_ref.at[words(bucket_row, n)], xbuf.at[words(buffer_row, n)], in_sem.at[slot])

    def scatter(slot, bucket_row, buffer_row, n):
        return pltpu.make_async_copy(
            ybuf.at[words(buffer_row, n)], yb_ref.at[words(bucket_row, n)], out_sem.at[slot])

    def start_gather(blk):
        @pl.when(blk < nbt)
        def _():
            segments(blk, lax.rem(blk, 2), lambda *a: gather(*a).start())

    def wait_gather(blk):
        @pl.when(blk < nbt)
        def _():
            slot = lax.rem(blk, 2)
            gather(slot, 0, slot * EXPERT_ROWS, block_rows(blk)[1]).wait()

    def start_scatter(blk):
        @pl.when(blk < nbt)
        def _():
            segments(blk, lax.rem(blk, 2), lambda *a: scatter(*a).start())

    def wait_scatter(blk):
        @pl.when((blk >= 0) & (blk < nbt))
        def _():
            slot = lax.rem(blk, 2)
            scatter(slot, 0, slot * EXPERT_ROWS, block_rows(blk)[1]).wait()

    def mlp(base, n_rows, rows_left):
        valid = lax.broadcasted_iota(I32, (n_rows, 1), 0) < rows_left
        x = jnp.where(valid, _load_rows(xbuf, base, n_rows), 0.0)
        gt = _dot(x, wbf[0]) + bg_ref[0]
        up = _dot(x, wbf[1]) + bu_ref[0]
        gt = jnp.minimum(gt, SWIGLU_LIMIT)
        up = jnp.clip(up, -SWIGLU_LIMIT, SWIGLU_LIMIT)
        act = gt * _sigmoid(gt * SWIGLU_ALPHA) * (up + 1.0)
        _store_rows(ybuf, base, _dot(act.astype(BF16), wbf[2]) + bd_ref[0])

    def block_body(blk, carry):
        wait_gather(blk)
        start_gather(blk + 1)
        wait_scatter(blk - 2)
        rows_left = block_rows(blk)[1]
        base = lax.rem(blk, 2) * EXPERT_ROWS

        @pl.when(rows_left > EXPERT_SUB_ROWS)
        def _():
            mlp(base, EXPERT_ROWS, rows_left)

        @pl.when(rows_left <= EXPERT_SUB_ROWS)
        def _():
            mlp(base, EXPERT_SUB_ROWS, rows_left)

        start_scatter(blk)
        return carry

    first_block = bs_ref[b]
    n_mine = nb_ref[b]

    @pl.when(b == 0)
    def _():
        xbuf[...] = jnp.zeros_like(xbuf)
        start_gather(0)

    @pl.when(n_mine > 0)
    def _():
        wbf[0] = wg_ref[0].astype(BF16)
        wbf[1] = wu_ref[0].astype(BF16)
        wbf[2] = wd_ref[0].astype(BF16)

    lax.fori_loop(first_block, first_block + n_mine, block_body, 0)

    @pl.when(b == N_EXPERTS - 1)
    def _():
        wait_scatter(nbt - 2)
        wait_scatter(nbt - 1)


def _expert_call(be, bstart, nb, nbt, cum_flat, off_flat, tlo, thi, buckets, p, n_tiles):
    wspec = pl.BlockSpec((1, D_MODEL, D_MODEL), lambda e, *_: (e, 0, 0))
    bspec = pl.BlockSpec((1, 1, D_MODEL), lambda e, *_: (e, 0, 0))
    any_spec = pl.BlockSpec(memory_space=pl.ANY)
    return pl.pallas_call(
        functools.partial(_expert_kernel, n_tiles=n_tiles),
        grid_spec=pltpu.PrefetchScalarGridSpec(
            num_scalar_prefetch=8,
            grid=(N_EXPERTS,),
            in_specs=[any_spec, wspec, wspec, wspec, bspec, bspec, bspec],
            out_specs=any_spec,
            scratch_shapes=[
                pltpu.VMEM((2 * EXPERT_ROWS * ROW_CHUNKS, LANES), U32),
                pltpu.VMEM((2 * EXPERT_ROWS * ROW_CHUNKS, LANES), U32),
                pltpu.VMEM((3, D_MODEL, D_MODEL), BF16),
                pltpu.SemaphoreType.DMA((2,)),
                pltpu.SemaphoreType.DMA((2,)),
            ],
        ),
        out_shape=jax.ShapeDtypeStruct(buckets.shape, U32),
        compiler_params=pltpu.CompilerParams(
            dimension_semantics=("arbitrary",), vmem_limit_bytes=VMEM_LIMIT),
        name="experts",
    )(be, bstart, nb, nbt, cum_flat, off_flat, tlo, thi, buckets, p['w_gate'], p['w_up'], p['w_down'],
      p['b_gate'][:, None, :], p['b_up'][:, None, :], p['b_down'][:, None, :])


def _combine_kernel(x2_ref, yb_ref, lp_ref, tw_ref, gf_ref, y_ref, ybf, pick):
    for c in range(TILE_SLOTS // COMBINE_CHUNK):
        rows = slice(c * COMBINE_CHUNK, (c + 1) * COMBINE_CHUNK)
        ybf[rows, :] = _load_rows(yb_ref, c * COMBINE_CHUNK, COMBINE_CHUNK)
    slot = lax.broadcasted_iota(I32, (TOKEN_TILE, TILE_SLOTS), 1)
    lp = lp_ref[...]
    tw = tw_ref[...]
    sel = jnp.zeros((TOKEN_TILE, TILE_SLOTS), F32)
    for kk in range(TOP_K):
        sel = jnp.where(slot == lp[:, kk:kk + 1], tw[:, kk:kk + 1], sel)
    pick[...] = sel.astype(BF16)
    y_ref[...] = _rmsnorm(x2_ref[...] + _dot(pick[...], ybf[...]), gf_ref[...])


def _combine_call(x2, ybuckets, lp, tw, gf, tile_off):
    n_tok = x2.shape[0]
    n_tiles = n_tok // TOKEN_TILE
    return pl.pallas_call(
        _combine_kernel,
        grid=(n_tiles,),
        in_specs=[
            pl.BlockSpec((TOKEN_TILE, D_MODEL), lambda i: (i, 0)),
            pl.BlockSpec((TILE_SLOTS * ROW_CHUNKS, LANES), lambda i: (tile_off + i, 0)),
            pl.BlockSpec((TOKEN_TILE, TOP_K), lambda i: (i, 0)),
            pl.BlockSpec((TOKEN_TILE, TOP_K), lambda i: (i, 0)),
            pl.BlockSpec((1, D_MODEL), lambda i: (0, 0)),
        ],
        out_specs=pl.BlockSpec((TOKEN_TILE, D_MODEL), lambda i: (i, 0)),
        scratch_shapes=[
            pltpu.VMEM((TILE_SLOTS, D_MODEL), BF16),
            pltpu.VMEM((TOKEN_TILE, TILE_SLOTS), BF16),
        ],
        out_shape=jax.ShapeDtypeStruct((n_tok, D_MODEL), F32),
        compiler_params=pltpu.CompilerParams(
            dimension_semantics=("arbitrary",), vmem_limit_bytes=VMEM_LIMIT),
        name="combine",
    )(x2, ybuckets, lp, tw, gf)


def kernel(x_prompt, x_sample, state_ret, norm1_g, w_in, ln_v_g, ln_v_b, w_s, b_s, w_out, norm2_g, w_router, b_router, w_gate, b_gate, w_up, b_up, w_down, b_down, norm_f_g):
    bsz, seq, _ = x_prompt.shape
    dbsz, dseq, _ = x_sample.shape
    n_p = bsz * seq
    n_s = dbsz * dseq
    n_total = n_p + n_s
    past_len = 2048
    assert w_in.shape[0] == 1, "single layer"

    wr = w_router[0]
    wr_hi = wr.astype(BF16)
    wr_lo = (wr - wr_hi.astype(F32)).astype(BF16)
    p = dict(norm1_g=norm1_g, w_in=w_in[0].astype(BF16), ln_v_g=ln_v_g, ln_v_b=ln_v_b, w_s=w_s[0], b_s=b_s[0],
             w_out=w_out[0].astype(BF16), norm2_g=norm2_g, wr_hi=wr_hi, wr_lo=wr_lo, b_router=b_router,
             w_gate=w_gate[0], b_gate=b_gate[0], w_up=w_up[0], b_up=b_up[0], w_down=w_down[0], b_down=b_down[0])

    x2_p, h2_p, lp_p, tw_p, ct_p, of_p, st_p = _prompt_call(x_prompt, p)
    x2_s, h2_s, lp_s, tw_s, ct_s, of_s, st_s, vn_s = _decode_call(x_sample, state_ret[0], past_len, p)

    n_tiles = n_total // TOKEN_TILE
    lp = jnp.concatenate([lp_p, lp_s], axis=0).reshape(n_tiles, 1, TILE_SLOTS)
    counts = jnp.concatenate([ct_p, ct_s], axis=0).reshape(n_tiles, N_EXPERTS)
    offsets = jnp.concatenate([of_p, of_s], axis=0).reshape(n_tiles * N_EXPERTS).astype(I32)

    n_blocks = -(-(n_total * TOP_K + N_EXPERTS * (EXPERT_ROWS - 1)) // EXPERT_ROWS)
    cum, nb, bstart, be, nbt, tlo, thi = _plan_call(counts, n_blocks)

    buckets = _dispatch_call(h2_p, h2_s, lp)
    ybuckets = _expert_call(be.reshape(n_blocks), bstart.reshape(N_EXPERTS), nb.reshape(N_EXPERTS), nbt.reshape(1),
                            cum.reshape((n_tiles + 1) * N_EXPERTS), offsets, tlo.reshape(n_blocks),
                            thi.reshape(n_blocks), buckets, p, n_tiles)

    y_p = _combine_call(x2_p, ybuckets, lp_p, tw_p, norm_f_g[None, :], 0)
    y_s = _combine_call(x2_s, ybuckets, lp_s, tw_s, norm_f_g[None, :], n_p // TOKEN_TILE)

    return (y_p.reshape(bsz, seq, D_MODEL), y_s.reshape(dbsz, dseq, D_MODEL),
            st_p[None], st_s[None], vn_s.reshape(1, dbsz, dseq, GM_WIDTH))
```

```python
import functools

import numpy as np
import jax
import jax.numpy as jnp
from jax import lax
from jax.experimental import pallas as pl
from jax.experimental.pallas import tpu as pltpu

F32 = jnp.float32
BF16 = jnp.bfloat16
I32 = jnp.int32

D_MODEL = 1024
RET_WIDTH = 512
N_HEADS = 4
HEAD_DIM = 128
GM_WIDTH = 512
GM_CHUNK = 128
RET_CHUNK = 64
N_EXPERTS = 32
TOP_K = 4
SWIGLU_LIMIT = 7.0
SWIGLU_ALPHA = 1.702
ROPE_BASE = 10000.0
EPS = 1e-6

TOKEN_TILE = 512
PROMPT_RET_BLOCK = 256
EXPERT_ROWS = 512
EXPERT_SUB_ROWS = 256
TILE_SLOTS = TOKEN_TILE * TOP_K
COMBINE_CHUNK = 256
VMEM_LIMIT = 56 * 1024 * 1024
LANES = 128
U32 = jnp.uint32
ROW_CHUNKS = D_MODEL // LANES // 2
SPLIT = 16.0
_LOG_GAMMA = [float(np.log1p(-np.float32(2.0) ** np.float32(-5.0 - hd)).astype(np.float32)) for hd in range(N_HEADS)]


def _load_rows(ref, first_row, n_rows):
    base = first_row * ROW_CHUNKS
    lo, hi = [], []
    for c in range(ROW_CHUNKS):
        w = ref[pl.ds(base + c, n_rows, stride=ROW_CHUNKS), :]
        lo.append(pltpu.bitcast(lax.shift_left(w, jnp.uint32(16)), F32))
        hi.append(pltpu.bitcast(lax.bitwise_and(w, jnp.uint32(0xFFFF0000)), F32))
    return jnp.concatenate(lo + hi, axis=1).astype(BF16)


def _store_rows(ref, first_row, val):
    base = first_row * ROW_CHUNKS
    bits = pltpu.bitcast(val.astype(BF16).astype(F32), U32)
    half = D_MODEL // 2
    for c in range(ROW_CHUNKS):
        low = lax.shift_right_logical(bits[:, c * LANES:(c + 1) * LANES], jnp.uint32(16))
        ref[pl.ds(base + c, val.shape[0], stride=ROW_CHUNKS), :] = lax.bitwise_or(
            bits[:, half + c * LANES:half + (c + 1) * LANES], low)


def _row_tile(row):
    return pl.ds(pl.multiple_of(row * ROW_CHUNKS, ROW_CHUNKS), ROW_CHUNKS)


def _rmsnorm(x, g):
    ms = jnp.mean(x * x, axis=-1, keepdims=True)
    return x * lax.rsqrt(ms + EPS) * g


def _gelu(x):
    c = np.float32(np.sqrt(2.0 / np.pi))
    return x * (0.5 * (1.0 + jnp.tanh(c * (x + 0.044715 * (x * x * x)))))


def _sigmoid(x):
    return 1.0 / (1.0 + jnp.exp(-x))


def _dot(a, b):
    return jnp.dot(a, b, preferred_element_type=F32)


def _dot_nt(a, b):
    return lax.dot_general(a, b, (((1,), (1,)), ((), ())), preferred_element_type=F32)


def _dot_tn(a, b):
    return lax.dot_general(a, b, (((0,), (0,)), ((), ())), preferred_element_type=F32)


def _exact_count_dot(a, b, *, counts_on_left):
    cnt = a if counts_on_left else b
    hi = jnp.floor(cnt * (1.0 / SPLIT))
    lo = cnt - SPLIT * hi
    if counts_on_left:
        return SPLIT * _dot(hi.astype(BF16), b) + _dot(lo.astype(BF16), b)
    return SPLIT * _dot(a, hi.astype(BF16)) + _dot(a, lo.astype(BF16))


def _rotary(xh, cos2, sin2):
    return xh * cos2 + pltpu.roll(xh, HEAD_DIM // 2, axis=1) * sin2


def _gated_head_norm(o, gate):
    mu = jnp.mean(o, axis=-1, keepdims=True)
    oc = o - mu
    var = jnp.mean(oc * oc, axis=-1, keepdims=True)
    return ((gate * _sigmoid(gate)) * (oc * lax.rsqrt(var + EPS))).astype(BF16)


def _gmlp_inputs(u, vg, lng, lnb):
    uf = _gelu(u)
    vf = _gelu(vg)
    mu = jnp.mean(vf, axis=-1, keepdims=True)
    vc = vf - mu
    var = jnp.mean(vc * vc, axis=-1, keepdims=True)
    return uf, vc * lax.rsqrt(var + EPS) * lng + lnb


def _route_and_store(x, mix_scr, wout_ref, g2_ref, wrh_ref, wrl_ref, br_ref, tri_ref, upper_ref,
                     x2_ref, h2_ref, lp_ref, tw_ref, ct_ref, of_ref):
    h2 = _out_proj(x, mix_scr, wout_ref, g2_ref, x2_ref, h2_ref)
    _route(h2, wrh_ref, wrl_ref, br_ref, tri_ref, upper_ref, lp_ref, tw_ref, ct_ref, of_ref)


def _out_proj(x, mix_scr, wout_ref, g2_ref, x2_ref, h2_ref):
    x2 = x + _dot(mix_scr[...], wout_ref[...])
    x2_ref[...] = x2
    h2 = _rmsnorm(x2, g2_ref[...])
    _store_rows(h2_ref, 0, h2)
    return h2


def _route(h2, wrh_ref, wrl_ref, br_ref, tri_ref, upper_ref, lp_ref, tw_ref, ct_ref, of_ref):
    _route_place(_route_choose(h2, wrh_ref, wrl_ref, br_ref), tri_ref, upper_ref, lp_ref, tw_ref, ct_ref, of_ref)


def _route_choose(h2, wrh_ref, wrl_ref, br_ref):
    tile = h2.shape[0]
    h_hi = h2.astype(BF16)
    h_lo = (h2 - h_hi.astype(F32)).astype(BF16)
    logits = (_dot(h_hi, wrh_ref[...]) + _dot(h_lo, wrh_ref[...]) + _dot(h_hi, wrl_ref[...])) + br_ref[...]

    iota_e = lax.broadcasted_iota(I32, (tile, N_EXPERTS), 1)
    iota_ef = iota_e.astype(F32)
    lg = logits
    vals = []
    idxs = []
    for _ in range(TOP_K):
        m = jnp.max(lg, axis=-1, keepdims=True)
        idx = jnp.min(jnp.where(lg == m, iota_ef, float(N_EXPERTS)), axis=-1, keepdims=True).astype(I32)
        vals.append(m)
        idxs.append(idx)
        lg = jnp.where(iota_e == idx, -jnp.inf, lg)
    exps = [jnp.exp(vk - vals[0]) for vk in vals]
    denom = exps[0] + exps[1] + exps[2] + exps[3]
    return idxs, exps, denom


def _route_place(choice, tri_ref, upper_ref, lp_ref, tw_ref, ct_ref, of_ref):
    idxs, exps, denom = choice
    tile = denom.shape[0]
    iota_e = lax.broadcasted_iota(I32, (tile, N_EXPERTS), 1)
    iota_k = lax.broadcasted_iota(I32, (tile, TOP_K), 1)
    onehot = jnp.zeros((tile, N_EXPERTS), F32)
    for idx in idxs:
        onehot = onehot + (iota_e == idx).astype(F32)
    rank = _dot(tri_ref[...], onehot.astype(BF16))
    count = jnp.sum(onehot, axis=0, keepdims=True)
    offset = _exact_count_dot(jnp.broadcast_to(count, (8, N_EXPERTS)), upper_ref[...], counts_on_left=True)[0:1]
    where_to = rank + offset
    lp = jnp.zeros((tile, TOP_K), I32)
    tw = jnp.zeros((tile, TOP_K), F32)
    for kk in range(TOP_K):
        pos = jnp.sum(jnp.where(iota_e == idxs[kk], where_to, 0.0), axis=-1, keepdims=True).astype(I32)
        lp = jnp.where(iota_k == kk, pos, lp)
        tw = jnp.where(iota_k == kk, exps[kk] / denom, tw)
    lp_ref[...] = lp
    tw_ref[...] = tw
    ct_ref[0] = count
    of_ref[0] = offset


def _prompt_kernel(x_ref, g1_ref, win_ref, cq_ref, sq_ref, ck_ref, sk_ref, dmat_ref, qdec_ref, kdec_ref,
                   sdec_ref, lng_ref, lnb_ref, ws_ref, bst_ref, wout_ref, g2_ref, wrh_ref, wrl_ref, br_ref,
                   tri_ref, upper_ref,
                   x2_ref, h2_ref, lp_ref, tw_ref, ct_ref, of_ref, st_ref,
                   s_scr, s_prev, mix_scr, x2_carry, *, ret_block, tiles_per_seq, n_tiles):
    tile = TOKEN_TILE
    step = pl.program_id(0)

    @pl.when(step == 0)
    def _():
        s_scr[...] = jnp.zeros_like(s_scr)
        s_prev[...] = jnp.zeros_like(s_prev)
        x2_carry[...] = jnp.zeros_like(x2_carry)

    h2_prev = _rmsnorm(x2_carry[...], g2_ref[...])
    _store_rows(h2_ref, 0, h2_prev)
    choice = _route_choose(h2_prev, wrh_ref, wrl_ref, br_ref)

    redo = step == n_tiles
    t = lax.rem(jnp.minimum(step, n_tiles - 1), tiles_per_seq)
    start = jnp.where(redo, s_prev[...], jnp.where(t == 0, 0.0, s_scr[...]))
    s_prev[...] = start
    s_scr[...] = start

    x = x_ref[0]
    h = _rmsnorm(x, g1_ref[...]).astype(BF16)

    def proj(i):
        return _dot(h, win_ref[:, i * RET_WIDTH:(i + 1) * RET_WIDTH])

    q = proj(0)
    k = proj(1)
    v = proj(2)
    gate = proj(3)
    u = proj(4)
    vg = proj(5)
    _route_place(choice, tri_ref, upper_ref, lp_ref, tw_ref, ct_ref, of_ref)
    cq = cq_ref[...]
    sq = sq_ref[...]
    ck = ck_ref[...]
    sk = sk_ref[...]

    for hd in range(N_HEADS):
        cols = slice(hd * HEAD_DIM, (hd + 1) * HEAD_DIM)
        qr = _rotary(q[:, cols], cq, sq)
        kr = _rotary(k[:, cols], ck, sk)
        vh = v[:, cols].astype(BF16)
        dm = dmat_ref[hd]
        qd = qdec_ref[hd]
        kd = kdec_ref[hd]
        sd = sdec_ref[hd]
        for c in range(tile // ret_block):
            rows = slice(c * ret_block, (c + 1) * ret_block)
            qb = qr[rows].astype(BF16)
            kb = kr[rows]
            vb = vh[rows]
            state = s_scr[hd]
            scores = _dot_nt(qb, kb.astype(BF16)) * dm
            o = _dot(scores.astype(BF16), vb) + qd * _dot(qb, state.astype(BF16))
            s_scr[hd] = sd * state + _dot_tn((kb * kd).astype(BF16), vb)
            mix_scr[rows, cols] = _gated_head_norm(o, gate[rows, cols])

    uf, vn = _gmlp_inputs(u, vg, lng_ref[...], lnb_ref[...])
    vnb = vn.astype(BF16)
    r_i = lax.broadcasted_iota(I32, (GM_CHUNK, GM_CHUNK), 0)
    c_i = lax.broadcasted_iota(I32, (GM_CHUNK, GM_CHUNK), 1)
    for hd in range(N_HEADS):
        cols = slice(hd * HEAD_DIM, (hd + 1) * HEAD_DIM)
        w_tril = jnp.where(r_i >= c_i, ws_ref[hd], 0.0).astype(BF16)
        bias = bst_ref[:, hd:hd + 1]
        for c in range(tile // GM_CHUNK):
            rows = slice(c * GM_CHUNK, (c + 1) * GM_CHUNK)
            mixed = _dot(w_tril, vnb[rows, cols]) + bias
            mix_scr[rows, RET_WIDTH + hd * HEAD_DIM:RET_WIDTH + (hd + 1) * HEAD_DIM] = (
                uf[rows, cols] * mixed).astype(BF16)

    x2 = x + _dot(mix_scr[...], wout_ref[...])
    x2_ref[...] = x2
    x2_carry[...] = x2

    @pl.when(t == tiles_per_seq - 1)
    def _():
        st_ref[0] = s_scr[...]


def _decode_kernel(x_ref, s0_ref, g1_ref, win_ref, cq_ref, sq_ref, ck_ref, sk_ref, qdec_ref, kdec_ref,
                   sdec_ref, lng_ref, lnb_ref, ws_ref, bst_ref, wout_ref, g2_ref, wrh_ref, wrl_ref, br_ref,
                   tri_ref, upper_ref,
                   x2_ref, h2_ref, lp_ref, tw_ref, ct_ref, of_ref, st_ref, vn_ref,
                   mix_scr, q_scr, k_scr, v_scr, oc_scr, st_stage, st_sem, *, n_streams, frames):
    tile = n_streams * frames
    x = x_ref[...]
    h = _rmsnorm(x, g1_ref[...]).astype(BF16)

    def proj(i):
        return _dot(h, win_ref[:, i * RET_WIDTH:(i + 1) * RET_WIDTH])

    q = proj(0)
    k = proj(1)
    v = proj(2)
    gate = proj(3)
    u = proj(4)
    vg = proj(5)
    cq = cq_ref[...]
    sq = sq_ref[...]
    ck = ck_ref[...]
    sk = sk_ref[...]

    r_i = lax.broadcasted_iota(I32, (tile, tile), 0)
    c_i = lax.broadcasted_iota(I32, (tile, tile), 1)
    shift = frames.bit_length() - 1
    r_frame = lax.bitwise_and(r_i, frames - 1)
    c_frame = lax.bitwise_and(c_i, frames - 1)
    keep = (lax.shift_right_logical(r_i, shift) == lax.shift_right_logical(c_i, shift)) & (r_frame >= c_frame)
    frame_gap = (r_frame - c_frame).astype(F32)

    o_in = []
    for hd in range(N_HEADS):
        cols = slice(hd * HEAD_DIM, (hd + 1) * HEAD_DIM)
        qr = _rotary(q[:, cols], cq, sq)
        kr = _rotary(k[:, cols], ck, sk)
        vh = v[:, cols].astype(BF16)
        qb = qr.astype(BF16)
        decay = jnp.where(keep, jnp.exp(frame_gap * _LOG_GAMMA[hd]), 0.0)
        scores = _dot_nt(qb, kr.astype(BF16)) * decay
        o_in.append(_dot(scores.astype(BF16), vh))
        q_scr[hd] = qb
        k_scr[hd] = (kr * kdec_ref[hd]).astype(BF16)
        v_scr[hd] = vh

    def state_copy(b):
        slot = lax.rem(b, 2)
        return pltpu.make_async_copy(st_stage.at[slot], st_ref.at[b], st_sem.at[slot])

    def stream_body(b, carry):
        rows = pl.ds(pl.multiple_of(b * frames, frames), frames)
        slot = lax.rem(b, 2)

        @pl.when(b >= 2)
        def _():
            state_copy(b - 2).wait()

        for hd in range(N_HEADS):
            state = s0_ref[b, hd]
            oc_scr[hd, rows, :] = _dot(q_scr[hd, rows, :], state.astype(BF16))
            st_stage[slot, hd] = sdec_ref[hd] * state + _dot_tn(k_scr[hd, rows, :], v_scr[hd, rows, :])
        state_copy(b).start()
        return carry

    lax.fori_loop(0, n_streams, stream_body, 0)
    state_copy(n_streams - 2).wait()
    state_copy(n_streams - 1).wait()

    for hd in range(N_HEADS):
        cols = slice(hd * HEAD_DIM, (hd + 1) * HEAD_DIM)
        o = o_in[hd] + qdec_ref[hd] * oc_scr[hd]
        mix_scr[:, cols] = _gated_head_norm(o, gate[:, cols])

    uf, vn = _gmlp_inputs(u, vg, lng_ref[...], lnb_ref[...])
    vn_ref[...] = vn
    vnb = vn.astype(BF16)
    sel = (lax.bitwise_and(lax.broadcasted_iota(I32, (tile, frames), 0), frames - 1)
           == lax.broadcasted_iota(I32, (tile, frames), 1)).astype(BF16)
    for hd in range(N_HEADS):
        cols = slice(hd * HEAD_DIM, (hd + 1) * HEAD_DIM)
        w_rows = _dot(sel, ws_ref[hd].astype(BF16)).astype(BF16)
        w_blk = jnp.where(keep, _dot_nt(w_rows, sel), 0.0).astype(BF16)
        mixed = _dot(w_blk, vnb[:, cols]) + bst_ref[:, hd:hd + 1]
        mix_scr[:, RET_WIDTH + hd * HEAD_DIM:RET_WIDTH + (hd + 1) * HEAD_DIM] = (uf[:, cols] * mixed).astype(BF16)

    _route_and_store(x, mix_scr, wout_ref, g2_ref, wrh_ref, wrl_ref, br_ref, tri_ref, upper_ref,
                     x2_ref, h2_ref, lp_ref, tw_ref, ct_ref, of_ref)


def _rope_tables(pos):
    half = HEAD_DIM // 2
    inv = ROPE_BASE ** (-jnp.arange(half, dtype=F32) / half)
    ang = pos.astype(F32)[:, None] * inv[None, :]
    cos = jnp.cos(ang)
    sin = jnp.sin(ang)
    cq = jnp.concatenate([cos, cos], axis=-1)
    sq = jnp.concatenate([-sin, sin], axis=-1)
    scale = HEAD_DIM ** -0.5
    return cq, sq, cq * scale, sq * scale


def _decay_tables(block):
    log_g = jnp.log1p(-(2.0 ** (-5.0 - jnp.arange(N_HEADS, dtype=F32))))
    idx = jnp.arange(block, dtype=F32)
    diff = idx[:, None] - idx[None, :]
    dmat = jnp.where(diff[None] >= 0, jnp.exp(jnp.maximum(diff, 0.0)[None] * log_g[:, None, None]), 0.0)
    q_dec = jnp.exp((idx + 1.0)[None, :] * log_g[:, None])
    k_dec = jnp.exp((block - 1.0 - idx)[None, :] * log_g[:, None])
    s_dec = jnp.exp(block * log_g)
    bc = lambda a: jnp.broadcast_to(a[:, :, None], (N_HEADS, block, HEAD_DIM))
    return dmat, bc(q_dec), bc(k_dec), jnp.broadcast_to(s_dec[:, None, None], (N_HEADS, 1, HEAD_DIM))


def _routing_constants(tile):
    tri = (jnp.arange(tile)[:, None] > jnp.arange(tile)[None, :]).astype(BF16)
    upper = (jnp.arange(N_EXPERTS)[:, None] < jnp.arange(N_EXPERTS)[None, :]).astype(BF16)
    return tri, upper


def _front_out(n_tok, n_tiles):
    return [
        jax.ShapeDtypeStruct((n_tok, D_MODEL), F32),
        jax.ShapeDtypeStruct((n_tok * ROW_CHUNKS, LANES), U32),
        jax.ShapeDtypeStruct((n_tok, TOP_K), I32),
        jax.ShapeDtypeStruct((n_tok, TOP_K), F32),
        jax.ShapeDtypeStruct((n_tiles, 1, N_EXPERTS), F32),
        jax.ShapeDtypeStruct((n_tiles, 1, N_EXPERTS), F32),
    ]


def _prompt_call(x, p):
    bsz, seq, _ = x.shape
    tile = TOKEN_TILE
    n_t = seq // tile
    cq, sq, ck, sk = _rope_tables(jnp.arange(seq, dtype=I32))
    dmat, qdec, kdec, sdec = _decay_tables(PROMPT_RET_BLOCK)
    tri, upper = _routing_constants(tile)
    ws = p['w_s'][:, :GM_CHUNK, :GM_CHUNK]
    bst = p['b_s'][:, :GM_CHUNK].T

    n_tiles = bsz * n_t
    cur = lambda s: jnp.minimum(s, n_tiles - 1)
    prev = lambda s: jnp.maximum(s - 1, 0)
    const = lambda shape: pl.BlockSpec(shape, lambda s: (0,) * len(shape))
    pos = lambda: pl.BlockSpec((tile, HEAD_DIM), lambda s: (lax.rem(cur(s), n_t), 0))
    in_specs = [
        pl.BlockSpec((1, tile, D_MODEL), lambda s: (cur(s) // n_t, lax.rem(cur(s), n_t), 0)),
        const((1, D_MODEL)), const((D_MODEL, 6 * RET_WIDTH)),
        pos(), pos(), pos(), pos(),
        const((N_HEADS, PROMPT_RET_BLOCK, PROMPT_RET_BLOCK)),
        const((N_HEADS, PROMPT_RET_BLOCK, HEAD_DIM)), const((N_HEADS, PROMPT_RET_BLOCK, HEAD_DIM)),
        const((N_HEADS, 1, HEAD_DIM)),
        const((1, GM_WIDTH)), const((1, GM_WIDTH)),
        const((N_HEADS, GM_CHUNK, GM_CHUNK)), const((GM_CHUNK, N_HEADS)),
        const((D_MODEL, D_MODEL)), const((1, D_MODEL)),
        const((D_MODEL, N_EXPERTS)), const((D_MODEL, N_EXPERTS)), const((1, N_EXPERTS)),
        const((tile, tile)), const((N_EXPERTS, N_EXPERTS)),
    ]
    out_specs = [
        pl.BlockSpec((tile, D_MODEL), lambda s: (cur(s), 0)),
        pl.BlockSpec((tile * ROW_CHUNKS, LANES), lambda s: (prev(s), 0)),
        pl.BlockSpec((tile, TOP_K), lambda s: (prev(s), 0)),
        pl.BlockSpec((tile, TOP_K), lambda s: (prev(s), 0)),
        pl.BlockSpec((1, 1, N_EXPERTS), lambda s: (prev(s), 0, 0)),
        pl.BlockSpec((1, 1, N_EXPERTS), lambda s: (prev(s), 0, 0)),
        pl.BlockSpec((1, N_HEADS, HEAD_DIM, HEAD_DIM), lambda s: (cur(s) // n_t, 0, 0, 0)),
    ]
    out_shape = _front_out(bsz * seq, n_tiles) + [jax.ShapeDtypeStruct((bsz, N_HEADS, HEAD_DIM, HEAD_DIM), F32)]
    return pl.pallas_call(
        functools.partial(_prompt_kernel, ret_block=PROMPT_RET_BLOCK, tiles_per_seq=n_t, n_tiles=n_tiles),
        grid=(n_tiles + 1,),
        in_specs=in_specs,
        out_specs=out_specs,
        out_shape=out_shape,
        scratch_shapes=[
            pltpu.VMEM((N_HEADS, HEAD_DIM, HEAD_DIM), F32),
            pltpu.VMEM((N_HEADS, HEAD_DIM, HEAD_DIM), F32),
            pltpu.VMEM((tile, D_MODEL), BF16),
            pltpu.VMEM((tile, D_MODEL), F32),
        ],
        compiler_params=pltpu.CompilerParams(
            dimension_semantics=("arbitrary",), vmem_limit_bytes=VMEM_LIMIT),
        name="front_prompt",
    )(x, p['norm1_g'], p['w_in'], cq, sq, ck, sk, dmat, qdec, kdec, sdec, p['ln_v_g'], p['ln_v_b'], ws, bst,
      p['w_out'], p['norm2_g'], p['wr_hi'], p['wr_lo'], p['b_router'], tri, upper)


def _decode_call(x, state, past_len, p):
    n_streams, frames, _ = x.shape
    tile = n_streams * frames
    assert tile == TOKEN_TILE and frames <= RET_CHUNK and frames & (frames - 1) == 0
    rope = _rope_tables(past_len + jnp.arange(frames, dtype=I32))
    cq, sq, ck, sk = [jnp.tile(a, (n_streams, 1)) for a in rope]
    _, qdec, kdec, sdec = _decay_tables(frames)
    qdec = jnp.tile(qdec, (1, n_streams, 1))
    kdec = jnp.tile(kdec, (1, n_streams, 1))
    tri, upper = _routing_constants(tile)
    ws = p['w_s'][:, :frames, :frames]
    bst = jnp.tile(p['b_s'][:, :frames].T, (n_streams, 1))

    def whole(a):
        return pl.BlockSpec(a.shape, lambda i, n=a.ndim: (0,) * n, pipeline_mode=pl.Buffered(1))

    args = (x.reshape(tile, D_MODEL), state, p['norm1_g'], p['w_in'], cq, sq, ck, sk, qdec, kdec, sdec,
            p['ln_v_g'], p['ln_v_b'], ws, bst, p['w_out'], p['norm2_g'], p['wr_hi'], p['wr_lo'], p['b_router'],
            tri, upper)
    out_shape = _front_out(tile, 1) + [
        jax.ShapeDtypeStruct(state.shape, F32),
        jax.ShapeDtypeStruct((tile, GM_WIDTH), F32),
    ]
    out_specs = [pl.BlockSpec(s.shape, lambda i, n=len(s.shape): (0,) * n) for s in out_shape]
    out_specs[6] = pl.BlockSpec(memory_space=pl.ANY)
    return pl.pallas_call(
        functools.partial(_decode_kernel, n_streams=n_streams, frames=frames),
        grid=(1,),
        in_specs=[whole(a) for a in args],
        out_specs=out_specs,
        out_shape=out_shape,
        scratch_shapes=[
            pltpu.VMEM((tile, D_MODEL), BF16),
            pltpu.VMEM((N_HEADS, tile, HEAD_DIM), BF16),
            pltpu.VMEM((N_HEADS, tile, HEAD_DIM), BF16),
            pltpu.VMEM((N_HEADS, tile, HEAD_DIM), BF16),
            pltpu.VMEM((N_HEADS, tile, HEAD_DIM), F32),
            pltpu.VMEM((2, N_HEADS, HEAD_DIM, HEAD_DIM), F32),
            pltpu.SemaphoreType.DMA((2,)),
        ],
        compiler_params=pltpu.CompilerParams(
            dimension_semantics=("arbitrary",), vmem_limit_bytes=VMEM_LIMIT),
        name="front_decode",
    )(*args)


def _plan_kernel(ct_ref, lower_ref, upper_ref, cum_ref, nb_ref, bs_ref, be_ref, nbt_ref, tlo_ref, thi_ref,
                 *, n_blocks):
    counts = ct_ref[...]
    cum = _exact_count_dot(lower_ref[...], counts, counts_on_left=False)
    cum_ref[...] = cum.astype(I32)
    n_tiles = counts.shape[0]
    total = cum[n_tiles:n_tiles + 1]
    nb = jnp.floor((total + (EXPERT_ROWS - 1)) * (1.0 / EXPERT_ROWS))
    bstart = _dot(jnp.broadcast_to(nb, (8, N_EXPERTS)).astype(BF16), upper_ref[...])[0:1]
    bend = bstart + nb
    nb_ref[...] = nb.astype(I32)
    bs_ref[...] = bstart.astype(I32)
    blk = lax.broadcasted_iota(I32, (n_blocks, N_EXPERTS), 0).astype(F32)
    be = jnp.minimum(jnp.sum((bend <= blk).astype(F32), axis=-1, keepdims=True), N_EXPERTS - 1.0)
    be_ref[...] = be.astype(I32)
    nbt_ref[...] = jnp.sum(nb, axis=-1, keepdims=True).astype(I32)
    mine = lax.broadcasted_iota(I32, (n_blocks, N_EXPERTS), 1).astype(F32) == be
    pick = lambda row: jnp.sum(jnp.where(mine, row, 0.0), axis=-1, keepdims=True)
    r0 = (blk[:, 0:1] - pick(bstart)) * EXPERT_ROWS
    tlo = jnp.zeros((n_blocks, 1), F32)
    thi = jnp.zeros((n_blocks, 1), F32)
    seg_lo = pick(cum[0:1])
    for i in range(n_tiles):
        seg_hi = pick(cum[i + 1:i + 2])
        tlo = tlo + (seg_hi <= r0).astype(F32)
        thi = thi + (seg_lo < r0 + EXPERT_ROWS).astype(F32)
        seg_lo = seg_hi
    tlo_ref[...] = tlo.astype(I32)
    thi_ref[...] = thi.astype(I32)


def _plan_call(counts, n_blocks):
    n_tiles = counts.shape[0]
    lower = (jnp.arange(n_tiles + 1)[:, None] > jnp.arange(n_tiles)[None, :]).astype(BF16)
    upper = (jnp.arange(N_EXPERTS)[:, None] < jnp.arange(N_EXPERTS)[None, :]).astype(BF16)
    return pl.pallas_call(
        functools.partial(_plan_kernel, n_blocks=n_blocks),
        out_shape=[
            jax.ShapeDtypeStruct((n_tiles + 1, N_EXPERTS), I32),
            jax.ShapeDtypeStruct((1, N_EXPERTS), I32),
            jax.ShapeDtypeStruct((1, N_EXPERTS), I32),
            jax.ShapeDtypeStruct((n_blocks, 1), I32),
            jax.ShapeDtypeStruct((1, 1), I32),
            jax.ShapeDtypeStruct((n_blocks, 1), I32),
            jax.ShapeDtypeStruct((n_blocks, 1), I32),
        ],
        name="plan",
    )(counts, lower, upper)


def _dispatch_kernel(h2p_ref, h2s_ref, lp_ref, out_ref, *, n_prompt_tiles):
    i = pl.program_id(0)

    def permute(src_ref):
        def body(t, c):
            row = src_ref[_row_tile(t), :]
            for kk in range(TOP_K):
                out_ref[_row_tile(lp_ref[0, 0, t * TOP_K + kk]), :] = row
            return c

        lax.fori_loop(0, TOKEN_TILE, body, 0, unroll=8)

    @pl.when(i < n_prompt_tiles)
    def _():
        permute(h2p_ref)

    @pl.when(i >= n_prompt_tiles)
    def _():
        permute(h2s_ref)


def _dispatch_call(h2_p, h2_s, lp_tiles):
    rows = TOKEN_TILE * ROW_CHUNKS
    n_p_tiles = h2_p.shape[0] // rows
    n_s_tiles = h2_s.shape[0] // rows
    n_tiles = n_p_tiles + n_s_tiles
    return pl.pallas_call(
        functools.partial(_dispatch_kernel, n_prompt_tiles=n_p_tiles),
        grid=(n_tiles,),
        in_specs=[
            pl.BlockSpec((rows, LANES), lambda i: (jnp.minimum(i, n_p_tiles - 1), 0)),
            pl.BlockSpec((rows, LANES), lambda i: (jnp.maximum(i - n_p_tiles, 0), 0)),
            pl.BlockSpec((1, 1, TILE_SLOTS), lambda i: (i, 0, 0), memory_space=pltpu.SMEM),
        ],
        out_specs=pl.BlockSpec((TILE_SLOTS * ROW_CHUNKS, LANES), lambda i: (i, 0)),
        out_shape=jax.ShapeDtypeStruct((n_tiles * TILE_SLOTS * ROW_CHUNKS, LANES), U32),
        compiler_params=pltpu.CompilerParams(
            dimension_semantics=("arbitrary",), vmem_limit_bytes=VMEM_LIMIT),
        name="dispatch",
    )(h2_p, h2_s, lp_tiles)


def _expert_kernel(be_ref, bs_ref, nb_ref, nbt_ref, cum_ref, off_ref, tlo_ref, thi_ref,
                   xb_ref, wg_ref, wu_ref, wd_ref, bg_ref, bu_ref, bd_ref,
                   yb_ref, xbuf, ybuf, wbf, in_sem, out_sem, *, n_tiles):
    b = pl.program_id(0)
    nbt = nbt_ref[0]

    def block_rows(blk):
        e = be_ref[blk]
        r0 = (blk - bs_ref[e]) * EXPERT_ROWS
        return r0, jnp.minimum(cum_ref[n_tiles * N_EXPERTS + e] - r0, EXPERT_ROWS)

    def segments(blk, slot, fn):
        e = be_ref[blk]
        r0 = (blk - bs_ref[e]) * EXPERT_ROWS

        def body(i, c):
            seg_lo = cum_ref[i * N_EXPERTS + e]
            seg_hi = cum_ref[(i + 1) * N_EXPERTS + e]
            lo = jnp.maximum(seg_lo, r0)
            hi = jnp.minimum(seg_hi, r0 + EXPERT_ROWS)

            @pl.when(hi > lo)
            def _():
                bucket_row = i * TILE_SLOTS + off_ref[i * N_EXPERTS + e] + (lo - seg_lo)
                fn(slot, bucket_row, slot * EXPERT_ROWS + (lo - r0), hi - lo)
            return c

        lax.fori_loop(tlo_ref[blk], thi_ref[blk], body, 0)

    def words(row, n):
        return pl.ds(pl.multiple_of(row * ROW_CHUNKS, ROW_CHUNKS), n * ROW_CHUNKS)

    def gather(slot, bucket_row, buffer_row, n):
        return pltpu.make_async_copy(
            xb_ref.at[words(bucket_row, n)], xbuf.at[words(buffer_row, n)], in_sem.at[slot])

    def scatter(slot, bucket_row, buffer_row, n):
        return pltpu.make_async_copy(
            ybuf.at[words(buffer_row, n)], yb_ref.at[words(bucket_row, n)], out_sem.at[slot])

    def start_gather(blk):
        @pl.when(blk < nbt)
        def _():
            segments(blk, lax.rem(blk, 2), lambda *a: gather(*a).start())

    def wait_gather(blk):
        @pl.when(blk < nbt)
        def _():
            slot = lax.rem(blk, 2)
            gather(slot, 0, slot * EXPERT_ROWS, block_rows(blk)[1]).wait()

    def start_scatter(blk):
        @pl.when(blk < nbt)
        def _():
            segments(blk, lax.rem(blk, 2), lambda *a: scatter(*a).start())

    def wait_scatter(blk):
        @pl.when((blk >= 0) & (blk < nbt))
        def _():
            slot = lax.rem(blk, 2)
            scatter(slot, 0, slot * EXPERT_ROWS, block_rows(blk)[1]).wait()

    def mlp(base, n_rows, rows_left):
        valid = lax.broadcasted_iota(I32, (n_rows, 1), 0) < rows_left
        x = jnp.where(valid, _load_rows(xbuf, base, n_rows), 0.0)
        gt = _dot(x, wbf[0]) + bg_ref[0]
        up = _dot(x, wbf[1]) + bu_ref[0]
        gt = jnp.minimum(gt, SWIGLU_LIMIT)
        up = jnp.clip(up, -SWIGLU_LIMIT, SWIGLU_LIMIT)
        act = gt * _sigmoid(gt * SWIGLU_ALPHA) * (up + 1.0)
        _store_rows(ybuf, base, _dot(act.astype(BF16), wbf[2]) + bd_ref[0])

    def block_body(blk, carry):
        wait_gather(blk)
        start_gather(blk + 1)
        wait_scatter(blk - 2)
        rows_left = block_rows(blk)[1]
        base = lax.rem(blk, 2) * EXPERT_ROWS

        @pl.when(rows_left > EXPERT_SUB_ROWS)
        def _():
            mlp(base, EXPERT_ROWS, rows_left)

        @pl.when(rows_left <= EXPERT_SUB_ROWS)
        def _():
            mlp(base, EXPERT_SUB_ROWS, rows_left)

        start_scatter(blk)
        return carry

    first_block = bs_ref[b]
    n_mine = nb_ref[b]

    @pl.when(b == 0)
    def _():
        xbuf[...] = jnp.zeros_like(xbuf)
        start_gather(0)

    @pl.when(n_mine > 0)
    def _():
        wbf[0] = wg_ref[0].astype(BF16)
        wbf[1] = wu_ref[0].astype(BF16)
        wbf[2] = wd_ref[0].astype(BF16)

    lax.fori_loop(first_block, first_block + n_mine, block_body, 0)

    @pl.when(b == N_EXPERTS - 1)
    def _():
        wait_scatter(nbt - 2)
        wait_scatter(nbt - 1)


def _expert_call(be, bstart, nb, nbt, cum_flat, off_flat, tlo, thi, buckets, p, n_tiles):
    wspec = pl.BlockSpec((1, D_MODEL, D_MODEL), lambda e, *_: (e, 0, 0))
    bspec = pl.BlockSpec((1, 1, D_MODEL), lambda e, *_: (e, 0, 0))
    any_spec = pl.BlockSpec(memory_space=pl.ANY)
    return pl.pallas_call(
        functools.partial(_expert_kernel, n_tiles=n_tiles),
        grid_spec=pltpu.PrefetchScalarGridSpec(
            num_scalar_prefetch=8,
            grid=(N_EXPERTS,),
            in_specs=[any_spec, wspec, wspec, wspec, bspec, bspec, bspec],
            out_specs=any_spec,
            scratch_shapes=[
                pltpu.VMEM((2 * EXPERT_ROWS * ROW_CHUNKS, LANES), U32),
                pltpu.VMEM((2 * EXPERT_ROWS * ROW_CHUNKS, LANES), U32),
                pltpu.VMEM((3, D_MODEL, D_MODEL), BF16),
                pltpu.SemaphoreType.DMA((2,)),
                pltpu.SemaphoreType.DMA((2,)),
            ],
        ),
        out_shape=jax.ShapeDtypeStruct(buckets.shape, U32),
        compiler_params=pltpu.CompilerParams(
            dimension_semantics=("arbitrary",), vmem_limit_bytes=VMEM_LIMIT),
        name="experts",
    )(be, bstart, nb, nbt, cum_flat, off_flat, tlo, thi, buckets, p['w_gate'], p['w_up'], p['w_down'],
      p['b_gate'][:, None, :], p['b_up'][:, None, :], p['b_down'][:, None, :])


def _combine_kernel(x2_ref, yb_ref, lp_ref, tw_ref, gf_ref, y_ref, ybf, pick):
    for c in range(TILE_SLOTS // COMBINE_CHUNK):
        rows = slice(c * COMBINE_CHUNK, (c + 1) * COMBINE_CHUNK)
        ybf[rows, :] = _load_rows(yb_ref, c * COMBINE_CHUNK, COMBINE_CHUNK)
    slot = lax.broadcasted_iota(I32, (TOKEN_TILE, TILE_SLOTS), 1)
    lp = lp_ref[...]
    tw = tw_ref[...]
    sel = jnp.zeros((TOKEN_TILE, TILE_SLOTS), F32)
    for kk in range(TOP_K):
        sel = jnp.where(slot == lp[:, kk:kk + 1], tw[:, kk:kk + 1], sel)
    pick[...] = sel.astype(BF16)
    y_ref[...] = _rmsnorm(x2_ref[...] + _dot(pick[...], ybf[...]), gf_ref[...])


def _combine_call(x2, ybuckets, lp, tw, gf, tile_off):
    n_tok = x2.shape[0]
    n_tiles = n_tok // TOKEN_TILE
    return pl.pallas_call(
        _combine_kernel,
        grid=(n_tiles,),
        in_specs=[
            pl.BlockSpec((TOKEN_TILE, D_MODEL), lambda i: (i, 0)),
            pl.BlockSpec((TILE_SLOTS * ROW_CHUNKS, LANES), lambda i: (tile_off + i, 0)),
            pl.BlockSpec((TOKEN_TILE, TOP_K), lambda i: (i, 0)),
            pl.BlockSpec((TOKEN_TILE, TOP_K), lambda i: (i, 0)),
            pl.BlockSpec((1, D_MODEL), lambda i: (0, 0)),
        ],
        out_specs=pl.BlockSpec((TOKEN_TILE, D_MODEL), lambda i: (i, 0)),
        scratch_shapes=[
            pltpu.VMEM((TILE_SLOTS, D_MODEL), BF16),
            pltpu.VMEM((TOKEN_TILE, TILE_SLOTS), BF16),
        ],
        out_shape=jax.ShapeDtypeStruct((n_tok, D_MODEL), F32),
        compiler_params=pltpu.CompilerParams(
            dimension_semantics=("arbitrary",), vmem_limit_bytes=VMEM_LIMIT),
        name="combine",
    )(x2, ybuckets, lp, tw, gf)


def kernel(x_prompt, x_sample, state_ret, norm1_g, w_in, ln_v_g, ln_v_b, w_s, b_s, w_out, norm2_g, w_router, b_router, w_gate, b_gate, w_up, b_up, w_down, b_down, norm_f_g):
    bsz, seq, _ = x_prompt.shape
    dbsz, dseq, _ = x_sample.shape
    n_p = bsz * seq
    n_s = dbsz * dseq
    n_total = n_p + n_s
    past_len = 2048
    assert w_in.shape[0] == 1, "single layer"

    wr = w_router[0]
    wr_hi = wr.astype(BF16)
    wr_lo = (wr - wr_hi.astype(F32)).astype(BF16)
    p = dict(norm1_g=norm1_g, w_in=w_in[0].astype(BF16), ln_v_g=ln_v_g, ln_v_b=ln_v_b, w_s=w_s[0], b_s=b_s[0],
             w_out=w_out[0].astype(BF16), norm2_g=norm2_g, wr_hi=wr_hi, wr_lo=wr_lo, b_router=b_router,
             w_gate=w_gate[0], b_gate=b_gate[0], w_up=w_up[0], b_up=b_up[0], w_down=w_down[0], b_down=b_down[0])

    x2_p, h2_p, lp_p, tw_p, ct_p, of_p, st_p = _prompt_call(x_prompt, p)
    x2_s, h2_s, lp_s, tw_s, ct_s, of_s, st_s, vn_s = _decode_call(x_sample, state_ret[0], past_len, p)

    n_tiles = n_total // TOKEN_TILE
    lp = jnp.concatenate([lp_p, lp_s], axis=0).reshape(n_tiles, 1, TILE_SLOTS)
    counts = jnp.concatenate([ct_p, ct_s], axis=0).reshape(n_tiles, N_EXPERTS)
    offsets = jnp.concatenate([of_p, of_s], axis=0).reshape(n_tiles * N_EXPERTS).astype(I32)

    n_blocks = -(-(n_total * TOP_K + N_EXPERTS * (EXPERT_ROWS - 1)) // EXPERT_ROWS)
    cum, nb, bstart, be, nbt, tlo, thi = _plan_call(counts, n_blocks)

    buckets = _dispatch_call(h2_p, h2_s, lp)
    ybuckets = _expert_call(be.reshape(n_blocks), bstart.reshape(N_EXPERTS), nb.reshape(N_EXPERTS), nbt.reshape(1),
                            cum.reshape((n_tiles + 1) * N_EXPERTS), offsets, tlo.reshape(n_blocks),
                            thi.reshape(n_blocks), buckets, p, n_tiles)

    y_p = _combine_call(x2_p, ybuckets, lp_p, tw_p, norm_f_g[None, :], 0)
    y_s = _combine_call(x2_s, ybuckets, lp_s, tw_s, norm_f_g[None, :], n_p // TOKEN_TILE)

    return (y_p.reshape(bsz, seq, D_MODEL), y_s.reshape(dbsz, dseq, D_MODEL),
            st_p[None], st_s[None], vn_s.reshape(1, dbsz, dseq, GM_WIDTH))
```

```python
import functools

import numpy as np
import jax
import jax.numpy as jnp
from jax import lax
from jax.experimental import pallas as pl
from jax.experimental.pallas import tpu as pltpu

F32 = jnp.float32
BF16 = jnp.bfloat16
I32 = jnp.int32

D_MODEL = 1024
RET_WIDTH = 512
N_HEADS = 4
HEAD_DIM = 128
GM_WIDTH = 512
GM_CHUNK = 128
RET_CHUNK = 64
N_EXPERTS = 32
TOP_K = 4
SWIGLU_LIMIT = 7.0
SWIGLU_ALPHA = 1.702
ROPE_BASE = 10000.0
EPS = 1e-6

TOKEN_TILE = 512
PROMPT_RET_BLOCK = 256
EXPERT_ROWS = 512
EXPERT_SUB_ROWS = 256
TILE_SLOTS = TOKEN_TILE * TOP_K
COMBINE_CHUNK = 256
VMEM_LIMIT = 56 * 1024 * 1024
LANES = 128
U32 = jnp.uint32
ROW_CHUNKS = D_MODEL // LANES // 2
SPLIT = 16.0
_LOG_GAMMA = [float(np.log1p(-np.float32(2.0) ** np.float32(-5.0 - hd)).astype(np.float32)) for hd in range(N_HEADS)]


def _load_rows(ref, first_row, n_rows):
    base = first_row * ROW_CHUNKS
    lo, hi = [], []
    for c in range(ROW_CHUNKS):
        w = ref[pl.ds(base + c, n_rows, stride=ROW_CHUNKS), :]
        lo.append(pltpu.bitcast(lax.shift_left(w, jnp.uint32(16)), F32))
        hi.append(pltpu.bitcast(lax.bitwise_and(w, jnp.uint32(0xFFFF0000)), F32))
    return jnp.concatenate(lo + hi, axis=1).astype(BF16)


def _store_rows(ref, first_row, val):
    base = first_row * ROW_CHUNKS
    bits = pltpu.bitcast(val.astype(BF16).astype(F32), U32)
    half = D_MODEL // 2
    for c in range(ROW_CHUNKS):
        low = lax.shift_right_logical(bits[:, c * LANES:(c + 1) * LANES], jnp.uint32(16))
        ref[pl.ds(base + c, val.shape[0], stride=ROW_CHUNKS), :] = lax.bitwise_or(
            bits[:, half + c * LANES:half + (c + 1) * LANES], low)


def _row_tile(row):
    return pl.ds(pl.multiple_of(row * ROW_CHUNKS, ROW_CHUNKS), ROW_CHUNKS)


def _rmsnorm(x, g):
    ms = jnp.mean(x * x, axis=-1, keepdims=True)
    return x * lax.rsqrt(ms + EPS) * g


def _gelu(x):
    c = np.float32(np.sqrt(2.0 / np.pi))
    return x * (0.5 * (1.0 + jnp.tanh(c * (x + 0.044715 * (x * x * x)))))


def _sigmoid(x):
    return 1.0 / (1.0 + jnp.exp(-x))


def _dot(a, b):
    return jnp.dot(a, b, preferred_element_type=F32)


def _dot_nt(a, b):
    return lax.dot_general(a, b, (((1,), (1,)), ((), ())), preferred_element_type=F32)


def _dot_tn(a, b):
    return lax.dot_general(a, b, (((0,), (0,)), ((), ())), preferred_element_type=F32)


def _exact_count_dot(a, b, *, counts_on_left):
    cnt = a if counts_on_left else b
    hi = jnp.floor(cnt * (1.0 / SPLIT))
    lo = cnt - SPLIT * hi
    if counts_on_left:
        return SPLIT * _dot(hi.astype(BF16), b) + _dot(lo.astype(BF16), b)
    return SPLIT * _dot(a, hi.astype(BF16)) + _dot(a, lo.astype(BF16))


def _rotary(xh, cos2, sin2):
    return xh * cos2 + pltpu.roll(xh, HEAD_DIM // 2, axis=1) * sin2


def _gated_head_norm(o, gate):
    mu = jnp.mean(o, axis=-1, keepdims=True)
    oc = o - mu
    var = jnp.mean(oc * oc, axis=-1, keepdims=True)
    return ((gate * _sigmoid(gate)) * (oc * lax.rsqrt(var + EPS))).astype(BF16)


def _gmlp_inputs(u, vg, lng, lnb):
    uf = _gelu(u)
    vf = _gelu(vg)
    mu = jnp.mean(vf, axis=-1, keepdims=True)
    vc = vf - mu
    var = jnp.mean(vc * vc, axis=-1, keepdims=True)
    return uf, vc * lax.rsqrt(var + EPS) * lng + lnb


def _route_and_store(x, mix_scr, wout_ref, g2_ref, wr_ref, br_ref, tri_ref, upper_ref,
                     x2_ref, h2_ref, lp_ref, tw_ref, ct_ref, of_ref, wp_ref):
    h2 = _out_proj(x, mix_scr, wout_ref, g2_ref, x2_ref, h2_ref)
    _route(h2, wr_ref, br_ref, tri_ref, upper_ref, lp_ref, tw_ref, ct_ref, of_ref, wp_ref)


def _out_proj(x, mix_scr, wout_ref, g2_ref, x2_ref, h2_ref):
    x2 = x + _dot(mix_scr[...], wout_ref[...])
    x2_ref[...] = x2
    h2 = _rmsnorm(x2, g2_ref[...])
    _store_rows(h2_ref, 0, h2)
    return h2


def _route(h2, wr_ref, br_ref, tri_ref, upper_ref, lp_ref, tw_ref, ct_ref, of_ref, wp_ref):
    _route_place(_route_choose(h2, wr_ref, br_ref), tri_ref, upper_ref, lp_ref, tw_ref, ct_ref, of_ref, wp_ref)


def _route_choose(h2, wr_ref, br_ref):
    tile = h2.shape[0]
    logits = _dot(h2.astype(BF16), wr_ref[...]) + br_ref[...]

    iota_e = lax.broadcasted_iota(I32, (tile, N_EXPERTS), 1)
    iota_ef = iota_e.astype(F32)
    lg = logits
    vals = []
    idxs = []
    for _ in range(TOP_K):
        m = jnp.max(lg, axis=-1, keepdims=True)
        idx = jnp.min(jnp.where(lg == m, iota_ef, float(N_EXPERTS)), axis=-1, keepdims=True).astype(I32)
        vals.append(m)
        idxs.append(idx)
        lg = jnp.where(iota_e == idx, -jnp.inf, lg)
    exps = [jnp.exp(vk - vals[0]) for vk in vals]
    denom = exps[0] + exps[1] + exps[2] + exps[3]
    return idxs, exps, denom


def _route_place(choice, tri_ref, upper_ref, lp_ref, tw_ref, ct_ref, of_ref, wp_ref):
    idxs, exps, denom = choice
    tile = denom.shape[0]
    iota_e = lax.broadcasted_iota(I32, (tile, N_EXPERTS), 1)
    iota_k = lax.broadcasted_iota(I32, (tile, TOP_K), 1)
    onehot = jnp.zeros((tile, N_EXPERTS), F32)
    for idx in idxs:
        onehot = onehot + (iota_e == idx).astype(F32)
    rank = _dot(tri_ref[...], onehot.astype(BF16))
    count = jnp.sum(onehot, axis=0, keepdims=True)
    offset = _exact_count_dot(jnp.broadcast_to(count, (8, N_EXPERTS)), upper_ref[...], counts_on_left=True)[0:1]
    where_to = rank + offset
    lp = jnp.zeros((tile, TOP_K), I32)
    tw = jnp.zeros((tile, TOP_K), F32)
    word_rows = []
    for kk in range(TOP_K):
        pos = jnp.sum(jnp.where(iota_e == idxs[kk], where_to, 0.0), axis=-1, keepdims=True)
        lp = jnp.where(iota_k == kk, pos.astype(I32), lp)
        tw = jnp.where(iota_k == kk, exps[kk] / denom, tw)
        word_rows.append(jnp.transpose(jnp.broadcast_to(pos * float(ROW_CHUNKS), (tile, LANES)))[0:1, :])
    lp_ref[...] = lp
    tw_ref[...] = tw
    ct_ref[0] = count
    of_ref[0] = offset
    wp_ref[0] = jnp.concatenate(word_rows, axis=1).astype(I32)


def _prompt_kernel(x_ref, g1_ref, win_ref, cq_ref, sq_ref, ck_ref, sk_ref, dmat_ref, qdec_ref, kdec_ref,
                   sdec_ref, lng_ref, lnb_ref, ws_ref, bst_ref, wout_ref, g2_ref, wr_ref, br_ref,
                   tri_ref, upper_ref,
                   x2_ref, h2_ref, lp_ref, tw_ref, ct_ref, of_ref, wp_ref, st_ref,
                   s_scr, s_prev, mix_scr, x2_carry, *, ret_block, tiles_per_seq, n_tiles):
    tile = TOKEN_TILE
    step = pl.program_id(0)

    @pl.when(step == 0)
    def _():
        s_scr[...] = jnp.zeros_like(s_scr)
        s_prev[...] = jnp.zeros_like(s_prev)
        x2_carry[...] = jnp.zeros_like(x2_carry)

    h2_prev = _rmsnorm(x2_carry[...], g2_ref[...])
    _store_rows(h2_ref, 0, h2_prev)
    choice = _route_choose(h2_prev, wr_ref, br_ref)

    redo = step == n_tiles
    t = lax.rem(jnp.minimum(step, n_tiles - 1), tiles_per_seq)
    start = jnp.where(redo, s_prev[...], jnp.where(t == 0, 0.0, s_scr[...]))
    s_prev[...] = start
    s_scr[...] = start

    x = x_ref[0]
    h = _rmsnorm(x, g1_ref[...]).astype(BF16)

    def proj(i):
        return _dot(h, win_ref[:, i * RET_WIDTH:(i + 1) * RET_WIDTH])

    q = proj(0)
    k = proj(1)
    v = proj(2)
    gate = proj(3)
    u = proj(4)
    vg = proj(5)
    _route_place(choice, tri_ref, upper_ref, lp_ref, tw_ref, ct_ref, of_ref, wp_ref)
    cq = cq_ref[...]
    sq = sq_ref[...]
    ck = ck_ref[...]
    sk = sk_ref[...]

    for hd in range(N_HEADS):
        cols = slice(hd * HEAD_DIM, (hd + 1) * HEAD_DIM)
        qr = _rotary(q[:, cols], cq, sq)
        kr = _rotary(k[:, cols], ck, sk)
        vh = v[:, cols].astype(BF16)
        dm = dmat_ref[hd]
        qd = qdec_ref[hd]
        kd = kdec_ref[hd]
        sd = sdec_ref[hd]
        for c in range(tile // ret_block):
            rows = slice(c * ret_block, (c + 1) * ret_block)
            qb = qr[rows].astype(BF16)
            kb = kr[rows]
            vb = vh[rows]
            state = s_scr[hd]
            scores = _dot_nt(qb, kb.astype(BF16)) * dm
            o = _dot(scores.astype(BF16), vb) + qd * _dot(qb, state.astype(BF16))
            s_scr[hd] = sd * state + _dot_tn((kb * kd).astype(BF16), vb)
            mix_scr[rows, cols] = _gated_head_norm(o, gate[rows, cols])

    uf, vn = _gmlp_inputs(u, vg, lng_ref[...], lnb_ref[...])
    vnb = vn.astype(BF16)
    r_i = lax.broadcasted_iota(I32, (GM_CHUNK, GM_CHUNK), 0)
    c_i = lax.broadcasted_iota(I32, (GM_CHUNK, GM_CHUNK), 1)
    for hd in range(N_HEADS):
        cols = slice(hd * HEAD_DIM, (hd + 1) * HEAD_DIM)
        w_tril = jnp.where(r_i >= c_i, ws_ref[hd], 0.0).astype(BF16)
        bias = bst_ref[:, hd:hd + 1]
        for c in range(tile // GM_CHUNK):
            rows = slice(c * GM_CHUNK, (c + 1) * GM_CHUNK)
            mixed = _dot(w_tril, vnb[rows, cols]) + bias
            mix_scr[rows, RET_WIDTH + hd * HEAD_DIM:RET_WIDTH + (hd + 1) * HEAD_DIM] = (
                uf[rows, cols] * mixed).astype(BF16)

    x2 = x + _dot(mix_scr[...], wout_ref[...])
    x2_ref[...] = x2
    x2_carry[...] = x2

    @pl.when(t == tiles_per_seq - 1)
    def _():
        st_ref[0] = s_scr[...]


def _decode_kernel(x_ref, s0_ref, g1_ref, win_ref, cq_ref, sq_ref, ck_ref, sk_ref, qdec_ref, kdec_ref,
                   sdec_ref, lng_ref, lnb_ref, ws_ref, bst_ref, wout_ref, g2_ref, wr_ref, br_ref,
                   tri_ref, upper_ref,
                   x2_ref, h2_ref, lp_ref, tw_ref, ct_ref, of_ref, wp_ref, st_ref, vn_ref,
                   mix_scr, q_scr, k_scr, v_scr, oc_scr, st_stage, st_sem, *, n_streams, frames):
    tile = n_streams * frames
    x = x_ref[...]
    h = _rmsnorm(x, g1_ref[...]).astype(BF16)

    def proj(i):
        return _dot(h, win_ref[:, i * RET_WIDTH:(i + 1) * RET_WIDTH])

    q = proj(0)
    k = proj(1)
    v = proj(2)
    gate = proj(3)
    u = proj(4)
    vg = proj(5)
    cq = cq_ref[...]
    sq = sq_ref[...]
    ck = ck_ref[...]
    sk = sk_ref[...]

    r_i = lax.broadcasted_iota(I32, (tile, tile), 0)
    c_i = lax.broadcasted_iota(I32, (tile, tile), 1)
    shift = frames.bit_length() - 1
    r_frame = lax.bitwise_and(r_i, frames - 1)
    c_frame = lax.bitwise_and(c_i, frames - 1)
    keep = (lax.shift_right_logical(r_i, shift) == lax.shift_right_logical(c_i, shift)) & (r_frame >= c_frame)
    frame_gap = (r_frame - c_frame).astype(F32)

    o_in = []
    for hd in range(N_HEADS):
        cols = slice(hd * HEAD_DIM, (hd + 1) * HEAD_DIM)
        qr = _rotary(q[:, cols], cq, sq)
        kr = _rotary(k[:, cols], ck, sk)
        vh = v[:, cols].astype(BF16)
        qb = qr.astype(BF16)
        decay = jnp.where(keep, jnp.exp(frame_gap * _LOG_GAMMA[hd]), 0.0)
        scores = _dot_nt(qb, kr.astype(BF16)) * decay
        o_in.append(_dot(scores.astype(BF16), vh))
        q_scr[hd] = qb
        k_scr[hd] = (kr * kdec_ref[hd]).astype(BF16)
        v_scr[hd] = vh

    def state_copy(b):
        slot = lax.rem(b, 2)
        return pltpu.make_async_copy(st_stage.at[slot], st_ref.at[b], st_sem.at[slot])

    def stream_body(b, carry):
        rows = pl.ds(pl.multiple_of(b * frames, frames), frames)
        slot = lax.rem(b, 2)

        @pl.when(b >= 2)
        def _():
            state_copy(b - 2).wait()

        for hd in range(N_HEADS):
            state = s0_ref[b, hd]
            oc_scr[hd, rows, :] = _dot(q_scr[hd, rows, :], state.astype(BF16))
            st_stage[slot, hd] = sdec_ref[hd] * state + _dot_tn(k_scr[hd, rows, :], v_scr[hd, rows, :])
        state_copy(b).start()
        return carry

    lax.fori_loop(0, n_streams, stream_body, 0)
    state_copy(n_streams - 2).wait()
    state_copy(n_streams - 1).wait()

    for hd in range(N_HEADS):
        cols = slice(hd * HEAD_DIM, (hd + 1) * HEAD_DIM)
        o = o_in[hd] + qdec_ref[hd] * oc_scr[hd]
        mix_scr[:, cols] = _gated_head_norm(o, gate[:, cols])

    uf, vn = _gmlp_inputs(u, vg, lng_ref[...], lnb_ref[...])
    vn_ref[...] = vn
    vnb = vn.astype(BF16)
    sel = (lax.bitwise_and(lax.broadcasted_iota(I32, (tile, frames), 0), frames - 1)
           == lax.broadcasted_iota(I32, (tile, frames), 1)).astype(BF16)
    for hd in range(N_HEADS):
        cols = slice(hd * HEAD_DIM, (hd + 1) * HEAD_DIM)
        w_rows = _dot(sel, ws_ref[hd].astype(BF16)).astype(BF16)
        w_blk = jnp.where(keep, _dot_nt(w_rows, sel), 0.0).astype(BF16)
        mixed = _dot(w_blk, vnb[:, cols]) + bst_ref[:, hd:hd + 1]
        mix_scr[:, RET_WIDTH + hd * HEAD_DIM:RET_WIDTH + (hd + 1) * HEAD_DIM] = (uf[:, cols] * mixed).astype(BF16)

    _route_and_store(x, mix_scr, wout_ref, g2_ref, wr_ref, br_ref, tri_ref, upper_ref,
                     x2_ref, h2_ref, lp_ref, tw_ref, ct_ref, of_ref, wp_ref)


def _rope_tables(pos):
    half = HEAD_DIM // 2
    inv = ROPE_BASE ** (-jnp.arange(half, dtype=F32) / half)
    ang = pos.astype(F32)[:, None] * inv[None, :]
    cos = jnp.cos(ang)
    sin = jnp.sin(ang)
    cq = jnp.concatenate([cos, cos], axis=-1)
    sq = jnp.concatenate([-sin, sin], axis=-1)
    scale = HEAD_DIM ** -0.5
    return cq, sq, cq * scale, sq * scale


def _decay_tables(block):
    log_g = jnp.log1p(-(2.0 ** (-5.0 - jnp.arange(N_HEADS, dtype=F32))))
    idx = jnp.arange(block, dtype=F32)
    diff = idx[:, None] - idx[None, :]
    dmat = jnp.where(diff[None] >= 0, jnp.exp(jnp.maximum(diff, 0.0)[None] * log_g[:, None, None]), 0.0)
    q_dec = jnp.exp((idx + 1.0)[None, :] * log_g[:, None])
    k_dec = jnp.exp((block - 1.0 - idx)[None, :] * log_g[:, None])
    s_dec = jnp.exp(block * log_g)
    bc = lambda a: jnp.broadcast_to(a[:, :, None], (N_HEADS, block, HEAD_DIM))
    return dmat, bc(q_dec), bc(k_dec), jnp.broadcast_to(s_dec[:, None, None], (N_HEADS, 1, HEAD_DIM))


def _routing_constants(tile):
    tri = (jnp.arange(tile)[:, None] > jnp.arange(tile)[None, :]).astype(BF16)
    upper = (jnp.arange(N_EXPERTS)[:, None] < jnp.arange(N_EXPERTS)[None, :]).astype(BF16)
    return tri, upper


def _front_out(n_tok, n_tiles):
    return [
        jax.ShapeDtypeStruct((n_tok, D_MODEL), F32),
        jax.ShapeDtypeStruct((n_tok * ROW_CHUNKS, LANES), U32),
        jax.ShapeDtypeStruct((n_tok, TOP_K), I32),
        jax.ShapeDtypeStruct((n_tok, TOP_K), F32),
        jax.ShapeDtypeStruct((n_tiles, 1, N_EXPERTS), F32),
        jax.ShapeDtypeStruct((n_tiles, 1, N_EXPERTS), F32),
        jax.ShapeDtypeStruct((n_tiles, 1, TILE_SLOTS), I32),
    ]


def _prompt_call(x, p):
    bsz, seq, _ = x.shape
    tile = TOKEN_TILE
    n_t = seq // tile
    cq, sq, ck, sk = _rope_tables(jnp.arange(seq, dtype=I32))
    dmat, qdec, kdec, sdec = _decay_tables(PROMPT_RET_BLOCK)
    tri, upper = _routing_constants(tile)
    ws = p['w_s'][:, :GM_CHUNK, :GM_CHUNK]
    bst = p['b_s'][:, :GM_CHUNK].T

    n_tiles = bsz * n_t
    cur = lambda s: jnp.minimum(s, n_tiles - 1)
    prev = lambda s: jnp.maximum(s - 1, 0)
    const = lambda shape: pl.BlockSpec(shape, lambda s: (0,) * len(shape))
    pos = lambda: pl.BlockSpec((tile, HEAD_DIM), lambda s: (lax.rem(cur(s), n_t), 0))
    in_specs = [
        pl.BlockSpec((1, tile, D_MODEL), lambda s: (cur(s) // n_t, lax.rem(cur(s), n_t), 0)),
        const((1, D_MODEL)), const((D_MODEL, 6 * RET_WIDTH)),
        pos(), pos(), pos(), pos(),
        const((N_HEADS, PROMPT_RET_BLOCK, PROMPT_RET_BLOCK)),
        const((N_HEADS, PROMPT_RET_BLOCK, HEAD_DIM)), const((N_HEADS, PROMPT_RET_BLOCK, HEAD_DIM)),
        const((N_HEADS, 1, HEAD_DIM)),
        const((1, GM_WIDTH)), const((1, GM_WIDTH)),
        const((N_HEADS, GM_CHUNK, GM_CHUNK)), const((GM_CHUNK, N_HEADS)),
        const((D_MODEL, D_MODEL)), const((1, D_MODEL)),
        const((D_MODEL, N_EXPERTS)), const((1, N_EXPERTS)),
        const((tile, tile)), const((N_EXPERTS, N_EXPERTS)),
    ]
    out_specs = [
        pl.BlockSpec((tile, D_MODEL), lambda s: (cur(s), 0)),
        pl.BlockSpec((tile * ROW_CHUNKS, LANES), lambda s: (prev(s), 0)),
        pl.BlockSpec((tile, TOP_K), lambda s: (prev(s), 0)),
        pl.BlockSpec((tile, TOP_K), lambda s: (prev(s), 0)),
        pl.BlockSpec((1, 1, N_EXPERTS), lambda s: (prev(s), 0, 0)),
        pl.BlockSpec((1, 1, N_EXPERTS), lambda s: (prev(s), 0, 0)),
        pl.BlockSpec((1, 1, TILE_SLOTS), lambda s: (prev(s), 0, 0)),
        pl.BlockSpec((1, N_HEADS, HEAD_DIM, HEAD_DIM), lambda s: (cur(s) // n_t, 0, 0, 0)),
    ]
    out_shape = _front_out(bsz * seq, n_tiles) + [jax.ShapeDtypeStruct((bsz, N_HEADS, HEAD_DIM, HEAD_DIM), F32)]
    return pl.pallas_call(
        functools.partial(_prompt_kernel, ret_block=PROMPT_RET_BLOCK, tiles_per_seq=n_t, n_tiles=n_tiles),
        grid=(n_tiles + 1,),
        in_specs=in_specs,
        out_specs=out_specs,
        out_shape=out_shape,
        scratch_shapes=[
            pltpu.VMEM((N_HEADS, HEAD_DIM, HEAD_DIM), F32),
            pltpu.VMEM((N_HEADS, HEAD_DIM, HEAD_DIM), F32),
            pltpu.VMEM((tile, D_MODEL), BF16),
            pltpu.VMEM((tile, D_MODEL), F32),
        ],
        compiler_params=pltpu.CompilerParams(
            dimension_semantics=("arbitrary",), vmem_limit_bytes=VMEM_LIMIT),
        name="front_prompt",
    )(x, p['norm1_g'], p['w_in'], cq, sq, ck, sk, dmat, qdec, kdec, sdec, p['ln_v_g'], p['ln_v_b'], ws, bst,
      p['w_out'], p['norm2_g'], p['w_router'], p['b_router'], tri, upper)


def _decode_call(x, state, past_len, p):
    n_streams, frames, _ = x.shape
    tile = n_streams * frames
    assert tile == TOKEN_TILE and frames <= RET_CHUNK and frames & (frames - 1) == 0
    rope = _rope_tables(past_len + jnp.arange(frames, dtype=I32))
    cq, sq, ck, sk = [jnp.tile(a, (n_streams, 1)) for a in rope]
    _, qdec, kdec, sdec = _decay_tables(frames)
    qdec = jnp.tile(qdec, (1, n_streams, 1))
    kdec = jnp.tile(kdec, (1, n_streams, 1))
    tri, upper = _routing_constants(tile)
    ws = p['w_s'][:, :frames, :frames]
    bst = jnp.tile(p['b_s'][:, :frames].T, (n_streams, 1))

    def whole(a):
        return pl.BlockSpec(a.shape, lambda i, n=a.ndim: (0,) * n, pipeline_mode=pl.Buffered(1))

    args = (x.reshape(tile, D_MODEL), state, p['norm1_g'], p['w_in'], cq, sq, ck, sk, qdec, kdec, sdec,
            p['ln_v_g'], p['ln_v_b'], ws, bst, p['w_out'], p['norm2_g'], p['w_router'], p['b_router'],
            tri, upper)
    out_shape = _front_out(tile, 1) + [
        jax.ShapeDtypeStruct(state.shape, F32),
        jax.ShapeDtypeStruct((tile, GM_WIDTH), F32),
    ]
    out_specs = [pl.BlockSpec(s.shape, lambda i, n=len(s.shape): (0,) * n) for s in out_shape]
    out_specs[7] = pl.BlockSpec(memory_space=pl.ANY)
    return pl.pallas_call(
        functools.partial(_decode_kernel, n_streams=n_streams, frames=frames),
        grid=(1,),
        in_specs=[whole(a) for a in args],
        out_specs=out_specs,
        out_shape=out_shape,
        scratch_shapes=[
            pltpu.VMEM((tile, D_MODEL), BF16),
            pltpu.VMEM((N_HEADS, tile, HEAD_DIM), BF16),
            pltpu.VMEM((N_HEADS, tile, HEAD_DIM), BF16),
            pltpu.VMEM((N_HEADS, tile, HEAD_DIM), BF16),
            pltpu.VMEM((N_HEADS, tile, HEAD_DIM), F32),
            pltpu.VMEM((2, N_HEADS, HEAD_DIM, HEAD_DIM), F32),
            pltpu.SemaphoreType.DMA((2,)),
        ],
        compiler_params=pltpu.CompilerParams(
            dimension_semantics=("arbitrary",), vmem_limit_bytes=VMEM_LIMIT),
        name="front_decode",
    )(*args)


def _plan_kernel(ct_ref, lower_ref, upper_ref, cum_ref, nb_ref, bs_ref, be_ref, nbt_ref, tlo_ref, thi_ref,
                 *, n_blocks):
    counts = ct_ref[...]
    cum = _exact_count_dot(lower_ref[...], counts, counts_on_left=False)
    cum_ref[...] = cum.astype(I32)
    n_tiles = counts.shape[0]
    total = cum[n_tiles:n_tiles + 1]
    nb = jnp.floor((total + (EXPERT_ROWS - 1)) * (1.0 / EXPERT_ROWS))
    bstart = _dot(jnp.broadcast_to(nb, (8, N_EXPERTS)).astype(BF16), upper_ref[...])[0:1]
    bend = bstart + nb
    nb_ref[...] = nb.astype(I32)
    bs_ref[...] = bstart.astype(I32)
    blk = lax.broadcasted_iota(I32, (n_blocks, N_EXPERTS), 0).astype(F32)
    be = jnp.minimum(jnp.sum((bend <= blk).astype(F32), axis=-1, keepdims=True), N_EXPERTS - 1.0)
    be_ref[...] = be.astype(I32)
    nbt_ref[...] = jnp.sum(nb, axis=-1, keepdims=True).astype(I32)
    mine = lax.broadcasted_iota(I32, (n_blocks, N_EXPERTS), 1).astype(F32) == be
    pick = lambda row: jnp.sum(jnp.where(mine, row, 0.0), axis=-1, keepdims=True)
    r0 = (blk[:, 0:1] - pick(bstart)) * EXPERT_ROWS
    tlo = jnp.zeros((n_blocks, 1), F32)
    thi = jnp.zeros((n_blocks, 1), F32)
    seg_lo = pick(cum[0:1])
    for i in range(n_tiles):
        seg_hi = pick(cum[i + 1:i + 2])
        tlo = tlo + (seg_hi <= r0).astype(F32)
        thi = thi + (seg_lo < r0 + EXPERT_ROWS).astype(F32)
        seg_lo = seg_hi
    tlo_ref[...] = tlo.astype(I32)
    thi_ref[...] = thi.astype(I32)


def _plan_call(counts, n_blocks):
    n_tiles = counts.shape[0]
    lower = (jnp.arange(n_tiles + 1)[:, None] > jnp.arange(n_tiles)[None, :]).astype(BF16)
    upper = (jnp.arange(N_EXPERTS)[:, None] < jnp.arange(N_EXPERTS)[None, :]).astype(BF16)
    return pl.pallas_call(
        functools.partial(_plan_kernel, n_blocks=n_blocks),
        out_shape=[
            jax.ShapeDtypeStruct((n_tiles + 1, N_EXPERTS), I32),
            jax.ShapeDtypeStruct((1, N_EXPERTS), I32),
            jax.ShapeDtypeStruct((1, N_EXPERTS), I32),
            jax.ShapeDtypeStruct((n_blocks, 1), I32),
            jax.ShapeDtypeStruct((1, 1), I32),
            jax.ShapeDtypeStruct((n_blocks, 1), I32),
            jax.ShapeDtypeStruct((n_blocks, 1), I32),
        ],
        name="plan",
    )(counts, lower, upper)


def _dispatch_kernel(h2p_ref, h2s_ref, wp_ref, out_ref, *, n_prompt_tiles):
    i = pl.program_id(0)

    def permute(src_ref):
        def body(t, c):
            row = src_ref[_row_tile(t), :]
            for kk in range(TOP_K):
                first = pl.multiple_of(wp_ref[0, 0, kk * TOKEN_TILE + t], ROW_CHUNKS)
                out_ref[pl.ds(first, ROW_CHUNKS), :] = row
            return c

        lax.fori_loop(0, TOKEN_TILE, body, 0, unroll=8)

    @pl.when(i < n_prompt_tiles)
    def _():
        permute(h2p_ref)

    @pl.when(i >= n_prompt_tiles)
    def _():
        permute(h2s_ref)


def _dispatch_call(h2_p, h2_s, lp_tiles):
    rows = TOKEN_TILE * ROW_CHUNKS
    n_p_tiles = h2_p.shape[0] // rows
    n_s_tiles = h2_s.shape[0] // rows
    n_tiles = n_p_tiles + n_s_tiles
    return pl.pallas_call(
        functools.partial(_dispatch_kernel, n_prompt_tiles=n_p_tiles),
        grid=(n_tiles,),
        in_specs=[
            pl.BlockSpec((rows, LANES), lambda i: (jnp.minimum(i, n_p_tiles - 1), 0)),
            pl.BlockSpec((rows, LANES), lambda i: (jnp.maximum(i - n_p_tiles, 0), 0)),
            pl.BlockSpec((1, 1, TILE_SLOTS), lambda i: (i, 0, 0), memory_space=pltpu.SMEM),
        ],
        out_specs=pl.BlockSpec((TILE_SLOTS * ROW_CHUNKS, LANES), lambda i: (i, 0)),
        out_shape=jax.ShapeDtypeStruct((n_tiles * TILE_SLOTS * ROW_CHUNKS, LANES), U32),
        compiler_params=pltpu.CompilerParams(
            dimension_semantics=("arbitrary",), vmem_limit_bytes=VMEM_LIMIT),
        name="dispatch",
    )(h2_p, h2_s, lp_tiles)


def _expert_kernel(be_ref, bs_ref, nb_ref, nbt_ref, cum_ref, off_ref, tlo_ref, thi_ref,
                   xb_ref, wg_ref, wu_ref, wd_ref, bg_ref, bu_ref, bd_ref,
                   yb_ref, xbuf, ybuf, wbf, in_sem, out_sem, *, n_tiles):
    b = pl.program_id(0)
    nbt = nbt_ref[0]

    def block_rows(blk):
        e = be_ref[blk]
        r0 = (blk - bs_ref[e]) * EXPERT_ROWS
        return r0, jnp.minimum(cum_ref[n_tiles * N_EXPERTS + e] - r0, EXPERT_ROWS)

    def segments(blk, slot, fn):
        e = be_ref[blk]
        r0 = (blk - bs_ref[e]) * EXPERT_ROWS

        def body(i, c):
            seg_lo = cum_ref[i * N_EXPERTS + e]
            seg_hi = cum_ref[(i + 1) * N_EXPERTS + e]
            lo = jnp.maximum(seg_lo, r0)
            hi = jnp.minimum(seg_hi, r0 + EXPERT_ROWS)

            @pl.when(hi > lo)
            def _():
                bucket_row = i * TILE_SLOTS + off_ref[i * N_EXPERTS + e] + (lo - seg_lo)
                fn(slot, bucket_row, slot * EXPERT_ROWS + (lo - r0), hi - lo)
            return c

        lax.fori_loop(tlo_ref[blk], thi_ref[blk], body, 0)

    def words(row, n):
        return pl.ds(pl.multiple_of(row * ROW_CHUNKS, ROW_CHUNKS), n * ROW_CHUNKS)

    def gather(slot, bucket_row, buffer_row, n):
        return pltpu.make_async_copy(
            xb_ref.at[words(bucket_row, n)], xbuf.at[words(buffer_row, n)], in_sem.at[slot])

    def scatter(slot, bucket_row, buffer_row, n):
        return pltpu.make_async_copy(
            ybuf.at[words(buffer_row, n)], yb_ref.at[words(bucket_row, n)], out_sem.at[slot])

    def start_gather(blk):
        @pl.when(blk < nbt)
        def _():
            segments(blk, lax.rem(blk, 2), lambda *a: gather(*a).start())

    def wait_gather(blk):
        @pl.when(blk < nbt)
        def _():
            slot = lax.rem(blk, 2)
            gather(slot, 0, slot * EXPERT_ROWS, block_rows(blk)[1]).wait()

    def start_scatter(blk):
        @pl.when(blk < nbt)
        def _():
            segments(blk, lax.rem(blk, 2), lambda *a: scatter(*a).start())

    def wait_scatter(blk):
        @pl.when((blk >= 0) & (blk < nbt))
        def _():
            slot = lax.rem(blk, 2)
            scatter(slot, 0, slot * EXPERT_ROWS, block_rows(blk)[1]).wait()

    def mlp(base, n_rows, rows_left):
        valid = lax.broadcasted_iota(I32, (n_rows, 1), 0) < rows_left
        x = jnp.where(valid, _load_rows(xbuf, base, n_rows), 0.0)
        gt = _dot(x, wbf[0]) + bg_ref[0]
        up = _dot(x, wbf[1]) + bu_ref[0]
        gt = jnp.minimum(gt, SWIGLU_LIMIT)
        up = jnp.clip(up, -SWIGLU_LIMIT, SWIGLU_LIMIT)
        act = gt * _sigmoid(gt * SWIGLU_ALPHA) * (up + 1.0)
        _store_rows(ybuf, base, _dot(act.astype(BF16), wbf[2]) + bd_ref[0])

    def block_body(blk, carry):
        wait_gather(blk)
        start_gather(blk + 1)
        wait_scatter(blk - 2)
        rows_left = block_rows(blk)[1]
        base = lax.rem(blk, 2) * EXPERT_ROWS

        @pl.when(rows_left > EXPERT_SUB_ROWS)
        def _():
            mlp(base, EXPERT_ROWS, rows_left)

        @pl.when(rows_left <= EXPERT_SUB_ROWS)
        def _():
            mlp(base, EXPERT_SUB_ROWS, rows_left)

        start_scatter(blk)
        return carry

    first_block = bs_ref[b]
    n_mine = nb_ref[b]

    @pl.when(b == 0)
    def _():
        xbuf[...] = jnp.zeros_like(xbuf)
        start_gather(0)

    @pl.when(n_mine > 0)
    def _():
        wbf[0] = wg_ref[0].astype(BF16)
        wbf[1] = wu_ref[0].astype(BF16)
        wbf[2] = wd_ref[0].astype(BF16)

    lax.fori_loop(first_block, first_block + n_mine, block_body, 0)

    @pl.when(b == N_EXPERTS - 1)
    def _():
        wait_scatter(nbt - 2)
        wait_scatter(nbt - 1)


def _expert_call(be, bstart, nb, nbt, cum_flat, off_flat, tlo, thi, buckets, p, n_tiles):
    wspec = pl.BlockSpec((1, D_MODEL, D_MODEL), lambda e, *_: (e, 0, 0))
    bspec = pl.BlockSpec((1, 1, D_MODEL), lambda e, *_: (e, 0, 0))
    any_spec = pl.BlockSpec(memory_space=pl.ANY)
    return pl.pallas_call(
        functools.partial(_expert_kernel, n_tiles=n_tiles),
        grid_spec=pltpu.PrefetchScalarGridSpec(
            num_scalar_prefetch=8,
            grid=(N_EXPERTS,),
            in_specs=[any_spec, wspec, wspec, wspec, bspec, bspec, bspec],
            out_specs=any_spec,
            scratch_shapes=[
                pltpu.VMEM((2 * EXPERT_ROWS * ROW_CHUNKS, LANES), U32),
                pltpu.VMEM((2 * EXPERT_ROWS * ROW_CHUNKS, LANES), U32),
                pltpu.VMEM((3, D_MODEL, D_MODEL), BF16),
                pltpu.SemaphoreType.DMA((2,)),
                pltpu.SemaphoreType.DMA((2,)),
            ],
        ),
        out_shape=jax.ShapeDtypeStruct(buckets.shape, U32),
        compiler_params=pltpu.CompilerParams(
            dimension_semantics=("arbitrary",), vmem_limit_bytes=VMEM_LIMIT),
        name="experts",
    )(be, bstart, nb, nbt, cum_flat, off_flat, tlo, thi, buckets, p['w_gate'], p['w_up'], p['w_down'],
      p['b_gate'][:, None, :], p['b_up'][:, None, :], p['b_down'][:, None, :])


def _combine_kernel(x2_ref, yb_ref, lp_ref, tw_ref, gf_ref, y_ref, ybf, pick):
    for c in range(TILE_SLOTS // COMBINE_CHUNK):
        rows = slice(c * COMBINE_CHUNK, (c + 1) * COMBINE_CHUNK)
        ybf[rows, :] = _load_rows(yb_ref, c * COMBINE_CHUNK, COMBINE_CHUNK)
    slot = lax.broadcasted_iota(I32, (TOKEN_TILE, TILE_SLOTS), 1)
    lp = lp_ref[...]
    tw = tw_ref[...]
    sel = jnp.zeros((TOKEN_TILE, TILE_SLOTS), F32)
    for kk in range(TOP_K):
        sel = jnp.where(slot == lp[:, kk:kk + 1], tw[:, kk:kk + 1], sel)
    pick[...] = sel.astype(BF16)
    y_ref[...] = _rmsnorm(x2_ref[...] + _dot(pick[...], ybf[...]), gf_ref[...])


def _combine_call(x2, ybuckets, lp, tw, gf, tile_off):
    n_tok = x2.shape[0]
    n_tiles = n_tok // TOKEN_TILE
    return pl.pallas_call(
        _combine_kernel,
        grid=(n_tiles,),
        in_specs=[
            pl.BlockSpec((TOKEN_TILE, D_MODEL), lambda i: (i, 0)),
            pl.BlockSpec((TILE_SLOTS * ROW_CHUNKS, LANES), lambda i: (tile_off + i, 0)),
            pl.BlockSpec((TOKEN_TILE, TOP_K), lambda i: (i, 0)),
            pl.BlockSpec((TOKEN_TILE, TOP_K), lambda i: (i, 0)),
            pl.BlockSpec((1, D_MODEL), lambda i: (0, 0)),
        ],
        out_specs=pl.BlockSpec((TOKEN_TILE, D_MODEL), lambda i: (i, 0)),
        scratch_shapes=[
            pltpu.VMEM((TILE_SLOTS, D_MODEL), BF16),
            pltpu.VMEM((TOKEN_TILE, TILE_SLOTS), BF16),
        ],
        out_shape=jax.ShapeDtypeStruct((n_tok, D_MODEL), F32),
        compiler_params=pltpu.CompilerParams(
            dimension_semantics=("arbitrary",), vmem_limit_bytes=VMEM_LIMIT),
        name="combine",
    )(x2, ybuckets, lp, tw, gf)


def kernel(x_prompt, x_sample, state_ret, norm1_g, w_in, ln_v_g, ln_v_b, w_s, b_s, w_out, norm2_g, w_router, b_router, w_gate, b_gate, w_up, b_up, w_down, b_down, norm_f_g):
    bsz, seq, _ = x_prompt.shape
    dbsz, dseq, _ = x_sample.shape
    n_p = bsz * seq
    n_s = dbsz * dseq
    n_total = n_p + n_s
    past_len = 2048
    assert w_in.shape[0] == 1, "single layer"

    p = dict(norm1_g=norm1_g, w_in=w_in[0].astype(BF16), ln_v_g=ln_v_g, ln_v_b=ln_v_b, w_s=w_s[0], b_s=b_s[0],
             w_out=w_out[0].astype(BF16), norm2_g=norm2_g, w_router=w_router[0].astype(BF16), b_router=b_router,
             w_gate=w_gate[0], b_gate=b_gate[0], w_up=w_up[0], b_up=b_up[0], w_down=w_down[0], b_down=b_down[0])

    x2_p, h2_p, lp_p, tw_p, ct_p, of_p, wp_p, st_p = _prompt_call(x_prompt, p)
    x2_s, h2_s, lp_s, tw_s, ct_s, of_s, wp_s, st_s, vn_s = _decode_call(x_sample, state_ret[0], past_len, p)

    n_tiles = n_total // TOKEN_TILE
    word_pos = jnp.concatenate([wp_p, wp_s], axis=0)
    counts = jnp.concatenate([ct_p, ct_s], axis=0).reshape(n_tiles, N_EXPERTS)
    offsets = jnp.concatenate([of_p, of_s], axis=0).reshape(n_tiles * N_EXPERTS).astype(I32)

    n_blocks = -(-(n_total * TOP_K + N_EXPERTS * (EXPERT_ROWS - 1)) // EXPERT_ROWS)
    cum, nb, bstart, be, nbt, tlo, thi = _plan_call(counts, n_blocks)

    buckets = _dispatch_call(h2_p, h2_s, word_pos)
    ybuckets = _expert_call(be.reshape(n_blocks), bstart.reshape(N_EXPERTS), nb.reshape(N_EXPERTS), nbt.reshape(1),
                            cum.reshape((n_tiles + 1) * N_EXPERTS), offsets, tlo.reshape(n_blocks),
                            thi.reshape(n_blocks), buckets, p, n_tiles)

    y_p = _combine_call(x2_p, ybuckets, lp_p, tw_p, norm_f_g[None, :], 0)
    y_s = _combine_call(x2_s, ybuckets, lp_s, tw_s, norm_f_g[None, :], n_p // TOKEN_TILE)

    return (y_p.reshape(bsz, seq, D_MODEL), y_s.reshape(dbsz, dseq, D_MODEL),
            st_p[None], st_s[None], vn_s.reshape(1, dbsz, dseq, GM_WIDTH))
```

```python
import functools

import numpy as np
import jax
import jax.numpy as jnp
from jax import lax
from jax.experimental import pallas as pl
from jax.experimental.pallas import tpu as pltpu

F32 = jnp.float32
BF16 = jnp.bfloat16
I32 = jnp.int32

D_MODEL = 1024
RET_WIDTH = 512
N_HEADS = 4
HEAD_DIM = 128
GM_WIDTH = 512
GM_CHUNK = 128
RET_CHUNK = 64
N_EXPERTS = 32
TOP_K = 4
SWIGLU_LIMIT = 7.0
SWIGLU_ALPHA = 1.702
ROPE_BASE = 10000.0
EPS = 1e-6

TOKEN_TILE = 512
PROMPT_RET_BLOCK = 256
EXPERT_ROWS = 512
EXPERT_SUB_ROWS = 256
TILE_SLOTS = TOKEN_TILE * TOP_K
COMBINE_CHUNK = 256
VMEM_LIMIT = 56 * 1024 * 1024
LANES = 128
U32 = jnp.uint32
ROW_CHUNKS = D_MODEL // LANES // 2
K_SCALE = HEAD_DIM ** -0.5
SPLIT = 16.0
_LOG_GAMMA = [float(np.log1p(-np.float32(2.0) ** np.float32(-5.0 - hd)).astype(np.float32)) for hd in range(N_HEADS)]


def _load_rows(ref, first_row, n_rows):
    base = first_row * ROW_CHUNKS
    lo, hi = [], []
    for c in range(ROW_CHUNKS):
        w = ref[pl.ds(base + c, n_rows, stride=ROW_CHUNKS), :]
        lo.append(pltpu.bitcast(lax.shift_left(w, jnp.uint32(16)), F32))
        hi.append(pltpu.bitcast(lax.bitwise_and(w, jnp.uint32(0xFFFF0000)), F32))
    return jnp.concatenate(lo + hi, axis=1).astype(BF16)


def _store_rows(ref, first_row, val):
    base = first_row * ROW_CHUNKS
    bits = pltpu.bitcast(val.astype(BF16).astype(F32), U32)
    half = D_MODEL // 2
    for c in range(ROW_CHUNKS):
        low = lax.shift_right_logical(bits[:, c * LANES:(c + 1) * LANES], jnp.uint32(16))
        ref[pl.ds(base + c, val.shape[0], stride=ROW_CHUNKS), :] = lax.bitwise_or(
            bits[:, half + c * LANES:half + (c + 1) * LANES], low)


def _row_tile(row):
    return pl.ds(pl.multiple_of(row * ROW_CHUNKS, ROW_CHUNKS), ROW_CHUNKS)


def _rmsnorm(x, g):
    ms = jnp.mean(x * x, axis=-1, keepdims=True)
    return x * lax.rsqrt(ms + EPS) * g


def _gelu(x):
    c = np.float32(np.sqrt(2.0 / np.pi))
    return x * (0.5 * (1.0 + jnp.tanh(c * (x + 0.044715 * (x * x * x)))))


def _sigmoid(x):
    return 1.0 / (1.0 + jnp.exp(-x))


def _dot(a, b):
    return jnp.dot(a, b, preferred_element_type=F32)


def _dot_nt(a, b):
    return lax.dot_general(a, b, (((1,), (1,)), ((), ())), preferred_element_type=F32)


def _dot_tn(a, b):
    return lax.dot_general(a, b, (((0,), (0,)), ((), ())), preferred_element_type=F32)


def _exact_count_dot(a, b, *, counts_on_left):
    cnt = a if counts_on_left else b
    hi = jnp.floor(cnt * (1.0 / SPLIT))
    lo = cnt - SPLIT * hi
    if counts_on_left:
        return SPLIT * _dot(hi.astype(BF16), b) + _dot(lo.astype(BF16), b)
    return SPLIT * _dot(a, hi.astype(BF16)) + _dot(a, lo.astype(BF16))


def _rotary(xh, cos2, sin2):
    return xh * cos2 + pltpu.roll(xh, HEAD_DIM // 2, axis=1) * sin2


def _gated_head_norm(o, gate):
    mu = jnp.mean(o, axis=-1, keepdims=True)
    oc = o - mu
    var = jnp.mean(oc * oc, axis=-1, keepdims=True)
    return ((gate * _sigmoid(gate)) * (oc * lax.rsqrt(var + EPS))).astype(BF16)


def _gmlp_inputs(u, vg, lng, lnb):
    uf = _gelu(u)
    vf = _gelu(vg)
    mu = jnp.mean(vf, axis=-1, keepdims=True)
    vc = vf - mu
    var = jnp.mean(vc * vc, axis=-1, keepdims=True)
    return uf, vc * lax.rsqrt(var + EPS) * lng + lnb


def _route_and_store(x, mix_scr, wout_ref, g2_ref, wr_ref, br_ref, tri_ref, upper_ref,
                     x2_ref, h2_ref, lp_ref, tw_ref, ct_ref, of_ref, wp_ref):
    h2 = _out_proj(x, mix_scr, wout_ref, g2_ref, x2_ref, h2_ref)
    _route(h2, wr_ref, br_ref, tri_ref, upper_ref, lp_ref, tw_ref, ct_ref, of_ref, wp_ref)


def _out_proj(x, mix_scr, wout_ref, g2_ref, x2_ref, h2_ref):
    x2 = x + _dot(mix_scr[...], wout_ref[...])
    x2_ref[...] = x2
    h2 = _rmsnorm(x2, g2_ref[...])
    _store_rows(h2_ref, 0, h2)
    return h2


def _route(h2, wr_ref, br_ref, tri_ref, upper_ref, lp_ref, tw_ref, ct_ref, of_ref, wp_ref):
    _route_place(_route_choose(h2, wr_ref, br_ref), tri_ref, upper_ref, lp_ref, tw_ref, ct_ref, of_ref, wp_ref)


def _route_choose(h2, wr_ref, br_ref):
    tile = h2.shape[0]
    logits = _dot(h2.astype(BF16), wr_ref[...]) + br_ref[...]

    iota_e = lax.broadcasted_iota(I32, (tile, N_EXPERTS), 1)
    iota_ef = iota_e.astype(F32)
    lg = logits
    vals = []
    idxs = []
    for _ in range(TOP_K):
        m = jnp.max(lg, axis=-1, keepdims=True)
        idx = jnp.min(jnp.where(lg == m, iota_ef, float(N_EXPERTS)), axis=-1, keepdims=True).astype(I32)
        vals.append(m)
        idxs.append(idx)
        lg = jnp.where(iota_e == idx, -jnp.inf, lg)
    exps = [jnp.exp(vk - vals[0]) for vk in vals]
    denom = exps[0] + exps[1] + exps[2] + exps[3]
    return idxs, exps, denom


def _route_place(choice, tri_ref, upper_ref, lp_ref, tw_ref, ct_ref, of_ref, wp_ref):
    idxs, exps, denom = choice
    tile = denom.shape[0]
    iota_e = lax.broadcasted_iota(I32, (tile, N_EXPERTS), 1)
    iota_k = lax.broadcasted_iota(I32, (tile, TOP_K), 1)
    onehot = jnp.zeros((tile, N_EXPERTS), F32)
    for idx in idxs:
        onehot = onehot + (iota_e == idx).astype(F32)
    rank = _dot(tri_ref[...], onehot.astype(BF16))
    count = jnp.sum(onehot, axis=0, keepdims=True)
    offset = _exact_count_dot(jnp.broadcast_to(count, (8, N_EXPERTS)), upper_ref[...], counts_on_left=True)[0:1]
    where_to = rank + offset
    lp = jnp.zeros((tile, TOP_K), I32)
    tw = jnp.zeros((tile, TOP_K), F32)
    word_rows = []
    for kk in range(TOP_K):
        pos = jnp.sum(jnp.where(iota_e == idxs[kk], where_to, 0.0), axis=-1, keepdims=True)
        lp = jnp.where(iota_k == kk, pos.astype(I32), lp)
        tw = jnp.where(iota_k == kk, exps[kk] / denom, tw)
        word_rows.append(jnp.transpose(jnp.broadcast_to(pos * float(ROW_CHUNKS), (tile, LANES)))[0:1, :])
    lp_ref[...] = lp
    tw_ref[...] = tw
    ct_ref[0] = count
    of_ref[0] = offset
    wp_ref[0] = jnp.concatenate(word_rows, axis=1).astype(I32)


def _prompt_kernel(x_ref, g1_ref, win_ref, cq_ref, sq_ref, dmat_ref, qdec_ref, kdec_ref,
                   sdec_ref, lng_ref, lnb_ref, ws_ref, bst_ref, wout_ref, g2_ref, wr_ref, br_ref,
                   tri_ref, upper_ref,
                   x2_ref, h2_ref, lp_ref, tw_ref, ct_ref, of_ref, wp_ref, st_ref,
                   s_scr, mix_scr, x2_carry, *, ret_block, tiles_per_seq, n_tiles):
    step = pl.program_id(0)

    @pl.when(step == 0)
    def _():
        s_scr[...] = jnp.zeros_like(s_scr)
        x2_carry[...] = jnp.zeros_like(x2_carry)

    def norm_prev_tile():
        h2_prev = _rmsnorm(x2_carry[...], g2_ref[...])
        _store_rows(h2_ref, 0, h2_prev)
        return _route_choose(h2_prev, wr_ref, br_ref)

    def place_prev_tile(choice):
        _route_place(choice, tri_ref, upper_ref, lp_ref, tw_ref, ct_ref, of_ref, wp_ref)

    @pl.when(step == n_tiles)
    def _():
        place_prev_tile(norm_prev_tile())

    @pl.when(step < n_tiles)
    def _():
        _prompt_step(step, norm_prev_tile, place_prev_tile, x_ref, g1_ref, win_ref, cq_ref, sq_ref,
                     dmat_ref, qdec_ref, kdec_ref, sdec_ref, lng_ref, lnb_ref, ws_ref, bst_ref, wout_ref,
                     x2_ref, st_ref, s_scr, mix_scr, x2_carry, ret_block=ret_block, tiles_per_seq=tiles_per_seq)


def _prompt_step(step, norm_prev_tile, place_prev_tile, x_ref, g1_ref, win_ref, cq_ref, sq_ref,
                 dmat_ref, qdec_ref, kdec_ref, sdec_ref, lng_ref, lnb_ref, ws_ref, bst_ref, wout_ref,
                 x2_ref, st_ref, s_scr, mix_scr, x2_carry, *, ret_block, tiles_per_seq):
    tile = TOKEN_TILE
    t = lax.rem(step, tiles_per_seq)

    choice = norm_prev_tile()
    s_scr[...] = jnp.where(t == 0, 0.0, s_scr[...])

    x = x_ref[0]
    h = _rmsnorm(x, g1_ref[...]).astype(BF16)

    def proj(i):
        return _dot(h, win_ref[:, i * RET_WIDTH:(i + 1) * RET_WIDTH])

    q = proj(0)
    k = proj(1)
    v = proj(2)
    gate = proj(3)
    u = proj(4)
    vg = proj(5)
    place_prev_tile(choice)
    cq = cq_ref[...]
    sq = sq_ref[...]
    ck = cq * K_SCALE
    sk = sq * K_SCALE

    for hd in range(N_HEADS):
        cols = slice(hd * HEAD_DIM, (hd + 1) * HEAD_DIM)
        qr = _rotary(q[:, cols], cq, sq)
        kr = _rotary(k[:, cols], ck, sk)
        vh = v[:, cols].astype(BF16)
        dm = dmat_ref[hd]
        qd = qdec_ref[hd]
        kd = kdec_ref[hd]
        sd = sdec_ref[hd]
        for c in range(tile // ret_block):
            rows = slice(c * ret_block, (c + 1) * ret_block)
            qb = qr[rows].astype(BF16)
            kb = kr[rows]
            vb = vh[rows]
            state = s_scr[hd]
            scores = _dot_nt(qb, kb.astype(BF16)) * dm
            o = _dot(scores.astype(BF16), vb) + qd * _dot(qb, state.astype(BF16))
            s_scr[hd] = sd * state + _dot_tn((kb * kd).astype(BF16), vb)
            mix_scr[rows, cols] = _gated_head_norm(o, gate[rows, cols])

    uf, vn = _gmlp_inputs(u, vg, lng_ref[...], lnb_ref[...])
    vnb = vn.astype(BF16)
    r_i = lax.broadcasted_iota(I32, (GM_CHUNK, GM_CHUNK), 0)
    c_i = lax.broadcasted_iota(I32, (GM_CHUNK, GM_CHUNK), 1)
    for hd in range(N_HEADS):
        cols = slice(hd * HEAD_DIM, (hd + 1) * HEAD_DIM)
        w_tril = jnp.where(r_i >= c_i, ws_ref[hd], 0.0).astype(BF16)
        bias = bst_ref[:, hd:hd + 1]
        for c in range(tile // GM_CHUNK):
            rows = slice(c * GM_CHUNK, (c + 1) * GM_CHUNK)
            mixed = _dot(w_tril, vnb[rows, cols]) + bias
            mix_scr[rows, RET_WIDTH + hd * HEAD_DIM:RET_WIDTH + (hd + 1) * HEAD_DIM] = (
                uf[rows, cols] * mixed).astype(BF16)

    x2 = x + _dot(mix_scr[...], wout_ref[...])
    x2_ref[...] = x2
    x2_carry[...] = x2

    @pl.when(t == tiles_per_seq - 1)
    def _():
        st_ref[0] = s_scr[...]


def _decode_kernel(x_ref, s0_ref, g1_ref, win_ref, cq_ref, sq_ref, qdec_ref, kdec_ref,
                   sdec_ref, lng_ref, lnb_ref, ws_ref, bst_ref, wout_ref, g2_ref, wr_ref, br_ref,
                   tri_ref, upper_ref,
                   x2_ref, h2_ref, lp_ref, tw_ref, ct_ref, of_ref, wp_ref, st_ref, vn_ref,
                   mix_scr, q_scr, k_scr, v_scr, oc_scr, st_stage, st_sem, *, n_streams, frames):
    tile = n_streams * frames
    x = x_ref[...]
    h = _rmsnorm(x, g1_ref[...]).astype(BF16)

    def proj(i):
        return _dot(h, win_ref[:, i * RET_WIDTH:(i + 1) * RET_WIDTH])

    q = proj(0)
    k = proj(1)
    v = proj(2)
    gate = proj(3)
    u = proj(4)
    vg = proj(5)
    cq = cq_ref[...]
    sq = sq_ref[...]
    ck = cq * K_SCALE
    sk = sq * K_SCALE

    r_i = lax.broadcasted_iota(I32, (tile, tile), 0)
    c_i = lax.broadcasted_iota(I32, (tile, tile), 1)
    shift = frames.bit_length() - 1
    r_frame = lax.bitwise_and(r_i, frames - 1)
    c_frame = lax.bitwise_and(c_i, frames - 1)
    keep = (lax.shift_right_logical(r_i, shift) == lax.shift_right_logical(c_i, shift)) & (r_frame >= c_frame)
    frame_gap = (r_frame - c_frame).astype(F32)

    o_in = []
    for hd in range(N_HEADS):
        cols = slice(hd * HEAD_DIM, (hd + 1) * HEAD_DIM)
        qr = _rotary(q[:, cols], cq, sq)
        kr = _rotary(k[:, cols], ck, sk)
        vh = v[:, cols].astype(BF16)
        qb = qr.astype(BF16)
        decay = jnp.where(keep, jnp.exp(frame_gap * _LOG_GAMMA[hd]), 0.0)
        scores = _dot_nt(qb, kr.astype(BF16)) * decay
        o_in.append(_dot(scores.astype(BF16), vh))
        q_scr[hd] = qb
        k_scr[hd] = (kr * kdec_ref[hd]).astype(BF16)
        v_scr[hd] = vh

    def state_copy(b):
        slot = lax.rem(b, 2)
        return pltpu.make_async_copy(st_stage.at[slot], st_ref.at[b], st_sem.at[slot])

    def stream_body(b, carry):
        rows = pl.ds(pl.multiple_of(b * frames, frames), frames)
        slot = lax.rem(b, 2)

        @pl.when(b >= 2)
        def _():
            state_copy(b - 2).wait()

        for hd in range(N_HEADS):
            state = s0_ref[b, hd]
            oc_scr[hd, rows, :] = _dot(q_scr[hd, rows, :], state.astype(BF16))
            st_stage[slot, hd] = sdec_ref[hd] * state + _dot_tn(k_scr[hd, rows, :], v_scr[hd, rows, :])
        state_copy(b).start()
        return carry

    lax.fori_loop(0, n_streams, stream_body, 0)
    state_copy(n_streams - 2).wait()
    state_copy(n_streams - 1).wait()

    for hd in range(N_HEADS):
        cols = slice(hd * HEAD_DIM, (hd + 1) * HEAD_DIM)
        o = o_in[hd] + qdec_ref[hd] * oc_scr[hd]
        mix_scr[:, cols] = _gated_head_norm(o, gate[:, cols])

    uf, vn = _gmlp_inputs(u, vg, lng_ref[...], lnb_ref[...])
    vn_ref[...] = vn
    vnb = vn.astype(BF16)
    sel = (lax.bitwise_and(lax.broadcasted_iota(I32, (tile, frames), 0), frames - 1)
           == lax.broadcasted_iota(I32, (tile, frames), 1)).astype(BF16)
    for hd in range(N_HEADS):
        cols = slice(hd * HEAD_DIM, (hd + 1) * HEAD_DIM)
        w_rows = _dot(sel, ws_ref[hd].astype(BF16)).astype(BF16)
        w_blk = jnp.where(keep, _dot_nt(w_rows, sel), 0.0).astype(BF16)
        mixed = _dot(w_blk, vnb[:, cols]) + bst_ref[:, hd:hd + 1]
        mix_scr[:, RET_WIDTH + hd * HEAD_DIM:RET_WIDTH + (hd + 1) * HEAD_DIM] = (uf[:, cols] * mixed).astype(BF16)

    _route_and_store(x, mix_scr, wout_ref, g2_ref, wr_ref, br_ref, tri_ref, upper_ref,
                     x2_ref, h2_ref, lp_ref, tw_ref, ct_ref, of_ref, wp_ref)


def _rope_tables(pos):
    half = HEAD_DIM // 2
    inv = np.float32(ROPE_BASE) ** (-np.arange(half, dtype=np.float32) / np.float32(half))
    ang = pos.astype(np.float32)[:, None] * inv[None, :]
    cos = np.cos(ang).astype(np.float32)
    sin = np.sin(ang).astype(np.float32)
    return np.concatenate([cos, cos], axis=-1), np.concatenate([-sin, sin], axis=-1)


def _decay_tables(block):
    log_g = np.asarray(_LOG_GAMMA, np.float32)
    idx = np.arange(block, dtype=np.float32)
    diff = idx[:, None] - idx[None, :]
    dmat = np.where(diff[None] >= 0, np.exp(np.maximum(diff, 0.0)[None] * log_g[:, None, None]), 0.0)
    q_dec = np.exp((idx + 1.0)[None, :] * log_g[:, None])
    k_dec = np.exp((block - 1.0 - idx)[None, :] * log_g[:, None])
    s_dec = np.exp(np.float32(block) * log_g)
    bc = lambda a: np.ascontiguousarray(
        np.broadcast_to(a[:, :, None], (N_HEADS, block, HEAD_DIM)).astype(np.float32))
    s_row = np.ascontiguousarray(np.broadcast_to(s_dec[:, None, None], (N_HEADS, 1, HEAD_DIM)).astype(np.float32))
    return dmat.astype(np.float32), bc(q_dec), bc(k_dec), s_row


def _routing_constants(tile):
    tri = jnp.asarray(np.arange(tile)[:, None] > np.arange(tile)[None, :], BF16)
    upper = jnp.asarray(np.arange(N_EXPERTS)[:, None] < np.arange(N_EXPERTS)[None, :], BF16)
    return tri, upper


def _front_out(n_tok, n_tiles):
    return [
        jax.ShapeDtypeStruct((n_tok, D_MODEL), F32),
        jax.ShapeDtypeStruct((n_tok * ROW_CHUNKS, LANES), U32),
        jax.ShapeDtypeStruct((n_tok, TOP_K), I32),
        jax.ShapeDtypeStruct((n_tok, TOP_K), F32),
        jax.ShapeDtypeStruct((n_tiles, 1, N_EXPERTS), F32),
        jax.ShapeDtypeStruct((n_tiles, 1, N_EXPERTS), F32),
        jax.ShapeDtypeStruct((n_tiles, 1, TILE_SLOTS), I32),
    ]


def _prompt_call(x, p):
    bsz, seq, _ = x.shape
    tile = TOKEN_TILE
    n_t = seq // tile
    cq, sq = _rope_tables(np.arange(seq))
    dmat, qdec, kdec, sdec = _decay_tables(PROMPT_RET_BLOCK)
    tri, upper = _routing_constants(tile)
    ws = p['w_s'][:, :GM_CHUNK, :GM_CHUNK]
    bst = p['b_s'][:, :GM_CHUNK].T

    n_tiles = bsz * n_t
    cur = lambda s: jnp.minimum(s, n_tiles - 1)
    prev = lambda s: jnp.maximum(s - 1, 0)
    const = lambda shape: pl.BlockSpec(shape, lambda s: (0,) * len(shape))
    pos = lambda: pl.BlockSpec((tile, HEAD_DIM), lambda s: (lax.rem(cur(s), n_t), 0))
    in_specs = [
        pl.BlockSpec((1, tile, D_MODEL), lambda s: (cur(s) // n_t, lax.rem(cur(s), n_t), 0)),
        const((1, D_MODEL)), const((D_MODEL, 6 * RET_WIDTH)),
        pos(), pos(),
        const((N_HEADS, PROMPT_RET_BLOCK, PROMPT_RET_BLOCK)),
        const((N_HEADS, PROMPT_RET_BLOCK, HEAD_DIM)), const((N_HEADS, PROMPT_RET_BLOCK, HEAD_DIM)),
        const((N_HEADS, 1, HEAD_DIM)),
        const((1, GM_WIDTH)), const((1, GM_WIDTH)),
        const((N_HEADS, GM_CHUNK, GM_CHUNK)), const((GM_CHUNK, N_HEADS)),
        const((D_MODEL, D_MODEL)), const((1, D_MODEL)),
        const((D_MODEL, N_EXPERTS)), const((1, N_EXPERTS)),
        const((tile, tile)), const((N_EXPERTS, N_EXPERTS)),
    ]
    out_specs = [
        pl.BlockSpec((tile, D_MODEL), lambda s: (cur(s), 0)),
        pl.BlockSpec((tile * ROW_CHUNKS, LANES), lambda s: (prev(s), 0)),
        pl.BlockSpec((tile, TOP_K), lambda s: (prev(s), 0)),
        pl.BlockSpec((tile, TOP_K), lambda s: (prev(s), 0)),
        pl.BlockSpec((1, 1, N_EXPERTS), lambda s: (prev(s), 0, 0)),
        pl.BlockSpec((1, 1, N_EXPERTS), lambda s: (prev(s), 0, 0)),
        pl.BlockSpec((1, 1, TILE_SLOTS), lambda s: (prev(s), 0, 0)),
        pl.BlockSpec((1, N_HEADS, HEAD_DIM, HEAD_DIM), lambda s: (cur(s) // n_t, 0, 0, 0)),
    ]
    out_shape = _front_out(bsz * seq, n_tiles) + [jax.ShapeDtypeStruct((bsz, N_HEADS, HEAD_DIM, HEAD_DIM), F32)]
    return pl.pallas_call(
        functools.partial(_prompt_kernel, ret_block=PROMPT_RET_BLOCK, tiles_per_seq=n_t, n_tiles=n_tiles),
        grid=(n_tiles + 1,),
        in_specs=in_specs,
        out_specs=out_specs,
        out_shape=out_shape,
        scratch_shapes=[
            pltpu.VMEM((N_HEADS, HEAD_DIM, HEAD_DIM), F32),
            pltpu.VMEM((tile, D_MODEL), BF16),
            pltpu.VMEM((tile, D_MODEL), F32),
        ],
        compiler_params=pltpu.CompilerParams(
            dimension_semantics=("arbitrary",), vmem_limit_bytes=VMEM_LIMIT),
        name="front_prompt",
    )(x, p['norm1_g'], p['w_in'], cq, sq, dmat, qdec, kdec, sdec, p['ln_v_g'], p['ln_v_b'], ws, bst,
      p['w_out'], p['norm2_g'], p['w_router'], p['b_router'], tri, upper)


def _decode_call(x, state, past_len, p):
    n_streams, frames, _ = x.shape
    tile = n_streams * frames
    assert tile == TOKEN_TILE and frames <= RET_CHUNK and frames & (frames - 1) == 0
    cq, sq = [np.tile(a, (n_streams, 1)) for a in _rope_tables(past_len + np.arange(frames))]
    _, qdec, kdec, sdec = _decay_tables(frames)
    qdec = np.tile(qdec, (1, n_streams, 1))
    kdec = np.tile(kdec, (1, n_streams, 1))
    tri, upper = _routing_constants(tile)
    ws = p['w_s'][:, :frames, :frames]
    bst = jnp.tile(p['b_s'][:, :frames].T, (n_streams, 1))

    def whole(a):
        return pl.BlockSpec(a.shape, lambda i, n=a.ndim: (0,) * n, pipeline_mode=pl.Buffered(1))

    args = (x.reshape(tile, D_MODEL), state, p['norm1_g'], p['w_in'], cq, sq, qdec, kdec, sdec,
            p['ln_v_g'], p['ln_v_b'], ws, bst, p['w_out'], p['norm2_g'], p['w_router'], p['b_router'],
            tri, upper)
    out_shape = _front_out(tile, 1) + [
        jax.ShapeDtypeStruct(state.shape, F32),
        jax.ShapeDtypeStruct((tile, GM_WIDTH), F32),
    ]
    out_specs = [pl.BlockSpec(s.shape, lambda i, n=len(s.shape): (0,) * n) for s in out_shape]
    out_specs[7] = pl.BlockSpec(memory_space=pl.ANY)
    return pl.pallas_call(
        functools.partial(_decode_kernel, n_streams=n_streams, frames=frames),
        grid=(1,),
        in_specs=[whole(a) for a in args],
        out_specs=out_specs,
        out_shape=out_shape,
        scratch_shapes=[
            pltpu.VMEM((tile, D_MODEL), BF16),
            pltpu.VMEM((N_HEADS, tile, HEAD_DIM), BF16),
            pltpu.VMEM((N_HEADS, tile, HEAD_DIM), BF16),
            pltpu.VMEM((N_HEADS, tile, HEAD_DIM), BF16),
            pltpu.VMEM((N_HEADS, tile, HEAD_DIM), F32),
            pltpu.VMEM((2, N_HEADS, HEAD_DIM, HEAD_DIM), F32),
            pltpu.SemaphoreType.DMA((2,)),
        ],
        compiler_params=pltpu.CompilerParams(
            dimension_semantics=("arbitrary",), vmem_limit_bytes=VMEM_LIMIT),
        name="front_decode",
    )(*args)


def _plan_kernel(ct_ref, lower_ref, upper_ref, cum_ref, nb_ref, bs_ref, be_ref, nbt_ref, tlo_ref, thi_ref,
                 *, n_blocks):
    counts = ct_ref[...]
    cum = _exact_count_dot(lower_ref[...], counts, counts_on_left=False)
    cum_ref[...] = cum.astype(I32)
    n_tiles = counts.shape[0]
    total = cum[n_tiles:n_tiles + 1]
    nb = jnp.floor((total + (EXPERT_ROWS - 1)) * (1.0 / EXPERT_ROWS))
    bstart = _dot(jnp.broadcast_to(nb, (8, N_EXPERTS)).astype(BF16), upper_ref[...])[0:1]
    bend = bstart + nb
    nb_ref[...] = nb.astype(I32)
    bs_ref[...] = bstart.astype(I32)
    blk = lax.broadcasted_iota(I32, (n_blocks, N_EXPERTS), 0).astype(F32)
    be = jnp.minimum(jnp.sum((bend <= blk).astype(F32), axis=-1, keepdims=True), N_EXPERTS - 1.0)
    be_ref[...] = be.astype(I32)
    nbt_ref[...] = jnp.sum(nb, axis=-1, keepdims=True).astype(I32)
    mine = lax.broadcasted_iota(I32, (n_blocks, N_EXPERTS), 1).astype(F32) == be
    pick = lambda row: jnp.sum(jnp.where(mine, row, 0.0), axis=-1, keepdims=True)
    r0 = (blk[:, 0:1] - pick(bstart)) * EXPERT_ROWS
    tlo = jnp.zeros((n_blocks, 1), F32)
    thi = jnp.zeros((n_blocks, 1), F32)
    seg_lo = pick(cum[0:1])
    for i in range(n_tiles):
        seg_hi = pick(cum[i + 1:i + 2])
        tlo = tlo + (seg_hi <= r0).astype(F32)
        thi = thi + (seg_lo < r0 + EXPERT_ROWS).astype(F32)
        seg_lo = seg_hi
    tlo_ref[...] = tlo.astype(I32)
    thi_ref[...] = thi.astype(I32)


def _plan_call(counts, n_blocks):
    n_tiles = counts.shape[0]
    lower = (jnp.arange(n_tiles + 1)[:, None] > jnp.arange(n_tiles)[None, :]).astype(BF16)
    upper = (jnp.arange(N_EXPERTS)[:, None] < jnp.arange(N_EXPERTS)[None, :]).astype(BF16)
    return pl.pallas_call(
        functools.partial(_plan_kernel, n_blocks=n_blocks),
        out_shape=[
            jax.ShapeDtypeStruct((n_tiles + 1, N_EXPERTS), I32),
            jax.ShapeDtypeStruct((1, N_EXPERTS), I32),
            jax.ShapeDtypeStruct((1, N_EXPERTS), I32),
            jax.ShapeDtypeStruct((n_blocks, 1), I32),
            jax.ShapeDtypeStruct((1, 1), I32),
            jax.ShapeDtypeStruct((n_blocks, 1), I32),
            jax.ShapeDtypeStruct((n_blocks, 1), I32),
        ],
        name="plan",
    )(counts, lower, upper)


def _dispatch_kernel(h2p_ref, h2s_ref, wp_ref, out_ref, *, n_prompt_tiles):
    i = pl.program_id(0)

    def permute(src_ref):
        def body(t, c):
            row = src_ref[_row_tile(t), :]
            for kk in range(TOP_K):
                first = pl.multiple_of(wp_ref[0, 0, kk * TOKEN_TILE + t], ROW_CHUNKS)
                out_ref[pl.ds(first, ROW_CHUNKS), :] = row
            return c

        lax.fori_loop(0, TOKEN_TILE, body, 0, unroll=8)

    @pl.when(i < n_prompt_tiles)
    def _():
        permute(h2p_ref)

    @pl.when(i >= n_prompt_tiles)
    def _():
        permute(h2s_ref)


def _dispatch_call(h2_p, h2_s, lp_tiles):
    rows = TOKEN_TILE * ROW_CHUNKS
    n_p_tiles = h2_p.shape[0] // rows
    n_s_tiles = h2_s.shape[0] // rows
    n_tiles = n_p_tiles + n_s_tiles
    return pl.pallas_call(
        functools.partial(_dispatch_kernel, n_prompt_tiles=n_p_tiles),
        grid=(n_tiles,),
        in_specs=[
            pl.BlockSpec((rows, LANES), lambda i: (jnp.minimum(i, n_p_tiles - 1), 0)),
            pl.BlockSpec((rows, LANES), lambda i: (jnp.maximum(i - n_p_tiles, 0), 0)),
            pl.BlockSpec((1, 1, TILE_SLOTS), lambda i: (i, 0, 0), memory_space=pltpu.SMEM),
        ],
        out_specs=pl.BlockSpec((TILE_SLOTS * ROW_CHUNKS, LANES), lambda i: (i, 0)),
        out_shape=jax.ShapeDtypeStruct((n_tiles * TILE_SLOTS * ROW_CHUNKS, LANES), U32),
        compiler_params=pltpu.CompilerParams(
            dimension_semantics=("arbitrary",), vmem_limit_bytes=VMEM_LIMIT),
        name="dispatch",
    )(h2_p, h2_s, lp_tiles)


def _expert_kernel(be_ref, bs_ref, nb_ref, nbt_ref, cum_ref, off_ref, tlo_ref, thi_ref,
                   xb_ref, wg_ref, wu_ref, wd_ref, bg_ref, bu_ref, bd_ref,
                   yb_ref, xbuf, ybuf, wbf, in_sem, out_sem, *, n_tiles):
    b = pl.program_id(0)
    nbt = nbt_ref[0]

    def block_rows(blk):
        e = be_ref[blk]
        r0 = (blk - bs_ref[e]) * EXPERT_ROWS
        return r0, jnp.minimum(cum_ref[n_tiles * N_EXPERTS + e] - r0, EXPERT_ROWS)

    def segments(blk, slot, fn):
        e = be_ref[blk]
        r0 = (blk - bs_ref[e]) * EXPERT_ROWS

        def body(i, c):
            seg_lo = cum_ref[i * N_EXPERTS + e]
            seg_hi = cum_ref[(i + 1) * N_EXPERTS + e]
            lo = jnp.maximum(seg_lo, r0)
            hi = jnp.minimum(seg_hi, r0 + EXPERT_ROWS)

            @pl.when(hi > lo)
            def _():
                bucket_row = i * TILE_SLOTS + off_ref[i * N_EXPERTS + e] + (lo - seg_lo)
                fn(slot, bucket_row, slot * EXPERT_ROWS + (lo - r0), hi - lo)
            return c

        lax.fori_loop(tlo_ref[blk], thi_ref[blk], body, 0)

    def words(row, n):
        return pl.ds(pl.multiple_of(row * ROW_CHUNKS, ROW_CHUNKS), n * ROW_CHUNKS)

    def gather(slot, bucket_row, buffer_row, n):
        return pltpu.make_async_copy(
            xb_ref.at[words(bucket_row, n)], xbuf.at[words(buffer_row, n)], in_sem.at[slot])

    def scatter(slot, bucket_row, buffer_row, n):
        return pltpu.make_async_copy(
            ybuf.at[words(buffer_row, n)], yb_ref.at[words(bucket_row, n)], out_sem.at[slot])

    def start_gather(blk):
        @pl.when(blk < nbt)
        def _():
            segments(blk, lax.rem(blk, 2), lambda *a: gather(*a).start())

    def wait_gather(blk):
        @pl.when(blk < nbt)
        def _():
            slot = lax.rem(blk, 2)
            gather(slot, 0, slot * EXPERT_ROWS, block_rows(blk)[1]).wait()

    def start_scatter(blk):
        @pl.when(blk < nbt)
        def _():
            segments(blk, lax.rem(blk, 2), lambda *a: scatter(*a).start())

    def wait_scatter(blk):
        @pl.when((blk >= 0) & (blk < nbt))
        def _():
            slot = lax.rem(blk, 2)
            scatter(slot, 0, slot * EXPERT_ROWS, block_rows(blk)[1]).wait()

    def mlp(base, n_rows, rows_left):
        valid = lax.broadcasted_iota(I32, (n_rows, 1), 0) < rows_left
        x = jnp.where(valid, _load_rows(xbuf, base, n_rows), 0.0)
        gt = _dot(x, wbf[0]) + bg_ref[0]
        up = _dot(x, wbf[1]) + bu_ref[0]
        gt = jnp.minimum(gt, SWIGLU_LIMIT)
        up = jnp.clip(up, -SWIGLU_LIMIT, SWIGLU_LIMIT)
        act = gt * _sigmoid(gt * SWIGLU_ALPHA) * (up + 1.0)
        _store_rows(ybuf, base, _dot(act.astype(BF16), wbf[2]) + bd_ref[0])

    def block_body(blk, carry):
        wait_gather(blk)
        start_gather(blk + 1)
        wait_scatter(blk - 2)
        rows_left = block_rows(blk)[1]
        base = lax.rem(blk, 2) * EXPERT_ROWS

        @pl.when(rows_left > EXPERT_SUB_ROWS)
        def _():
            mlp(base, EXPERT_ROWS, rows_left)

        @pl.when(rows_left <= EXPERT_SUB_ROWS)
        def _():
            mlp(base, EXPERT_SUB_ROWS, rows_left)

        start_scatter(blk)
        return carry

    first_block = bs_ref[b]
    n_mine = nb_ref[b]

    @pl.when(b == 0)
    def _():
        xbuf[...] = jnp.zeros_like(xbuf)
        start_gather(0)

    @pl.when(n_mine > 0)
    def _():
        wbf[0] = wg_ref[0].astype(BF16)
        wbf[1] = wu_ref[0].astype(BF16)
        wbf[2] = wd_ref[0].astype(BF16)

    lax.fori_loop(first_block, first_block + n_mine, block_body, 0)

    @pl.when(b == N_EXPERTS - 1)
    def _():
        wait_scatter(nbt - 2)
        wait_scatter(nbt - 1)


def _expert_call(be, bstart, nb, nbt, cum_flat, off_flat, tlo, thi, buckets, p, n_tiles):
    wspec = pl.BlockSpec((1, D_MODEL, D_MODEL), lambda e, *_: (e, 0, 0))
    bspec = pl.BlockSpec((1, 1, D_MODEL), lambda e, *_: (e, 0, 0))
    any_spec = pl.BlockSpec(memory_space=pl.ANY)
    return pl.pallas_call(
        functools.partial(_expert_kernel, n_tiles=n_tiles),
        grid_spec=pltpu.PrefetchScalarGridSpec(
            num_scalar_prefetch=8,
            grid=(N_EXPERTS,),
            in_specs=[any_spec, wspec, wspec, wspec, bspec, bspec, bspec],
            out_specs=any_spec,
            scratch_shapes=[
                pltpu.VMEM((2 * EXPERT_ROWS * ROW_CHUNKS, LANES), U32),
                pltpu.VMEM((2 * EXPERT_ROWS * ROW_CHUNKS, LANES), U32),
                pltpu.VMEM((3, D_MODEL, D_MODEL), BF16),
                pltpu.SemaphoreType.DMA((2,)),
                pltpu.SemaphoreType.DMA((2,)),
            ],
        ),
        out_shape=jax.ShapeDtypeStruct(buckets.shape, U32),
        compiler_params=pltpu.CompilerParams(
            dimension_semantics=("arbitrary",), vmem_limit_bytes=VMEM_LIMIT),
        name="experts",
    )(be, bstart, nb, nbt, cum_flat, off_flat, tlo, thi, buckets, p['w_gate'], p['w_up'], p['w_down'],
      p['b_gate'][:, None, :], p['b_up'][:, None, :], p['b_down'][:, None, :])


def _combine_kernel(x2_ref, yb_ref, lp_ref, tw_ref, gf_ref, y_ref, ybf, pick):
    for c in range(TILE_SLOTS // COMBINE_CHUNK):
        rows = slice(c * COMBINE_CHUNK, (c + 1) * COMBINE_CHUNK)
        ybf[rows, :] = _load_rows(yb_ref, c * COMBINE_CHUNK, COMBINE_CHUNK)
    slot = lax.broadcasted_iota(I32, (TOKEN_TILE, TILE_SLOTS), 1)
    lp = lp_ref[...]
    tw = tw_ref[...]
    sel = jnp.zeros((TOKEN_TILE, TILE_SLOTS), F32)
    for kk in range(TOP_K):
        sel = jnp.where(slot == lp[:, kk:kk + 1], tw[:, kk:kk + 1], sel)
    pick[...] = sel.astype(BF16)
    y_ref[...] = _rmsnorm(x2_ref[...] + _dot(pick[...], ybf[...]), gf_ref[...])


def _combine_call(x2, ybuckets, lp, tw, gf, tile_off):
    n_tok = x2.shape[0]
    n_tiles = n_tok // TOKEN_TILE
    return pl.pallas_call(
        _combine_kernel,
        grid=(n_tiles,),
        in_specs=[
            pl.BlockSpec((TOKEN_TILE, D_MODEL), lambda i: (i, 0)),
            pl.BlockSpec((TILE_SLOTS * ROW_CHUNKS, LANES), lambda i: (tile_off + i, 0)),
            pl.BlockSpec((TOKEN_TILE, TOP_K), lambda i: (i, 0)),
            pl.BlockSpec((TOKEN_TILE, TOP_K), lambda i: (i, 0)),
            pl.BlockSpec((1, D_MODEL), lambda i: (0, 0)),
        ],
        out_specs=pl.BlockSpec((TOKEN_TILE, D_MODEL), lambda i: (i, 0)),
        scratch_shapes=[
            pltpu.VMEM((TILE_SLOTS, D_MODEL), BF16),
            pltpu.VMEM((TOKEN_TILE, TILE_SLOTS), BF16),
        ],
        out_shape=jax.ShapeDtypeStruct((n_tok, D_MODEL), F32),
        compiler_params=pltpu.CompilerParams(
            dimension_semantics=("arbitrary",), vmem_limit_bytes=VMEM_LIMIT),
        name="combine",
    )(x2, ybuckets, lp, tw, gf)


def kernel(x_prompt, x_sample, state_ret, norm1_g, w_in, ln_v_g, ln_v_b, w_s, b_s, w_out, norm2_g, w_router, b_router, w_gate, b_gate, w_up, b_up, w_down, b_down, norm_f_g):
    bsz, seq, _ = x_prompt.shape
    dbsz, dseq, _ = x_sample.shape
    n_p = bsz * seq
    n_s = dbsz * dseq
    n_total = n_p + n_s
    past_len = 2048
    assert w_in.shape[0] == 1, "single layer"

    p = dict(norm1_g=norm1_g, w_in=w_in[0].astype(BF16), ln_v_g=ln_v_g, ln_v_b=ln_v_b, w_s=w_s[0], b_s=b_s[0],
             w_out=w_out[0].astype(BF16), norm2_g=norm2_g, w_router=w_router[0].astype(BF16), b_router=b_router,
             w_gate=w_gate[0], b_gate=b_gate[0], w_up=w_up[0], b_up=b_up[0], w_down=w_down[0], b_down=b_down[0])

    x2_p, h2_p, lp_p, tw_p, ct_p, of_p, wp_p, st_p = _prompt_call(x_prompt, p)
    x2_s, h2_s, lp_s, tw_s, ct_s, of_s, wp_s, st_s, vn_s = _decode_call(x_sample, state_ret[0], past_len, p)

    n_tiles = n_total // TOKEN_TILE
    word_pos = jnp.concatenate([wp_p, wp_s], axis=0)
    counts = jnp.concatenate([ct_p, ct_s], axis=0).reshape(n_tiles, N_EXPERTS)
    offsets = jnp.concatenate([of_p, of_s], axis=0).reshape(n_tiles * N_EXPERTS).astype(I32)

    n_blocks = -(-(n_total * TOP_K + N_EXPERTS * (EXPERT_ROWS - 1)) // EXPERT_ROWS)
    cum, nb, bstart, be, nbt, tlo, thi = _plan_call(counts, n_blocks)

    buckets = _dispatch_call(h2_p, h2_s, word_pos)
    ybuckets = _expert_call(be.reshape(n_blocks), bstart.reshape(N_EXPERTS), nb.reshape(N_EXPERTS), nbt.reshape(1),
                            cum.reshape((n_tiles + 1) * N_EXPERTS), offsets, tlo.reshape(n_blocks),
                            thi.reshape(n_blocks), buckets, p, n_tiles)

    y_p = _combine_call(x2_p, ybuckets, lp_p, tw_p, norm_f_g[None, :], 0)
    y_s = _combine_call(x2_s, ybuckets, lp_s, tw_s, norm_f_g[None, :], n_p // TOKEN_TILE)

    return (y_p.reshape(bsz, seq, D_MODEL), y_s.reshape(dbsz, dseq, D_MODEL),
            st_p[None], st_s[None], vn_s.reshape(1, dbsz, dseq, GM_WIDTH))
```

```python
import functools

import numpy as np
import jax
import jax.numpy as jnp
from jax import lax
from jax.experimental import pallas as pl
from jax.experimental.pallas import tpu as pltpu

F32 = jnp.float32
BF16 = jnp.bfloat16
I32 = jnp.int32

D_MODEL = 1024
RET_WIDTH = 512
N_HEADS = 4
HEAD_DIM = 128
GM_WIDTH = 512
GM_CHUNK = 128
RET_CHUNK = 64
N_EXPERTS = 32
TOP_K = 4
SWIGLU_LIMIT = 7.0
SWIGLU_ALPHA = 1.702
ROPE_BASE = 10000.0
EPS = 1e-6

TOKEN_TILE = 512
PROMPT_RET_BLOCK = 256
EXPERT_ROWS = 1024
EXPERT_PASS_PLANS = ((256, (256,)), (512, (512,)), (768, (512, 256)), (1024, (512, 512)))
TILE_SLOTS = TOKEN_TILE * TOP_K
COMBINE_CHUNK = 256
VMEM_LIMIT = 56 * 1024 * 1024
LANES = 128
U32 = jnp.uint32
ROW_CHUNKS = D_MODEL // LANES // 2
K_SCALE = HEAD_DIM ** -0.5
SPLIT = 16.0
_LOG_GAMMA = [float(np.log1p(-np.float32(2.0) ** np.float32(-5.0 - hd)).astype(np.float32)) for hd in range(N_HEADS)]


def _load_rows(ref, first_row, n_rows):
    base = first_row * ROW_CHUNKS
    lo, hi = [], []
    for c in range(ROW_CHUNKS):
        w = ref[pl.ds(base + c, n_rows, stride=ROW_CHUNKS), :]
        lo.append(pltpu.bitcast(lax.shift_left(w, jnp.uint32(16)), F32))
        hi.append(pltpu.bitcast(lax.bitwise_and(w, jnp.uint32(0xFFFF0000)), F32))
    return jnp.concatenate(lo + hi, axis=1).astype(BF16)


def _store_rows(ref, first_row, val):
    base = first_row * ROW_CHUNKS
    bits = pltpu.bitcast(val.astype(BF16).astype(F32), U32)
    half = D_MODEL // 2
    for c in range(ROW_CHUNKS):
        low = lax.shift_right_logical(bits[:, c * LANES:(c + 1) * LANES], jnp.uint32(16))
        ref[pl.ds(base + c, val.shape[0], stride=ROW_CHUNKS), :] = lax.bitwise_or(
            bits[:, half + c * LANES:half + (c + 1) * LANES], low)


def _row_tile(row):
    return pl.ds(pl.multiple_of(row * ROW_CHUNKS, ROW_CHUNKS), ROW_CHUNKS)


def _rmsnorm(x, g):
    ms = jnp.mean(x * x, axis=-1, keepdims=True)
    return x * lax.rsqrt(ms + EPS) * g


def _gelu(x):
    c = np.float32(np.sqrt(2.0 / np.pi))
    return x * (0.5 * (1.0 + jnp.tanh(c * (x + 0.044715 * (x * x * x)))))


def _sigmoid(x):
    return 1.0 / (1.0 + jnp.exp(-x))


def _dot(a, b):
    return jnp.dot(a, b, preferred_element_type=F32)


def _dot_nt(a, b):
    return lax.dot_general(a, b, (((1,), (1,)), ((), ())), preferred_element_type=F32)


def _dot_tn(a, b):
    return lax.dot_general(a, b, (((0,), (0,)), ((), ())), preferred_element_type=F32)


def _exact_count_dot(a, b, *, counts_on_left):
    cnt = a if counts_on_left else b
    hi = jnp.floor(cnt * (1.0 / SPLIT))
    lo = cnt - SPLIT * hi
    if counts_on_left:
        return SPLIT * _dot(hi.astype(BF16), b) + _dot(lo.astype(BF16), b)
    return SPLIT * _dot(a, hi.astype(BF16)) + _dot(a, lo.astype(BF16))


def _rotary(xh, cos2, sin2):
    return xh * cos2 + pltpu.roll(xh, HEAD_DIM // 2, axis=1) * sin2


def _gated_head_norm(o, gate):
    mu = jnp.mean(o, axis=-1, keepdims=True)
    oc = o - mu
    var = jnp.mean(oc * oc, axis=-1, keepdims=True)
    return ((gate * _sigmoid(gate)) * (oc * lax.rsqrt(var + EPS))).astype(BF16)


def _gmlp_inputs(u, vg, lng, lnb):
    uf = _gelu(u)
    vf = _gelu(vg)
    mu = jnp.mean(vf, axis=-1, keepdims=True)
    vc = vf - mu
    var = jnp.mean(vc * vc, axis=-1, keepdims=True)
    return uf, vc * lax.rsqrt(var + EPS) * lng + lnb


def _route_and_store(x, mix_scr, wout_ref, g2_ref, wr_ref, br_ref, tri_ref, upper_ref,
                     x2_ref, h2_ref, lp_ref, tw_ref, ct_ref, of_ref, wp_ref):
    h2 = _out_proj(x, mix_scr, wout_ref, g2_ref, x2_ref, h2_ref)
    _route(h2, wr_ref, br_ref, tri_ref, upper_ref, lp_ref, tw_ref, ct_ref, of_ref, wp_ref)


def _out_proj(x, mix_scr, wout_ref, g2_ref, x2_ref, h2_ref):
    x2 = x + _dot(mix_scr[...], wout_ref[...])
    x2_ref[...] = x2
    h2 = _rmsnorm(x2, g2_ref[...])
    _store_rows(h2_ref, 0, h2)
    return h2


def _route(h2, wr_ref, br_ref, tri_ref, upper_ref, lp_ref, tw_ref, ct_ref, of_ref, wp_ref):
    _route_place(_route_choose(h2, wr_ref, br_ref), tri_ref, upper_ref, lp_ref, tw_ref, ct_ref, of_ref, wp_ref)


def _route_choose(h2, wr_ref, br_ref):
    tile = h2.shape[0]
    logits = _dot(h2.astype(BF16), wr_ref[...]) + br_ref[...]

    iota_e = lax.broadcasted_iota(I32, (tile, N_EXPERTS), 1)
    iota_ef = iota_e.astype(F32)
    lg = logits
    vals = []
    idxs = []
    for _ in range(TOP_K):
        m = jnp.max(lg, axis=-1, keepdims=True)
        idx = jnp.min(jnp.where(lg == m, iota_ef, float(N_EXPERTS)), axis=-1, keepdims=True).astype(I32)
        vals.append(m)
        idxs.append(idx)
        lg = jnp.where(iota_e == idx, -jnp.inf, lg)
    exps = [jnp.exp(vk - vals[0]) for vk in vals]
    denom = exps[0] + exps[1] + exps[2] + exps[3]
    return idxs, exps, denom


def _route_place(choice, tri_ref, upper_ref, lp_ref, tw_ref, ct_ref, of_ref, wp_ref):
    idxs, exps, denom = choice
    tile = denom.shape[0]
    iota_e = lax.broadcasted_iota(I32, (tile, N_EXPERTS), 1)
    iota_k = lax.broadcasted_iota(I32, (tile, TOP_K), 1)
    onehot = jnp.zeros((tile, N_EXPERTS), F32)
    for idx in idxs:
        onehot = onehot + (iota_e == idx).astype(F32)
    rank = _dot(tri_ref[...], onehot.astype(BF16))
    count = jnp.sum(onehot, axis=0, keepdims=True)
    offset = _exact_count_dot(jnp.broadcast_to(count, (8, N_EXPERTS)), upper_ref[...], counts_on_left=True)[0:1]
    where_to = rank + offset
    lp = jnp.zeros((tile, TOP_K), I32)
    tw = jnp.zeros((tile, TOP_K), F32)
    word_rows = []
    for kk in range(TOP_K):
        pos = jnp.sum(jnp.where(iota_e == idxs[kk], where_to, 0.0), axis=-1, keepdims=True)
        lp = jnp.where(iota_k == kk, pos.astype(I32), lp)
        tw = jnp.where(iota_k == kk, exps[kk] / denom, tw)
        word_rows.append(jnp.transpose(jnp.broadcast_to(pos * float(ROW_CHUNKS), (tile, LANES)))[0:1, :])
    lp_ref[...] = lp
    tw_ref[...] = tw
    ct_ref[0] = count
    of_ref[0] = offset
    wp_ref[0] = jnp.concatenate(word_rows, axis=1).astype(I32)


def _prompt_kernel(x_ref, g1_ref, win_ref, cq_ref, sq_ref, dmat_ref, qdec_ref, kdec_ref,
                   sdec_ref, lng_ref, lnb_ref, ws_ref, bst_ref, wout_ref, g2_ref, wr_ref, br_ref,
                   tri_ref, upper_ref,
                   x2_ref, h2_ref, lp_ref, tw_ref, ct_ref, of_ref, wp_ref, st_ref,
                   s_scr, mix_scr, x2_carry, *, ret_block, tiles_per_seq, n_tiles):
    step = pl.program_id(0)

    @pl.when(step == 0)
    def _():
        s_scr[...] = jnp.zeros_like(s_scr)
        x2_carry[...] = jnp.zeros_like(x2_carry)

    def norm_prev_tile():
        h2_prev = _rmsnorm(x2_carry[...], g2_ref[...])
        _store_rows(h2_ref, 0, h2_prev)
        return _route_choose(h2_prev, wr_ref, br_ref)

    def place_prev_tile(choice):
        _route_place(choice, tri_ref, upper_ref, lp_ref, tw_ref, ct_ref, of_ref, wp_ref)

    @pl.when(step == n_tiles)
    def _():
        place_prev_tile(norm_prev_tile())

    @pl.when(step < n_tiles)
    def _():
        _prompt_step(step, norm_prev_tile, place_prev_tile, x_ref, g1_ref, win_ref, cq_ref, sq_ref,
                     dmat_ref, qdec_ref, kdec_ref, sdec_ref, lng_ref, lnb_ref, ws_ref, bst_ref, wout_ref,
                     x2_ref, st_ref, s_scr, mix_scr, x2_carry, ret_block=ret_block, tiles_per_seq=tiles_per_seq)


def _prompt_step(step, norm_prev_tile, place_prev_tile, x_ref, g1_ref, win_ref, cq_ref, sq_ref,
                 dmat_ref, qdec_ref, kdec_ref, sdec_ref, lng_ref, lnb_ref, ws_ref, bst_ref, wout_ref,
                 x2_ref, st_ref, s_scr, mix_scr, x2_carry, *, ret_block, tiles_per_seq):
    tile = TOKEN_TILE
    t = lax.rem(step, tiles_per_seq)

    choice = norm_prev_tile()
    s_scr[...] = jnp.where(t == 0, 0.0, s_scr[...])

    x = x_ref[0]
    h = _rmsnorm(x, g1_ref[...]).astype(BF16)

    def proj(i):
        return _dot(h, win_ref[:, i * RET_WIDTH:(i + 1) * RET_WIDTH])

    q = proj(0)
    k = proj(1)
    v = proj(2)
    gate = proj(3)
    u = proj(4)
    vg = proj(5)
    place_prev_tile(choice)
    cq = cq_ref[...]
    sq = sq_ref[...]
    ck = cq * K_SCALE
    sk = sq * K_SCALE

    for hd in range(N_HEADS):
        cols = slice(hd * HEAD_DIM, (hd + 1) * HEAD_DIM)
        qr = _rotary(q[:, cols], cq, sq)
        kr = _rotary(k[:, cols], ck, sk)
        vh = v[:, cols].astype(BF16)
        dm = dmat_ref[hd]
        qd = qdec_ref[hd]
        kd = kdec_ref[hd]
        sd = sdec_ref[hd]
        for c in range(tile // ret_block):
            rows = slice(c * ret_block, (c + 1) * ret_block)
            qb = qr[rows].astype(BF16)
            kb = kr[rows]
            vb = vh[rows]
            state = s_scr[hd]
            scores = _dot_nt(qb, kb.astype(BF16)) * dm
            o = _dot(scores.astype(BF16), vb) + qd * _dot(qb, state.astype(BF16))
            s_scr[hd] = sd * state + _dot_tn((kb * kd).astype(BF16), vb)
            mix_scr[rows, cols] = _gated_head_norm(o, gate[rows, cols])

    uf, vn = _gmlp_inputs(u, vg, lng_ref[...], lnb_ref[...])
    vnb = vn.astype(BF16)
    r_i = lax.broadcasted_iota(I32, (GM_CHUNK, GM_CHUNK), 0)
    c_i = lax.broadcasted_iota(I32, (GM_CHUNK, GM_CHUNK), 1)
    for hd in range(N_HEADS):
        cols = slice(hd * HEAD_DIM, (hd + 1) * HEAD_DIM)
        w_tril = jnp.where(r_i >= c_i, ws_ref[hd], 0.0).astype(BF16)
        bias = bst_ref[:, hd:hd + 1]
        for c in range(tile // GM_CHUNK):
            rows = slice(c * GM_CHUNK, (c + 1) * GM_CHUNK)
            mixed = _dot(w_tril, vnb[rows, cols]) + bias
            mix_scr[rows, RET_WIDTH + hd * HEAD_DIM:RET_WIDTH + (hd + 1) * HEAD_DIM] = (
                uf[rows, cols] * mixed).astype(BF16)

    x2 = x + _dot(mix_scr[...], wout_ref[...])
    x2_ref[...] = x2
    x2_carry[...] = x2

    @pl.when(t == tiles_per_seq - 1)
    def _():
        st_ref[0] = s_scr[...]


def _decode_kernel(x_ref, s0_ref, g1_ref, win_ref, cq_ref, sq_ref, qdec_ref, kdec_ref,
                   sdec_ref, lng_ref, lnb_ref, ws_ref, bst_ref, wout_ref, g2_ref, wr_ref, br_ref,
                   tri_ref, upper_ref,
                   x2_ref, h2_ref, lp_ref, tw_ref, ct_ref, of_ref, wp_ref, st_ref, vn_ref,
                   mix_scr, q_scr, k_scr, v_scr, oc_scr, st_stage, st_sem, *, n_streams, frames):
    tile = n_streams * frames
    x = x_ref[...]
    h = _rmsnorm(x, g1_ref[...]).astype(BF16)

    def proj(i):
        return _dot(h, win_ref[:, i * RET_WIDTH:(i + 1) * RET_WIDTH])

    q = proj(0)
    k = proj(1)
    v = proj(2)
    gate = proj(3)
    u = proj(4)
    vg = proj(5)
    cq = cq_ref[...]
    sq = sq_ref[...]
    ck = cq * K_SCALE
    sk = sq * K_SCALE

    r_i = lax.broadcasted_iota(I32, (tile, tile), 0)
    c_i = lax.broadcasted_iota(I32, (tile, tile), 1)
    shift = frames.bit_length() - 1
    r_frame = lax.bitwise_and(r_i, frames - 1)
    c_frame = lax.bitwise_and(c_i, frames - 1)
    keep = (lax.shift_right_logical(r_i, shift) == lax.shift_right_logical(c_i, shift)) & (r_frame >= c_frame)
    frame_gap = (r_frame - c_frame).astype(F32)

    o_in = []
    for hd in range(N_HEADS):
        cols = slice(hd * HEAD_DIM, (hd + 1) * HEAD_DIM)
        qr = _rotary(q[:, cols], cq, sq)
        kr = _rotary(k[:, cols], ck, sk)
        vh = v[:, cols].astype(BF16)
        qb = qr.astype(BF16)
        decay = jnp.where(keep, jnp.exp(frame_gap * _LOG_GAMMA[hd]), 0.0)
        scores = _dot_nt(qb, kr.astype(BF16)) * decay
        o_in.append(_dot(scores.astype(BF16), vh))
        q_scr[hd] = qb
        k_scr[hd] = (kr * kdec_ref[hd]).astype(BF16)
        v_scr[hd] = vh

    def state_copy(b):
        slot = lax.rem(b, 2)
        return pltpu.make_async_copy(st_stage.at[slot], st_ref.at[b], st_sem.at[slot])

    def stream_body(b, carry):
        rows = pl.ds(pl.multiple_of(b * frames, frames), frames)
        slot = lax.rem(b, 2)

        @pl.when(b >= 2)
        def _():
            state_copy(b - 2).wait()

        for hd in range(N_HEADS):
            state = s0_ref[b, hd]
            oc_scr[hd, rows, :] = _dot(q_scr[hd, rows, :], state.astype(BF16))
            st_stage[slot, hd] = sdec_ref[hd] * state + _dot_tn(k_scr[hd, rows, :], v_scr[hd, rows, :])
        state_copy(b).start()
        return carry

    lax.fori_loop(0, n_streams, stream_body, 0)
    state_copy(n_streams - 2).wait()
    state_copy(n_streams - 1).wait()

    for hd in range(N_HEADS):
        cols = slice(hd * HEAD_DIM, (hd + 1) * HEAD_DIM)
        o = o_in[hd] + qdec_ref[hd] * oc_scr[hd]
        mix_scr[:, cols] = _gated_head_norm(o, gate[:, cols])

    uf, vn = _gmlp_inputs(u, vg, lng_ref[...], lnb_ref[...])
    vn_ref[...] = vn
    vnb = vn.astype(BF16)
    sel = (lax.bitwise_and(lax.broadcasted_iota(I32, (tile, frames), 0), frames - 1)
           == lax.broadcasted_iota(I32, (tile, frames), 1)).astype(BF16)
    for hd in range(N_HEADS):
        cols = slice(hd * HEAD_DIM, (hd + 1) * HEAD_DIM)
        w_rows = _dot(sel, ws_ref[hd].astype(BF16)).astype(BF16)
        w_blk = jnp.where(keep, _dot_nt(w_rows, sel), 0.0).astype(BF16)
        mixed = _dot(w_blk, vnb[:, cols]) + bst_ref[:, hd:hd + 1]
        mix_scr[:, RET_WIDTH + hd * HEAD_DIM:RET_WIDTH + (hd + 1) * HEAD_DIM] = (uf[:, cols] * mixed).astype(BF16)

    _route_and_store(x, mix_scr, wout_ref, g2_ref, wr_ref, br_ref, tri_ref, upper_ref,
                     x2_ref, h2_ref, lp_ref, tw_ref, ct_ref, of_ref, wp_ref)


def _rope_tables(pos):
    half = HEAD_DIM // 2
    inv = np.float32(ROPE_BASE) ** (-np.arange(half, dtype=np.float32) / np.float32(half))
    ang = pos.astype(np.float32)[:, None] * inv[None, :]
    cos = np.cos(ang).astype(np.float32)
    sin = np.sin(ang).astype(np.float32)
    return np.concatenate([cos, cos], axis=-1), np.concatenate([-sin, sin], axis=-1)


def _decay_tables(block):
    log_g = np.asarray(_LOG_GAMMA, np.float32)
    idx = np.arange(block, dtype=np.float32)
    diff = idx[:, None] - idx[None, :]
    dmat = np.where(diff[None] >= 0, np.exp(np.maximum(diff, 0.0)[None] * log_g[:, None, None]), 0.0)
    q_dec = np.exp((idx + 1.0)[None, :] * log_g[:, None])
    k_dec = np.exp((block - 1.0 - idx)[None, :] * log_g[:, None])
    s_dec = np.exp(np.float32(block) * log_g)
    bc = lambda a: np.ascontiguousarray(
        np.broadcast_to(a[:, :, None], (N_HEADS, block, HEAD_DIM)).astype(np.float32))
    s_row = np.ascontiguousarray(np.broadcast_to(s_dec[:, None, None], (N_HEADS, 1, HEAD_DIM)).astype(np.float32))
    return dmat.astype(np.float32), bc(q_dec), bc(k_dec), s_row


def _routing_constants(tile):
    tri = jnp.asarray(np.arange(tile)[:, None] > np.arange(tile)[None, :], BF16)
    upper = jnp.asarray(np.arange(N_EXPERTS)[:, None] < np.arange(N_EXPERTS)[None, :], BF16)
    return tri, upper


def _front_out(n_tok, n_tiles):
    return [
        jax.ShapeDtypeStruct((n_tok, D_MODEL), F32),
        jax.ShapeDtypeStruct((n_tok * ROW_CHUNKS, LANES), U32),
        jax.ShapeDtypeStruct((n_tok, TOP_K), I32),
        jax.ShapeDtypeStruct((n_tok, TOP_K), F32),
        jax.ShapeDtypeStruct((n_tiles, 1, N_EXPERTS), F32),
        jax.ShapeDtypeStruct((n_tiles, 1, N_EXPERTS), F32),
        jax.ShapeDtypeStruct((n_tiles, 1, TILE_SLOTS), I32),
    ]


def _prompt_call(x, p):
    bsz, seq, _ = x.shape
    tile = TOKEN_TILE
    n_t = seq // tile
    cq, sq = _rope_tables(np.arange(seq))
    dmat, qdec, kdec, sdec = _decay_tables(PROMPT_RET_BLOCK)
    tri, upper = _routing_constants(tile)
    ws = p['w_s'][:, :GM_CHUNK, :GM_CHUNK]
    bst = p['b_s'][:, :GM_CHUNK].T

    n_tiles = bsz * n_t
    cur = lambda s: jnp.minimum(s, n_tiles - 1)
    prev = lambda s: jnp.maximum(s - 1, 0)
    const = lambda shape: pl.BlockSpec(shape, lambda s: (0,) * len(shape))
    pos = lambda: pl.BlockSpec((tile, HEAD_DIM), lambda s: (lax.rem(cur(s), n_t), 0))
    in_specs = [
        pl.BlockSpec((1, tile, D_MODEL), lambda s: (cur(s) // n_t, lax.rem(cur(s), n_t), 0)),
        const((1, D_MODEL)), const((D_MODEL, 6 * RET_WIDTH)),
        pos(), pos(),
        const((N_HEADS, PROMPT_RET_BLOCK, PROMPT_RET_BLOCK)),
        const((N_HEADS, PROMPT_RET_BLOCK, HEAD_DIM)), const((N_HEADS, PROMPT_RET_BLOCK, HEAD_DIM)),
        const((N_HEADS, 1, HEAD_DIM)),
        const((1, GM_WIDTH)), const((1, GM_WIDTH)),
        const((N_HEADS, GM_CHUNK, GM_CHUNK)), const((GM_CHUNK, N_HEADS)),
        const((D_MODEL, D_MODEL)), const((1, D_MODEL)),
        const((D_MODEL, N_EXPERTS)), const((1, N_EXPERTS)),
        const((tile, tile)), const((N_EXPERTS, N_EXPERTS)),
    ]
    out_specs = [
        pl.BlockSpec((tile, D_MODEL), lambda s: (cur(s), 0)),
        pl.BlockSpec((tile * ROW_CHUNKS, LANES), lambda s: (prev(s), 0)),
        pl.BlockSpec((tile, TOP_K), lambda s: (prev(s), 0)),
        pl.BlockSpec((tile, TOP_K), lambda s: (prev(s), 0)),
        pl.BlockSpec((1, 1, N_EXPERTS), lambda s: (prev(s), 0, 0)),
        pl.BlockSpec((1, 1, N_EXPERTS), lambda s: (prev(s), 0, 0)),
        pl.BlockSpec((1, 1, TILE_SLOTS), lambda s: (prev(s), 0, 0)),
        pl.BlockSpec((1, N_HEADS, HEAD_DIM, HEAD_DIM), lambda s: (cur(s) // n_t, 0, 0, 0)),
    ]
    out_shape = _front_out(bsz * seq, n_tiles) + [jax.ShapeDtypeStruct((bsz, N_HEADS, HEAD_DIM, HEAD_DIM), F32)]
    return pl.pallas_call(
        functools.partial(_prompt_kernel, ret_block=PROMPT_RET_BLOCK, tiles_per_seq=n_t, n_tiles=n_tiles),
        grid=(n_tiles + 1,),
        in_specs=in_specs,
        out_specs=out_specs,
        out_shape=out_shape,
        scratch_shapes=[
            pltpu.VMEM((N_HEADS, HEAD_DIM, HEAD_DIM), F32),
            pltpu.VMEM((tile, D_MODEL), BF16),
            pltpu.VMEM((tile, D_MODEL), F32),
        ],
        compiler_params=pltpu.CompilerParams(
            dimension_semantics=("arbitrary",), vmem_limit_bytes=VMEM_LIMIT),
        name="front_prompt",
    )(x, p['norm1_g'], p['w_in'], cq, sq, dmat, qdec, kdec, sdec, p['ln_v_g'], p['ln_v_b'], ws, bst,
      p['w_out'], p['norm2_g'], p['w_router'], p['b_router'], tri, upper)


def _decode_call(x, state, past_len, p):
    n_streams, frames, _ = x.shape
    tile = n_streams * frames
    assert tile == TOKEN_TILE and frames <= RET_CHUNK and frames & (frames - 1) == 0
    cq, sq = [np.tile(a, (n_streams, 1)) for a in _rope_tables(past_len + np.arange(frames))]
    _, qdec, kdec, sdec = _decay_tables(frames)
    qdec = np.tile(qdec, (1, n_streams, 1))
    kdec = np.tile(kdec, (1, n_streams, 1))
    tri, upper = _routing_constants(tile)
    ws = p['w_s'][:, :frames, :frames]
    bst = jnp.tile(p['b_s'][:, :frames].T, (n_streams, 1))

    def whole(a):
        return pl.BlockSpec(a.shape, lambda i, n=a.ndim: (0,) * n, pipeline_mode=pl.Buffered(1))

    args = (x.reshape(tile, D_MODEL), state, p['norm1_g'], p['w_in'], cq, sq, qdec, kdec, sdec,
            p['ln_v_g'], p['ln_v_b'], ws, bst, p['w_out'], p['norm2_g'], p['w_router'], p['b_router'],
            tri, upper)
    out_shape = _front_out(tile, 1) + [
        jax.ShapeDtypeStruct(state.shape, F32),
        jax.ShapeDtypeStruct((tile, GM_WIDTH), F32),
    ]
    out_specs = [pl.BlockSpec(s.shape, lambda i, n=len(s.shape): (0,) * n) for s in out_shape]
    out_specs[7] = pl.BlockSpec(memory_space=pl.ANY)
    return pl.pallas_call(
        functools.partial(_decode_kernel, n_streams=n_streams, frames=frames),
        grid=(1,),
        in_specs=[whole(a) for a in args],
        out_specs=out_specs,
        out_shape=out_shape,
        scratch_shapes=[
            pltpu.VMEM((tile, D_MODEL), BF16),
            pltpu.VMEM((N_HEADS, tile, HEAD_DIM), BF16),
            pltpu.VMEM((N_HEADS, tile, HEAD_DIM), BF16),
            pltpu.VMEM((N_HEADS, tile, HEAD_DIM), BF16),
            pltpu.VMEM((N_HEADS, tile, HEAD_DIM), F32),
            pltpu.VMEM((2, N_HEADS, HEAD_DIM, HEAD_DIM), F32),
            pltpu.SemaphoreType.DMA((2,)),
        ],
        compiler_params=pltpu.CompilerParams(
            dimension_semantics=("arbitrary",), vmem_limit_bytes=VMEM_LIMIT),
        name="front_decode",
    )(*args)


def _plan_kernel(ct_ref, lower_ref, upper_ref, cum_ref, nb_ref, bs_ref, be_ref, nbt_ref, tlo_ref, thi_ref,
                 *, n_blocks):
    counts = ct_ref[...]
    cum = _exact_count_dot(lower_ref[...], counts, counts_on_left=False)
    cum_ref[...] = cum.astype(I32)
    n_tiles = counts.shape[0]
    total = cum[n_tiles:n_tiles + 1]
    nb = jnp.floor((total + (EXPERT_ROWS - 1)) * (1.0 / EXPERT_ROWS))
    bstart = _dot(jnp.broadcast_to(nb, (8, N_EXPERTS)).astype(BF16), upper_ref[...])[0:1]
    bend = bstart + nb
    nb_ref[...] = nb.astype(I32)
    bs_ref[...] = bstart.astype(I32)
    blk = lax.broadcasted_iota(I32, (n_blocks, N_EXPERTS), 0).astype(F32)
    be = jnp.minimum(jnp.sum((bend <= blk).astype(F32), axis=-1, keepdims=True), N_EXPERTS - 1.0)
    be_ref[...] = be.astype(I32)
    nbt_ref[...] = jnp.sum(nb, axis=-1, keepdims=True).astype(I32)
    mine = lax.broadcasted_iota(I32, (n_blocks, N_EXPERTS), 1).astype(F32) == be
    pick = lambda row: jnp.sum(jnp.where(mine, row, 0.0), axis=-1, keepdims=True)
    r0 = (blk[:, 0:1] - pick(bstart)) * EXPERT_ROWS
    tlo = jnp.zeros((n_blocks, 1), F32)
    thi = jnp.zeros((n_blocks, 1), F32)
    seg_lo = pick(cum[0:1])
    for i in range(n_tiles):
        seg_hi = pick(cum[i + 1:i + 2])
        tlo = tlo + (seg_hi <= r0).astype(F32)
        thi = thi + (seg_lo < r0 + EXPERT_ROWS).astype(F32)
        seg_lo = seg_hi
    tlo_ref[...] = tlo.astype(I32)
    thi_ref[...] = thi.astype(I32)


def _plan_call(counts, n_blocks):
    n_tiles = counts.shape[0]
    lower = (jnp.arange(n_tiles + 1)[:, None] > jnp.arange(n_tiles)[None, :]).astype(BF16)
    upper = (jnp.arange(N_EXPERTS)[:, None] < jnp.arange(N_EXPERTS)[None, :]).astype(BF16)
    return pl.pallas_call(
        functools.partial(_plan_kernel, n_blocks=n_blocks),
        out_shape=[
            jax.ShapeDtypeStruct((n_tiles + 1, N_EXPERTS), I32),
            jax.ShapeDtypeStruct((1, N_EXPERTS), I32),
            jax.ShapeDtypeStruct((1, N_EXPERTS), I32),
            jax.ShapeDtypeStruct((n_blocks, 1), I32),
            jax.ShapeDtypeStruct((1, 1), I32),
            jax.ShapeDtypeStruct((n_blocks, 1), I32),
            jax.ShapeDtypeStruct((n_blocks, 1), I32),
        ],
        name="plan",
    )(counts, lower, upper)


def _dispatch_kernel(h2p_ref, h2s_ref, wp_ref, out_ref, *, n_prompt_tiles):
    i = pl.program_id(0)

    def permute(src_ref):
        def body(t, c):
            row = src_ref[_row_tile(t), :]
            for kk in range(TOP_K):
                first = pl.multiple_of(wp_ref[0, 0, kk * TOKEN_TILE + t], ROW_CHUNKS)
                out_ref[pl.ds(first, ROW_CHUNKS), :] = row
            return c

        lax.fori_loop(0, TOKEN_TILE, body, 0, unroll=8)

    @pl.when(i < n_prompt_tiles)
    def _():
        permute(h2p_ref)

    @pl.when(i >= n_prompt_tiles)
    def _():
        permute(h2s_ref)


def _dispatch_call(h2_p, h2_s, lp_tiles):
    rows = TOKEN_TILE * ROW_CHUNKS
    n_p_tiles = h2_p.shape[0] // rows
    n_s_tiles = h2_s.shape[0] // rows
    n_tiles = n_p_tiles + n_s_tiles
    return pl.pallas_call(
        functools.partial(_dispatch_kernel, n_prompt_tiles=n_p_tiles),
        grid=(n_tiles,),
        in_specs=[
            pl.BlockSpec((rows, LANES), lambda i: (jnp.minimum(i, n_p_tiles - 1), 0)),
            pl.BlockSpec((rows, LANES), lambda i: (jnp.maximum(i - n_p_tiles, 0), 0)),
            pl.BlockSpec((1, 1, TILE_SLOTS), lambda i: (i, 0, 0), memory_space=pltpu.SMEM),
        ],
        out_specs=pl.BlockSpec((TILE_SLOTS * ROW_CHUNKS, LANES), lambda i: (i, 0)),
        out_shape=jax.ShapeDtypeStruct((n_tiles * TILE_SLOTS * ROW_CHUNKS, LANES), U32),
        compiler_params=pltpu.CompilerParams(
            dimension_semantics=("arbitrary",), vmem_limit_bytes=VMEM_LIMIT),
        name="dispatch",
    )(h2_p, h2_s, lp_tiles)


def _expert_kernel(be_ref, bs_ref, nb_ref, nbt_ref, cum_ref, off_ref, tlo_ref, thi_ref,
                   xb_ref, wg_ref, wu_ref, wd_ref, bg_ref, bu_ref, bd_ref,
                   yb_ref, xbuf, ybuf, wbf, in_sem, out_sem, *, n_tiles):
    b = pl.program_id(0)
    nbt = nbt_ref[0]

    def block_rows(blk):
        e = be_ref[blk]
        r0 = (blk - bs_ref[e]) * EXPERT_ROWS
        return r0, jnp.minimum(cum_ref[n_tiles * N_EXPERTS + e] - r0, EXPERT_ROWS)

    def segments(blk, slot, fn):
        e = be_ref[blk]
        r0 = (blk - bs_ref[e]) * EXPERT_ROWS

        def body(i, c):
            seg_lo = cum_ref[i * N_EXPERTS + e]
            seg_hi = cum_ref[(i + 1) * N_EXPERTS + e]
            lo = jnp.maximum(seg_lo, r0)
            hi = jnp.minimum(seg_hi, r0 + EXPERT_ROWS)

            @pl.when(hi > lo)
            def _():
                bucket_row = i * TILE_SLOTS + off_ref[i * N_EXPERTS + e] + (lo - seg_lo)
                fn(slot, bucket_row, slot * EXPERT_ROWS + (lo - r0), hi - lo)
            return c

        lax.fori_loop(tlo_ref[blk], thi_ref[blk], body, 0)

    def words(row, n):
        return pl.ds(pl.multiple_of(row * ROW_CHUNKS, ROW_CHUNKS), n * ROW_CHUNKS)

    def gather(slot, bucket_row, buffer_row, n):
        return pltpu.make_async_copy(
            xb_ref.at[words(bucket_row, n)], xbuf.at[words(buffer_row, n)], in_sem.at[slot])

    def scatter(slot, bucket_row, buffer_row, n):
        return pltpu.make_async_copy(
            ybuf.at[words(buffer_row, n)], yb_ref.at[words(bucket_row, n)], out_sem.at[slot])

    def start_gather(blk):
        @pl.when(blk < nbt)
        def _():
            segments(blk, lax.rem(blk, 2), lambda *a: gather(*a).start())

    def wait_gather(blk):
        @pl.when(blk < nbt)
        def _():
            slot = lax.rem(blk, 2)
            gather(slot, 0, slot * EXPERT_ROWS, block_rows(blk)[1]).wait()

    def start_scatter(blk):
        @pl.when(blk < nbt)
        def _():
            segments(blk, lax.rem(blk, 2), lambda *a: scatter(*a).start())

    def wait_scatter(blk):
        @pl.when((blk >= 0) & (blk < nbt))
        def _():
            slot = lax.rem(blk, 2)
            scatter(slot, 0, slot * EXPERT_ROWS, block_rows(blk)[1]).wait()

    def mlp(base, n_rows, rows_left):
        valid = lax.broadcasted_iota(I32, (n_rows, 1), 0) < rows_left
        x = jnp.where(valid, _load_rows(xbuf, base, n_rows), 0.0)
        gt = _dot(x, wbf[0]) + bg_ref[0]
        up = _dot(x, wbf[1]) + bu_ref[0]
        gt = jnp.minimum(gt, SWIGLU_LIMIT)
        up = jnp.clip(up, -SWIGLU_LIMIT, SWIGLU_LIMIT)
        act = gt * _sigmoid(gt * SWIGLU_ALPHA) * (up + 1.0)
        _store_rows(ybuf, base, _dot(act.astype(BF16), wbf[2]) + bd_ref[0])

    def block_body(blk, carry):
        wait_gather(blk)
        start_gather(blk + 1)
        wait_scatter(blk - 2)
        rows_left = block_rows(blk)[1]
        base = lax.rem(blk, 2) * EXPERT_ROWS

        lower = 0
        for upper, passes in EXPERT_PASS_PLANS:
            @pl.when((rows_left > lower) & (rows_left <= upper))
            def _(passes=passes):
                first = 0
                for n_rows in passes:
                    mlp(base + first, n_rows, rows_left - first)
                    first += n_rows
            lower = upper

        start_scatter(blk)
        return carry

    first_block = bs_ref[b]
    n_mine = nb_ref[b]

    @pl.when(b == 0)
    def _():
        xbuf[...] = jnp.zeros_like(xbuf)
        start_gather(0)

    @pl.when(n_mine > 0)
    def _():
        wbf[0] = wg_ref[0].astype(BF16)
        wbf[1] = wu_ref[0].astype(BF16)
        wbf[2] = wd_ref[0].astype(BF16)

    lax.fori_loop(first_block, first_block + n_mine, block_body, 0)

    @pl.when(b == N_EXPERTS - 1)
    def _():
        wait_scatter(nbt - 2)
        wait_scatter(nbt - 1)


def _expert_call(be, bstart, nb, nbt, cum_flat, off_flat, tlo, thi, buckets, p, n_tiles):
    wspec = pl.BlockSpec((1, D_MODEL, D_MODEL), lambda e, *_: (e, 0, 0))
    bspec = pl.BlockSpec((1, 1, D_MODEL), lambda e, *_: (e, 0, 0))
    any_spec = pl.BlockSpec(memory_space=pl.ANY)
    return pl.pallas_call(
        functools.partial(_expert_kernel, n_tiles=n_tiles),
        grid_spec=pltpu.PrefetchScalarGridSpec(
            num_scalar_prefetch=8,
            grid=(N_EXPERTS,),
            in_specs=[any_spec, wspec, wspec, wspec, bspec, bspec, bspec],
            out_specs=any_spec,
            scratch_shapes=[
                pltpu.VMEM((2 * EXPERT_ROWS * ROW_CHUNKS, LANES), U32),
                pltpu.VMEM((2 * EXPERT_ROWS * ROW_CHUNKS, LANES), U32),
                pltpu.VMEM((3, D_MODEL, D_MODEL), BF16),
                pltpu.SemaphoreType.DMA((2,)),
                pltpu.SemaphoreType.DMA((2,)),
            ],
        ),
        out_shape=jax.ShapeDtypeStruct(buckets.shape, U32),
        compiler_params=pltpu.CompilerParams(
            dimension_semantics=("arbitrary",), vmem_limit_bytes=VMEM_LIMIT),
        name="experts",
    )(be, bstart, nb, nbt, cum_flat, off_flat, tlo, thi, buckets, p['w_gate'], p['w_up'], p['w_down'],
      p['b_gate'][:, None, :], p['b_up'][:, None, :], p['b_down'][:, None, :])


def _combine_kernel(x2_ref, yb_ref, lp_ref, tw_ref, gf_ref, y_ref, ybf, pick):
    for c in range(TILE_SLOTS // COMBINE_CHUNK):
        rows = slice(c * COMBINE_CHUNK, (c + 1) * COMBINE_CHUNK)
        ybf[rows, :] = _load_rows(yb_ref, c * COMBINE_CHUNK, COMBINE_CHUNK)
    slot = lax.broadcasted_iota(I32, (TOKEN_TILE, TILE_SLOTS), 1)
    lp = lp_ref[...]
    tw = tw_ref[...]
    sel = jnp.zeros((TOKEN_TILE, TILE_SLOTS), F32)
    for kk in range(TOP_K):
        sel = jnp.where(slot == lp[:, kk:kk + 1], tw[:, kk:kk + 1], sel)
    pick[...] = sel.astype(BF16)
    y_ref[...] = _rmsnorm(x2_ref[...] + _dot(pick[...], ybf[...]), gf_ref[...])


def _combine_call(x2, ybuckets, lp, tw, gf, tile_off):
    n_tok = x2.shape[0]
    n_tiles = n_tok // TOKEN_TILE
    return pl.pallas_call(
        _combine_kernel,
        grid=(n_tiles,),
        in_specs=[
            pl.BlockSpec((TOKEN_TILE, D_MODEL), lambda i: (i, 0)),
            pl.BlockSpec((TILE_SLOTS * ROW_CHUNKS, LANES), lambda i: (tile_off + i, 0)),
            pl.BlockSpec((TOKEN_TILE, TOP_K), lambda i: (i, 0)),
            pl.BlockSpec((TOKEN_TILE, TOP_K), lambda i: (i, 0)),
            pl.BlockSpec((1, D_MODEL), lambda i: (0, 0)),
        ],
        out_specs=pl.BlockSpec((TOKEN_TILE, D_MODEL), lambda i: (i, 0)),
        scratch_shapes=[
            pltpu.VMEM((TILE_SLOTS, D_MODEL), BF16),
            pltpu.VMEM((TOKEN_TILE, TILE_SLOTS), BF16),
        ],
        out_shape=jax.ShapeDtypeStruct((n_tok, D_MODEL), F32),
        compiler_params=pltpu.CompilerParams(
            dimension_semantics=("arbitrary",), vmem_limit_bytes=VMEM_LIMIT),
        name="combine",
    )(x2, ybuckets, lp, tw, gf)


def kernel(x_prompt, x_sample, state_ret, norm1_g, w_in, ln_v_g, ln_v_b, w_s, b_s, w_out, norm2_g, w_router, b_router, w_gate, b_gate, w_up, b_up, w_down, b_down, norm_f_g):
    bsz, seq, _ = x_prompt.shape
    dbsz, dseq, _ = x_sample.shape
    n_p = bsz * seq
    n_s = dbsz * dseq
    n_total = n_p + n_s
    past_len = 2048
    assert w_in.shape[0] == 1, "single layer"

    p = dict(norm1_g=norm1_g, w_in=w_in[0].astype(BF16), ln_v_g=ln_v_g, ln_v_b=ln_v_b, w_s=w_s[0], b_s=b_s[0],
             w_out=w_out[0].astype(BF16), norm2_g=norm2_g, w_router=w_router[0].astype(BF16), b_router=b_router,
             w_gate=w_gate[0], b_gate=b_gate[0], w_up=w_up[0], b_up=b_up[0], w_down=w_down[0], b_down=b_down[0])

    x2_p, h2_p, lp_p, tw_p, ct_p, of_p, wp_p, st_p = _prompt_call(x_prompt, p)
    x2_s, h2_s, lp_s, tw_s, ct_s, of_s, wp_s, st_s, vn_s = _decode_call(x_sample, state_ret[0], past_len, p)

    n_tiles = n_total // TOKEN_TILE
    word_pos = jnp.concatenate([wp_p, wp_s], axis=0)
    counts = jnp.concatenate([ct_p, ct_s], axis=0).reshape(n_tiles, N_EXPERTS)
    offsets = jnp.concatenate([of_p, of_s], axis=0).reshape(n_tiles * N_EXPERTS).astype(I32)

    n_blocks = -(-(n_total * TOP_K + N_EXPERTS * (EXPERT_ROWS - 1)) // EXPERT_ROWS)
    cum, nb, bstart, be, nbt, tlo, thi = _plan_call(counts, n_blocks)

    buckets = _dispatch_call(h2_p, h2_s, word_pos)
    ybuckets = _expert_call(be.reshape(n_blocks), bstart.reshape(N_EXPERTS), nb.reshape(N_EXPERTS), nbt.reshape(1),
                            cum.reshape((n_tiles + 1) * N_EXPERTS), offsets, tlo.reshape(n_blocks),
                            thi.reshape(n_blocks), buckets, p, n_tiles)

    y_p = _combine_call(x2_p, ybuckets, lp_p, tw_p, norm_f_g[None, :], 0)
    y_s = _combine_call(x2_s, ybuckets, lp_s, tw_s, norm_f_g[None, :], n_p // TOKEN_TILE)

    return (y_p.reshape(bsz, seq, D_MODEL), y_s.reshape(dbsz, dseq, D_MODEL),
            st_p[None], st_s[None], vn_s.reshape(1, dbsz, dseq, GM_WIDTH))
```

```python
import functools

import numpy as np
import jax
import jax.numpy as jnp
from jax import lax
from jax.experimental import pallas as pl
from jax.experimental.pallas import tpu as pltpu

F32 = jnp.float32
BF16 = jnp.bfloat16
I32 = jnp.int32

D_MODEL = 1024
RET_WIDTH = 512
N_HEADS = 4
HEAD_DIM = 128
GM_WIDTH = 512
GM_CHUNK = 128
RET_CHUNK = 64
N_EXPERTS = 32
TOP_K = 4
SWIGLU_LIMIT = 7.0
SWIGLU_ALPHA = 1.702
ROPE_BASE = 10000.0
EPS = 1e-6

TOKEN_TILE = 512
PROMPT_RET_BLOCK = 256
EXPERT_ROWS = 1024
EXPERT_PASS_PLANS = ((128, (128,)), (256, (256,)), (384, (256, 128)), (512, (512,)), (640, (512, 128)),
                     (768, (512, 256)), (896, (512, 256, 128)), (1024, (512, 512)))
TILE_SLOTS = TOKEN_TILE * TOP_K
COMBINE_CHUNK = 256
VMEM_LIMIT = 56 * 1024 * 1024
LANES = 128
U32 = jnp.uint32
ROW_CHUNKS = D_MODEL // LANES // 2
K_SCALE = HEAD_DIM ** -0.5
SPLIT = 16.0
_LOG_GAMMA = [float(np.log1p(-np.float32(2.0) ** np.float32(-5.0 - hd)).astype(np.float32)) for hd in range(N_HEADS)]


def _load_rows(ref, first_row, n_rows):
    base = first_row * ROW_CHUNKS
    lo, hi = [], []
    for c in range(ROW_CHUNKS):
        w = ref[pl.ds(base + c, n_rows, stride=ROW_CHUNKS), :]
        lo.append(pltpu.bitcast(lax.shift_left(w, jnp.uint32(16)), F32))
        hi.append(pltpu.bitcast(lax.bitwise_and(w, jnp.uint32(0xFFFF0000)), F32))
    return jnp.concatenate(lo + hi, axis=1).astype(BF16)


def _store_rows(ref, first_row, val):
    base = first_row * ROW_CHUNKS
    bits = pltpu.bitcast(val.astype(BF16).astype(F32), U32)
    half = D_MODEL // 2
    for c in range(ROW_CHUNKS):
        low = lax.shift_right_logical(bits[:, c * LANES:(c + 1) * LANES], jnp.uint32(16))
        ref[pl.ds(base + c, val.shape[0], stride=ROW_CHUNKS), :] = lax.bitwise_or(
            bits[:, half + c * LANES:half + (c + 1) * LANES], low)


def _row_tile(row):
    return pl.ds(pl.multiple_of(row * ROW_CHUNKS, ROW_CHUNKS), ROW_CHUNKS)


def _rmsnorm(x, g):
    ms = jnp.mean(x * x, axis=-1, keepdims=True)
    return x * lax.rsqrt(ms + EPS) * g


def _gelu(x):
    c = np.float32(np.sqrt(2.0 / np.pi))
    return x * (0.5 * (1.0 + jnp.tanh(c * (x + 0.044715 * (x * x * x)))))


def _sigmoid(x):
    return 1.0 / (1.0 + jnp.exp(-x))


def _dot(a, b):
    return jnp.dot(a, b, preferred_element_type=F32)


def _dot_nt(a, b):
    return lax.dot_general(a, b, (((1,), (1,)), ((), ())), preferred_element_type=F32)


def _dot_tn(a, b):
    return lax.dot_general(a, b, (((0,), (0,)), ((), ())), preferred_element_type=F32)


def _exact_count_dot(a, b, *, counts_on_left):
    cnt = a if counts_on_left else b
    hi = jnp.floor(cnt * (1.0 / SPLIT))
    lo = cnt - SPLIT * hi
    if counts_on_left:
        return SPLIT * _dot(hi.astype(BF16), b) + _dot(lo.astype(BF16), b)
    return SPLIT * _dot(a, hi.astype(BF16)) + _dot(a, lo.astype(BF16))


def _rotary(xh, cos2, sin2):
    return xh * cos2 + pltpu.roll(xh, HEAD_DIM // 2, axis=1) * sin2


def _gated_head_norm(o, gate):
    mu = jnp.mean(o, axis=-1, keepdims=True)
    oc = o - mu
    var = jnp.mean(oc * oc, axis=-1, keepdims=True)
    return ((gate * _sigmoid(gate)) * (oc * lax.rsqrt(var + EPS))).astype(BF16)


def _gmlp_inputs(u, vg, lng, lnb):
    uf = _gelu(u)
    vf = _gelu(vg)
    mu = jnp.mean(vf, axis=-1, keepdims=True)
    vc = vf - mu
    var = jnp.mean(vc * vc, axis=-1, keepdims=True)
    return uf, vc * lax.rsqrt(var + EPS) * lng + lnb


def _route_and_store(x, mix_scr, wout_ref, g2_ref, wr_ref, br_ref, tri_ref, upper_ref,
                     x2_ref, h2_ref, lp_ref, tw_ref, ct_ref, of_ref, wp_ref):
    h2 = _out_proj(x, mix_scr, wout_ref, g2_ref, x2_ref, h2_ref)
    _route(h2, wr_ref, br_ref, tri_ref, upper_ref, lp_ref, tw_ref, ct_ref, of_ref, wp_ref)


def _out_proj(x, mix_scr, wout_ref, g2_ref, x2_ref, h2_ref):
    x2 = x + _dot(mix_scr[...], wout_ref[...])
    x2_ref[...] = x2
    h2 = _rmsnorm(x2, g2_ref[...])
    _store_rows(h2_ref, 0, h2)
    return h2


def _route(h2, wr_ref, br_ref, tri_ref, upper_ref, lp_ref, tw_ref, ct_ref, of_ref, wp_ref):
    _route_place(_route_choose(h2, wr_ref, br_ref), tri_ref, upper_ref, lp_ref, tw_ref, ct_ref, of_ref, wp_ref)


def _route_choose(h2, wr_ref, br_ref):
    tile = h2.shape[0]
    logits = _dot(h2.astype(BF16), wr_ref[...]) + br_ref[...]

    iota_e = lax.broadcasted_iota(I32, (tile, N_EXPERTS), 1)
    iota_ef = iota_e.astype(F32)
    lg = logits
    vals = []
    idxs = []
    for _ in range(TOP_K):
        m = jnp.max(lg, axis=-1, keepdims=True)
        idx = jnp.min(jnp.where(lg == m, iota_ef, float(N_EXPERTS)), axis=-1, keepdims=True).astype(I32)
        vals.append(m)
        idxs.append(idx)
        lg = jnp.where(iota_e == idx, -jnp.inf, lg)
    exps = [jnp.exp(vk - vals[0]) for vk in vals]
    denom = exps[0] + exps[1] + exps[2] + exps[3]
    return idxs, exps, denom


def _route_place(choice, tri_ref, upper_ref, lp_ref, tw_ref, ct_ref, of_ref, wp_ref):
    idxs, exps, denom = choice
    tile = denom.shape[0]
    iota_e = lax.broadcasted_iota(I32, (tile, N_EXPERTS), 1)
    iota_k = lax.broadcasted_iota(I32, (tile, TOP_K), 1)
    onehot = jnp.zeros((tile, N_EXPERTS), F32)
    for idx in idxs:
        onehot = onehot + (iota_e == idx).astype(F32)
    rank = _dot(tri_ref[...], onehot.astype(BF16))
    count = jnp.sum(onehot, axis=0, keepdims=True)
    offset = _exact_count_dot(jnp.broadcast_to(count, (8, N_EXPERTS)), upper_ref[...], counts_on_left=True)[0:1]
    where_to = rank + offset
    lp = jnp.zeros((tile, TOP_K), I32)
    tw = jnp.zeros((tile, TOP_K), F32)
    word_rows = []
    for kk in range(TOP_K):
        pos = jnp.sum(jnp.where(iota_e == idxs[kk], where_to, 0.0), axis=-1, keepdims=True)
        lp = jnp.where(iota_k == kk, pos.astype(I32), lp)
        tw = jnp.where(iota_k == kk, exps[kk] / denom, tw)
        word_rows.append(jnp.transpose(jnp.broadcast_to(pos * float(ROW_CHUNKS), (tile, LANES)))[0:1, :])
    lp_ref[...] = lp
    tw_ref[...] = tw
    ct_ref[0] = count
    of_ref[0] = offset
    wp_ref[0] = jnp.concatenate(word_rows, axis=1).astype(I32)


def _prompt_kernel(x_ref, g1_ref, win_ref, cq_ref, sq_ref, dmat_ref, qdec_ref, kdec_ref,
                   sdec_ref, lng_ref, lnb_ref, ws_ref, bst_ref, wout_ref, g2_ref, wr_ref, br_ref,
                   tri_ref, upper_ref,
                   x2_ref, h2_ref, lp_ref, tw_ref, ct_ref, of_ref, wp_ref, st_ref,
                   s_scr, mix_scr, x2_carry, *, ret_block, tiles_per_seq, n_tiles):
    step = pl.program_id(0)

    @pl.when(step == 0)
    def _():
        s_scr[...] = jnp.zeros_like(s_scr)
        x2_carry[...] = jnp.zeros_like(x2_carry)

    def norm_prev_tile():
        h2_prev = _rmsnorm(x2_carry[...], g2_ref[...])
        _store_rows(h2_ref, 0, h2_prev)
        return _route_choose(h2_prev, wr_ref, br_ref)

    def place_prev_tile(choice):
        _route_place(choice, tri_ref, upper_ref, lp_ref, tw_ref, ct_ref, of_ref, wp_ref)

    @pl.when(step == n_tiles)
    def _():
        place_prev_tile(norm_prev_tile())

    @pl.when(step < n_tiles)
    def _():
        _prompt_step(step, norm_prev_tile, place_prev_tile, x_ref, g1_ref, win_ref, cq_ref, sq_ref,
                     dmat_ref, qdec_ref, kdec_ref, sdec_ref, lng_ref, lnb_ref, ws_ref, bst_ref, wout_ref,
                     x2_ref, st_ref, s_scr, mix_scr, x2_carry, ret_block=ret_block, tiles_per_seq=tiles_per_seq)


def _prompt_step(step, norm_prev_tile, place_prev_tile, x_ref, g1_ref, win_ref, cq_ref, sq_ref,
                 dmat_ref, qdec_ref, kdec_ref, sdec_ref, lng_ref, lnb_ref, ws_ref, bst_ref, wout_ref,
                 x2_ref, st_ref, s_scr, mix_scr, x2_carry, *, ret_block, tiles_per_seq):
    tile = TOKEN_TILE
    t = lax.rem(step, tiles_per_seq)

    choice = norm_prev_tile()
    s_scr[...] = jnp.where(t == 0, 0.0, s_scr[...])

    x = x_ref[0]
    h = _rmsnorm(x, g1_ref[...]).astype(BF16)

    def proj(i):
        return _dot(h, win_ref[:, i * RET_WIDTH:(i + 1) * RET_WIDTH])

    q = proj(0)
    k = proj(1)
    v = proj(2)
    gate = proj(3)
    u = proj(4)
    vg = proj(5)
    place_prev_tile(choice)
    cq = cq_ref[...]
    sq = sq_ref[...]
    ck = cq * K_SCALE
    sk = sq * K_SCALE

    for hd in range(N_HEADS):
        cols = slice(hd * HEAD_DIM, (hd + 1) * HEAD_DIM)
        qr = _rotary(q[:, cols], cq, sq)
        kr = _rotary(k[:, cols], ck, sk)
        vh = v[:, cols].astype(BF16)
        dm = dmat_ref[hd]
        qd = qdec_ref[hd]
        kd = kdec_ref[hd]
        sd = sdec_ref[hd]
        for c in range(tile // ret_block):
            rows = slice(c * ret_block, (c + 1) * ret_block)
            qb = qr[rows].astype(BF16)
            kb = kr[rows]
            vb = vh[rows]
            state = s_scr[hd]
            scores = _dot_nt(qb, kb.astype(BF16)) * dm
            o = _dot(scores.astype(BF16), vb) + qd * _dot(qb, state.astype(BF16))
            s_scr[hd] = sd * state + _dot_tn((kb * kd).astype(BF16), vb)
            mix_scr[rows, cols] = _gated_head_norm(o, gate[rows, cols])

    uf, vn = _gmlp_inputs(u, vg, lng_ref[...], lnb_ref[...])
    vnb = vn.astype(BF16)
    r_i = lax.broadcasted_iota(I32, (GM_CHUNK, GM_CHUNK), 0)
    c_i = lax.broadcasted_iota(I32, (GM_CHUNK, GM_CHUNK), 1)
    for hd in range(N_HEADS):
        cols = slice(hd * HEAD_DIM, (hd + 1) * HEAD_DIM)
        w_tril = jnp.where(r_i >= c_i, ws_ref[hd], 0.0).astype(BF16)
        bias = bst_ref[:, hd:hd + 1]
        for c in range(tile // GM_CHUNK):
            rows = slice(c * GM_CHUNK, (c + 1) * GM_CHUNK)
            mixed = _dot(w_tril, vnb[rows, cols]) + bias
            mix_scr[rows, RET_WIDTH + hd * HEAD_DIM:RET_WIDTH + (hd + 1) * HEAD_DIM] = (
                uf[rows, cols] * mixed).astype(BF16)

    x2 = x + _dot(mix_scr[...], wout_ref[...])
    x2_ref[...] = x2
    x2_carry[...] = x2

    @pl.when(t == tiles_per_seq - 1)
    def _():
        st_ref[0] = s_scr[...]


def _decode_kernel(x_ref, s0_ref, g1_ref, win_ref, cq_ref, sq_ref, qdec_ref, kdec_ref,
                   sdec_ref, lng_ref, lnb_ref, ws_ref, bst_ref, wout_ref, g2_ref, wr_ref, br_ref,
                   tri_ref, upper_ref,
                   x2_ref, h2_ref, lp_ref, tw_ref, ct_ref, of_ref, wp_ref, st_ref, vn_ref,
                   mix_scr, q_scr, k_scr, v_scr, oc_scr, st_stage, st_sem, *, n_streams, frames):
    tile = n_streams * frames
    x = x_ref[...]
    h = _rmsnorm(x, g1_ref[...]).astype(BF16)

    def proj(i):
        return _dot(h, win_ref[:, i * RET_WIDTH:(i + 1) * RET_WIDTH])

    q = proj(0)
    k = proj(1)
    v = proj(2)
    gate = proj(3)
    u = proj(4)
    vg = proj(5)
    cq = cq_ref[...]
    sq = sq_ref[...]
    ck = cq * K_SCALE
    sk = sq * K_SCALE

    r_i = lax.broadcasted_iota(I32, (tile, tile), 0)
    c_i = lax.broadcasted_iota(I32, (tile, tile), 1)
    shift = frames.bit_length() - 1
    r_frame = lax.bitwise_and(r_i, frames - 1)
    c_frame = lax.bitwise_and(c_i, frames - 1)
    keep = (lax.shift_right_logical(r_i, shift) == lax.shift_right_logical(c_i, shift)) & (r_frame >= c_frame)
    frame_gap = (r_frame - c_frame).astype(F32)

    o_in = []
    for hd in range(N_HEADS):
        cols = slice(hd * HEAD_DIM, (hd + 1) * HEAD_DIM)
        qr = _rotary(q[:, cols], cq, sq)
        kr = _rotary(k[:, cols], ck, sk)
        vh = v[:, cols].astype(BF16)
        qb = qr.astype(BF16)
        decay = jnp.where(keep, jnp.exp(frame_gap * _LOG_GAMMA[hd]), 0.0)
        scores = _dot_nt(qb, kr.astype(BF16)) * decay
        o_in.append(_dot(scores.astype(BF16), vh))
        q_scr[hd] = qb
        k_scr[hd] = (kr * kdec_ref[hd]).astype(BF16)
        v_scr[hd] = vh

    def state_copy(b):
        slot = lax.rem(b, 2)
        return pltpu.make_async_copy(st_stage.at[slot], st_ref.at[b], st_sem.at[slot])

    def stream_body(b, carry):
        rows = pl.ds(pl.multiple_of(b * frames, frames), frames)
        slot = lax.rem(b, 2)

        @pl.when(b >= 2)
        def _():
            state_copy(b - 2).wait()

        for hd in range(N_HEADS):
            state = s0_ref[b, hd]
            oc_scr[hd, rows, :] = _dot(q_scr[hd, rows, :], state.astype(BF16))
            st_stage[slot, hd] = sdec_ref[hd] * state + _dot_tn(k_scr[hd, rows, :], v_scr[hd, rows, :])
        state_copy(b).start()
        return carry

    lax.fori_loop(0, n_streams, stream_body, 0)
    state_copy(n_streams - 2).wait()
    state_copy(n_streams - 1).wait()

    for hd in range(N_HEADS):
        cols = slice(hd * HEAD_DIM, (hd + 1) * HEAD_DIM)
        o = o_in[hd] + qdec_ref[hd] * oc_scr[hd]
        mix_scr[:, cols] = _gated_head_norm(o, gate[:, cols])

    uf, vn = _gmlp_inputs(u, vg, lng_ref[...], lnb_ref[...])
    vn_ref[...] = vn
    vnb = vn.astype(BF16)
    sel = (lax.bitwise_and(lax.broadcasted_iota(I32, (tile, frames), 0), frames - 1)
           == lax.broadcasted_iota(I32, (tile, frames), 1)).astype(BF16)
    for hd in range(N_HEADS):
        cols = slice(hd * HEAD_DIM, (hd + 1) * HEAD_DIM)
        w_rows = _dot(sel, ws_ref[hd].astype(BF16)).astype(BF16)
        w_blk = jnp.where(keep, _dot_nt(w_rows, sel), 0.0).astype(BF16)
        mixed = _dot(w_blk, vnb[:, cols]) + bst_ref[:, hd:hd + 1]
        mix_scr[:, RET_WIDTH + hd * HEAD_DIM:RET_WIDTH + (hd + 1) * HEAD_DIM] = (uf[:, cols] * mixed).astype(BF16)

    _route_and_store(x, mix_scr, wout_ref, g2_ref, wr_ref, br_ref, tri_ref, upper_ref,
                     x2_ref, h2_ref, lp_ref, tw_ref, ct_ref, of_ref, wp_ref)


def _rope_tables(pos):
    half = HEAD_DIM // 2
    inv = np.float32(ROPE_BASE) ** (-np.arange(half, dtype=np.float32) / np.float32(half))
    ang = pos.astype(np.float32)[:, None] * inv[None, :]
    cos = np.cos(ang).astype(np.float32)
    sin = np.sin(ang).astype(np.float32)
    return np.concatenate([cos, cos], axis=-1), np.concatenate([-sin, sin], axis=-1)


def _decay_tables(block):
    log_g = np.asarray(_LOG_GAMMA, np.float32)
    idx = np.arange(block, dtype=np.float32)
    diff = idx[:, None] - idx[None, :]
    dmat = np.where(diff[None] >= 0, np.exp(np.maximum(diff, 0.0)[None] * log_g[:, None, None]), 0.0)
    q_dec = np.exp((idx + 1.0)[None, :] * log_g[:, None])
    k_dec = np.exp((block - 1.0 - idx)[None, :] * log_g[:, None])
    s_dec = np.exp(np.float32(block) * log_g)
    bc = lambda a: np.ascontiguousarray(
        np.broadcast_to(a[:, :, None], (N_HEADS, block, HEAD_DIM)).astype(np.float32))
    s_row = np.ascontiguousarray(np.broadcast_to(s_dec[:, None, None], (N_HEADS, 1, HEAD_DIM)).astype(np.float32))
    return dmat.astype(np.float32), bc(q_dec), bc(k_dec), s_row


def _routing_constants(tile):
    tri = jnp.asarray(np.arange(tile)[:, None] > np.arange(tile)[None, :], BF16)
    upper = jnp.asarray(np.arange(N_EXPERTS)[:, None] < np.arange(N_EXPERTS)[None, :], BF16)
    return tri, upper


def _front_out(n_tok, n_tiles):
    return [
        jax.ShapeDtypeStruct((n_tok, D_MODEL), F32),
        jax.ShapeDtypeStruct((n_tok * ROW_CHUNKS, LANES), U32),
        jax.ShapeDtypeStruct((n_tok, TOP_K), I32),
        jax.ShapeDtypeStruct((n_tok, TOP_K), F32),
        jax.ShapeDtypeStruct((n_tiles, 1, N_EXPERTS), F32),
        jax.ShapeDtypeStruct((n_tiles, 1, N_EXPERTS), F32),
        jax.ShapeDtypeStruct((n_tiles, 1, TILE_SLOTS), I32),
    ]


def _prompt_call(x, p):
    bsz, seq, _ = x.shape
    tile = TOKEN_TILE
    n_t = seq // tile
    cq, sq = _rope_tables(np.arange(seq))
    dmat, qdec, kdec, sdec = _decay_tables(PROMPT_RET_BLOCK)
    tri, upper = _routing_constants(tile)
    ws = p['w_s'][:, :GM_CHUNK, :GM_CHUNK]
    bst = p['b_s'][:, :GM_CHUNK].T

    n_tiles = bsz * n_t
    cur = lambda s: jnp.minimum(s, n_tiles - 1)
    prev = lambda s: jnp.maximum(s - 1, 0)
    const = lambda shape: pl.BlockSpec(shape, lambda s: (0,) * len(shape))
    pos = lambda: pl.BlockSpec((tile, HEAD_DIM), lambda s: (lax.rem(cur(s), n_t), 0))
    in_specs = [
        pl.BlockSpec((1, tile, D_MODEL), lambda s: (cur(s) // n_t, lax.rem(cur(s), n_t), 0)),
        const((1, D_MODEL)), const((D_MODEL, 6 * RET_WIDTH)),
        pos(), pos(),
        const((N_HEADS, PROMPT_RET_BLOCK, PROMPT_RET_BLOCK)),
        const((N_HEADS, PROMPT_RET_BLOCK, HEAD_DIM)), const((N_HEADS, PROMPT_RET_BLOCK, HEAD_DIM)),
        const((N_HEADS, 1, HEAD_DIM)),
        const((1, GM_WIDTH)), const((1, GM_WIDTH)),
        const((N_HEADS, GM_CHUNK, GM_CHUNK)), const((GM_CHUNK, N_HEADS)),
        const((D_MODEL, D_MODEL)), const((1, D_MODEL)),
        const((D_MODEL, N_EXPERTS)), const((1, N_EXPERTS)),
        const((tile, tile)), const((N_EXPERTS, N_EXPERTS)),
    ]
    out_specs = [
        pl.BlockSpec((tile, D_MODEL), lambda s: (cur(s), 0)),
        pl.BlockSpec((tile * ROW_CHUNKS, LANES), lambda s: (prev(s), 0)),
        pl.BlockSpec((tile, TOP_K), lambda s: (prev(s), 0)),
        pl.BlockSpec((tile, TOP_K), lambda s: (prev(s), 0)),
        pl.BlockSpec((1, 1, N_EXPERTS), lambda s: (prev(s), 0, 0)),
        pl.BlockSpec((1, 1, N_EXPERTS), lambda s: (prev(s), 0, 0)),
        pl.BlockSpec((1, 1, TILE_SLOTS), lambda s: (prev(s), 0, 0)),
        pl.BlockSpec((1, N_HEADS, HEAD_DIM, HEAD_DIM), lambda s: (cur(s) // n_t, 0, 0, 0)),
    ]
    out_shape = _front_out(bsz * seq, n_tiles) + [jax.ShapeDtypeStruct((bsz, N_HEADS, HEAD_DIM, HEAD_DIM), F32)]
    return pl.pallas_call(
        functools.partial(_prompt_kernel, ret_block=PROMPT_RET_BLOCK, tiles_per_seq=n_t, n_tiles=n_tiles),
        grid=(n_tiles + 1,),
        in_specs=in_specs,
        out_specs=out_specs,
        out_shape=out_shape,
        scratch_shapes=[
            pltpu.VMEM((N_HEADS, HEAD_DIM, HEAD_DIM), F32),
            pltpu.VMEM((tile, D_MODEL), BF16),
            pltpu.VMEM((tile, D_MODEL), F32),
        ],
        compiler_params=pltpu.CompilerParams(
            dimension_semantics=("arbitrary",), vmem_limit_bytes=VMEM_LIMIT),
        name="front_prompt",
    )(x, p['norm1_g'], p['w_in'], cq, sq, dmat, qdec, kdec, sdec, p['ln_v_g'], p['ln_v_b'], ws, bst,
      p['w_out'], p['norm2_g'], p['w_router'], p['b_router'], tri, upper)


def _decode_call(x, state, past_len, p):
    n_streams, frames, _ = x.shape
    tile = n_streams * frames
    assert tile == TOKEN_TILE and frames <= RET_CHUNK and frames & (frames - 1) == 0
    cq, sq = [np.tile(a, (n_streams, 1)) for a in _rope_tables(past_len + np.arange(frames))]
    _, qdec, kdec, sdec = _decay_tables(frames)
    qdec = np.tile(qdec, (1, n_streams, 1))
    kdec = np.tile(kdec, (1, n_streams, 1))
    tri, upper = _routing_constants(tile)
    ws = p['w_s'][:, :frames, :frames]
    bst = jnp.tile(p['b_s'][:, :frames].T, (n_streams, 1))

    def whole(a):
        return pl.BlockSpec(a.shape, lambda i, n=a.ndim: (0,) * n, pipeline_mode=pl.Buffered(1))

    args = (x.reshape(tile, D_MODEL), state, p['norm1_g'], p['w_in'], cq, sq, qdec, kdec, sdec,
            p['ln_v_g'], p['ln_v_b'], ws, bst, p['w_out'], p['norm2_g'], p['w_router'], p['b_router'],
            tri, upper)
    out_shape = _front_out(tile, 1) + [
        jax.ShapeDtypeStruct(state.shape, F32),
        jax.ShapeDtypeStruct((tile, GM_WIDTH), F32),
    ]
    out_specs = [pl.BlockSpec(s.shape, lambda i, n=len(s.shape): (0,) * n) for s in out_shape]
    out_specs[7] = pl.BlockSpec(memory_space=pl.ANY)
    return pl.pallas_call(
        functools.partial(_decode_kernel, n_streams=n_streams, frames=frames),
        grid=(1,),
        in_specs=[whole(a) for a in args],
        out_specs=out_specs,
        out_shape=out_shape,
        scratch_shapes=[
            pltpu.VMEM((tile, D_MODEL), BF16),
            pltpu.VMEM((N_HEADS, tile, HEAD_DIM), BF16),
            pltpu.VMEM((N_HEADS, tile, HEAD_DIM), BF16),
            pltpu.VMEM((N_HEADS, tile, HEAD_DIM), BF16),
            pltpu.VMEM((N_HEADS, tile, HEAD_DIM), F32),
            pltpu.VMEM((2, N_HEADS, HEAD_DIM, HEAD_DIM), F32),
            pltpu.SemaphoreType.DMA((2,)),
        ],
        compiler_params=pltpu.CompilerParams(
            dimension_semantics=("arbitrary",), vmem_limit_bytes=VMEM_LIMIT),
        name="front_decode",
    )(*args)


def _plan_kernel(ct_ref, lower_ref, upper_ref, cum_ref, nb_ref, bs_ref, be_ref, nbt_ref, tlo_ref, thi_ref,
                 *, n_blocks):
    counts = ct_ref[...]
    cum = _exact_count_dot(lower_ref[...], counts, counts_on_left=False)
    cum_ref[...] = cum.astype(I32)
    n_tiles = counts.shape[0]
    total = cum[n_tiles:n_tiles + 1]
    nb = jnp.floor((total + (EXPERT_ROWS - 1)) * (1.0 / EXPERT_ROWS))
    bstart = _dot(jnp.broadcast_to(nb, (8, N_EXPERTS)).astype(BF16), upper_ref[...])[0:1]
    bend = bstart + nb
    nb_ref[...] = nb.astype(I32)
    bs_ref[...] = bstart.astype(I32)
    blk = lax.broadcasted_iota(I32, (n_blocks, N_EXPERTS), 0).astype(F32)
    be = jnp.minimum(jnp.sum((bend <= blk).astype(F32), axis=-1, keepdims=True), N_EXPERTS - 1.0)
    be_ref[...] = be.astype(I32)
    nbt_ref[...] = jnp.sum(nb, axis=-1, keepdims=True).astype(I32)
    mine = lax.broadcasted_iota(I32, (n_blocks, N_EXPERTS), 1).astype(F32) == be
    pick = lambda row: jnp.sum(jnp.where(mine, row, 0.0), axis=-1, keepdims=True)
    r0 = (blk[:, 0:1] - pick(bstart)) * EXPERT_ROWS
    tlo = jnp.zeros((n_blocks, 1), F32)
    thi = jnp.zeros((n_blocks, 1), F32)
    seg_lo = pick(cum[0:1])
    for i in range(n_tiles):
        seg_hi = pick(cum[i + 1:i + 2])
        tlo = tlo + (seg_hi <= r0).astype(F32)
        thi = thi + (seg_lo < r0 + EXPERT_ROWS).astype(F32)
        seg_lo = seg_hi
    tlo_ref[...] = tlo.astype(I32)
    thi_ref[...] = thi.astype(I32)


def _plan_call(counts, n_blocks):
    n_tiles = counts.shape[0]
    lower = (jnp.arange(n_tiles + 1)[:, None] > jnp.arange(n_tiles)[None, :]).astype(BF16)
    upper = (jnp.arange(N_EXPERTS)[:, None] < jnp.arange(N_EXPERTS)[None, :]).astype(BF16)
    return pl.pallas_call(
        functools.partial(_plan_kernel, n_blocks=n_blocks),
        out_shape=[
            jax.ShapeDtypeStruct((n_tiles + 1, N_EXPERTS), I32),
            jax.ShapeDtypeStruct((1, N_EXPERTS), I32),
            jax.ShapeDtypeStruct((1, N_EXPERTS), I32),
            jax.ShapeDtypeStruct((n_blocks, 1), I32),
            jax.ShapeDtypeStruct((1, 1), I32),
            jax.ShapeDtypeStruct((n_blocks, 1), I32),
            jax.ShapeDtypeStruct((n_blocks, 1), I32),
        ],
        name="plan",
    )(counts, lower, upper)


def _dispatch_kernel(h2p_ref, h2s_ref, wp_ref, out_ref, *, n_prompt_tiles):
    i = pl.program_id(0)

    def permute(src_ref):
        def body(t, c):
            row = src_ref[_row_tile(t), :]
            for kk in range(TOP_K):
                first = pl.multiple_of(wp_ref[0, 0, kk * TOKEN_TILE + t], ROW_CHUNKS)
                out_ref[pl.ds(first, ROW_CHUNKS), :] = row
            return c

        lax.fori_loop(0, TOKEN_TILE, body, 0, unroll=8)

    @pl.when(i < n_prompt_tiles)
    def _():
        permute(h2p_ref)

    @pl.when(i >= n_prompt_tiles)
    def _():
        permute(h2s_ref)


def _dispatch_call(h2_p, h2_s, lp_tiles):
    rows = TOKEN_TILE * ROW_CHUNKS
    n_p_tiles = h2_p.shape[0] // rows
    n_s_tiles = h2_s.shape[0] // rows
    n_tiles = n_p_tiles + n_s_tiles
    return pl.pallas_call(
        functools.partial(_dispatch_kernel, n_prompt_tiles=n_p_tiles),
        grid=(n_tiles,),
        in_specs=[
            pl.BlockSpec((rows, LANES), lambda i: (jnp.minimum(i, n_p_tiles - 1), 0)),
            pl.BlockSpec((rows, LANES), lambda i: (jnp.maximum(i - n_p_tiles, 0), 0)),
            pl.BlockSpec((1, 1, TILE_SLOTS), lambda i: (i, 0, 0), memory_space=pltpu.SMEM),
        ],
        out_specs=pl.BlockSpec((TILE_SLOTS * ROW_CHUNKS, LANES), lambda i: (i, 0)),
        out_shape=jax.ShapeDtypeStruct((n_tiles * TILE_SLOTS * ROW_CHUNKS, LANES), U32),
        compiler_params=pltpu.CompilerParams(
            dimension_semantics=("arbitrary",), vmem_limit_bytes=VMEM_LIMIT),
        name="dispatch",
    )(h2_p, h2_s, lp_tiles)


def _expert_kernel(be_ref, bs_ref, nb_ref, nbt_ref, cum_ref, off_ref, tlo_ref, thi_ref,
                   xb_ref, wg_ref, wu_ref, wd_ref, bg_ref, bu_ref, bd_ref,
                   yb_ref, xbuf, ybuf, wbf, in_sem, out_sem, *, n_tiles):
    b = pl.program_id(0)
    nbt = nbt_ref[0]

    def block_rows(blk):
        e = be_ref[blk]
        r0 = (blk - bs_ref[e]) * EXPERT_ROWS
        return r0, jnp.minimum(cum_ref[n_tiles * N_EXPERTS + e] - r0, EXPERT_ROWS)

    def segments(blk, slot, fn):
        e = be_ref[blk]
        r0 = (blk - bs_ref[e]) * EXPERT_ROWS

        def body(i, c):
            seg_lo = cum_ref[i * N_EXPERTS + e]
            seg_hi = cum_ref[(i + 1) * N_EXPERTS + e]
            lo = jnp.maximum(seg_lo, r0)
            hi = jnp.minimum(seg_hi, r0 + EXPERT_ROWS)

            @pl.when(hi > lo)
            def _():
                bucket_row = i * TILE_SLOTS + off_ref[i * N_EXPERTS + e] + (lo - seg_lo)
                fn(slot, bucket_row, slot * EXPERT_ROWS + (lo - r0), hi - lo)
            return c

        lax.fori_loop(tlo_ref[blk], thi_ref[blk], body, 0)

    def words(row, n):
        return pl.ds(pl.multiple_of(row * ROW_CHUNKS, ROW_CHUNKS), n * ROW_CHUNKS)

    def gather(slot, bucket_row, buffer_row, n):
        return pltpu.make_async_copy(
            xb_ref.at[words(bucket_row, n)], xbuf.at[words(buffer_row, n)], in_sem.at[slot])

    def scatter(slot, bucket_row, buffer_row, n):
        return pltpu.make_async_copy(
            ybuf.at[words(buffer_row, n)], yb_ref.at[words(bucket_row, n)], out_sem.at[slot])

    def start_gather(blk):
        @pl.when(blk < nbt)
        def _():
            segments(blk, lax.rem(blk, 2), lambda *a: gather(*a).start())

    def wait_gather(blk):
        @pl.when(blk < nbt)
        def _():
            slot = lax.rem(blk, 2)
            gather(slot, 0, slot * EXPERT_ROWS, block_rows(blk)[1]).wait()

    def start_scatter(blk):
        @pl.when(blk < nbt)
        def _():
            segments(blk, lax.rem(blk, 2), lambda *a: scatter(*a).start())

    def wait_scatter(blk):
        @pl.when((blk >= 0) & (blk < nbt))
        def _():
            slot = lax.rem(blk, 2)
            scatter(slot, 0, slot * EXPERT_ROWS, block_rows(blk)[1]).wait()

    def mlp(base, n_rows, rows_left):
        valid = lax.broadcasted_iota(I32, (n_rows, 1), 0) < rows_left
        x = jnp.where(valid, _load_rows(xbuf, base, n_rows), 0.0)
        gt = _dot(x, wbf[0]) + bg_ref[0]
        up = _dot(x, wbf[1]) + bu_ref[0]
        gt = jnp.minimum(gt, SWIGLU_LIMIT)
        up = jnp.clip(up, -SWIGLU_LIMIT, SWIGLU_LIMIT)
        act = gt * _sigmoid(gt * SWIGLU_ALPHA) * (up + 1.0)
        _store_rows(ybuf, base, _dot(act.astype(BF16), wbf[2]) + bd_ref[0])

    def block_body(blk, carry):
        wait_gather(blk)
        start_gather(blk + 1)
        wait_scatter(blk - 2)
        rows_left = block_rows(blk)[1]
        base = lax.rem(blk, 2) * EXPERT_ROWS

        lower = 0
        for upper, passes in EXPERT_PASS_PLANS:
            @pl.when((rows_left > lower) & (rows_left <= upper))
            def _(passes=passes):
                first = 0
                for n_rows in passes:
                    mlp(base + first, n_rows, rows_left - first)
                    first += n_rows
            lower = upper

        start_scatter(blk)
        return carry

    first_block = bs_ref[b]
    n_mine = nb_ref[b]

    @pl.when(b == 0)
    def _():
        xbuf[...] = jnp.zeros_like(xbuf)
        start_gather(0)

    @pl.when(n_mine > 0)
    def _():
        wbf[0] = wg_ref[0].astype(BF16)
        wbf[1] = wu_ref[0].astype(BF16)
        wbf[2] = wd_ref[0].astype(BF16)

    lax.fori_loop(first_block, first_block + n_mine, block_body, 0)

    @pl.when(b == N_EXPERTS - 1)
    def _():
        wait_scatter(nbt - 2)
        wait_scatter(nbt - 1)


def _expert_call(be, bstart, nb, nbt, cum_flat, off_flat, tlo, thi, buckets, p, n_tiles):
    wspec = pl.BlockSpec((1, D_MODEL, D_MODEL), lambda e, *_: (e, 0, 0))
    bspec = pl.BlockSpec((1, 1, D_MODEL), lambda e, *_: (e, 0, 0))
    any_spec = pl.BlockSpec(memory_space=pl.ANY)
    return pl.pallas_call(
        functools.partial(_expert_kernel, n_tiles=n_tiles),
        grid_spec=pltpu.PrefetchScalarGridSpec(
            num_scalar_prefetch=8,
            grid=(N_EXPERTS,),
            in_specs=[any_spec, wspec, wspec, wspec, bspec, bspec, bspec],
            out_specs=any_spec,
            scratch_shapes=[
                pltpu.VMEM((2 * EXPERT_ROWS * ROW_CHUNKS, LANES), U32),
                pltpu.VMEM((2 * EXPERT_ROWS * ROW_CHUNKS, LANES), U32),
                pltpu.VMEM((3, D_MODEL, D_MODEL), BF16),
                pltpu.SemaphoreType.DMA((2,)),
                pltpu.SemaphoreType.DMA((2,)),
            ],
        ),
        out_shape=jax.ShapeDtypeStruct(buckets.shape, U32),
        compiler_params=pltpu.CompilerParams(
            dimension_semantics=("arbitrary",), vmem_limit_bytes=VMEM_LIMIT),
        name="experts",
    )(be, bstart, nb, nbt, cum_flat, off_flat, tlo, thi, buckets, p['w_gate'], p['w_up'], p['w_down'],
      p['b_gate'][:, None, :], p['b_up'][:, None, :], p['b_down'][:, None, :])


def _combine_kernel(x2_ref, yb_ref, lp_ref, tw_ref, gf_ref, y_ref, ybf, pick):
    for c in range(TILE_SLOTS // COMBINE_CHUNK):
        rows = slice(c * COMBINE_CHUNK, (c + 1) * COMBINE_CHUNK)
        ybf[rows, :] = _load_rows(yb_ref, c * COMBINE_CHUNK, COMBINE_CHUNK)
    slot = lax.broadcasted_iota(I32, (TOKEN_TILE, TILE_SLOTS), 1)
    lp = lp_ref[...]
    tw = tw_ref[...]
    sel = jnp.zeros((TOKEN_TILE, TILE_SLOTS), F32)
    for kk in range(TOP_K):
        sel = jnp.where(slot == lp[:, kk:kk + 1], tw[:, kk:kk + 1], sel)
    pick[...] = sel.astype(BF16)
    y_ref[...] = _rmsnorm(x2_ref[...] + _dot(pick[...], ybf[...]), gf_ref[...])


def _combine_call(x2, ybuckets, lp, tw, gf, tile_off):
    n_tok = x2.shape[0]
    n_tiles = n_tok // TOKEN_TILE
    return pl.pallas_call(
        _combine_kernel,
        grid=(n_tiles,),
        in_specs=[
            pl.BlockSpec((TOKEN_TILE, D_MODEL), lambda i: (i, 0)),
            pl.BlockSpec((TILE_SLOTS * ROW_CHUNKS, LANES), lambda i: (tile_off + i, 0)),
            pl.BlockSpec((TOKEN_TILE, TOP_K), lambda i: (i, 0)),
            pl.BlockSpec((TOKEN_TILE, TOP_K), lambda i: (i, 0)),
            pl.BlockSpec((1, D_MODEL), lambda i: (0, 0)),
        ],
        out_specs=pl.BlockSpec((TOKEN_TILE, D_MODEL), lambda i: (i, 0)),
        scratch_shapes=[
            pltpu.VMEM((TILE_SLOTS, D_MODEL), BF16),
            pltpu.VMEM((TOKEN_TILE, TILE_SLOTS), BF16),
        ],
        out_shape=jax.ShapeDtypeStruct((n_tok, D_MODEL), F32),
        compiler_params=pltpu.CompilerParams(
            dimension_semantics=("arbitrary",), vmem_limit_bytes=VMEM_LIMIT),
        name="combine",
    )(x2, ybuckets, lp, tw, gf)


def kernel(x_prompt, x_sample, state_ret, norm1_g, w_in, ln_v_g, ln_v_b, w_s, b_s, w_out, norm2_g, w_router, b_router, w_gate, b_gate, w_up, b_up, w_down, b_down, norm_f_g):
    bsz, seq, _ = x_prompt.shape
    dbsz, dseq, _ = x_sample.shape
    n_p = bsz * seq
    n_s = dbsz * dseq
    n_total = n_p + n_s
    past_len = 2048
    assert w_in.shape[0] == 1, "single layer"

    p = dict(norm1_g=norm1_g, w_in=w_in[0].astype(BF16), ln_v_g=ln_v_g, ln_v_b=ln_v_b, w_s=w_s[0], b_s=b_s[0],
             w_out=w_out[0].astype(BF16), norm2_g=norm2_g, w_router=w_router[0].astype(BF16), b_router=b_router,
             w_gate=w_gate[0], b_gate=b_gate[0], w_up=w_up[0], b_up=b_up[0], w_down=w_down[0], b_down=b_down[0])

    x2_p, h2_p, lp_p, tw_p, ct_p, of_p, wp_p, st_p = _prompt_call(x_prompt, p)
    x2_s, h2_s, lp_s, tw_s, ct_s, of_s, wp_s, st_s, vn_s = _decode_call(x_sample, state_ret[0], past_len, p)

    n_tiles = n_total // TOKEN_TILE
    word_pos = jnp.concatenate([wp_p, wp_s], axis=0)
    counts = jnp.concatenate([ct_p, ct_s], axis=0).reshape(n_tiles, N_EXPERTS)
    offsets = jnp.concatenate([of_p, of_s], axis=0).reshape(n_tiles * N_EXPERTS).astype(I32)

    n_blocks = -(-(n_total * TOP_K + N_EXPERTS * (EXPERT_ROWS - 1)) // EXPERT_ROWS)
    cum, nb, bstart, be, nbt, tlo, thi = _plan_call(counts, n_blocks)

    buckets = _dispatch_call(h2_p, h2_s, word_pos)
    ybuckets = _expert_call(be.reshape(n_blocks), bstart.reshape(N_EXPERTS), nb.reshape(N_EXPERTS), nbt.reshape(1),
                            cum.reshape((n_tiles + 1) * N_EXPERTS), offsets, tlo.reshape(n_blocks),
                            thi.reshape(n_blocks), buckets, p, n_tiles)

    y_p = _combine_call(x2_p, ybuckets, lp_p, tw_p, norm_f_g[None, :], 0)
    y_s = _combine_call(x2_s, ybuckets, lp_s, tw_s, norm_f_g[None, :], n_p // TOKEN_TILE)

    return (y_p.reshape(bsz, seq, D_MODEL), y_s.reshape(dbsz, dseq, D_MODEL),
            st_p[None], st_s[None], vn_s.reshape(1, dbsz, dseq, GM_WIDTH))
```

```python
import functools

import numpy as np
import jax
import jax.numpy as jnp
from jax import lax
from jax.experimental import pallas as pl
from jax.experimental.pallas import tpu as pltpu

F32 = jnp.float32
BF16 = jnp.bfloat16
I32 = jnp.int32

D_MODEL = 1024
RET_WIDTH = 512
N_HEADS = 4
HEAD_DIM = 128
GM_WIDTH = 512
GM_CHUNK = 128
RET_CHUNK = 64
N_EXPERTS = 32
TOP_K = 4
SWIGLU_LIMIT = 7.0
SWIGLU_ALPHA = 1.702
ROPE_BASE = 10000.0
EPS = 1e-6

TOKEN_TILE = 512
PROMPT_RET_BLOCK = 256
EXPERT_ROWS = 1024
EXPERT_PASS_PLANS = ((256, (256,)), (512, (512,)), (768, (512, 256)), (1024, (512, 512)))
TILE_SLOTS = TOKEN_TILE * TOP_K
COMBINE_CHUNK = 256
VMEM_LIMIT = 56 * 1024 * 1024
LANES = 128
U32 = jnp.uint32
ROW_CHUNKS = D_MODEL // LANES // 2
K_SCALE = HEAD_DIM ** -0.5
SPLIT = 16.0
_LOG_GAMMA = [float(np.log1p(-np.float32(2.0) ** np.float32(-5.0 - hd)).astype(np.float32)) for hd in range(N_HEADS)]


def _load_rows(ref, first_row, n_rows):
    base = first_row * ROW_CHUNKS
    lo, hi = [], []
    for c in range(ROW_CHUNKS):
        w = ref[pl.ds(base + c, n_rows, stride=ROW_CHUNKS), :]
        lo.append(pltpu.bitcast(lax.shift_left(w, jnp.uint32(16)), F32))
        hi.append(pltpu.bitcast(lax.bitwise_and(w, jnp.uint32(0xFFFF0000)), F32))
    return jnp.concatenate(lo + hi, axis=1).astype(BF16)


def _store_rows(ref, first_row, val):
    base = first_row * ROW_CHUNKS
    bits = pltpu.bitcast(val.astype(BF16).astype(F32), U32)
    half = D_MODEL // 2
    for c in range(ROW_CHUNKS):
        low = lax.shift_right_logical(bits[:, c * LANES:(c + 1) * LANES], jnp.uint32(16))
        ref[pl.ds(base + c, val.shape[0], stride=ROW_CHUNKS), :] = lax.bitwise_or(
            bits[:, half + c * LANES:half + (c + 1) * LANES], low)


def _row_tile(row):
    return pl.ds(pl.multiple_of(row * ROW_CHUNKS, ROW_CHUNKS), ROW_CHUNKS)


def _rmsnorm(x, g):
    ms = jnp.mean(x * x, axis=-1, keepdims=True)
    return x * lax.rsqrt(ms + EPS) * g


def _gelu(x):
    c = np.float32(np.sqrt(2.0 / np.pi))
    return x * (0.5 * (1.0 + jnp.tanh(c * (x + 0.044715 * (x * x * x)))))


def _sigmoid(x):
    return 1.0 / (1.0 + jnp.exp(-x))


def _dot(a, b):
    return jnp.dot(a, b, preferred_element_type=F32)


def _dot_nt(a, b):
    return lax.dot_general(a, b, (((1,), (1,)), ((), ())), preferred_element_type=F32)


def _dot_tn(a, b):
    return lax.dot_general(a, b, (((0,), (0,)), ((), ())), preferred_element_type=F32)


def _exact_count_dot(a, b, *, counts_on_left):
    cnt = a if counts_on_left else b
    hi = jnp.floor(cnt * (1.0 / SPLIT))
    lo = cnt - SPLIT * hi
    if counts_on_left:
        return SPLIT * _dot(hi.astype(BF16), b) + _dot(lo.astype(BF16), b)
    return SPLIT * _dot(a, hi.astype(BF16)) + _dot(a, lo.astype(BF16))


def _rotary(xh, cos2, sin2):
    return xh * cos2 + pltpu.roll(xh, HEAD_DIM // 2, axis=1) * sin2


def _gated_head_norm(o, gate):
    mu = jnp.mean(o, axis=-1, keepdims=True)
    oc = o - mu
    var = jnp.mean(oc * oc, axis=-1, keepdims=True)
    return ((gate * _sigmoid(gate)) * (oc * lax.rsqrt(var + EPS))).astype(BF16)


def _gmlp_inputs(u, vg, lng, lnb):
    uf = _gelu(u)
    vf = _gelu(vg)
    mu = jnp.mean(vf, axis=-1, keepdims=True)
    vc = vf - mu
    var = jnp.mean(vc * vc, axis=-1, keepdims=True)
    return uf, vc * lax.rsqrt(var + EPS) * lng + lnb


def _route_and_store(x, mix_scr, wout_ref, g2_ref, wr_ref, br_ref, tri_ref, upper_ref,
                     x2_ref, h2_ref, lp_ref, tw_ref, ct_ref, of_ref, wp_ref):
    h2 = _out_proj(x, mix_scr, wout_ref, g2_ref, x2_ref, h2_ref)
    _route(h2, wr_ref, br_ref, tri_ref, upper_ref, lp_ref, tw_ref, ct_ref, of_ref, wp_ref)


def _out_proj(x, mix_scr, wout_ref, g2_ref, x2_ref, h2_ref):
    x2 = x + _dot(mix_scr[...], wout_ref[...])
    x2_ref[...] = x2
    h2 = _rmsnorm(x2, g2_ref[...])
    _store_rows(h2_ref, 0, h2)
    return h2


def _route(h2, wr_ref, br_ref, tri_ref, upper_ref, lp_ref, tw_ref, ct_ref, of_ref, wp_ref):
    _route_place(_route_choose(h2, wr_ref, br_ref), tri_ref, upper_ref, lp_ref, tw_ref, ct_ref, of_ref, wp_ref)


def _route_choose(h2, wr_ref, br_ref):
    tile = h2.shape[0]
    logits = _dot(h2.astype(BF16), wr_ref[...]) + br_ref[...]

    iota_e = lax.broadcasted_iota(I32, (tile, N_EXPERTS), 1)
    iota_ef = iota_e.astype(F32)
    lg = logits
    vals = []
    idxs = []
    for _ in range(TOP_K):
        m = jnp.max(lg, axis=-1, keepdims=True)
        idx = jnp.min(jnp.where(lg == m, iota_ef, float(N_EXPERTS)), axis=-1, keepdims=True).astype(I32)
        vals.append(m)
        idxs.append(idx)
        lg = jnp.where(iota_e == idx, -jnp.inf, lg)
    exps = [jnp.exp(vk - vals[0]) for vk in vals]
    denom = exps[0] + exps[1] + exps[2] + exps[3]
    return idxs, exps, denom


def _route_place(choice, tri_ref, upper_ref, lp_ref, tw_ref, ct_ref, of_ref, wp_ref):
    idxs, exps, denom = choice
    tile = denom.shape[0]
    iota_e = lax.broadcasted_iota(I32, (tile, N_EXPERTS), 1)
    iota_k = lax.broadcasted_iota(I32, (tile, TOP_K), 1)
    onehot = jnp.zeros((tile, N_EXPERTS), F32)
    for idx in idxs:
        onehot = onehot + (iota_e == idx).astype(F32)
    rank = _dot(tri_ref[...], onehot.astype(BF16))
    count = jnp.sum(onehot, axis=0, keepdims=True)
    offset = _exact_count_dot(jnp.broadcast_to(count, (8, N_EXPERTS)), upper_ref[...], counts_on_left=True)[0:1]
    where_to = rank + offset
    lp = jnp.zeros((tile, TOP_K), I32)
    tw = jnp.zeros((tile, TOP_K), F32)
    word_rows = []
    for kk in range(TOP_K):
        pos = jnp.sum(jnp.where(iota_e == idxs[kk], where_to, 0.0), axis=-1, keepdims=True)
        lp = jnp.where(iota_k == kk, pos.astype(I32), lp)
        tw = jnp.where(iota_k == kk, exps[kk] / denom, tw)
        word_rows.append(jnp.transpose(jnp.broadcast_to(pos * float(ROW_CHUNKS), (tile, LANES)))[0:1, :])
    lp_ref[...] = lp
    tw_ref[...] = tw
    ct_ref[0] = count
    of_ref[0] = offset
    wp_ref[0] = jnp.concatenate(word_rows, axis=1).astype(I32)


def _prompt_kernel(x_ref, g1_ref, win_ref, cq_ref, sq_ref, dmat_ref, qdec_ref, kdec_ref,
                   sdec_ref, lng_ref, lnb_ref, ws_ref, bst_ref, wout_ref, g2_ref, wr_ref, br_ref,
                   tri_ref, upper_ref,
                   x2_ref, lp_ref, tw_ref, ct_ref, of_ref, bk_ref, st_ref,
                   s_scr, mix_scr, x2_carry, h2w, wp_vmem, wp_smem, wp_sem, *, ret_block, tiles_per_seq, n_tiles):
    step = pl.program_id(0)
    tile = TOKEN_TILE
    cur_slot = lax.rem(step + 1, 2)
    old_slot = lax.rem(step, 2)

    def position_copy(slot):
        return pltpu.make_async_copy(wp_vmem.at[slot, 0, 0],
                                     wp_smem.at[pl.ds(pl.multiple_of(slot * TILE_SLOTS, TILE_SLOTS), TILE_SLOTS)],
                                     wp_sem.at[slot])

    @pl.when(step == 0)
    def _():
        s_scr[...] = jnp.zeros_like(s_scr)
        x2_carry[...] = jnp.zeros_like(x2_carry)
        h2w[...] = jnp.zeros_like(h2w)

        def init(j, c):
            wp_smem[j] = lax.rem(j, TILE_SLOTS) * ROW_CHUNKS
            return c

        lax.fori_loop(0, 2 * TILE_SLOTS, init, 0)

    @pl.when(step > 0)
    def _():
        position_copy(old_slot).wait()

    def norm_prev_tile():
        h2_prev = _rmsnorm(x2_carry[...], g2_ref[...])
        _store_rows(h2w, cur_slot * tile, h2_prev)
        return _route_choose(h2_prev, wr_ref, br_ref)

    def place_prev_tile(choice):
        _route_place(choice, tri_ref, upper_ref, lp_ref, tw_ref, ct_ref, of_ref, wp_vmem.at[cur_slot])

    def move_row(t):
        row = h2w[pl.ds(pl.multiple_of((old_slot * tile + t) * ROW_CHUNKS, ROW_CHUNKS), ROW_CHUNKS), :]
        for kk in range(TOP_K):
            first = wp_smem[old_slot * TILE_SLOTS + kk * tile + t]
            bk_ref[pl.ds(pl.multiple_of(first, ROW_CHUNKS), ROW_CHUNKS), :] = row

    def move_rows_loop():
        def body(t, c):
            move_row(t)
            return c
        lax.fori_loop(0, tile, body, 0, unroll=8)

    def move_rows_inline():
        for t in range(tile):
            move_row(t)

    @pl.when(step < n_tiles)
    def _():
        _prompt_step(step, norm_prev_tile, place_prev_tile, move_rows_inline, x_ref, g1_ref, win_ref, cq_ref,
                     sq_ref, dmat_ref, qdec_ref, kdec_ref, sdec_ref, lng_ref, lnb_ref, ws_ref, bst_ref, wout_ref,
                     x2_ref, st_ref, s_scr, mix_scr, x2_carry, ret_block=ret_block, tiles_per_seq=tiles_per_seq)
        position_copy(cur_slot).start()

    @pl.when(step == n_tiles)
    def _():
        place_prev_tile(norm_prev_tile())
        move_rows_loop()
        position_copy(cur_slot).start()

    @pl.when(step == n_tiles + 1)
    def _():
        move_rows_loop()


def _prompt_step(step, norm_prev_tile, place_prev_tile, move_old_rows, x_ref, g1_ref, win_ref, cq_ref, sq_ref,
                 dmat_ref, qdec_ref, kdec_ref, sdec_ref, lng_ref, lnb_ref, ws_ref, bst_ref, wout_ref,
                 x2_ref, st_ref, s_scr, mix_scr, x2_carry, *, ret_block, tiles_per_seq):
    tile = TOKEN_TILE
    t = lax.rem(step, tiles_per_seq)

    choice = norm_prev_tile()
    s_scr[...] = jnp.where(t == 0, 0.0, s_scr[...])

    x = x_ref[0]
    h = _rmsnorm(x, g1_ref[...]).astype(BF16)

    def proj(i):
        return _dot(h, win_ref[:, i * RET_WIDTH:(i + 1) * RET_WIDTH])

    q = proj(0)
    k = proj(1)
    v = proj(2)
    gate = proj(3)
    u = proj(4)
    vg = proj(5)
    place_prev_tile(choice)
    cq = cq_ref[...]
    sq = sq_ref[...]
    ck = cq * K_SCALE
    sk = sq * K_SCALE

    for hd in range(N_HEADS):
        cols = slice(hd * HEAD_DIM, (hd + 1) * HEAD_DIM)
        qr = _rotary(q[:, cols], cq, sq)
        kr = _rotary(k[:, cols], ck, sk)
        vh = v[:, cols].astype(BF16)
        dm = dmat_ref[hd]
        qd = qdec_ref[hd]
        kd = kdec_ref[hd]
        sd = sdec_ref[hd]
        for c in range(tile // ret_block):
            rows = slice(c * ret_block, (c + 1) * ret_block)
            qb = qr[rows].astype(BF16)
            kb = kr[rows]
            vb = vh[rows]
            state = s_scr[hd]
            scores = _dot_nt(qb, kb.astype(BF16)) * dm
            o = _dot(scores.astype(BF16), vb) + qd * _dot(qb, state.astype(BF16))
            s_scr[hd] = sd * state + _dot_tn((kb * kd).astype(BF16), vb)
            mix_scr[rows, cols] = _gated_head_norm(o, gate[rows, cols])

    uf, vn = _gmlp_inputs(u, vg, lng_ref[...], lnb_ref[...])
    vnb = vn.astype(BF16)
    r_i = lax.broadcasted_iota(I32, (GM_CHUNK, GM_CHUNK), 0)
    c_i = lax.broadcasted_iota(I32, (GM_CHUNK, GM_CHUNK), 1)
    for hd in range(N_HEADS):
        cols = slice(hd * HEAD_DIM, (hd + 1) * HEAD_DIM)
        w_tril = jnp.where(r_i >= c_i, ws_ref[hd], 0.0).astype(BF16)
        bias = bst_ref[:, hd:hd + 1]
        for c in range(tile // GM_CHUNK):
            rows = slice(c * GM_CHUNK, (c + 1) * GM_CHUNK)
            mixed = _dot(w_tril, vnb[rows, cols]) + bias
            mix_scr[rows, RET_WIDTH + hd * HEAD_DIM:RET_WIDTH + (hd + 1) * HEAD_DIM] = (
                uf[rows, cols] * mixed).astype(BF16)

    x2 = x + _dot(mix_scr[...], wout_ref[...])
    x2_ref[...] = x2
    x2_carry[...] = x2
    move_old_rows()

    @pl.when(t == tiles_per_seq - 1)
    def _():
        st_ref[0] = s_scr[...]


def _decode_kernel(x_ref, s0_ref, g1_ref, win_ref, cq_ref, sq_ref, qdec_ref, kdec_ref,
                   sdec_ref, lng_ref, lnb_ref, ws_ref, bst_ref, wout_ref, g2_ref, wr_ref, br_ref,
                   tri_ref, upper_ref,
                   x2_ref, h2_ref, lp_ref, tw_ref, ct_ref, of_ref, wp_ref, st_ref, vn_ref,
                   mix_scr, q_scr, k_scr, v_scr, oc_scr, st_stage, st_sem, *, n_streams, frames):
    tile = n_streams * frames
    x = x_ref[...]
    h = _rmsnorm(x, g1_ref[...]).astype(BF16)

    def proj(i):
        return _dot(h, win_ref[:, i * RET_WIDTH:(i + 1) * RET_WIDTH])

    q = proj(0)
    k = proj(1)
    v = proj(2)
    gate = proj(3)
    u = proj(4)
    vg = proj(5)
    cq = cq_ref[...]
    sq = sq_ref[...]
    ck = cq * K_SCALE
    sk = sq * K_SCALE

    r_i = lax.broadcasted_iota(I32, (tile, tile), 0)
    c_i = lax.broadcasted_iota(I32, (tile, tile), 1)
    shift = frames.bit_length() - 1
    r_frame = lax.bitwise_and(r_i, frames - 1)
    c_frame = lax.bitwise_and(c_i, frames - 1)
    keep = (lax.shift_right_logical(r_i, shift) == lax.shift_right_logical(c_i, shift)) & (r_frame >= c_frame)
    frame_gap = (r_frame - c_frame).astype(F32)

    o_in = []
    for hd in range(N_HEADS):
        cols = slice(hd * HEAD_DIM, (hd + 1) * HEAD_DIM)
        qr = _rotary(q[:, cols], cq, sq)
        kr = _rotary(k[:, cols], ck, sk)
        vh = v[:, cols].astype(BF16)
        qb = qr.astype(BF16)
        decay = jnp.where(keep, jnp.exp(frame_gap * _LOG_GAMMA[hd]), 0.0)
        scores = _dot_nt(qb, kr.astype(BF16)) * decay
        o_in.append(_dot(scores.astype(BF16), vh))
        q_scr[hd] = qb
        k_scr[hd] = (kr * kdec_ref[hd]).astype(BF16)
        v_scr[hd] = vh

    def state_copy(b):
        slot = lax.rem(b, 2)
        return pltpu.make_async_copy(st_stage.at[slot], st_ref.at[b], st_sem.at[slot])

    def stream_body(b, carry):
        rows = pl.ds(pl.multiple_of(b * frames, frames), frames)
        slot = lax.rem(b, 2)

        @pl.when(b >= 2)
        def _():
            state_copy(b - 2).wait()

        for hd in range(N_HEADS):
            state = s0_ref[b, hd]
            oc_scr[hd, rows, :] = _dot(q_scr[hd, rows, :], state.astype(BF16))
            st_stage[slot, hd] = sdec_ref[hd] * state + _dot_tn(k_scr[hd, rows, :], v_scr[hd, rows, :])
        state_copy(b).start()
        return carry

    lax.fori_loop(0, n_streams, stream_body, 0)
    state_copy(n_streams - 2).wait()
    state_copy(n_streams - 1).wait()

    for hd in range(N_HEADS):
        cols = slice(hd * HEAD_DIM, (hd + 1) * HEAD_DIM)
        o = o_in[hd] + qdec_ref[hd] * oc_scr[hd]
        mix_scr[:, cols] = _gated_head_norm(o, gate[:, cols])

    uf, vn = _gmlp_inputs(u, vg, lng_ref[...], lnb_ref[...])
    vn_ref[...] = vn
    vnb = vn.astype(BF16)
    sel = (lax.bitwise_and(lax.broadcasted_iota(I32, (tile, frames), 0), frames - 1)
           == lax.broadcasted_iota(I32, (tile, frames), 1)).astype(BF16)
    for hd in range(N_HEADS):
        cols = slice(hd * HEAD_DIM, (hd + 1) * HEAD_DIM)
        w_rows = _dot(sel, ws_ref[hd].astype(BF16)).astype(BF16)
        w_blk = jnp.where(keep, _dot_nt(w_rows, sel), 0.0).astype(BF16)
        mixed = _dot(w_blk, vnb[:, cols]) + bst_ref[:, hd:hd + 1]
        mix_scr[:, RET_WIDTH + hd * HEAD_DIM:RET_WIDTH + (hd + 1) * HEAD_DIM] = (uf[:, cols] * mixed).astype(BF16)

    _route_and_store(x, mix_scr, wout_ref, g2_ref, wr_ref, br_ref, tri_ref, upper_ref,
                     x2_ref, h2_ref, lp_ref, tw_ref, ct_ref, of_ref, wp_ref)


def _rope_tables(pos):
    half = HEAD_DIM // 2
    inv = np.float32(ROPE_BASE) ** (-np.arange(half, dtype=np.float32) / np.float32(half))
    ang = pos.astype(np.float32)[:, None] * inv[None, :]
    cos = np.cos(ang).astype(np.float32)
    sin = np.sin(ang).astype(np.float32)
    return np.concatenate([cos, cos], axis=-1), np.concatenate([-sin, sin], axis=-1)


def _decay_tables(block):
    log_g = np.asarray(_LOG_GAMMA, np.float32)
    idx = np.arange(block, dtype=np.float32)
    diff = idx[:, None] - idx[None, :]
    dmat = np.where(diff[None] >= 0, np.exp(np.maximum(diff, 0.0)[None] * log_g[:, None, None]), 0.0)
    q_dec = np.exp((idx + 1.0)[None, :] * log_g[:, None])
    k_dec = np.exp((block - 1.0 - idx)[None, :] * log_g[:, None])
    s_dec = np.exp(np.float32(block) * log_g)
    bc = lambda a: np.ascontiguousarray(
        np.broadcast_to(a[:, :, None], (N_HEADS, block, HEAD_DIM)).astype(np.float32))
    s_row = np.ascontiguousarray(np.broadcast_to(s_dec[:, None, None], (N_HEADS, 1, HEAD_DIM)).astype(np.float32))
    return dmat.astype(np.float32), bc(q_dec), bc(k_dec), s_row


def _routing_constants(tile):
    tri = jnp.asarray(np.arange(tile)[:, None] > np.arange(tile)[None, :], BF16)
    upper = jnp.asarray(np.arange(N_EXPERTS)[:, None] < np.arange(N_EXPERTS)[None, :], BF16)
    return tri, upper


def _front_out(n_tok, n_tiles):
    return [
        jax.ShapeDtypeStruct((n_tok, D_MODEL), F32),
        jax.ShapeDtypeStruct((n_tok * ROW_CHUNKS, LANES), U32),
        jax.ShapeDtypeStruct((n_tok, TOP_K), I32),
        jax.ShapeDtypeStruct((n_tok, TOP_K), F32),
        jax.ShapeDtypeStruct((n_tiles, 1, N_EXPERTS), F32),
        jax.ShapeDtypeStruct((n_tiles, 1, N_EXPERTS), F32),
        jax.ShapeDtypeStruct((n_tiles, 1, TILE_SLOTS), I32),
    ]


def _prompt_call(x, p):
    bsz, seq, _ = x.shape
    tile = TOKEN_TILE
    n_t = seq // tile
    cq, sq = _rope_tables(np.arange(seq))
    dmat, qdec, kdec, sdec = _decay_tables(PROMPT_RET_BLOCK)
    tri, upper = _routing_constants(tile)
    ws = p['w_s'][:, :GM_CHUNK, :GM_CHUNK]
    bst = p['b_s'][:, :GM_CHUNK].T

    n_tiles = bsz * n_t
    cur = lambda s: jnp.minimum(s, n_tiles - 1)
    prev = lambda s: jnp.clip(s - 1, 0, n_tiles - 1)
    old = lambda s: jnp.clip(s - 2, 0, n_tiles - 1)
    const = lambda shape: pl.BlockSpec(shape, lambda s: (0,) * len(shape))
    pos = lambda: pl.BlockSpec((tile, HEAD_DIM), lambda s: (lax.rem(cur(s), n_t), 0))
    in_specs = [
        pl.BlockSpec((1, tile, D_MODEL), lambda s: (cur(s) // n_t, lax.rem(cur(s), n_t), 0)),
        const((1, D_MODEL)), const((D_MODEL, 6 * RET_WIDTH)),
        pos(), pos(),
        const((N_HEADS, PROMPT_RET_BLOCK, PROMPT_RET_BLOCK)),
        const((N_HEADS, PROMPT_RET_BLOCK, HEAD_DIM)), const((N_HEADS, PROMPT_RET_BLOCK, HEAD_DIM)),
        const((N_HEADS, 1, HEAD_DIM)),
        const((1, GM_WIDTH)), const((1, GM_WIDTH)),
        const((N_HEADS, GM_CHUNK, GM_CHUNK)), const((GM_CHUNK, N_HEADS)),
        const((D_MODEL, D_MODEL)), const((1, D_MODEL)),
        const((D_MODEL, N_EXPERTS)), const((1, N_EXPERTS)),
        const((tile, tile)), const((N_EXPERTS, N_EXPERTS)),
    ]
    out_specs = [
        pl.BlockSpec((tile, D_MODEL), lambda s: (cur(s), 0)),
        pl.BlockSpec((tile, TOP_K), lambda s: (prev(s), 0)),
        pl.BlockSpec((tile, TOP_K), lambda s: (prev(s), 0)),
        pl.BlockSpec((1, 1, N_EXPERTS), lambda s: (prev(s), 0, 0)),
        pl.BlockSpec((1, 1, N_EXPERTS), lambda s: (prev(s), 0, 0)),
        pl.BlockSpec((TILE_SLOTS * ROW_CHUNKS, LANES), lambda s: (old(s), 0)),
        pl.BlockSpec((1, N_HEADS, HEAD_DIM, HEAD_DIM), lambda s: (cur(s) // n_t, 0, 0, 0)),
    ]
    n_tok = bsz * seq
    out_shape = [
        jax.ShapeDtypeStruct((n_tok, D_MODEL), F32),
        jax.ShapeDtypeStruct((n_tok, TOP_K), I32),
        jax.ShapeDtypeStruct((n_tok, TOP_K), F32),
        jax.ShapeDtypeStruct((n_tiles, 1, N_EXPERTS), F32),
        jax.ShapeDtypeStruct((n_tiles, 1, N_EXPERTS), F32),
        jax.ShapeDtypeStruct((n_tiles * TILE_SLOTS * ROW_CHUNKS, LANES), U32),
        jax.ShapeDtypeStruct((bsz, N_HEADS, HEAD_DIM, HEAD_DIM), F32),
    ]
    return pl.pallas_call(
        functools.partial(_prompt_kernel, ret_block=PROMPT_RET_BLOCK, tiles_per_seq=n_t, n_tiles=n_tiles),
        grid=(n_tiles + 2,),
        in_specs=in_specs,
        out_specs=out_specs,
        out_shape=out_shape,
        scratch_shapes=[
            pltpu.VMEM((N_HEADS, HEAD_DIM, HEAD_DIM), F32),
            pltpu.VMEM((tile, D_MODEL), BF16),
            pltpu.VMEM((tile, D_MODEL), F32),
            pltpu.VMEM((2 * tile * ROW_CHUNKS, LANES), U32),
            pltpu.VMEM((2, 1, 1, TILE_SLOTS), I32),
            pltpu.SMEM((2 * TILE_SLOTS,), I32),
            pltpu.SemaphoreType.DMA((2,)),
        ],
        compiler_params=pltpu.CompilerParams(
            dimension_semantics=("arbitrary",), vmem_limit_bytes=VMEM_LIMIT),
        name="front_prompt",
    )(x, p['norm1_g'], p['w_in'], cq, sq, dmat, qdec, kdec, sdec, p['ln_v_g'], p['ln_v_b'], ws, bst,
      p['w_out'], p['norm2_g'], p['w_router'], p['b_router'], tri, upper)


def _decode_call(x, state, past_len, p):
    n_streams, frames, _ = x.shape
    tile = n_streams * frames
    assert tile == TOKEN_TILE and frames <= RET_CHUNK and frames & (frames - 1) == 0
    cq, sq = [np.tile(a, (n_streams, 1)) for a in _rope_tables(past_len + np.arange(frames))]
    _, qdec, kdec, sdec = _decay_tables(frames)
    qdec = np.tile(qdec, (1, n_streams, 1))
    kdec = np.tile(kdec, (1, n_streams, 1))
    tri, upper = _routing_constants(tile)
    ws = p['w_s'][:, :frames, :frames]
    bst = jnp.tile(p['b_s'][:, :frames].T, (n_streams, 1))

    def whole(a):
        return pl.BlockSpec(a.shape, lambda i, n=a.ndim: (0,) * n, pipeline_mode=pl.Buffered(1))

    args = (x.reshape(tile, D_MODEL), state, p['norm1_g'], p['w_in'], cq, sq, qdec, kdec, sdec,
            p['ln_v_g'], p['ln_v_b'], ws, bst, p['w_out'], p['norm2_g'], p['w_router'], p['b_router'],
            tri, upper)
    out_shape = _front_out(tile, 1) + [
        jax.ShapeDtypeStruct(state.shape, F32),
        jax.ShapeDtypeStruct((tile, GM_WIDTH), F32),
    ]
    out_specs = [pl.BlockSpec(s.shape, lambda i, n=len(s.shape): (0,) * n) for s in out_shape]
    out_specs[7] = pl.BlockSpec(memory_space=pl.ANY)
    return pl.pallas_call(
        functools.partial(_decode_kernel, n_streams=n_streams, frames=frames),
        grid=(1,),
        in_specs=[whole(a) for a in args],
        out_specs=out_specs,
        out_shape=out_shape,
        scratch_shapes=[
            pltpu.VMEM((tile, D_MODEL), BF16),
            pltpu.VMEM((N_HEADS, tile, HEAD_DIM), BF16),
            pltpu.VMEM((N_HEADS, tile, HEAD_DIM), BF16),
            pltpu.VMEM((N_HEADS, tile, HEAD_DIM), BF16),
            pltpu.VMEM((N_HEADS, tile, HEAD_DIM), F32),
            pltpu.VMEM((2, N_HEADS, HEAD_DIM, HEAD_DIM), F32),
            pltpu.SemaphoreType.DMA((2,)),
        ],
        compiler_params=pltpu.CompilerParams(
            dimension_semantics=("arbitrary",), vmem_limit_bytes=VMEM_LIMIT),
        name="front_decode",
    )(*args)


def _plan_kernel(ct_ref, lower_ref, upper_ref, cum_ref, nb_ref, bs_ref, be_ref, nbt_ref, tlo_ref, thi_ref,
                 *, n_blocks):
    counts = ct_ref[...]
    cum = _exact_count_dot(lower_ref[...], counts, counts_on_left=False)
    cum_ref[...] = cum.astype(I32)
    n_tiles = counts.shape[0]
    total = cum[n_tiles:n_tiles + 1]
    nb = jnp.floor((total + (EXPERT_ROWS - 1)) * (1.0 / EXPERT_ROWS))
    bstart = _dot(jnp.broadcast_to(nb, (8, N_EXPERTS)).astype(BF16), upper_ref[...])[0:1]
    bend = bstart + nb
    nb_ref[...] = nb.astype(I32)
    bs_ref[...] = bstart.astype(I32)
    blk = lax.broadcasted_iota(I32, (n_blocks, N_EXPERTS), 0).astype(F32)
    be = jnp.minimum(jnp.sum((bend <= blk).astype(F32), axis=-1, keepdims=True), N_EXPERTS - 1.0)
    be_ref[...] = be.astype(I32)
    nbt_ref[...] = jnp.sum(nb, axis=-1, keepdims=True).astype(I32)
    mine = lax.broadcasted_iota(I32, (n_blocks, N_EXPERTS), 1).astype(F32) == be
    pick = lambda row: jnp.sum(jnp.where(mine, row, 0.0), axis=-1, keepdims=True)
    r0 = (blk[:, 0:1] - pick(bstart)) * EXPERT_ROWS
    tlo = jnp.zeros((n_blocks, 1), F32)
    thi = jnp.zeros((n_blocks, 1), F32)
    seg_lo = pick(cum[0:1])
    for i in range(n_tiles):
        seg_hi = pick(cum[i + 1:i + 2])
        tlo = tlo + (seg_hi <= r0).astype(F32)
        thi = thi + (seg_lo < r0 + EXPERT_ROWS).astype(F32)
        seg_lo = seg_hi
    tlo_ref[...] = tlo.astype(I32)
    thi_ref[...] = thi.astype(I32)


def _plan_call(counts, n_blocks):
    n_tiles = counts.shape[0]
    lower = (jnp.arange(n_tiles + 1)[:, None] > jnp.arange(n_tiles)[None, :]).astype(BF16)
    upper = (jnp.arange(N_EXPERTS)[:, None] < jnp.arange(N_EXPERTS)[None, :]).astype(BF16)
    return pl.pallas_call(
        functools.partial(_plan_kernel, n_blocks=n_blocks),
        out_shape=[
            jax.ShapeDtypeStruct((n_tiles + 1, N_EXPERTS), I32),
            jax.ShapeDtypeStruct((1, N_EXPERTS), I32),
            jax.ShapeDtypeStruct((1, N_EXPERTS), I32),
            jax.ShapeDtypeStruct((n_blocks, 1), I32),
            jax.ShapeDtypeStruct((1, 1), I32),
            jax.ShapeDtypeStruct((n_blocks, 1), I32),
            jax.ShapeDtypeStruct((n_blocks, 1), I32),
        ],
        name="plan",
    )(counts, lower, upper)


def _dispatch_kernel(h2_ref, wp_ref, out_ref):
    def body(t, c):
        row = h2_ref[_row_tile(t), :]
        for kk in range(TOP_K):
            first = pl.multiple_of(wp_ref[0, 0, kk * TOKEN_TILE + t], ROW_CHUNKS)
            out_ref[pl.ds(first, ROW_CHUNKS), :] = row
        return c

    lax.fori_loop(0, TOKEN_TILE, body, 0, unroll=8)


def _dispatch_call(h2, word_pos):
    rows = TOKEN_TILE * ROW_CHUNKS
    n_tiles = h2.shape[0] // rows
    return pl.pallas_call(
        _dispatch_kernel,
        grid=(n_tiles,),
        in_specs=[
            pl.BlockSpec((rows, LANES), lambda i: (i, 0)),
            pl.BlockSpec((1, 1, TILE_SLOTS), lambda i: (i, 0, 0), memory_space=pltpu.SMEM),
        ],
        out_specs=pl.BlockSpec((TILE_SLOTS * ROW_CHUNKS, LANES), lambda i: (i, 0)),
        out_shape=jax.ShapeDtypeStruct((n_tiles * TILE_SLOTS * ROW_CHUNKS, LANES), U32),
        compiler_params=pltpu.CompilerParams(
            dimension_semantics=("arbitrary",), vmem_limit_bytes=VMEM_LIMIT),
        name="dispatch",
    )(h2, word_pos)


def _expert_kernel(be_ref, bs_ref, nb_ref, nbt_ref, cum_ref, off_ref, tlo_ref, thi_ref,
                   xb_ref, xbd_ref, wg_ref, wu_ref, wd_ref, bg_ref, bu_ref, bd_ref,
                   yb_ref, xbuf, ybuf, wbf, in_sem, out_sem, *, n_tiles, n_prompt_slots):
    b = pl.program_id(0)
    nbt = nbt_ref[0]

    def block_rows(blk):
        e = be_ref[blk]
        r0 = (blk - bs_ref[e]) * EXPERT_ROWS
        return r0, jnp.minimum(cum_ref[n_tiles * N_EXPERTS + e] - r0, EXPERT_ROWS)

    def segments(blk, slot, fn):
        e = be_ref[blk]
        r0 = (blk - bs_ref[e]) * EXPERT_ROWS

        def body(i, c):
            seg_lo = cum_ref[i * N_EXPERTS + e]
            seg_hi = cum_ref[(i + 1) * N_EXPERTS + e]
            lo = jnp.maximum(seg_lo, r0)
            hi = jnp.minimum(seg_hi, r0 + EXPERT_ROWS)

            @pl.when(hi > lo)
            def _():
                bucket_row = i * TILE_SLOTS + off_ref[i * N_EXPERTS + e] + (lo - seg_lo)
                fn(slot, bucket_row, slot * EXPERT_ROWS + (lo - r0), hi - lo)
            return c

        lax.fori_loop(tlo_ref[blk], thi_ref[blk], body, 0)

    def words(row, n):
        return pl.ds(pl.multiple_of(row * ROW_CHUNKS, ROW_CHUNKS), n * ROW_CHUNKS)

    def gather(slot, bucket_row, buffer_row, n, src_ref=xb_ref):
        return pltpu.make_async_copy(
            src_ref.at[words(bucket_row, n)], xbuf.at[words(buffer_row, n)], in_sem.at[slot])

    def scatter(slot, bucket_row, buffer_row, n):
        return pltpu.make_async_copy(
            ybuf.at[words(buffer_row, n)], yb_ref.at[words(bucket_row, n)], out_sem.at[slot])

    def start_segment_gather(slot, bucket_row, buffer_row, n):
        @pl.when(bucket_row < n_prompt_slots)
        def _():
            gather(slot, bucket_row, buffer_row, n).start()

        @pl.when(bucket_row >= n_prompt_slots)
        def _():
            gather(slot, bucket_row - n_prompt_slots, buffer_row, n, xbd_ref).start()

    def start_gather(blk):
        @pl.when(blk < nbt)
        def _():
            segments(blk, lax.rem(blk, 2), start_segment_gather)

    def wait_gather(blk):
        @pl.when(blk < nbt)
        def _():
            slot = lax.rem(blk, 2)
            gather(slot, 0, slot * EXPERT_ROWS, block_rows(blk)[1]).wait()

    def start_scatter(blk):
        @pl.when(blk < nbt)
        def _():
            segments(blk, lax.rem(blk, 2), lambda *a: scatter(*a).start())

    def wait_scatter(blk):
        @pl.when((blk >= 0) & (blk < nbt))
        def _():
            slot = lax.rem(blk, 2)
            scatter(slot, 0, slot * EXPERT_ROWS, block_rows(blk)[1]).wait()

    def mlp(base, n_rows, rows_left):
        valid = lax.broadcasted_iota(I32, (n_rows, 1), 0) < rows_left
        x = jnp.where(valid, _load_rows(xbuf, base, n_rows), 0.0)
        gt = _dot(x, wbf[0]) + bg_ref[0]
        up = _dot(x, wbf[1]) + bu_ref[0]
        gt = jnp.minimum(gt, SWIGLU_LIMIT)
        up = jnp.clip(up, -SWIGLU_LIMIT, SWIGLU_LIMIT)
        act = gt * _sigmoid(gt * SWIGLU_ALPHA) * (up + 1.0)
        _store_rows(ybuf, base, _dot(act.astype(BF16), wbf[2]) + bd_ref[0])

    def block_body(blk, carry):
        wait_gather(blk)
        start_gather(blk + 1)
        wait_scatter(blk - 2)
        rows_left = block_rows(blk)[1]
        base = lax.rem(blk, 2) * EXPERT_ROWS

        lower = 0
        for upper, passes in EXPERT_PASS_PLANS:
            @pl.when((rows_left > lower) & (rows_left <= upper))
            def _(passes=passes):
                first = 0
                for n_rows in passes:
                    mlp(base + first, n_rows, rows_left - first)
                    first += n_rows
            lower = upper

        start_scatter(blk)
        return carry

    first_block = bs_ref[b]
    n_mine = nb_ref[b]

    @pl.when(b == 0)
    def _():
        xbuf[...] = jnp.zeros_like(xbuf)
        start_gather(0)

    @pl.when(n_mine > 0)
    def _():
        wbf[0] = wg_ref[0].astype(BF16)
        wbf[1] = wu_ref[0].astype(BF16)
        wbf[2] = wd_ref[0].astype(BF16)

    lax.fori_loop(first_block, first_block + n_mine, block_body, 0)

    @pl.when(b == N_EXPERTS - 1)
    def _():
        wait_scatter(nbt - 2)
        wait_scatter(nbt - 1)


def _expert_call(be, bstart, nb, nbt, cum_flat, off_flat, tlo, thi, buckets, buckets_decode, p, n_tiles):
    wspec = pl.BlockSpec((1, D_MODEL, D_MODEL), lambda e, *_: (e, 0, 0))
    bspec = pl.BlockSpec((1, 1, D_MODEL), lambda e, *_: (e, 0, 0))
    any_spec = pl.BlockSpec(memory_space=pl.ANY)
    n_prompt_slots = buckets.shape[0] // ROW_CHUNKS
    n_slots = n_prompt_slots + buckets_decode.shape[0] // ROW_CHUNKS
    return pl.pallas_call(
        functools.partial(_expert_kernel, n_tiles=n_tiles, n_prompt_slots=n_prompt_slots),
        grid_spec=pltpu.PrefetchScalarGridSpec(
            num_scalar_prefetch=8,
            grid=(N_EXPERTS,),
            in_specs=[any_spec, any_spec, wspec, wspec, wspec, bspec, bspec, bspec],
            out_specs=any_spec,
            scratch_shapes=[
                pltpu.VMEM((2 * EXPERT_ROWS * ROW_CHUNKS, LANES), U32),
                pltpu.VMEM((2 * EXPERT_ROWS * ROW_CHUNKS, LANES), U32),
                pltpu.VMEM((3, D_MODEL, D_MODEL), BF16),
                pltpu.SemaphoreType.DMA((2,)),
                pltpu.SemaphoreType.DMA((2,)),
            ],
        ),
        out_shape=jax.ShapeDtypeStruct((n_slots * ROW_CHUNKS, LANES), U32),
        compiler_params=pltpu.CompilerParams(
            dimension_semantics=("arbitrary",), vmem_limit_bytes=VMEM_LIMIT),
        name="experts",
    )(be, bstart, nb, nbt, cum_flat, off_flat, tlo, thi, buckets, buckets_decode,
      p['w_gate'], p['w_up'], p['w_down'],
      p['b_gate'][:, None, :], p['b_up'][:, None, :], p['b_down'][:, None, :])


def _combine_kernel(x2_ref, yb_ref, lp_ref, tw_ref, gf_ref, y_ref, ybf, pick):
    for c in range(TILE_SLOTS // COMBINE_CHUNK):
        rows = slice(c * COMBINE_CHUNK, (c + 1) * COMBINE_CHUNK)
        ybf[rows, :] = _load_rows(yb_ref, c * COMBINE_CHUNK, COMBINE_CHUNK)
    slot = lax.broadcasted_iota(I32, (TOKEN_TILE, TILE_SLOTS), 1)
    lp = lp_ref[...]
    tw = tw_ref[...]
    sel = jnp.zeros((TOKEN_TILE, TILE_SLOTS), F32)
    for kk in range(TOP_K):
        sel = jnp.where(slot == lp[:, kk:kk + 1], tw[:, kk:kk + 1], sel)
    pick[...] = sel.astype(BF16)
    y_ref[...] = _rmsnorm(x2_ref[...] + _dot(pick[...], ybf[...]), gf_ref[...])


def _combine_call(x2, ybuckets, lp, tw, gf, tile_off):
    n_tok = x2.shape[0]
    n_tiles = n_tok // TOKEN_TILE
    return pl.pallas_call(
        _combine_kernel,
        grid=(n_tiles,),
        in_specs=[
            pl.BlockSpec((TOKEN_TILE, D_MODEL), lambda i: (i, 0)),
            pl.BlockSpec((TILE_SLOTS * ROW_CHUNKS, LANES), lambda i: (tile_off + i, 0)),
            pl.BlockSpec((TOKEN_TILE, TOP_K), lambda i: (i, 0)),
            pl.BlockSpec((TOKEN_TILE, TOP_K), lambda i: (i, 0)),
            pl.BlockSpec((1, D_MODEL), lambda i: (0, 0)),
        ],
        out_specs=pl.BlockSpec((TOKEN_TILE, D_MODEL), lambda i: (i, 0)),
        scratch_shapes=[
            pltpu.VMEM((TILE_SLOTS, D_MODEL), BF16),
            pltpu.VMEM((TOKEN_TILE, TILE_SLOTS), BF16),
        ],
        out_shape=jax.ShapeDtypeStruct((n_tok, D_MODEL), F32),
        compiler_params=pltpu.CompilerParams(
            dimension_semantics=("arbitrary",), vmem_limit_bytes=VMEM_LIMIT),
        name="combine",
    )(x2, ybuckets, lp, tw, gf)


def kernel(x_prompt, x_sample, state_ret, norm1_g, w_in, ln_v_g, ln_v_b, w_s, b_s, w_out, norm2_g, w_router, b_router, w_gate, b_gate, w_up, b_up, w_down, b_down, norm_f_g):
    bsz, seq, _ = x_prompt.shape
    dbsz, dseq, _ = x_sample.shape
    n_p = bsz * seq
    n_s = dbsz * dseq
    n_total = n_p + n_s
    past_len = 2048
    assert w_in.shape[0] == 1, "single layer"

    p = dict(norm1_g=norm1_g, w_in=w_in[0].astype(BF16), ln_v_g=ln_v_g, ln_v_b=ln_v_b, w_s=w_s[0], b_s=b_s[0],
             w_out=w_out[0].astype(BF16), norm2_g=norm2_g, w_router=w_router[0].astype(BF16), b_router=b_router,
             w_gate=w_gate[0], b_gate=b_gate[0], w_up=w_up[0], b_up=b_up[0], w_down=w_down[0], b_down=b_down[0])

    x2_p, lp_p, tw_p, ct_p, of_p, buckets_p, st_p = _prompt_call(x_prompt, p)
    x2_s, h2_s, lp_s, tw_s, ct_s, of_s, wp_s, st_s, vn_s = _decode_call(x_sample, state_ret[0], past_len, p)

    n_tiles = n_total // TOKEN_TILE
    counts = jnp.concatenate([ct_p, ct_s], axis=0).reshape(n_tiles, N_EXPERTS)
    offsets = jnp.concatenate([of_p, of_s], axis=0).reshape(n_tiles * N_EXPERTS).astype(I32)

    n_blocks = -(-(n_total * TOP_K + N_EXPERTS * (EXPERT_ROWS - 1)) // EXPERT_ROWS)
    cum, nb, bstart, be, nbt, tlo, thi = _plan_call(counts, n_blocks)

    buckets_s = _dispatch_call(h2_s, wp_s)
    ybuckets = _expert_call(be.reshape(n_blocks), bstart.reshape(N_EXPERTS), nb.reshape(N_EXPERTS), nbt.reshape(1),
                            cum.reshape((n_tiles + 1) * N_EXPERTS), offsets, tlo.reshape(n_blocks),
                            thi.reshape(n_blocks), buckets_p, buckets_s, p, n_tiles)

    y_p = _combine_call(x2_p, ybuckets, lp_p, tw_p, norm_f_g[None, :], 0)
    y_s = _combine_call(x2_s, ybuckets, lp_s, tw_s, norm_f_g[None, :], n_p // TOKEN_TILE)

    return (y_p.reshape(bsz, seq, D_MODEL), y_s.reshape(dbsz, dseq, D_MODEL),
            st_p[None], st_s[None], vn_s.reshape(1, dbsz, dseq, GM_WIDTH))
```

```python
import functools

import numpy as np
import jax
import jax.numpy as jnp
from jax import lax
from jax.experimental import pallas as pl
from jax.experimental.pallas import tpu as pltpu

F32 = jnp.float32
BF16 = jnp.bfloat16
I32 = jnp.int32
U32 = jnp.uint32

D_MODEL = 1024
RET_WIDTH = 512
N_HEADS = 4
HEAD_DIM = 128
GM_WIDTH = 512
GM_CHUNK = 128
RET_CHUNK = 64
N_EXPERTS = 32
TOP_K = 4
SWIGLU_LIMIT = 7.0
SWIGLU_ALPHA = 1.702
ROPE_BASE = 10000.0
EPS = 1e-6

TOKEN_TILE = 512
PROMPT_RET_BLOCK = 256
EXPERT_ROWS = 1024
EXPERT_PASS_PLANS = ((256, (256,)), (512, (512,)), (768, (512, 256)), (1024, (512, 512)))
TILE_SLOTS = TOKEN_TILE * TOP_K
COMBINE_CHUNK = 256
VMEM_LIMIT = 56 * 1024 * 1024
LANES = 128
ROW_CHUNKS = D_MODEL // LANES // 2
K_SCALE = HEAD_DIM ** -0.5
SPLIT = 16.0
_LOG_GAMMA = [float(np.log1p(-np.float32(2.0) ** np.float32(-5.0 - hd)).astype(np.float32)) for hd in range(N_HEADS)]


def _load_rows(ref, first_row, n_rows):
    base = first_row * ROW_CHUNKS
    lo, hi = [], []
    for c in range(ROW_CHUNKS):
        w = ref[pl.ds(base + c, n_rows, stride=ROW_CHUNKS), :]
        lo.append(pltpu.bitcast(lax.shift_left(w, jnp.uint32(16)), F32))
        hi.append(pltpu.bitcast(lax.bitwise_and(w, jnp.uint32(0xFFFF0000)), F32))
    return jnp.concatenate(lo + hi, axis=1).astype(BF16)


def _store_rows(ref, first_row, val):
    base = first_row * ROW_CHUNKS
    bits = pltpu.bitcast(val.astype(BF16).astype(F32), U32)
    half = D_MODEL // 2
    for c in range(ROW_CHUNKS):
        low = lax.shift_right_logical(bits[:, c * LANES:(c + 1) * LANES], jnp.uint32(16))
        ref[pl.ds(base + c, val.shape[0], stride=ROW_CHUNKS), :] = lax.bitwise_or(
            bits[:, half + c * LANES:half + (c + 1) * LANES], low)


def _row_tile(row):
    return pl.ds(pl.multiple_of(row * ROW_CHUNKS, ROW_CHUNKS), ROW_CHUNKS)


def _rmsnorm(x, g):
    ms = jnp.mean(x * x, axis=-1, keepdims=True)
    return x * lax.rsqrt(ms + EPS) * g


def _gelu(x):
    c = np.float32(np.sqrt(2.0 / np.pi))
    return x * (0.5 * (1.0 + jnp.tanh(c * (x + 0.044715 * (x * x * x)))))


def _sigmoid(x):
    return 1.0 / (1.0 + jnp.exp(-x))


def _dot(a, b):
    return jnp.dot(a, b, preferred_element_type=F32)


def _dot_nt(a, b):
    return lax.dot_general(a, b, (((1,), (1,)), ((), ())), preferred_element_type=F32)


def _dot_tn(a, b):
    return lax.dot_general(a, b, (((0,), (0,)), ((), ())), preferred_element_type=F32)


def _exact_count_dot(a, b, *, counts_on_left):
    cnt = a if counts_on_left else b
    hi = jnp.floor(cnt * (1.0 / SPLIT))
    lo = cnt - SPLIT * hi
    if counts_on_left:
        return SPLIT * _dot(hi.astype(BF16), b) + _dot(lo.astype(BF16), b)
    return SPLIT * _dot(a, hi.astype(BF16)) + _dot(a, lo.astype(BF16))


def _rotary(xh, cos2, sin2):
    return xh * cos2 + pltpu.roll(xh, HEAD_DIM // 2, axis=1) * sin2


def _gated_head_norm(o, gate):
    mu = jnp.mean(o, axis=-1, keepdims=True)
    oc = o - mu
    var = jnp.mean(oc * oc, axis=-1, keepdims=True)
    return ((gate * _sigmoid(gate)) * (oc * lax.rsqrt(var + EPS))).astype(BF16)


def _gmlp_inputs(u, vg, lng, lnb):
    uf = _gelu(u)
    vf = _gelu(vg)
    mu = jnp.mean(vf, axis=-1, keepdims=True)
    vc = vf - mu
    var = jnp.mean(vc * vc, axis=-1, keepdims=True)
    return uf, vc * lax.rsqrt(var + EPS) * lng + lnb


def _route_and_store(x, mix_scr, wout_ref, g2_ref, wr_ref, br_ref, tri_ref, upper_ref,
                     x2_ref, h2_ref, lp_ref, tw_ref, ct_ref, of_ref, wp_ref):
    h2 = _out_proj(x, mix_scr, wout_ref, g2_ref, x2_ref, h2_ref)
    _route(h2, wr_ref, br_ref, tri_ref, upper_ref, lp_ref, tw_ref, ct_ref, of_ref, wp_ref)


def _out_proj(x, mix_scr, wout_ref, g2_ref, x2_ref, h2_ref):
    x2 = x + _dot(mix_scr[...], wout_ref[...])
    x2_ref[...] = x2
    h2 = _rmsnorm(x2, g2_ref[...])
    _store_rows(h2_ref, 0, h2)
    return h2


def _route(h2, wr_ref, br_ref, tri_ref, upper_ref, lp_ref, tw_ref, ct_ref, of_ref, wp_ref):
    _route_place(_route_choose(h2, wr_ref, br_ref), tri_ref, upper_ref, lp_ref, tw_ref, ct_ref, of_ref, wp_ref)


def _route_choose(h2, wr_ref, br_ref):
    tile = h2.shape[0]
    logits = _dot(h2.astype(BF16), wr_ref[...]) + br_ref[...]

    iota_e = lax.broadcasted_iota(I32, (tile, N_EXPERTS), 1)
    iota_ef = iota_e.astype(F32)
    lg = logits
    vals = []
    idxs = []
    for _ in range(TOP_K):
        m = jnp.max(lg, axis=-1, keepdims=True)
        idx = jnp.min(jnp.where(lg == m, iota_ef, float(N_EXPERTS)), axis=-1, keepdims=True).astype(I32)
        vals.append(m)
        idxs.append(idx)
        lg = jnp.where(iota_e == idx, -jnp.inf, lg)
    exps = [jnp.exp(vk - vals[0]) for vk in vals]
    denom = exps[0] + exps[1] + exps[2] + exps[3]
    return idxs, exps, denom


def _route_place(choice, tri_ref, upper_ref, lp_ref, tw_ref, ct_ref, of_ref, wp_ref):
    idxs, exps, denom = choice
    tile = denom.shape[0]
    iota_e = lax.broadcasted_iota(I32, (tile, N_EXPERTS), 1)
    iota_k = lax.broadcasted_iota(I32, (tile, TOP_K), 1)
    onehot = jnp.zeros((tile, N_EXPERTS), F32)
    for idx in idxs:
        onehot = onehot + (iota_e == idx).astype(F32)
    rank = _dot(tri_ref[...], onehot.astype(BF16))
    count = jnp.sum(onehot, axis=0, keepdims=True)
    offset = _exact_count_dot(jnp.broadcast_to(count, (8, N_EXPERTS)), upper_ref[...], counts_on_left=True)[0:1]
    where_to = rank + offset
    lp = jnp.zeros((tile, TOP_K), I32)
    tw = jnp.zeros((tile, TOP_K), F32)
    word_rows = []
    for kk in range(TOP_K):
        pos = jnp.sum(jnp.where(iota_e == idxs[kk], where_to, 0.0), axis=-1, keepdims=True)
        lp = jnp.where(iota_k == kk, pos.astype(I32), lp)
        tw = jnp.where(iota_k == kk, exps[kk] / denom, tw)
        word_rows.append(jnp.transpose(jnp.broadcast_to(pos * float(ROW_CHUNKS), (tile, LANES)))[0:1, :])
    lp_ref[...] = lp
    tw_ref[...] = tw
    ct_ref[0] = count
    of_ref[0] = offset
    wp_ref[0] = jnp.concatenate(word_rows, axis=1).astype(I32)


def _prompt_kernel(x_ref, g1_ref, win_ref, cq_ref, sq_ref, dmat_ref, qdec_ref, kdec_ref,
                   sdec_ref, lng_ref, lnb_ref, ws_ref, bst_ref, wout_ref, g2_ref, wr_ref, br_ref,
                   tri_ref, upper_ref,
                   x2_ref, h2_ref, lp_ref, tw_ref, ct_ref, of_ref, wp_ref, st_ref,
                   s_scr, mix_scr, x2_carry, *, ret_block, tiles_per_seq, n_tiles):
    step = pl.program_id(0)

    @pl.when(step == 0)
    def _():
        s_scr[...] = jnp.zeros_like(s_scr)
        x2_carry[...] = jnp.zeros_like(x2_carry)

    def norm_prev_tile():
        h2_prev = _rmsnorm(x2_carry[...], g2_ref[...])
        _store_rows(h2_ref, 0, h2_prev)
        return _route_choose(h2_prev, wr_ref, br_ref)

    def place_prev_tile(choice):
        _route_place(choice, tri_ref, upper_ref, lp_ref, tw_ref, ct_ref, of_ref, wp_ref)

    @pl.when(step == n_tiles)
    def _():
        place_prev_tile(norm_prev_tile())

    @pl.when(step < n_tiles)
    def _():
        _prompt_step(step, norm_prev_tile, place_prev_tile, x_ref, g1_ref, win_ref, cq_ref, sq_ref,
                     dmat_ref, qdec_ref, kdec_ref, sdec_ref, lng_ref, lnb_ref, ws_ref, bst_ref, wout_ref,
                     x2_ref, st_ref, s_scr, mix_scr, x2_carry, ret_block=ret_block, tiles_per_seq=tiles_per_seq)


def _prompt_step(step, norm_prev_tile, place_prev_tile, x_ref, g1_ref, win_ref, cq_ref, sq_ref,
                 dmat_ref, qdec_ref, kdec_ref, sdec_ref, lng_ref, lnb_ref, ws_ref, bst_ref, wout_ref,
                 x2_ref, st_ref, s_scr, mix_scr, x2_carry, *, ret_block, tiles_per_seq):
    tile = TOKEN_TILE
    t = lax.rem(step, tiles_per_seq)

    choice = norm_prev_tile()
    s_scr[...] = jnp.where(t == 0, 0.0, s_scr[...])

    x = x_ref[0]
    h = _rmsnorm(x, g1_ref[...]).astype(BF16)

    def proj(i):
        return _dot(h, win_ref[:, i * RET_WIDTH:(i + 1) * RET_WIDTH])

    q = proj(0)
    k = proj(1)
    v = proj(2)
    gate = proj(3)
    u = proj(4)
    vg = proj(5)
    place_prev_tile(choice)
    cq = cq_ref[...]
    sq = sq_ref[...]
    ck = cq * K_SCALE
    sk = sq * K_SCALE

    for hd in range(N_HEADS):
        cols = slice(hd * HEAD_DIM, (hd + 1) * HEAD_DIM)
        qr = _rotary(q[:, cols], cq, sq)
        kr = _rotary(k[:, cols], ck, sk)
        vh = v[:, cols].astype(BF16)
        dm = dmat_ref[hd]
        qd = qdec_ref[hd]
        kd = kdec_ref[hd]
        sd = sdec_ref[hd]
        for c in range(tile // ret_block):
            rows = slice(c * ret_block, (c + 1) * ret_block)
            qb = qr[rows].astype(BF16)
            kb = kr[rows]
            vb = vh[rows]
            state = s_scr[hd]
            scores = _dot_nt(qb, kb.astype(BF16)) * dm
            o = _dot(scores.astype(BF16), vb) + qd * _dot(qb, state.astype(BF16))
            s_scr[hd] = sd * state + _dot_tn((kb * kd).astype(BF16), vb)
            mix_scr[rows, cols] = _gated_head_norm(o, gate[rows, cols])

    uf, vn = _gmlp_inputs(u, vg, lng_ref[...], lnb_ref[...])
    vnb = vn.astype(BF16)
    r_i = lax.broadcasted_iota(I32, (GM_CHUNK, GM_CHUNK), 0)
    c_i = lax.broadcasted_iota(I32, (GM_CHUNK, GM_CHUNK), 1)
    for hd in range(N_HEADS):
        cols = slice(hd * HEAD_DIM, (hd + 1) * HEAD_DIM)
        w_tril = jnp.where(r_i >= c_i, ws_ref[hd], 0.0).astype(BF16)
        bias = bst_ref[:, hd:hd + 1]
        for c in range(tile // GM_CHUNK):
            rows = slice(c * GM_CHUNK, (c + 1) * GM_CHUNK)
            mixed = _dot(w_tril, vnb[rows, cols]) + bias
            mix_scr[rows, RET_WIDTH + hd * HEAD_DIM:RET_WIDTH + (hd + 1) * HEAD_DIM] = (
                uf[rows, cols] * mixed).astype(BF16)

    x2 = x + _dot(mix_scr[...], wout_ref[...])
    x2_ref[...] = x2
    x2_carry[...] = x2

    @pl.when(t == tiles_per_seq - 1)
    def _():
        st_ref[0] = s_scr[...]


def _decode_kernel(x_ref, s0_ref, g1_ref, win_ref, cq_ref, sq_ref, qdec_ref, kdec_ref,
                   sdec_ref, lng_ref, lnb_ref, ws_ref, bst_ref, wout_ref, g2_ref, wr_ref, br_ref,
                   tri_ref, upper_ref,
                   x2_ref, h2_ref, lp_ref, tw_ref, ct_ref, of_ref, wp_ref, st_ref, vn_ref,
                   mix_scr, q_scr, k_scr, v_scr, oc_scr, st_stage, st_sem, *, n_streams, frames):
    tile = n_streams * frames
    x = x_ref[...]
    h = _rmsnorm(x, g1_ref[...]).astype(BF16)

    def proj(i):
        return _dot(h, win_ref[:, i * RET_WIDTH:(i + 1) * RET_WIDTH])

    q = proj(0)
    k = proj(1)
    v = proj(2)
    gate = proj(3)
    u = proj(4)
    vg = proj(5)
    cq = cq_ref[...]
    sq = sq_ref[...]
    ck = cq * K_SCALE
    sk = sq * K_SCALE

    r_i = lax.broadcasted_iota(I32, (tile, tile), 0)
    c_i = lax.broadcasted_iota(I32, (tile, tile), 1)
    shift = frames.bit_length() - 1
    r_frame = lax.bitwise_and(r_i, frames - 1)
    c_frame = lax.bitwise_and(c_i, frames - 1)
    keep = (lax.shift_right_logical(r_i, shift) == lax.shift_right_logical(c_i, shift)) & (r_frame >= c_frame)
    frame_gap = (r_frame - c_frame).astype(F32)

    o_in = []
    for hd in range(N_HEADS):
        cols = slice(hd * HEAD_DIM, (hd + 1) * HEAD_DIM)
        qr = _rotary(q[:, cols], cq, sq)
        kr = _rotary(k[:, cols], ck, sk)
        vh = v[:, cols].astype(BF16)
        qb = qr.astype(BF16)
        decay = jnp.where(keep, jnp.exp(frame_gap * _LOG_GAMMA[hd]), 0.0)
        scores = _dot_nt(qb, kr.astype(BF16)) * decay
        o_in.append(_dot(scores.astype(BF16), vh))
        q_scr[hd] = qb
        k_scr[hd] = (kr * kdec_ref[hd]).astype(BF16)
        v_scr[hd] = vh

    def state_copy(b):
        slot = lax.rem(b, 2)
        return pltpu.make_async_copy(st_stage.at[slot], st_ref.at[b], st_sem.at[slot])

    def stream_body(b, carry):
        rows = pl.ds(pl.multiple_of(b * frames, frames), frames)
        slot = lax.rem(b, 2)

        @pl.when(b >= 2)
        def _():
            state_copy(b - 2).wait()

        for hd in range(N_HEADS):
            state = s0_ref[b, hd]
            oc_scr[hd, rows, :] = _dot(q_scr[hd, rows, :], state.astype(BF16))
            st_stage[slot, hd] = sdec_ref[hd] * state + _dot_tn(k_scr[hd, rows, :], v_scr[hd, rows, :])
        state_copy(b).start()
        return carry

    lax.fori_loop(0, n_streams, stream_body, 0)
    state_copy(n_streams - 2).wait()
    state_copy(n_streams - 1).wait()

    for hd in range(N_HEADS):
        cols = slice(hd * HEAD_DIM, (hd + 1) * HEAD_DIM)
        o = o_in[hd] + qdec_ref[hd] * oc_scr[hd]
        mix_scr[:, cols] = _gated_head_norm(o, gate[:, cols])

    uf, vn = _gmlp_inputs(u, vg, lng_ref[...], lnb_ref[...])
    vn_ref[...] = vn
    vnb = vn.astype(BF16)
    sel = (lax.bitwise_and(lax.broadcasted_iota(I32, (tile, frames), 0), frames - 1)
           == lax.broadcasted_iota(I32, (tile, frames), 1)).astype(BF16)
    for hd in range(N_HEADS):
        cols = slice(hd * HEAD_DIM, (hd + 1) * HEAD_DIM)
        w_rows = _dot(sel, ws_ref[hd].astype(BF16)).astype(BF16)
        w_blk = jnp.where(keep, _dot_nt(w_rows, sel), 0.0).astype(BF16)
        mixed = _dot(w_blk, vnb[:, cols]) + bst_ref[:, hd:hd + 1]
        mix_scr[:, RET_WIDTH + hd * HEAD_DIM:RET_WIDTH + (hd + 1) * HEAD_DIM] = (uf[:, cols] * mixed).astype(BF16)

    _route_and_store(x, mix_scr, wout_ref, g2_ref, wr_ref, br_ref, tri_ref, upper_ref,
                     x2_ref, h2_ref, lp_ref, tw_ref, ct_ref, of_ref, wp_ref)


def _rope_tables(pos):
    half = HEAD_DIM // 2
    inv = np.float32(ROPE_BASE) ** (-np.arange(half, dtype=np.float32) / np.float32(half))
    ang = pos.astype(np.float32)[:, None] * inv[None, :]
    cos = np.cos(ang).astype(np.float32)
    sin = np.sin(ang).astype(np.float32)
    return np.concatenate([cos, cos], axis=-1), np.concatenate([-sin, sin], axis=-1)


def _decay_tables(block):
    log_g = np.asarray(_LOG_GAMMA, np.float32)
    idx = np.arange(block, dtype=np.float32)
    diff = idx[:, None] - idx[None, :]
    dmat = np.where(diff[None] >= 0, np.exp(np.maximum(diff, 0.0)[None] * log_g[:, None, None]), 0.0)
    q_dec = np.exp((idx + 1.0)[None, :] * log_g[:, None])
    k_dec = np.exp((block - 1.0 - idx)[None, :] * log_g[:, None])
    s_dec = np.exp(np.float32(block) * log_g)
    bc = lambda a: np.ascontiguousarray(
        np.broadcast_to(a[:, :, None], (N_HEADS, block, HEAD_DIM)).astype(np.float32))
    s_row = np.ascontiguousarray(np.broadcast_to(s_dec[:, None, None], (N_HEADS, 1, HEAD_DIM)).astype(np.float32))
    return dmat.astype(np.float32), bc(q_dec), bc(k_dec), s_row


def _routing_constants(tile):
    tri = jnp.asarray(np.arange(tile)[:, None] > np.arange(tile)[None, :], BF16)
    upper = jnp.asarray(np.arange(N_EXPERTS)[:, None] < np.arange(N_EXPERTS)[None, :], BF16)
    return tri, upper


def _front_out(n_tok, n_tiles):
    return [
        jax.ShapeDtypeStruct((n_tok, D_MODEL), F32),
        jax.ShapeDtypeStruct((n_tok * ROW_CHUNKS, LANES), U32),
        jax.ShapeDtypeStruct((n_tok, TOP_K), I32),
        jax.ShapeDtypeStruct((n_tok, TOP_K), F32),
        jax.ShapeDtypeStruct((n_tiles, 1, N_EXPERTS), F32),
        jax.ShapeDtypeStruct((n_tiles, 1, N_EXPERTS), F32),
        jax.ShapeDtypeStruct((n_tiles, 1, TILE_SLOTS), I32),
    ]


def _prompt_call(x, p):
    bsz, seq, _ = x.shape
    tile = TOKEN_TILE
    n_t = seq // tile
    cq, sq = _rope_tables(np.arange(seq))
    dmat, qdec, kdec, sdec = _decay_tables(PROMPT_RET_BLOCK)
    tri, upper = _routing_constants(tile)
    ws = p['w_s'][:, :GM_CHUNK, :GM_CHUNK]
    bst = p['b_s'][:, :GM_CHUNK].T

    n_tiles = bsz * n_t
    cur = lambda s: jnp.minimum(s, n_tiles - 1)
    prev = lambda s: jnp.maximum(s - 1, 0)
    const = lambda shape: pl.BlockSpec(shape, lambda s: (0,) * len(shape))
    pos = lambda: pl.BlockSpec((tile, HEAD_DIM), lambda s: (lax.rem(cur(s), n_t), 0))
    in_specs = [
        pl.BlockSpec((1, tile, D_MODEL), lambda s: (cur(s) // n_t, lax.rem(cur(s), n_t), 0)),
        const((1, D_MODEL)), const((D_MODEL, 6 * RET_WIDTH)),
        pos(), pos(),
        const((N_HEADS, PROMPT_RET_BLOCK, PROMPT_RET_BLOCK)),
        const((N_HEADS, PROMPT_RET_BLOCK, HEAD_DIM)), const((N_HEADS, PROMPT_RET_BLOCK, HEAD_DIM)),
        const((N_HEADS, 1, HEAD_DIM)),
        const((1, GM_WIDTH)), const((1, GM_WIDTH)),
        const((N_HEADS, GM_CHUNK, GM_CHUNK)), const((GM_CHUNK, N_HEADS)),
        const((D_MODEL, D_MODEL)), const((1, D_MODEL)),
        const((D_MODEL, N_EXPERTS)), const((1, N_EXPERTS)),
        const((tile, tile)), const((N_EXPERTS, N_EXPERTS)),
    ]
    out_specs = [
        pl.BlockSpec((tile, D_MODEL), lambda s: (cur(s), 0)),
        pl.BlockSpec((tile * ROW_CHUNKS, LANES), lambda s: (prev(s), 0)),
        pl.BlockSpec((tile, TOP_K), lambda s: (prev(s), 0)),
        pl.BlockSpec((tile, TOP_K), lambda s: (prev(s), 0)),
        pl.BlockSpec((1, 1, N_EXPERTS), lambda s: (prev(s), 0, 0)),
        pl.BlockSpec((1, 1, N_EXPERTS), lambda s: (prev(s), 0, 0)),
        pl.BlockSpec((1, 1, TILE_SLOTS), lambda s: (prev(s), 0, 0)),
        pl.BlockSpec((1, N_HEADS, HEAD_DIM, HEAD_DIM), lambda s: (cur(s) // n_t, 0, 0, 0)),
    ]
    out_shape = _front_out(bsz * seq, n_tiles) + [jax.ShapeDtypeStruct((bsz, N_HEADS, HEAD_DIM, HEAD_DIM), F32)]
    return pl.pallas_call(
        functools.partial(_prompt_kernel, ret_block=PROMPT_RET_BLOCK, tiles_per_seq=n_t, n_tiles=n_tiles),
        grid=(n_tiles + 1,),
        in_specs=in_specs,
        out_specs=out_specs,
        out_shape=out_shape,
        scratch_shapes=[
            pltpu.VMEM((N_HEADS, HEAD_DIM, HEAD_DIM), F32),
            pltpu.VMEM((tile, D_MODEL), BF16),
            pltpu.VMEM((tile, D_MODEL), F32),
        ],
        compiler_params=pltpu.CompilerParams(
            dimension_semantics=("arbitrary",), vmem_limit_bytes=VMEM_LIMIT),
        name="front_prompt",
    )(x, p['norm1_g'], p['w_in'], cq, sq, dmat, qdec, kdec, sdec, p['ln_v_g'], p['ln_v_b'], ws, bst,
      p['w_out'], p['norm2_g'], p['w_router'], p['b_router'], tri, upper)


def _decode_call(x, state, past_len, p):
    n_streams, frames, _ = x.shape
    tile = n_streams * frames
    assert tile == TOKEN_TILE and frames <= RET_CHUNK and frames & (frames - 1) == 0
    cq, sq = [np.tile(a, (n_streams, 1)) for a in _rope_tables(past_len + np.arange(frames))]
    _, qdec, kdec, sdec = _decay_tables(frames)
    qdec = np.tile(qdec, (1, n_streams, 1))
    kdec = np.tile(kdec, (1, n_streams, 1))
    tri, upper = _routing_constants(tile)
    ws = p['w_s'][:, :frames, :frames]
    bst = jnp.tile(p['b_s'][:, :frames].T, (n_streams, 1))

    def whole(a):
        return pl.BlockSpec(a.shape, lambda i, n=a.ndim: (0,) * n, pipeline_mode=pl.Buffered(1))

    args = (x.reshape(tile, D_MODEL), state, p['norm1_g'], p['w_in'], cq, sq, qdec, kdec, sdec,
            p['ln_v_g'], p['ln_v_b'], ws, bst, p['w_out'], p['norm2_g'], p['w_router'], p['b_router'],
            tri, upper)
    out_shape = _front_out(tile, 1) + [
        jax.ShapeDtypeStruct(state.shape, F32),
        jax.ShapeDtypeStruct((tile, GM_WIDTH), F32),
    ]
    out_specs = [pl.BlockSpec(s.shape, lambda i, n=len(s.shape): (0,) * n) for s in out_shape]
    out_specs[7] = pl.BlockSpec(memory_space=pl.ANY)
    return pl.pallas_call(
        functools.partial(_decode_kernel, n_streams=n_streams, frames=frames),
        grid=(1,),
        in_specs=[whole(a) for a in args],
        out_specs=out_specs,
        out_shape=out_shape,
        scratch_shapes=[
            pltpu.VMEM((tile, D_MODEL), BF16),
            pltpu.VMEM((N_HEADS, tile, HEAD_DIM), BF16),
            pltpu.VMEM((N_HEADS, tile, HEAD_DIM), BF16),
            pltpu.VMEM((N_HEADS, tile, HEAD_DIM), BF16),
            pltpu.VMEM((N_HEADS, tile, HEAD_DIM), F32),
            pltpu.VMEM((2, N_HEADS, HEAD_DIM, HEAD_DIM), F32),
            pltpu.SemaphoreType.DMA((2,)),
        ],
        compiler_params=pltpu.CompilerParams(
            dimension_semantics=("arbitrary",), vmem_limit_bytes=VMEM_LIMIT),
        name="front_decode",
    )(*args)


def _plan_kernel(ct_ref, lower_ref, upper_ref, cum_ref, nb_ref, bs_ref, be_ref, nbt_ref, tlo_ref, thi_ref,
                 *, n_blocks):
    counts = ct_ref[...]
    cum = _exact_count_dot(lower_ref[...], counts, counts_on_left=False)
    cum_ref[...] = cum.astype(I32)
    n_tiles = counts.shape[0]
    total = cum[n_tiles:n_tiles + 1]
    nb = jnp.floor((total + (EXPERT_ROWS - 1)) * (1.0 / EXPERT_ROWS))
    bstart = _dot(jnp.broadcast_to(nb, (8, N_EXPERTS)).astype(BF16), upper_ref[...])[0:1]
    bend = bstart + nb
    nb_ref[...] = nb.astype(I32)
    bs_ref[...] = bstart.astype(I32)
    blk = lax.broadcasted_iota(I32, (n_blocks, N_EXPERTS), 0).astype(F32)
    be = jnp.minimum(jnp.sum((bend <= blk).astype(F32), axis=-1, keepdims=True), N_EXPERTS - 1.0)
    be_ref[...] = be.astype(I32)
    nbt_ref[...] = jnp.sum(nb, axis=-1, keepdims=True).astype(I32)
    mine = lax.broadcasted_iota(I32, (n_blocks, N_EXPERTS), 1).astype(F32) == be
    pick = lambda row: jnp.sum(jnp.where(mine, row, 0.0), axis=-1, keepdims=True)
    r0 = (blk[:, 0:1] - pick(bstart)) * EXPERT_ROWS
    tlo = jnp.zeros((n_blocks, 1), F32)
    thi = jnp.zeros((n_blocks, 1), F32)
    seg_lo = pick(cum[0:1])
    for i in range(n_tiles):
        seg_hi = pick(cum[i + 1:i + 2])
        tlo = tlo + (seg_hi <= r0).astype(F32)
        thi = thi + (seg_lo < r0 + EXPERT_ROWS).astype(F32)
        seg_lo = seg_hi
    tlo_ref[...] = tlo.astype(I32)
    thi_ref[...] = thi.astype(I32)


def _plan_call(counts, n_blocks):
    n_tiles = counts.shape[0]
    lower = (jnp.arange(n_tiles + 1)[:, None] > jnp.arange(n_tiles)[None, :]).astype(BF16)
    upper = (jnp.arange(N_EXPERTS)[:, None] < jnp.arange(N_EXPERTS)[None, :]).astype(BF16)
    return pl.pallas_call(
        functools.partial(_plan_kernel, n_blocks=n_blocks),
        out_shape=[
            jax.ShapeDtypeStruct((n_tiles + 1, N_EXPERTS), I32),
            jax.ShapeDtypeStruct((1, N_EXPERTS), I32),
            jax.ShapeDtypeStruct((1, N_EXPERTS), I32),
            jax.ShapeDtypeStruct((n_blocks, 1), I32),
            jax.ShapeDtypeStruct((1, 1), I32),
            jax.ShapeDtypeStruct((n_blocks, 1), I32),
            jax.ShapeDtypeStruct((n_blocks, 1), I32),
        ],
        name="plan",
    )(counts, lower, upper)


def _dispatch_kernel(h2p_ref, h2s_ref, wp_ref, out_ref, *, n_prompt_tiles):
    i = pl.program_id(0)

    def permute(src_ref):
        def body(t, c):
            row = src_ref[_row_tile(t), :]
            for kk in range(TOP_K):
                first = pl.multiple_of(wp_ref[0, 0, kk * TOKEN_TILE + t], ROW_CHUNKS)
                out_ref[pl.ds(first, ROW_CHUNKS), :] = row
            return c

        lax.fori_loop(0, TOKEN_TILE, body, 0, unroll=8)

    @pl.when(i < n_prompt_tiles)
    def _():
        permute(h2p_ref)

    @pl.when(i >= n_prompt_tiles)
    def _():
        permute(h2s_ref)


def _dispatch_call(h2_p, h2_s, word_pos):
    rows = TOKEN_TILE * ROW_CHUNKS
    n_p_tiles = h2_p.shape[0] // rows
    n_s_tiles = h2_s.shape[0] // rows
    n_tiles = n_p_tiles + n_s_tiles
    return pl.pallas_call(
        functools.partial(_dispatch_kernel, n_prompt_tiles=n_p_tiles),
        grid=(n_tiles,),
        in_specs=[
            pl.BlockSpec((rows, LANES), lambda i: (jnp.minimum(i, n_p_tiles - 1), 0)),
            pl.BlockSpec((rows, LANES), lambda i: (jnp.maximum(i - n_p_tiles, 0), 0)),
            pl.BlockSpec((1, 1, TILE_SLOTS), lambda i: (i, 0, 0), memory_space=pltpu.SMEM),
        ],
        out_specs=pl.BlockSpec((TILE_SLOTS * ROW_CHUNKS, LANES), lambda i: (i, 0)),
        out_shape=jax.ShapeDtypeStruct((n_tiles * TILE_SLOTS * ROW_CHUNKS, LANES), U32),
        compiler_params=pltpu.CompilerParams(
            dimension_semantics=("arbitrary",), vmem_limit_bytes=VMEM_LIMIT),
        name="dispatch",
    )(h2_p, h2_s, word_pos)


def _expert_kernel(be_ref, bs_ref, nb_ref, nbt_ref, cum_ref, off_ref, tlo_ref, thi_ref,
                   xb_ref, wg_ref, wu_ref, wd_ref, bg_ref, bu_ref, bd_ref,
                   yb_ref, xbuf, ybuf, wbf, in_sem, out_sem, *, n_tiles):
    b = pl.program_id(0)
    nbt = nbt_ref[0]

    def block_rows(blk):
        e = be_ref[blk]
        r0 = (blk - bs_ref[e]) * EXPERT_ROWS
        return r0, jnp.minimum(cum_ref[n_tiles * N_EXPERTS + e] - r0, EXPERT_ROWS)

    def segments(blk, slot, fn):
        e = be_ref[blk]
        r0 = (blk - bs_ref[e]) * EXPERT_ROWS

        def body(i, c):
            seg_lo = cum_ref[i * N_EXPERTS + e]
            seg_hi = cum_ref[(i + 1) * N_EXPERTS + e]
            lo = jnp.maximum(seg_lo, r0)
            hi = jnp.minimum(seg_hi, r0 + EXPERT_ROWS)

            @pl.when(hi > lo)
            def _():
                bucket_row = i * TILE_SLOTS + off_ref[i * N_EXPERTS + e] + (lo - seg_lo)
                fn(slot, bucket_row, slot * EXPERT_ROWS + (lo - r0), hi - lo)
            return c

        lax.fori_loop(tlo_ref[blk], thi_ref[blk], body, 0)

    def words(row, n):
        return pl.ds(pl.multiple_of(row * ROW_CHUNKS, ROW_CHUNKS), n * ROW_CHUNKS)

    def gather(slot, bucket_row, buffer_row, n):
        return pltpu.make_async_copy(
            xb_ref.at[words(bucket_row, n)], xbuf.at[words(buffer_row, n)], in_sem.at[slot])

    def scatter(slot, bucket_row, buffer_row, n):
        return pltpu.make_async_copy(
            ybuf.at[words(buffer_row, n)], yb_ref.at[words(bucket_row, n)], out_sem.at[slot])

    def start_gather(blk):
        @pl.when(blk < nbt)
        def _():
            segments(blk, lax.rem(blk, 2), lambda *a: gather(*a).start())

    def wait_gather(blk):
        @pl.when(blk < nbt)
        def _():
            slot = lax.rem(blk, 2)
            gather(slot, 0, slot * EXPERT_ROWS, block_rows(blk)[1]).wait()

    def start_scatter(blk):
        @pl.when(blk < nbt)
        def _():
            segments(blk, lax.rem(blk, 2), lambda *a: scatter(*a).start())

    def wait_scatter(blk):
        @pl.when((blk >= 0) & (blk < nbt))
        def _():
            slot = lax.rem(blk, 2)
            scatter(slot, 0, slot * EXPERT_ROWS, block_rows(blk)[1]).wait()

    def mlp(base, n_rows, rows_left):
        valid = lax.broadcasted_iota(I32, (n_rows, 1), 0) < rows_left
        x = jnp.where(valid, _load_rows(xbuf, base, n_rows), 0.0)
        gt = _dot(x, wbf[0]) + bg_ref[0]
        up = _dot(x, wbf[1]) + bu_ref[0]
        gt = jnp.minimum(gt, SWIGLU_LIMIT)
        up = jnp.clip(up, -SWIGLU_LIMIT, SWIGLU_LIMIT)
        act = gt * _sigmoid(gt * SWIGLU_ALPHA) * (up + 1.0)
        _store_rows(ybuf, base, _dot(act.astype(BF16), wbf[2]) + bd_ref[0])

    def block_body(blk, carry):
        wait_gather(blk)
        start_gather(blk + 1)
        wait_scatter(blk - 2)
        rows_left = block_rows(blk)[1]
        base = lax.rem(blk, 2) * EXPERT_ROWS

        lower = 0
        for upper, passes in EXPERT_PASS_PLANS:
            @pl.when((rows_left > lower) & (rows_left <= upper))
            def _(passes=passes):
                first = 0
                for n_rows in passes:
                    mlp(base + first, n_rows, rows_left - first)
                    first += n_rows
            lower = upper

        start_scatter(blk)
        return carry

    first_block = bs_ref[b]
    n_mine = nb_ref[b]

    @pl.when(b == 0)
    def _():
        xbuf[...] = jnp.zeros_like(xbuf)
        start_gather(0)

    @pl.when(n_mine > 0)
    def _():
        wbf[0] = wg_ref[0].astype(BF16)
        wbf[1] = wu_ref[0].astype(BF16)
        wbf[2] = wd_ref[0].astype(BF16)

    lax.fori_loop(first_block, first_block + n_mine, block_body, 0)

    @pl.when(b == N_EXPERTS - 1)
    def _():
        wait_scatter(nbt - 2)
        wait_scatter(nbt - 1)


def _expert_call(be, bstart, nb, nbt, cum_flat, off_flat, tlo, thi, buckets, p, n_tiles):
    wspec = pl.BlockSpec((1, D_MODEL, D_MODEL), lambda e, *_: (e, 0, 0))
    bspec = pl.BlockSpec((1, 1, D_MODEL), lambda e, *_: (e, 0, 0))
    any_spec = pl.BlockSpec(memory_space=pl.ANY)
    return pl.pallas_call(
        functools.partial(_expert_kernel, n_tiles=n_tiles),
        grid_spec=pltpu.PrefetchScalarGridSpec(
            num_scalar_prefetch=8,
            grid=(N_EXPERTS,),
            in_specs=[any_spec, wspec, wspec, wspec, bspec, bspec, bspec],
            out_specs=any_spec,
            scratch_shapes=[
                pltpu.VMEM((2 * EXPERT_ROWS * ROW_CHUNKS, LANES), U32),
                pltpu.VMEM((2 * EXPERT_ROWS * ROW_CHUNKS, LANES), U32),
                pltpu.VMEM((3, D_MODEL, D_MODEL), BF16),
                pltpu.SemaphoreType.DMA((2,)),
                pltpu.SemaphoreType.DMA((2,)),
            ],
        ),
        out_shape=jax.ShapeDtypeStruct(buckets.shape, U32),
        compiler_params=pltpu.CompilerParams(
            dimension_semantics=("arbitrary",), vmem_limit_bytes=VMEM_LIMIT),
        name="experts",
    )(be, bstart, nb, nbt, cum_flat, off_flat, tlo, thi, buckets, p['w_gate'], p['w_up'], p['w_down'],
      p['b_gate'][:, None, :], p['b_up'][:, None, :], p['b_down'][:, None, :])


def _combine_kernel(x2_ref, yb_ref, lp_ref, tw_ref, gf_ref, y_ref, ybf, pick):
    for c in range(TILE_SLOTS // COMBINE_CHUNK):
        rows = slice(c * COMBINE_CHUNK, (c + 1) * COMBINE_CHUNK)
        ybf[rows, :] = _load_rows(yb_ref, c * COMBINE_CHUNK, COMBINE_CHUNK)
    slot = lax.broadcasted_iota(I32, (TOKEN_TILE, TILE_SLOTS), 1)
    lp = lp_ref[...]
    tw = tw_ref[...]
    sel = jnp.zeros((TOKEN_TILE, TILE_SLOTS), F32)
    for kk in range(TOP_K):
        sel = jnp.where(slot == lp[:, kk:kk + 1], tw[:, kk:kk + 1], sel)
    pick[...] = sel.astype(BF16)
    y_ref[...] = _rmsnorm(x2_ref[...] + _dot(pick[...], ybf[...]), gf_ref[...])


def _combine_call(x2, ybuckets, lp, tw, gf, tile_off):
    n_tok = x2.shape[0]
    n_tiles = n_tok // TOKEN_TILE
    return pl.pallas_call(
        _combine_kernel,
        grid=(n_tiles,),
        in_specs=[
            pl.BlockSpec((TOKEN_TILE, D_MODEL), lambda i: (i, 0)),
            pl.BlockSpec((TILE_SLOTS * ROW_CHUNKS, LANES), lambda i: (tile_off + i, 0)),
            pl.BlockSpec((TOKEN_TILE, TOP_K), lambda i: (i, 0)),
            pl.BlockSpec((TOKEN_TILE, TOP_K), lambda i: (i, 0)),
            pl.BlockSpec((1, D_MODEL), lambda i: (0, 0)),
        ],
        out_specs=pl.BlockSpec((TOKEN_TILE, D_MODEL), lambda i: (i, 0)),
        scratch_shapes=[
            pltpu.VMEM((TILE_SLOTS, D_MODEL), BF16),
            pltpu.VMEM((TOKEN_TILE, TILE_SLOTS), BF16),
        ],
        out_shape=jax.ShapeDtypeStruct((n_tok, D_MODEL), F32),
        compiler_params=pltpu.CompilerParams(
            dimension_semantics=("arbitrary",), vmem_limit_bytes=VMEM_LIMIT),
        name="combine",
    )(x2, ybuckets, lp, tw, gf)


def kernel(x_prompt, x_sample, state_ret, norm1_g, w_in, ln_v_g, ln_v_b, w_s, b_s, w_out, norm2_g, w_router, b_router, w_gate, b_gate, w_up, b_up, w_down, b_down, norm_f_g):
    bsz, seq, _ = x_prompt.shape
    dbsz, dseq, _ = x_sample.shape
    n_p = bsz * seq
    n_s = dbsz * dseq
    n_total = n_p + n_s
    past_len = 2048
    assert w_in.shape[0] == 1, "single layer"

    p = dict(norm1_g=norm1_g, w_in=w_in[0].astype(BF16), ln_v_g=ln_v_g, ln_v_b=ln_v_b, w_s=w_s[0], b_s=b_s[0],
             w_out=w_out[0].astype(BF16), norm2_g=norm2_g, w_router=w_router[0].astype(BF16), b_router=b_router,
             w_gate=w_gate[0], b_gate=b_gate[0], w_up=w_up[0], b_up=b_up[0], w_down=w_down[0], b_down=b_down[0])

    x2_p, h2_p, lp_p, tw_p, ct_p, of_p, wp_p, st_p = _prompt_call(x_prompt, p)
    x2_s, h2_s, lp_s, tw_s, ct_s, of_s, wp_s, st_s, vn_s = _decode_call(x_sample, state_ret[0], past_len, p)

    n_tiles = n_total // TOKEN_TILE
    word_pos = jnp.concatenate([wp_p, wp_s], axis=0)
    counts = jnp.concatenate([ct_p, ct_s], axis=0).reshape(n_tiles, N_EXPERTS)
    offsets = jnp.concatenate([of_p, of_s], axis=0).reshape(n_tiles * N_EXPERTS).astype(I32)

    n_blocks = -(-(n_total * TOP_K + N_EXPERTS * (EXPERT_ROWS - 1)) // EXPERT_ROWS)
    cum, nb, bstart, be, nbt, tlo, thi = _plan_call(counts, n_blocks)

    buckets = _dispatch_call(h2_p, h2_s, word_pos)
    ybuckets = _expert_call(be.reshape(n_blocks), bstart.reshape(N_EXPERTS), nb.reshape(N_EXPERTS), nbt.reshape(1),
                            cum.reshape((n_tiles + 1) * N_EXPERTS), offsets, tlo.reshape(n_blocks),
                            thi.reshape(n_blocks), buckets, p, n_tiles)

    y_p = _combine_call(x2_p, ybuckets, lp_p, tw_p, norm_f_g[None, :], 0)
    y_s = _combine_call(x2_s, ybuckets, lp_s, tw_s, norm_f_g[None, :], n_p // TOKEN_TILE)

    return (y_p.reshape(bsz, seq, D_MODEL), y_s.reshape(dbsz, dseq, D_MODEL),
            st_p[None], st_s[None], vn_s.reshape(1, dbsz, dseq, GM_WIDTH))
```

```python
import functools

import numpy as np
import jax
import jax.numpy as jnp
from jax import lax
from jax.experimental import pallas as pl
from jax.experimental.pallas import tpu as pltpu

F32 = jnp.float32
BF16 = jnp.bfloat16
I32 = jnp.int32
U32 = jnp.uint32

D_MODEL = 1024
RET_WIDTH = 512
N_HEADS = 4
HEAD_DIM = 128
GM_WIDTH = 512
GM_CHUNK = 128
RET_CHUNK = 64
N_EXPERTS = 32
TOP_K = 4
SWIGLU_LIMIT = 7.0
SWIGLU_ALPHA = 1.702
ROPE_BASE = 10000.0
EPS = 1e-6

TOKEN_TILE = 512
PROMPT_RET_BLOCK = 256
EXPERT_ROWS = 1024
EXPERT_PASS_PLANS = ((256, (256,)), (512, (512,)), (768, (512, 256)), (1024, (512, 512)))
TILE_SLOTS = TOKEN_TILE * TOP_K
COMBINE_CHUNK = 256
VMEM_LIMIT = 56 * 1024 * 1024
LANES = 128
ROW_CHUNKS = D_MODEL // LANES // 2
K_SCALE = HEAD_DIM ** -0.5
SPLIT = 16.0
_LOG_GAMMA = [float(np.log1p(-np.float32(2.0) ** np.float32(-5.0 - hd)).astype(np.float32)) for hd in range(N_HEADS)]


def _load_rows(ref, first_row, n_rows):
    base = first_row * ROW_CHUNKS
    lo, hi = [], []
    for c in range(ROW_CHUNKS):
        w = ref[pl.ds(base + c, n_rows, stride=ROW_CHUNKS), :]
        lo.append(pltpu.bitcast(lax.shift_left(w, jnp.uint32(16)), F32))
        hi.append(pltpu.bitcast(lax.bitwise_and(w, jnp.uint32(0xFFFF0000)), F32))
    return jnp.concatenate(lo + hi, axis=1).astype(BF16)


def _store_rows(ref, first_row, val):
    base = first_row * ROW_CHUNKS
    bits = pltpu.bitcast(val.astype(BF16).astype(F32), U32)
    half = D_MODEL // 2
    for c in range(ROW_CHUNKS):
        low = lax.shift_right_logical(bits[:, c * LANES:(c + 1) * LANES], jnp.uint32(16))
        ref[pl.ds(base + c, val.shape[0], stride=ROW_CHUNKS), :] = lax.bitwise_or(
            bits[:, half + c * LANES:half + (c + 1) * LANES], low)


def _row_tile(row):
    return pl.ds(pl.multiple_of(row * ROW_CHUNKS, ROW_CHUNKS), ROW_CHUNKS)


def _rmsnorm(x, g):
    ms = jnp.mean(x * x, axis=-1, keepdims=True)
    return x * lax.rsqrt(ms + EPS) * g


def _gelu(x):
    c = np.float32(np.sqrt(2.0 / np.pi))
    return x * (0.5 * (1.0 + jnp.tanh(c * (x + 0.044715 * (x * x * x)))))


def _sigmoid(x):
    return 1.0 / (1.0 + jnp.exp(-x))


def _dot(a, b):
    return jnp.dot(a, b, preferred_element_type=F32)


def _dot_nt(a, b):
    return lax.dot_general(a, b, (((1,), (1,)), ((), ())), preferred_element_type=F32)


def _dot_tn(a, b):
    return lax.dot_general(a, b, (((0,), (0,)), ((), ())), preferred_element_type=F32)


def _exact_count_dot(a, b, *, counts_on_left):
    cnt = a if counts_on_left else b
    hi = jnp.floor(cnt * (1.0 / SPLIT))
    lo = cnt - SPLIT * hi
    if counts_on_left:
        return SPLIT * _dot(hi.astype(BF16), b) + _dot(lo.astype(BF16), b)
    return SPLIT * _dot(a, hi.astype(BF16)) + _dot(a, lo.astype(BF16))


def _rotary(xh, cos2, sin2):
    return xh * cos2 + pltpu.roll(xh, HEAD_DIM // 2, axis=1) * sin2


def _gated_head_norm(o, gate):
    mu = jnp.mean(o, axis=-1, keepdims=True)
    oc = o - mu
    var = jnp.mean(oc * oc, axis=-1, keepdims=True)
    return ((gate * _sigmoid(gate)) * (oc * lax.rsqrt(var + EPS))).astype(BF16)


def _gmlp_inputs(u, vg, lng, lnb):
    uf = _gelu(u)
    vf = _gelu(vg)
    mu = jnp.mean(vf, axis=-1, keepdims=True)
    vc = vf - mu
    var = jnp.mean(vc * vc, axis=-1, keepdims=True)
    return uf, vc * lax.rsqrt(var + EPS) * lng + lnb


def _route_and_store(x, mix_scr, wout_ref, g2_ref, wr_ref, br_ref, tri_ref, upper_ref,
                     x2_ref, h2_ref, lp_ref, tw_ref, ct_ref, of_ref, wp_ref):
    h2 = _out_proj(x, mix_scr, wout_ref, g2_ref, x2_ref, h2_ref)
    _route(h2, wr_ref, br_ref, tri_ref, upper_ref, lp_ref, tw_ref, ct_ref, of_ref, wp_ref)


def _out_proj(x, mix_scr, wout_ref, g2_ref, x2_ref, h2_ref):
    x2 = x + _dot(mix_scr[...], wout_ref[...])
    x2_ref[...] = x2
    h2 = _rmsnorm(x2, g2_ref[...])
    _store_rows(h2_ref, 0, h2)
    return h2


def _route(h2, wr_ref, br_ref, tri_ref, upper_ref, lp_ref, tw_ref, ct_ref, of_ref, wp_ref):
    _route_place(_route_choose(h2, wr_ref, br_ref), tri_ref, upper_ref, lp_ref, tw_ref, ct_ref, of_ref, wp_ref)


def _route_choose(h2, wr_ref, br_ref):
    tile = h2.shape[0]
    logits = _dot(h2.astype(BF16), wr_ref[...]) + br_ref[...]

    iota_e = lax.broadcasted_iota(I32, (tile, N_EXPERTS), 1)
    iota_ef = iota_e.astype(F32)
    lg = logits
    vals = []
    idxs = []
    for _ in range(TOP_K):
        m = jnp.max(lg, axis=-1, keepdims=True)
        idx = jnp.min(jnp.where(lg == m, iota_ef, float(N_EXPERTS)), axis=-1, keepdims=True).astype(I32)
        vals.append(m)
        idxs.append(idx)
        lg = jnp.where(iota_e == idx, -jnp.inf, lg)
    exps = [jnp.exp(vk - vals[0]) for vk in vals]
    denom = exps[0] + exps[1] + exps[2] + exps[3]
    return idxs, exps, denom


def _route_place(choice, tri_ref, upper_ref, lp_ref, tw_ref, ct_ref, of_ref, wp_ref):
    idxs, exps, denom = choice
    tile = denom.shape[0]
    iota_e = lax.broadcasted_iota(I32, (tile, N_EXPERTS), 1)
    iota_k = lax.broadcasted_iota(I32, (tile, TOP_K), 1)
    onehot = jnp.zeros((tile, N_EXPERTS), F32)
    for idx in idxs:
        onehot = onehot + (iota_e == idx).astype(F32)
    rank = _dot(tri_ref[...], onehot.astype(BF16))
    count = jnp.sum(onehot, axis=0, keepdims=True)
    offset = _exact_count_dot(jnp.broadcast_to(count, (8, N_EXPERTS)), upper_ref[...], counts_on_left=True)[0:1]
    where_to = rank + offset
    lp = jnp.zeros((tile, TOP_K), I32)
    tw = jnp.zeros((tile, TOP_K), F32)
    word_rows = []
    for kk in range(TOP_K):
        pos = jnp.sum(jnp.where(iota_e == idxs[kk], where_to, 0.0), axis=-1, keepdims=True)
        lp = jnp.where(iota_k == kk, pos.astype(I32), lp)
        tw = jnp.where(iota_k == kk, exps[kk] / denom, tw)
        word_rows.append(jnp.transpose(jnp.broadcast_to(pos * float(ROW_CHUNKS), (tile, LANES)))[0:1, :])
    lp_ref[...] = lp
    tw_ref[...] = tw
    ct_ref[0] = count
    of_ref[0] = offset
    wp_ref[0] = jnp.concatenate(word_rows, axis=1).astype(I32)


def _prompt_kernel(x_ref, g1_ref, win_ref, cq_ref, sq_ref, dmat_ref, qdec_ref, kdec_ref,
                   sdec_ref, lng_ref, lnb_ref, ws_ref, bst_ref, wout_ref, g2_ref, wr_ref, br_ref,
                   tri_ref, upper_ref,
                   x2_ref, h2_ref, lp_ref, tw_ref, ct_ref, of_ref, wp_ref, st_ref,
                   s_scr, mix_scr, x2_carry, *, ret_block, tiles_per_seq, n_tiles):
    step = pl.program_id(0)

    @pl.when(step == 0)
    def _():
        s_scr[...] = jnp.zeros_like(s_scr)
        x2_carry[...] = jnp.zeros_like(x2_carry)

    def norm_prev_tile():
        h2_prev = _rmsnorm(x2_carry[...], g2_ref[...])
        _store_rows(h2_ref, 0, h2_prev)
        return _route_choose(h2_prev, wr_ref, br_ref)

    def place_prev_tile(choice):
        _route_place(choice, tri_ref, upper_ref, lp_ref, tw_ref, ct_ref, of_ref, wp_ref)

    @pl.when(step == n_tiles)
    def _():
        place_prev_tile(norm_prev_tile())

    @pl.when(step < n_tiles)
    def _():
        _prompt_step(step, norm_prev_tile, place_prev_tile, x_ref, g1_ref, win_ref, cq_ref, sq_ref,
                     dmat_ref, qdec_ref, kdec_ref, sdec_ref, lng_ref, lnb_ref, ws_ref, bst_ref, wout_ref,
                     x2_ref, st_ref, s_scr, mix_scr, x2_carry, ret_block=ret_block, tiles_per_seq=tiles_per_seq)


def _prompt_step(step, norm_prev_tile, place_prev_tile, x_ref, g1_ref, win_ref, cq_ref, sq_ref,
                 dmat_ref, qdec_ref, kdec_ref, sdec_ref, lng_ref, lnb_ref, ws_ref, bst_ref, wout_ref,
                 x2_ref, st_ref, s_scr, mix_scr, x2_carry, *, ret_block, tiles_per_seq):
    tile = TOKEN_TILE
    t = lax.rem(step, tiles_per_seq)

    choice = norm_prev_tile()
    s_scr[...] = jnp.where(t == 0, 0.0, s_scr[...])

    x = x_ref[0]
    h = _rmsnorm(x, g1_ref[...]).astype(BF16)

    def proj(i):
        return _dot(h, win_ref[:, i * RET_WIDTH:(i + 1) * RET_WIDTH])

    q = proj(0)
    k = proj(1)
    v = proj(2)
    gate = proj(3)
    u = proj(4)
    vg = proj(5)
    place_prev_tile(choice)
    cq = cq_ref[...]
    sq = sq_ref[...]
    ck = cq * K_SCALE
    sk = sq * K_SCALE

    for hd in range(N_HEADS):
        cols = slice(hd * HEAD_DIM, (hd + 1) * HEAD_DIM)
        qr = _rotary(q[:, cols], cq, sq)
        kr = _rotary(k[:, cols], ck, sk)
        vh = v[:, cols].astype(BF16)
        dm = dmat_ref[hd]
        qd = qdec_ref[hd]
        kd = kdec_ref[hd]
        sd = sdec_ref[hd]
        for c in range(tile // ret_block):
            rows = slice(c * ret_block, (c + 1) * ret_block)
            qb = qr[rows].astype(BF16)
            kb = kr[rows]
            vb = vh[rows]
            state = s_scr[hd]
            scores = _dot_nt(qb, kb.astype(BF16)) * dm
            o = _dot(scores.astype(BF16), vb) + qd * _dot(qb, state.astype(BF16))
            s_scr[hd] = sd * state + _dot_tn((kb * kd).astype(BF16), vb)
            mix_scr[rows, cols] = _gated_head_norm(o, gate[rows, cols])

    uf, vn = _gmlp_inputs(u, vg, lng_ref[...], lnb_ref[...])
    vnb = vn.astype(BF16)
    r_i = lax.broadcasted_iota(I32, (GM_CHUNK, GM_CHUNK), 0)
    c_i = lax.broadcasted_iota(I32, (GM_CHUNK, GM_CHUNK), 1)
    for hd in range(N_HEADS):
        cols = slice(hd * HEAD_DIM, (hd + 1) * HEAD_DIM)
        w_tril = jnp.where(r_i >= c_i, ws_ref[hd], 0.0).astype(BF16)
        bias = bst_ref[:, hd:hd + 1]
        for c in range(tile // GM_CHUNK):
            rows = slice(c * GM_CHUNK, (c + 1) * GM_CHUNK)
            mixed = _dot(w_tril, vnb[rows, cols]) + bias
            mix_scr[rows, RET_WIDTH + hd * HEAD_DIM:RET_WIDTH + (hd + 1) * HEAD_DIM] = (
                uf[rows, cols] * mixed).astype(BF16)

    x2 = x + _dot(mix_scr[...], wout_ref[...])
    x2_ref[...] = x2
    x2_carry[...] = x2

    @pl.when(t == tiles_per_seq - 1)
    def _():
        st_ref[0] = s_scr[...]


def _decode_kernel(x_ref, s0_ref, g1_ref, win_ref, cq_ref, sq_ref, qdec_ref, kdec_ref,
                   sdec_ref, lng_ref, lnb_ref, ws_ref, bst_ref, wout_ref, g2_ref, wr_ref, br_ref,
                   tri_ref, upper_ref,
                   x2_ref, h2_ref, lp_ref, tw_ref, ct_ref, of_ref, wp_ref, st_ref, vn_ref,
                   mix_scr, q_scr, k_scr, v_scr, oc_scr, st_stage, st_sem, *, n_streams, frames):
    tile = n_streams * frames
    x = x_ref[...]
    h = _rmsnorm(x, g1_ref[...]).astype(BF16)

    def proj(i):
        return _dot(h, win_ref[:, i * RET_WIDTH:(i + 1) * RET_WIDTH])

    q = proj(0)
    k = proj(1)
    v = proj(2)
    gate = proj(3)
    u = proj(4)
    vg = proj(5)
    cq = cq_ref[...]
    sq = sq_ref[...]
    ck = cq * K_SCALE
    sk = sq * K_SCALE

    r_i = lax.broadcasted_iota(I32, (tile, tile), 0)
    c_i = lax.broadcasted_iota(I32, (tile, tile), 1)
    shift = frames.bit_length() - 1
    r_frame = lax.bitwise_and(r_i, frames - 1)
    c_frame = lax.bitwise_and(c_i, frames - 1)
    keep = (lax.shift_right_logical(r_i, shift) == lax.shift_right_logical(c_i, shift)) & (r_frame >= c_frame)
    frame_gap = (r_frame - c_frame).astype(F32)

    o_in = []
    for hd in range(N_HEADS):
        cols = slice(hd * HEAD_DIM, (hd + 1) * HEAD_DIM)
        qr = _rotary(q[:, cols], cq, sq)
        kr = _rotary(k[:, cols], ck, sk)
        vh = v[:, cols].astype(BF16)
        qb = qr.astype(BF16)
        decay = jnp.where(keep, jnp.exp(frame_gap * _LOG_GAMMA[hd]), 0.0)
        scores = _dot_nt(qb, kr.astype(BF16)) * decay
        o_in.append(_dot(scores.astype(BF16), vh))
        q_scr[hd] = qb
        k_scr[hd] = (kr * kdec_ref[hd]).astype(BF16)
        v_scr[hd] = vh

    def state_copy(b):
        slot = lax.rem(b, 2)
        return pltpu.make_async_copy(st_stage.at[slot], st_ref.at[b], st_sem.at[slot])

    def stream_body(b, carry):
        rows = pl.ds(pl.multiple_of(b * frames, frames), frames)
        slot = lax.rem(b, 2)

        @pl.when(b >= 2)
        def _():
            state_copy(b - 2).wait()

        for hd in range(N_HEADS):
            state = s0_ref[b, hd]
            oc_scr[hd, rows, :] = _dot(q_scr[hd, rows, :], state.astype(BF16))
            st_stage[slot, hd] = sdec_ref[hd] * state + _dot_tn(k_scr[hd, rows, :], v_scr[hd, rows, :])
        state_copy(b).start()
        return carry

    lax.fori_loop(0, n_streams, stream_body, 0)
    state_copy(n_streams - 2).wait()
    state_copy(n_streams - 1).wait()

    for hd in range(N_HEADS):
        cols = slice(hd * HEAD_DIM, (hd + 1) * HEAD_DIM)
        o = o_in[hd] + qdec_ref[hd] * oc_scr[hd]
        mix_scr[:, cols] = _gated_head_norm(o, gate[:, cols])

    uf, vn = _gmlp_inputs(u, vg, lng_ref[...], lnb_ref[...])
    vn_ref[...] = vn
    vnb = vn.astype(BF16)
    sel = (lax.bitwise_and(lax.broadcasted_iota(I32, (tile, frames), 0), frames - 1)
           == lax.broadcasted_iota(I32, (tile, frames), 1)).astype(BF16)
    for hd in range(N_HEADS):
        cols = slice(hd * HEAD_DIM, (hd + 1) * HEAD_DIM)
        w_rows = _dot(sel, ws_ref[hd].astype(BF16)).astype(BF16)
        w_blk = jnp.where(keep, _dot_nt(w_rows, sel), 0.0).astype(BF16)
        mixed = _dot(w_blk, vnb[:, cols]) + bst_ref[:, hd:hd + 1]
        mix_scr[:, RET_WIDTH + hd * HEAD_DIM:RET_WIDTH + (hd + 1) * HEAD_DIM] = (uf[:, cols] * mixed).astype(BF16)

    _route_and_store(x, mix_scr, wout_ref, g2_ref, wr_ref, br_ref, tri_ref, upper_ref,
                     x2_ref, h2_ref, lp_ref, tw_ref, ct_ref, of_ref, wp_ref)


def _rope_tables(pos):
    half = HEAD_DIM // 2
    inv = np.float32(ROPE_BASE) ** (-np.arange(half, dtype=np.float32) / np.float32(half))
    ang = pos.astype(np.float32)[:, None] * inv[None, :]
    cos = np.cos(ang).astype(np.float32)
    sin = np.sin(ang).astype(np.float32)
    return np.concatenate([cos, cos], axis=-1), np.concatenate([-sin, sin], axis=-1)


def _decay_tables(block):
    log_g = np.asarray(_LOG_GAMMA, np.float32)
    idx = np.arange(block, dtype=np.float32)
    diff = idx[:, None] - idx[None, :]
    dmat = np.where(diff[None] >= 0, np.exp(np.maximum(diff, 0.0)[None] * log_g[:, None, None]), 0.0)
    q_dec = np.exp((idx + 1.0)[None, :] * log_g[:, None])
    k_dec = np.exp((block - 1.0 - idx)[None, :] * log_g[:, None])
    s_dec = np.exp(np.float32(block) * log_g)
    bc = lambda a: np.ascontiguousarray(
        np.broadcast_to(a[:, :, None], (N_HEADS, block, HEAD_DIM)).astype(np.float32))
    s_row = np.ascontiguousarray(np.broadcast_to(s_dec[:, None, None], (N_HEADS, 1, HEAD_DIM)).astype(np.float32))
    return dmat.astype(np.float32), bc(q_dec), bc(k_dec), s_row


def _routing_constants(tile):
    tri = jnp.asarray(np.arange(tile)[:, None] > np.arange(tile)[None, :], BF16)
    upper = jnp.asarray(np.arange(N_EXPERTS)[:, None] < np.arange(N_EXPERTS)[None, :], BF16)
    return tri, upper


def _front_out(n_tok, n_tiles):
    return [
        jax.ShapeDtypeStruct((n_tok, D_MODEL), F32),
        jax.ShapeDtypeStruct((n_tok * ROW_CHUNKS, LANES), U32),
        jax.ShapeDtypeStruct((n_tok, TOP_K), I32),
        jax.ShapeDtypeStruct((n_tok, TOP_K), F32),
        jax.ShapeDtypeStruct((n_tiles, 1, N_EXPERTS), F32),
        jax.ShapeDtypeStruct((n_tiles, 1, N_EXPERTS), F32),
        jax.ShapeDtypeStruct((n_tiles, 1, TILE_SLOTS), I32),
    ]


def _prompt_call(x, p):
    bsz, seq, _ = x.shape
    tile = TOKEN_TILE
    n_t = seq // tile
    cq, sq = _rope_tables(np.arange(seq))
    dmat, qdec, kdec, sdec = _decay_tables(PROMPT_RET_BLOCK)
    tri, upper = _routing_constants(tile)
    ws = p['w_s'][:, :GM_CHUNK, :GM_CHUNK]
    bst = p['b_s'][:, :GM_CHUNK].T

    n_tiles = bsz * n_t
    cur = lambda s: jnp.minimum(s, n_tiles - 1)
    prev = lambda s: jnp.maximum(s - 1, 0)
    const = lambda shape: pl.BlockSpec(shape, lambda s: (0,) * len(shape))
    pos = lambda: pl.BlockSpec((tile, HEAD_DIM), lambda s: (lax.rem(cur(s), n_t), 0))
    in_specs = [
        pl.BlockSpec((1, tile, D_MODEL), lambda s: (cur(s) // n_t, lax.rem(cur(s), n_t), 0)),
        const((1, D_MODEL)), const((D_MODEL, 6 * RET_WIDTH)),
        pos(), pos(),
        const((N_HEADS, PROMPT_RET_BLOCK, PROMPT_RET_BLOCK)),
        const((N_HEADS, PROMPT_RET_BLOCK, HEAD_DIM)), const((N_HEADS, PROMPT_RET_BLOCK, HEAD_DIM)),
        const((N_HEADS, 1, HEAD_DIM)),
        const((1, GM_WIDTH)), const((1, GM_WIDTH)),
        const((N_HEADS, GM_CHUNK, GM_CHUNK)), const((GM_CHUNK, N_HEADS)),
        const((D_MODEL, D_MODEL)), const((1, D_MODEL)),
        const((D_MODEL, N_EXPERTS)), const((1, N_EXPERTS)),
        const((tile, tile)), const((N_EXPERTS, N_EXPERTS)),
    ]
    out_specs = [
        pl.BlockSpec((tile, D_MODEL), lambda s: (cur(s), 0)),
        pl.BlockSpec((tile * ROW_CHUNKS, LANES), lambda s: (prev(s), 0)),
        pl.BlockSpec((tile, TOP_K), lambda s: (prev(s), 0)),
        pl.BlockSpec((tile, TOP_K), lambda s: (prev(s), 0)),
        pl.BlockSpec((1, 1, N_EXPERTS), lambda s: (prev(s), 0, 0)),
        pl.BlockSpec((1, 1, N_EXPERTS), lambda s: (prev(s), 0, 0)),
        pl.BlockSpec((1, 1, TILE_SLOTS), lambda s: (prev(s), 0, 0)),
        pl.BlockSpec((1, N_HEADS, HEAD_DIM, HEAD_DIM), lambda s: (cur(s) // n_t, 0, 0, 0)),
    ]
    out_shape = _front_out(bsz * seq, n_tiles) + [jax.ShapeDtypeStruct((bsz, N_HEADS, HEAD_DIM, HEAD_DIM), F32)]
    return pl.pallas_call(
        functools.partial(_prompt_kernel, ret_block=PROMPT_RET_BLOCK, tiles_per_seq=n_t, n_tiles=n_tiles),
        grid=(n_tiles + 1,),
        in_specs=in_specs,
        out_specs=out_specs,
        out_shape=out_shape,
        scratch_shapes=[
            pltpu.VMEM((N_HEADS, HEAD_DIM, HEAD_DIM), F32),
            pltpu.VMEM((tile, D_MODEL), BF16),
            pltpu.VMEM((tile, D_MODEL), F32),
        ],
        compiler_params=pltpu.CompilerParams(
            dimension_semantics=("arbitrary",), vmem_limit_bytes=VMEM_LIMIT),
        name="front_prompt",
    )(x, p['norm1_g'], p['w_in'], cq, sq, dmat, qdec, kdec, sdec, p['ln_v_g'], p['ln_v_b'], ws, bst,
      p['w_out'], p['norm2_g'], p['w_router'], p['b_router'], tri, upper)


def _decode_call(x, state, past_len, p):
    n_streams, frames, _ = x.shape
    tile = n_streams * frames
    assert tile == TOKEN_TILE and frames <= RET_CHUNK and frames & (frames - 1) == 0
    cq, sq = [np.tile(a, (n_streams, 1)) for a in _rope_tables(past_len + np.arange(frames))]
    _, qdec, kdec, sdec = _decay_tables(frames)
    qdec = np.tile(qdec, (1, n_streams, 1))
    kdec = np.tile(kdec, (1, n_streams, 1))
    tri, upper = _routing_constants(tile)
    ws = p['w_s'][:, :frames, :frames]
    bst = jnp.tile(p['b_s'][:, :frames].T, (n_streams, 1))

    def whole(a):
        return pl.BlockSpec(a.shape, lambda i, n=a.ndim: (0,) * n, pipeline_mode=pl.Buffered(1))

    args = (x.reshape(tile, D_MODEL), state, p['norm1_g'], p['w_in'], cq, sq, qdec, kdec, sdec,
            p['ln_v_g'], p['ln_v_b'], ws, bst, p['w_out'], p['norm2_g'], p['w_router'], p['b_router'],
            tri, upper)
    out_shape = _front_out(tile, 1) + [
        jax.ShapeDtypeStruct(state.shape, F32),
        jax.ShapeDtypeStruct((tile, GM_WIDTH), F32),
    ]
    out_specs = [pl.BlockSpec(s.shape, lambda i, n=len(s.shape): (0,) * n) for s in out_shape]
    out_specs[7] = pl.BlockSpec(memory_space=pl.ANY)
    return pl.pallas_call(
        functools.partial(_decode_kernel, n_streams=n_streams, frames=frames),
        grid=(1,),
        in_specs=[whole(a) for a in args],
        out_specs=out_specs,
        out_shape=out_shape,
        scratch_shapes=[
            pltpu.VMEM((tile, D_MODEL), BF16),
            pltpu.VMEM((N_HEADS, tile, HEAD_DIM), BF16),
            pltpu.VMEM((N_HEADS, tile, HEAD_DIM), BF16),
            pltpu.VMEM((N_HEADS, tile, HEAD_DIM), BF16),
            pltpu.VMEM((N_HEADS, tile, HEAD_DIM), F32),
            pltpu.VMEM((2, N_HEADS, HEAD_DIM, HEAD_DIM), F32),
            pltpu.SemaphoreType.DMA((2,)),
        ],
        compiler_params=pltpu.CompilerParams(
            dimension_semantics=("arbitrary",), vmem_limit_bytes=VMEM_LIMIT),
        name="front_decode",
    )(*args)


def _plan_kernel(ct_ref, lower_ref, upper_ref, cum_ref, nb_ref, bs_ref, be_ref, nbt_ref, tlo_ref, thi_ref,
                 *, n_blocks):
    counts = ct_ref[...]
    cum = _exact_count_dot(lower_ref[...], counts, counts_on_left=False)
    cum_ref[...] = cum.astype(I32)
    n_tiles = counts.shape[0]
    total = cum[n_tiles:n_tiles + 1]
    nb = jnp.floor((total + (EXPERT_ROWS - 1)) * (1.0 / EXPERT_ROWS))
    bstart = _dot(jnp.broadcast_to(nb, (8, N_EXPERTS)).astype(BF16), upper_ref[...])[0:1]
    bend = bstart + nb
    nb_ref[...] = nb.astype(I32)
    bs_ref[...] = bstart.astype(I32)
    blk = lax.broadcasted_iota(I32, (n_blocks, N_EXPERTS), 0).astype(F32)
    be = jnp.minimum(jnp.sum((bend <= blk).astype(F32), axis=-1, keepdims=True), N_EXPERTS - 1.0)
    be_ref[...] = be.astype(I32)
    nbt_ref[...] = jnp.sum(nb, axis=-1, keepdims=True).astype(I32)
    mine = lax.broadcasted_iota(I32, (n_blocks, N_EXPERTS), 1).astype(F32) == be
    pick = lambda row: jnp.sum(jnp.where(mine, row, 0.0), axis=-1, keepdims=True)
    r0 = (blk[:, 0:1] - pick(bstart)) * EXPERT_ROWS
    tlo = jnp.zeros((n_blocks, 1), F32)
    thi = jnp.zeros((n_blocks, 1), F32)
    seg_lo = pick(cum[0:1])
    for i in range(n_tiles):
        seg_hi = pick(cum[i + 1:i + 2])
        tlo = tlo + (seg_hi <= r0).astype(F32)
        thi = thi + (seg_lo < r0 + EXPERT_ROWS).astype(F32)
        seg_lo = seg_hi
    tlo_ref[...] = tlo.astype(I32)
    thi_ref[...] = thi.astype(I32)


def _plan_call(counts, n_blocks):
    n_tiles = counts.shape[0]
    lower = (jnp.arange(n_tiles + 1)[:, None] > jnp.arange(n_tiles)[None, :]).astype(BF16)
    upper = (jnp.arange(N_EXPERTS)[:, None] < jnp.arange(N_EXPERTS)[None, :]).astype(BF16)
    return pl.pallas_call(
        functools.partial(_plan_kernel, n_blocks=n_blocks),
        out_shape=[
            jax.ShapeDtypeStruct((n_tiles + 1, N_EXPERTS), I32),
            jax.ShapeDtypeStruct((1, N_EXPERTS), I32),
            jax.ShapeDtypeStruct((1, N_EXPERTS), I32),
            jax.ShapeDtypeStruct((n_blocks, 1), I32),
            jax.ShapeDtypeStruct((1, 1), I32),
            jax.ShapeDtypeStruct((n_blocks, 1), I32),
            jax.ShapeDtypeStruct((n_blocks, 1), I32),
        ],
        name="plan",
    )(counts, lower, upper)


def _dispatch_kernel(h2p_ref, h2s_ref, wp_ref, out_ref, *, n_prompt_tiles):
    i = pl.program_id(0)

    def permute(src_ref):
        def body(t, c):
            row = src_ref[_row_tile(t), :]
            for kk in range(TOP_K):
                first = pl.multiple_of(wp_ref[0, 0, kk * TOKEN_TILE + t], ROW_CHUNKS)
                out_ref[pl.ds(first, ROW_CHUNKS), :] = row
            return c

        lax.fori_loop(0, TOKEN_TILE, body, 0, unroll=32)

    @pl.when(i < n_prompt_tiles)
    def _():
        permute(h2p_ref)

    @pl.when(i >= n_prompt_tiles)
    def _():
        permute(h2s_ref)


def _dispatch_call(h2_p, h2_s, word_pos):
    rows = TOKEN_TILE * ROW_CHUNKS
    n_p_tiles = h2_p.shape[0] // rows
    n_s_tiles = h2_s.shape[0] // rows
    n_tiles = n_p_tiles + n_s_tiles
    return pl.pallas_call(
        functools.partial(_dispatch_kernel, n_prompt_tiles=n_p_tiles),
        grid=(n_tiles,),
        in_specs=[
            pl.BlockSpec((rows, LANES), lambda i: (jnp.minimum(i, n_p_tiles - 1), 0)),
            pl.BlockSpec((rows, LANES), lambda i: (jnp.maximum(i - n_p_tiles, 0), 0)),
            pl.BlockSpec((1, 1, TILE_SLOTS), lambda i: (i, 0, 0), memory_space=pltpu.SMEM),
        ],
        out_specs=pl.BlockSpec((TILE_SLOTS * ROW_CHUNKS, LANES), lambda i: (i, 0)),
        out_shape=jax.ShapeDtypeStruct((n_tiles * TILE_SLOTS * ROW_CHUNKS, LANES), U32),
        compiler_params=pltpu.CompilerParams(
            dimension_semantics=("arbitrary",), vmem_limit_bytes=VMEM_LIMIT),
        name="dispatch",
    )(h2_p, h2_s, word_pos)


def _expert_kernel(be_ref, bs_ref, nb_ref, nbt_ref, cum_ref, off_ref, tlo_ref, thi_ref,
                   xb_ref, wg_ref, wu_ref, wd_ref, bg_ref, bu_ref, bd_ref,
                   yb_ref, xbuf, ybuf, wbf, in_sem, out_sem, *, n_tiles):
    b = pl.program_id(0)
    nbt = nbt_ref[0]

    def block_rows(blk):
        e = be_ref[blk]
        r0 = (blk - bs_ref[e]) * EXPERT_ROWS
        return r0, jnp.minimum(cum_ref[n_tiles * N_EXPERTS + e] - r0, EXPERT_ROWS)

    def segments(blk, slot, fn):
        e = be_ref[blk]
        r0 = (blk - bs_ref[e]) * EXPERT_ROWS

        def body(i, c):
            seg_lo = cum_ref[i * N_EXPERTS + e]
            seg_hi = cum_ref[(i + 1) * N_EXPERTS + e]
            lo = jnp.maximum(seg_lo, r0)
            hi = jnp.minimum(seg_hi, r0 + EXPERT_ROWS)

            @pl.when(hi > lo)
            def _():
                bucket_row = i * TILE_SLOTS + off_ref[i * N_EXPERTS + e] + (lo - seg_lo)
                fn(slot, bucket_row, slot * EXPERT_ROWS + (lo - r0), hi - lo)
            return c

        lax.fori_loop(tlo_ref[blk], thi_ref[blk], body, 0)

    def words(row, n):
        return pl.ds(pl.multiple_of(row * ROW_CHUNKS, ROW_CHUNKS), n * ROW_CHUNKS)

    def gather(slot, bucket_row, buffer_row, n):
        return pltpu.make_async_copy(
            xb_ref.at[words(bucket_row, n)], xbuf.at[words(buffer_row, n)], in_sem.at[slot])

    def scatter(slot, bucket_row, buffer_row, n):
        return pltpu.make_async_copy(
            ybuf.at[words(buffer_row, n)], yb_ref.at[words(bucket_row, n)], out_sem.at[slot])

    def start_gather(blk):
        @pl.when(blk < nbt)
        def _():
            segments(blk, lax.rem(blk, 2), lambda *a: gather(*a).start())

    def wait_gather(blk):
        @pl.when(blk < nbt)
        def _():
            slot = lax.rem(blk, 2)
            gather(slot, 0, slot * EXPERT_ROWS, block_rows(blk)[1]).wait()

    def start_scatter(blk):
        @pl.when(blk < nbt)
        def _():
            segments(blk, lax.rem(blk, 2), lambda *a: scatter(*a).start())

    def wait_scatter(blk):
        @pl.when((blk >= 0) & (blk < nbt))
        def _():
            slot = lax.rem(blk, 2)
            scatter(slot, 0, slot * EXPERT_ROWS, block_rows(blk)[1]).wait()

    def mlp(base, n_rows, rows_left):
        valid = lax.broadcasted_iota(I32, (n_rows, 1), 0) < rows_left
        x = jnp.where(valid, _load_rows(xbuf, base, n_rows), 0.0)
        gt = _dot(x, wbf[0]) + bg_ref[0]
        up = _dot(x, wbf[1]) + bu_ref[0]
        gt = jnp.minimum(gt, SWIGLU_LIMIT)
        up = jnp.clip(up, -SWIGLU_LIMIT, SWIGLU_LIMIT)
        act = gt * _sigmoid(gt * SWIGLU_ALPHA) * (up + 1.0)
        _store_rows(ybuf, base, _dot(act.astype(BF16), wbf[2]) + bd_ref[0])

    def block_body(blk, carry):
        wait_gather(blk)
        start_gather(blk + 1)
        wait_scatter(blk - 2)
        rows_left = block_rows(blk)[1]
        base = lax.rem(blk, 2) * EXPERT_ROWS

        lower = 0
        for upper, passes in EXPERT_PASS_PLANS:
            @pl.when((rows_left > lower) & (rows_left <= upper))
            def _(passes=passes):
                first = 0
                for n_rows in passes:
                    mlp(base + first, n_rows, rows_left - first)
                    first += n_rows
            lower = upper

        start_scatter(blk)
        return carry

    first_block = bs_ref[b]
    n_mine = nb_ref[b]

    @pl.when(b == 0)
    def _():
        xbuf[...] = jnp.zeros_like(xbuf)
        start_gather(0)

    @pl.when(n_mine > 0)
    def _():
        wbf[0] = wg_ref[0].astype(BF16)
        wbf[1] = wu_ref[0].astype(BF16)
        wbf[2] = wd_ref[0].astype(BF16)

    lax.fori_loop(first_block, first_block + n_mine, block_body, 0)

    @pl.when(b == N_EXPERTS - 1)
    def _():
        wait_scatter(nbt - 2)
        wait_scatter(nbt - 1)


def _expert_call(be, bstart, nb, nbt, cum_flat, off_flat, tlo, thi, buckets, p, n_tiles):
    wspec = pl.BlockSpec((1, D_MODEL, D_MODEL), lambda e, *_: (e, 0, 0))
    bspec = pl.BlockSpec((1, 1, D_MODEL), lambda e, *_: (e, 0, 0))
    any_spec = pl.BlockSpec(memory_space=pl.ANY)
    return pl.pallas_call(
        functools.partial(_expert_kernel, n_tiles=n_tiles),
        grid_spec=pltpu.PrefetchScalarGridSpec(
            num_scalar_prefetch=8,
            grid=(N_EXPERTS,),
            in_specs=[any_spec, wspec, wspec, wspec, bspec, bspec, bspec],
            out_specs=any_spec,
            scratch_shapes=[
                pltpu.VMEM((2 * EXPERT_ROWS * ROW_CHUNKS, LANES), U32),
                pltpu.VMEM((2 * EXPERT_ROWS * ROW_CHUNKS, LANES), U32),
                pltpu.VMEM((3, D_MODEL, D_MODEL), BF16),
                pltpu.SemaphoreType.DMA((2,)),
                pltpu.SemaphoreType.DMA((2,)),
            ],
        ),
        out_shape=jax.ShapeDtypeStruct(buckets.shape, U32),
        compiler_params=pltpu.CompilerParams(
            dimension_semantics=("arbitrary",), vmem_limit_bytes=VMEM_LIMIT),
        name="experts",
    )(be, bstart, nb, nbt, cum_flat, off_flat, tlo, thi, buckets, p['w_gate'], p['w_up'], p['w_down'],
      p['b_gate'][:, None, :], p['b_up'][:, None, :], p['b_down'][:, None, :])


def _combine_kernel(x2_ref, yb_ref, lp_ref, tw_ref, gf_ref, y_ref, ybf, pick):
    for c in range(TILE_SLOTS // COMBINE_CHUNK):
        rows = slice(c * COMBINE_CHUNK, (c + 1) * COMBINE_CHUNK)
        ybf[rows, :] = _load_rows(yb_ref, c * COMBINE_CHUNK, COMBINE_CHUNK)
    slot = lax.broadcasted_iota(I32, (TOKEN_TILE, TILE_SLOTS), 1)
    lp = lp_ref[...]
    tw = tw_ref[...]
    sel = jnp.zeros((TOKEN_TILE, TILE_SLOTS), F32)
    for kk in range(TOP_K):
        sel = jnp.where(slot == lp[:, kk:kk + 1], tw[:, kk:kk + 1], sel)
    pick[...] = sel.astype(BF16)
    y_ref[...] = _rmsnorm(x2_ref[...] + _dot(pick[...], ybf[...]), gf_ref[...])


def _combine_call(x2, ybuckets, lp, tw, gf, tile_off):
    n_tok = x2.shape[0]
    n_tiles = n_tok // TOKEN_TILE
    return pl.pallas_call(
        _combine_kernel,
        grid=(n_tiles,),
        in_specs=[
            pl.BlockSpec((TOKEN_TILE, D_MODEL), lambda i: (i, 0)),
            pl.BlockSpec((TILE_SLOTS * ROW_CHUNKS, LANES), lambda i: (tile_off + i, 0)),
            pl.BlockSpec((TOKEN_TILE, TOP_K), lambda i: (i, 0)),
            pl.BlockSpec((TOKEN_TILE, TOP_K), lambda i: (i, 0)),
            pl.BlockSpec((1, D_MODEL), lambda i: (0, 0)),
        ],
        out_specs=pl.BlockSpec((TOKEN_TILE, D_MODEL), lambda i: (i, 0)),
        scratch_shapes=[
            pltpu.VMEM((TILE_SLOTS, D_MODEL), BF16),
            pltpu.VMEM((TOKEN_TILE, TILE_SLOTS), BF16),
        ],
        out_shape=jax.ShapeDtypeStruct((n_tok, D_MODEL), F32),
        compiler_params=pltpu.CompilerParams(
            dimension_semantics=("arbitrary",), vmem_limit_bytes=VMEM_LIMIT),
        name="combine",
    )(x2, ybuckets, lp, tw, gf)


def kernel(x_prompt, x_sample, state_ret, norm1_g, w_in, ln_v_g, ln_v_b, w_s, b_s, w_out, norm2_g, w_router, b_router, w_gate, b_gate, w_up, b_up, w_down, b_down, norm_f_g):
    bsz, seq, _ = x_prompt.shape
    dbsz, dseq, _ = x_sample.shape
    n_p = bsz * seq
    n_s = dbsz * dseq
    n_total = n_p + n_s
    past_len = 2048
    assert w_in.shape[0] == 1, "single layer"

    p = dict(norm1_g=norm1_g, w_in=w_in[0].astype(BF16), ln_v_g=ln_v_g, ln_v_b=ln_v_b, w_s=w_s[0], b_s=b_s[0],
             w_out=w_out[0].astype(BF16), norm2_g=norm2_g, w_router=w_router[0].astype(BF16), b_router=b_router,
             w_gate=w_gate[0], b_gate=b_gate[0], w_up=w_up[0], b_up=b_up[0], w_down=w_down[0], b_down=b_down[0])

    x2_p, h2_p, lp_p, tw_p, ct_p, of_p, wp_p, st_p = _prompt_call(x_prompt, p)
    x2_s, h2_s, lp_s, tw_s, ct_s, of_s, wp_s, st_s, vn_s = _decode_call(x_sample, state_ret[0], past_len, p)

    n_tiles = n_total // TOKEN_TILE
    word_pos = jnp.concatenate([wp_p, wp_s], axis=0)
    counts = jnp.concatenate([ct_p, ct_s], axis=0).reshape(n_tiles, N_EXPERTS)
    offsets = jnp.concatenate([of_p, of_s], axis=0).reshape(n_tiles * N_EXPERTS).astype(I32)

    n_blocks = -(-(n_total * TOP_K + N_EXPERTS * (EXPERT_ROWS - 1)) // EXPERT_ROWS)
    cum, nb, bstart, be, nbt, tlo, thi = _plan_call(counts, n_blocks)

    buckets = _dispatch_call(h2_p, h2_s, word_pos)
    ybuckets = _expert_call(be.reshape(n_blocks), bstart.reshape(N_EXPERTS), nb.reshape(N_EXPERTS), nbt.reshape(1),
                            cum.reshape((n_tiles + 1) * N_EXPERTS), offsets, tlo.reshape(n_blocks),
                            thi.reshape(n_blocks), buckets, p, n_tiles)

    y_p = _combine_call(x2_p, ybuckets, lp_p, tw_p, norm_f_g[None, :], 0)
    y_s = _combine_call(x2_s, ybuckets, lp_s, tw_s, norm_f_g[None, :], n_p // TOKEN_TILE)

    return (y_p.reshape(bsz, seq, D_MODEL), y_s.reshape(dbsz, dseq, D_MODEL),
            st_p[None], st_s[None], vn_s.reshape(1, dbsz, dseq, GM_WIDTH))
```

```python
import functools

import numpy as np
import jax
import jax.numpy as jnp
from jax import lax
from jax.experimental import pallas as pl
from jax.experimental.pallas import tpu as pltpu

F32 = jnp.float32
BF16 = jnp.bfloat16
I32 = jnp.int32
U32 = jnp.uint32

D_MODEL = 1024
RET_WIDTH = 512
N_HEADS = 4
HEAD_DIM = 128
GM_WIDTH = 512
GM_CHUNK = 128
RET_CHUNK = 64
N_EXPERTS = 32
TOP_K = 4
SWIGLU_LIMIT = 7.0
SWIGLU_ALPHA = 1.702
ROPE_BASE = 10000.0
EPS = 1e-6

TOKEN_TILE = 512
PROMPT_RET_BLOCK = 256
EXPERT_ROWS = 1024
EXPERT_PASS_PLANS = ((256, (256,)), (512, (512,)), (768, (512, 256)), (1024, (512, 512)))
TILE_SLOTS = TOKEN_TILE * TOP_K
COMBINE_CHUNK = 256
VMEM_LIMIT = 56 * 1024 * 1024
LANES = 128
ROW_CHUNKS = D_MODEL // LANES // 2
K_SCALE = HEAD_DIM ** -0.5
SPLIT = 16.0
_LOG_GAMMA = [float(np.log1p(-np.float32(2.0) ** np.float32(-5.0 - hd)).astype(np.float32)) for hd in range(N_HEADS)]


def _load_rows(ref, first_row, n_rows):
    base = first_row * ROW_CHUNKS
    lo, hi = [], []
    for c in range(ROW_CHUNKS):
        w = ref[pl.ds(base + c, n_rows, stride=ROW_CHUNKS), :]
        lo.append(pltpu.bitcast(lax.shift_left(w, jnp.uint32(16)), F32))
        hi.append(pltpu.bitcast(lax.bitwise_and(w, jnp.uint32(0xFFFF0000)), F32))
    return jnp.concatenate(lo + hi, axis=1).astype(BF16)


def _store_rows(ref, first_row, val):
    base = first_row * ROW_CHUNKS
    bits = pltpu.bitcast(val.astype(BF16).astype(F32), U32)
    half = D_MODEL // 2
    for c in range(ROW_CHUNKS):
        low = lax.shift_right_logical(bits[:, c * LANES:(c + 1) * LANES], jnp.uint32(16))
        ref[pl.ds(base + c, val.shape[0], stride=ROW_CHUNKS), :] = lax.bitwise_or(
            bits[:, half + c * LANES:half + (c + 1) * LANES], low)


def _row_tile(row):
    return pl.ds(pl.multiple_of(row * ROW_CHUNKS, ROW_CHUNKS), ROW_CHUNKS)


def _rmsnorm(x, g):
    ms = jnp.mean(x * x, axis=-1, keepdims=True)
    return x * lax.rsqrt(ms + EPS) * g


def _gelu(x):
    c = np.float32(np.sqrt(2.0 / np.pi))
    return x * (0.5 * (1.0 + jnp.tanh(c * (x + 0.044715 * (x * x * x)))))


def _sigmoid(x):
    return 1.0 / (1.0 + jnp.exp(-x))


def _dot(a, b):
    return jnp.dot(a, b, preferred_element_type=F32)


def _dot_nt(a, b):
    return lax.dot_general(a, b, (((1,), (1,)), ((), ())), preferred_element_type=F32)


def _dot_tn(a, b):
    return lax.dot_general(a, b, (((0,), (0,)), ((), ())), preferred_element_type=F32)


def _exact_count_dot(a, b, *, counts_on_left):
    cnt = a if counts_on_left else b
    hi = jnp.floor(cnt * (1.0 / SPLIT))
    lo = cnt - SPLIT * hi
    if counts_on_left:
        return SPLIT * _dot(hi.astype(BF16), b) + _dot(lo.astype(BF16), b)
    return SPLIT * _dot(a, hi.astype(BF16)) + _dot(a, lo.astype(BF16))


def _rotary(xh, cos2, sin2):
    return xh * cos2 + pltpu.roll(xh, HEAD_DIM // 2, axis=1) * sin2


def _gated_head_norm(o, gate):
    mu = jnp.mean(o, axis=-1, keepdims=True)
    oc = o - mu
    var = jnp.mean(oc * oc, axis=-1, keepdims=True)
    return ((gate * _sigmoid(gate)) * (oc * lax.rsqrt(var + EPS))).astype(BF16)


def _gmlp_inputs(u, vg, lng, lnb):
    uf = _gelu(u)
    vf = _gelu(vg)
    mu = jnp.mean(vf, axis=-1, keepdims=True)
    vc = vf - mu
    var = jnp.mean(vc * vc, axis=-1, keepdims=True)
    return uf, vc * lax.rsqrt(var + EPS) * lng + lnb


def _route_and_store(x, mix_scr, wout_ref, g2_ref, wr_ref, br_ref, tri_ref, upper_ref,
                     x2_ref, h2_ref, lp_ref, tw_ref, ct_ref, of_ref, wp_ref):
    h2 = _out_proj(x, mix_scr, wout_ref, g2_ref, x2_ref, h2_ref)
    _route(h2, wr_ref, br_ref, tri_ref, upper_ref, lp_ref, tw_ref, ct_ref, of_ref, wp_ref)


def _out_proj(x, mix_scr, wout_ref, g2_ref, x2_ref, h2_ref):
    x2 = x + _dot(mix_scr[...], wout_ref[...])
    x2_ref[...] = x2
    h2 = _rmsnorm(x2, g2_ref[...])
    _store_rows(h2_ref, 0, h2)
    return h2


def _route(h2, wr_ref, br_ref, tri_ref, upper_ref, lp_ref, tw_ref, ct_ref, of_ref, wp_ref):
    _route_place(_route_choose(h2, wr_ref, br_ref), tri_ref, upper_ref, lp_ref, tw_ref, ct_ref, of_ref, wp_ref)


def _route_choose(h2, wr_ref, br_ref):
    tile = h2.shape[0]
    logits = _dot(h2.astype(BF16), wr_ref[...]) + br_ref[...]

    iota_e = lax.broadcasted_iota(I32, (tile, N_EXPERTS), 1)
    iota_ef = iota_e.astype(F32)
    lg = logits
    vals = []
    idxs = []
    for _ in range(TOP_K):
        m = jnp.max(lg, axis=-1, keepdims=True)
        idx = jnp.min(jnp.where(lg == m, iota_ef, float(N_EXPERTS)), axis=-1, keepdims=True).astype(I32)
        vals.append(m)
        idxs.append(idx)
        lg = jnp.where(iota_e == idx, -jnp.inf, lg)
    exps = [jnp.exp(vk - vals[0]) for vk in vals]
    denom = exps[0] + exps[1] + exps[2] + exps[3]
    return idxs, exps, denom


def _route_place(choice, tri_ref, upper_ref, lp_ref, tw_ref, ct_ref, of_ref, wp_ref):
    idxs, exps, denom = choice
    tile = denom.shape[0]
    iota_e = lax.broadcasted_iota(I32, (tile, N_EXPERTS), 1)
    iota_k = lax.broadcasted_iota(I32, (tile, TOP_K), 1)
    onehot = jnp.zeros((tile, N_EXPERTS), F32)
    for idx in idxs:
        onehot = onehot + (iota_e == idx).astype(F32)
    rank = _dot(tri_ref[...], onehot.astype(BF16))
    count = jnp.sum(onehot, axis=0, keepdims=True)
    offset = _exact_count_dot(jnp.broadcast_to(count, (8, N_EXPERTS)), upper_ref[...], counts_on_left=True)[0:1]
    where_to = rank + offset
    lp = jnp.zeros((tile, TOP_K), I32)
    tw = jnp.zeros((tile, TOP_K), F32)
    word_rows = []
    for kk in range(TOP_K):
        pos = jnp.sum(jnp.where(iota_e == idxs[kk], where_to, 0.0), axis=-1, keepdims=True)
        lp = jnp.where(iota_k == kk, pos.astype(I32), lp)
        tw = jnp.where(iota_k == kk, exps[kk] / denom, tw)
        word_rows.append(jnp.transpose(jnp.broadcast_to(pos * float(ROW_CHUNKS), (tile, LANES)))[0:1, :])
    lp_ref[...] = lp
    tw_ref[...] = tw
    ct_ref[0] = count
    of_ref[0] = offset
    wp_ref[0] = jnp.concatenate(word_rows, axis=1).astype(I32)


def _prompt_kernel(x_ref, g1_ref, win_ref, cq_ref, sq_ref, dmat_ref, qdec_ref, kdec_ref,
                   sdec_ref, lng_ref, lnb_ref, ws_ref, bst_ref, wout_ref, g2_ref, wr_ref, br_ref,
                   tri_ref, upper_ref,
                   x2_ref, h2_ref, lp_ref, tw_ref, ct_ref, of_ref, wp_ref, st_ref,
                   s_scr, mix_scr, x2_carry, *, ret_block, tiles_per_seq, n_tiles):
    step = pl.program_id(0)

    @pl.when(step == 0)
    def _():
        s_scr[...] = jnp.zeros_like(s_scr)
        x2_carry[...] = jnp.zeros_like(x2_carry)

    def norm_prev_tile():
        h2_prev = _rmsnorm(x2_carry[...], g2_ref[...])
        _store_rows(h2_ref, 0, h2_prev)
        return _route_choose(h2_prev, wr_ref, br_ref)

    def place_prev_tile(choice):
        _route_place(choice, tri_ref, upper_ref, lp_ref, tw_ref, ct_ref, of_ref, wp_ref)

    @pl.when(step == n_tiles)
    def _():
        place_prev_tile(norm_prev_tile())

    @pl.when(step < n_tiles)
    def _():
        _prompt_step(step, norm_prev_tile, place_prev_tile, x_ref, g1_ref, win_ref, cq_ref, sq_ref,
                     dmat_ref, qdec_ref, kdec_ref, sdec_ref, lng_ref, lnb_ref, ws_ref, bst_ref, wout_ref,
                     x2_ref, st_ref, s_scr, mix_scr, x2_carry, ret_block=ret_block, tiles_per_seq=tiles_per_seq)


def _prompt_step(step, norm_prev_tile, place_prev_tile, x_ref, g1_ref, win_ref, cq_ref, sq_ref,
                 dmat_ref, qdec_ref, kdec_ref, sdec_ref, lng_ref, lnb_ref, ws_ref, bst_ref, wout_ref,
                 x2_ref, st_ref, s_scr, mix_scr, x2_carry, *, ret_block, tiles_per_seq):
    tile = TOKEN_TILE
    t = lax.rem(step, tiles_per_seq)

    choice = norm_prev_tile()
    s_scr[...] = jnp.where(t == 0, 0.0, s_scr[...])

    x = x_ref[0]
    h = _rmsnorm(x, g1_ref[...]).astype(BF16)

    def proj(i):
        return _dot(h, win_ref[:, i * RET_WIDTH:(i + 1) * RET_WIDTH])

    q = proj(0)
    k = proj(1)
    v = proj(2)
    gate = proj(3)
    u = proj(4)
    vg = proj(5)
    place_prev_tile(choice)
    cq = cq_ref[...]
    sq = sq_ref[...]
    ck = cq * K_SCALE
    sk = sq * K_SCALE

    for hd in range(N_HEADS):
        cols = slice(hd * HEAD_DIM, (hd + 1) * HEAD_DIM)
        qr = _rotary(q[:, cols], cq, sq)
        kr = _rotary(k[:, cols], ck, sk)
        vh = v[:, cols].astype(BF16)
        dm = dmat_ref[hd]
        qd = qdec_ref[hd]
        kd = kdec_ref[hd]
        sd = sdec_ref[hd]
        for c in range(tile // ret_block):
            rows = slice(c * ret_block, (c + 1) * ret_block)
            qb = qr[rows].astype(BF16)
            kb = kr[rows]
            vb = vh[rows]
            state = s_scr[hd]
            scores = _dot_nt(qb, kb.astype(BF16)) * dm
            o = _dot(scores.astype(BF16), vb) + qd * _dot(qb, state.astype(BF16))
            s_scr[hd] = sd * state + _dot_tn((kb * kd).astype(BF16), vb)
            mix_scr[rows, cols] = _gated_head_norm(o, gate[rows, cols])

    uf, vn = _gmlp_inputs(u, vg, lng_ref[...], lnb_ref[...])
    vnb = vn.astype(BF16)
    r_i = lax.broadcasted_iota(I32, (GM_CHUNK, GM_CHUNK), 0)
    c_i = lax.broadcasted_iota(I32, (GM_CHUNK, GM_CHUNK), 1)
    for hd in range(N_HEADS):
        cols = slice(hd * HEAD_DIM, (hd + 1) * HEAD_DIM)
        w_tril = jnp.where(r_i >= c_i, ws_ref[hd], 0.0).astype(BF16)
        bias = bst_ref[:, hd:hd + 1]
        for c in range(tile // GM_CHUNK):
            rows = slice(c * GM_CHUNK, (c + 1) * GM_CHUNK)
            mixed = _dot(w_tril, vnb[rows, cols]) + bias
            mix_scr[rows, RET_WIDTH + hd * HEAD_DIM:RET_WIDTH + (hd + 1) * HEAD_DIM] = (
                uf[rows, cols] * mixed).astype(BF16)

    x2 = x + _dot(mix_scr[...], wout_ref[...])
    x2_ref[...] = x2
    x2_carry[...] = x2

    @pl.when(t == tiles_per_seq - 1)
    def _():
        st_ref[0] = s_scr[...]


def _decode_kernel(x_ref, s0_ref, g1_ref, win_ref, cq_ref, sq_ref, qdec_ref, kdec_ref,
                   sdec_ref, lng_ref, lnb_ref, ws_ref, bst_ref, wout_ref, g2_ref, wr_ref, br_ref,
                   tri_ref, upper_ref,
                   x2_ref, h2_ref, lp_ref, tw_ref, ct_ref, of_ref, wp_ref, st_ref, vn_ref,
                   mix_scr, q_scr, k_scr, v_scr, oc_scr, st_stage, st_sem, *, n_streams, frames):
    tile = n_streams * frames
    x = x_ref[...]
    h = _rmsnorm(x, g1_ref[...]).astype(BF16)

    def proj(i):
        return _dot(h, win_ref[:, i * RET_WIDTH:(i + 1) * RET_WIDTH])

    q = proj(0)
    k = proj(1)
    v = proj(2)
    gate = proj(3)
    u = proj(4)
    vg = proj(5)
    cq = cq_ref[...]
    sq = sq_ref[...]
    ck = cq * K_SCALE
    sk = sq * K_SCALE

    r_i = lax.broadcasted_iota(I32, (tile, tile), 0)
    c_i = lax.broadcasted_iota(I32, (tile, tile), 1)
    shift = frames.bit_length() - 1
    r_frame = lax.bitwise_and(r_i, frames - 1)
    c_frame = lax.bitwise_and(c_i, frames - 1)
    keep = (lax.shift_right_logical(r_i, shift) == lax.shift_right_logical(c_i, shift)) & (r_frame >= c_frame)
    frame_gap = (r_frame - c_frame).astype(F32)

    o_in = []
    for hd in range(N_HEADS):
        cols = slice(hd * HEAD_DIM, (hd + 1) * HEAD_DIM)
        qr = _rotary(q[:, cols], cq, sq)
        kr = _rotary(k[:, cols], ck, sk)
        vh = v[:, cols].astype(BF16)
        qb = qr.astype(BF16)
        decay = jnp.where(keep, jnp.exp(frame_gap * _LOG_GAMMA[hd]), 0.0)
        scores = _dot_nt(qb, kr.astype(BF16)) * decay
        o_in.append(_dot(scores.astype(BF16), vh))
        q_scr[hd] = qb
        k_scr[hd] = (kr * kdec_ref[hd]).astype(BF16)
        v_scr[hd] = vh

    def state_copy(b):
        slot = lax.rem(b, 2)
        return pltpu.make_async_copy(st_stage.at[slot], st_ref.at[b], st_sem.at[slot])

    def stream_body(b, carry):
        rows = pl.ds(pl.multiple_of(b * frames, frames), frames)
        slot = lax.rem(b, 2)

        @pl.when(b >= 2)
        def _():
            state_copy(b - 2).wait()

        for hd in range(N_HEADS):
            state = s0_ref[b, hd]
            oc_scr[hd, rows, :] = _dot(q_scr[hd, rows, :], state.astype(BF16))
            st_stage[slot, hd] = sdec_ref[hd] * state + _dot_tn(k_scr[hd, rows, :], v_scr[hd, rows, :])
        state_copy(b).start()
        return carry

    lax.fori_loop(0, n_streams, stream_body, 0)
    state_copy(n_streams - 2).wait()
    state_copy(n_streams - 1).wait()

    for hd in range(N_HEADS):
        cols = slice(hd * HEAD_DIM, (hd + 1) * HEAD_DIM)
        o = o_in[hd] + qdec_ref[hd] * oc_scr[hd]
        mix_scr[:, cols] = _gated_head_norm(o, gate[:, cols])

    uf, vn = _gmlp_inputs(u, vg, lng_ref[...], lnb_ref[...])
    vn_ref[...] = vn
    vnb = vn.astype(BF16)
    sel = (lax.bitwise_and(lax.broadcasted_iota(I32, (tile, frames), 0), frames - 1)
           == lax.broadcasted_iota(I32, (tile, frames), 1)).astype(BF16)
    for hd in range(N_HEADS):
        cols = slice(hd * HEAD_DIM, (hd + 1) * HEAD_DIM)
        w_rows = _dot(sel, ws_ref[hd].astype(BF16)).astype(BF16)
        w_blk = jnp.where(keep, _dot_nt(w_rows, sel), 0.0).astype(BF16)
        mixed = _dot(w_blk, vnb[:, cols]) + bst_ref[:, hd:hd + 1]
        mix_scr[:, RET_WIDTH + hd * HEAD_DIM:RET_WIDTH + (hd + 1) * HEAD_DIM] = (uf[:, cols] * mixed).astype(BF16)

    _route_and_store(x, mix_scr, wout_ref, g2_ref, wr_ref, br_ref, tri_ref, upper_ref,
                     x2_ref, h2_ref, lp_ref, tw_ref, ct_ref, of_ref, wp_ref)


def _rope_tables(pos):
    half = HEAD_DIM // 2
    inv = np.float32(ROPE_BASE) ** (-np.arange(half, dtype=np.float32) / np.float32(half))
    ang = pos.astype(np.float32)[:, None] * inv[None, :]
    cos = np.cos(ang).astype(np.float32)
    sin = np.sin(ang).astype(np.float32)
    return np.concatenate([cos, cos], axis=-1), np.concatenate([-sin, sin], axis=-1)


def _decay_tables(block):
    log_g = np.asarray(_LOG_GAMMA, np.float32)
    idx = np.arange(block, dtype=np.float32)
    diff = idx[:, None] - idx[None, :]
    dmat = np.where(diff[None] >= 0, np.exp(np.maximum(diff, 0.0)[None] * log_g[:, None, None]), 0.0)
    q_dec = np.exp((idx + 1.0)[None, :] * log_g[:, None])
    k_dec = np.exp((block - 1.0 - idx)[None, :] * log_g[:, None])
    s_dec = np.exp(np.float32(block) * log_g)
    bc = lambda a: np.ascontiguousarray(
        np.broadcast_to(a[:, :, None], (N_HEADS, block, HEAD_DIM)).astype(np.float32))
    s_row = np.ascontiguousarray(np.broadcast_to(s_dec[:, None, None], (N_HEADS, 1, HEAD_DIM)).astype(np.float32))
    return dmat.astype(np.float32), bc(q_dec), bc(k_dec), s_row


def _routing_constants(tile):
    tri = jnp.asarray(np.arange(tile)[:, None] > np.arange(tile)[None, :], BF16)
    upper = jnp.asarray(np.arange(N_EXPERTS)[:, None] < np.arange(N_EXPERTS)[None, :], BF16)
    return tri, upper


def _front_out(n_tok, n_tiles):
    return [
        jax.ShapeDtypeStruct((n_tok, D_MODEL), F32),
        jax.ShapeDtypeStruct((n_tok * ROW_CHUNKS, LANES), U32),
        jax.ShapeDtypeStruct((n_tok, TOP_K), I32),
        jax.ShapeDtypeStruct((n_tok, TOP_K), F32),
        jax.ShapeDtypeStruct((n_tiles, 1, N_EXPERTS), F32),
        jax.ShapeDtypeStruct((n_tiles, 1, N_EXPERTS), F32),
        jax.ShapeDtypeStruct((n_tiles, 1, TILE_SLOTS), I32),
    ]


def _prompt_call(x, p):
    bsz, seq, _ = x.shape
    tile = TOKEN_TILE
    n_t = seq // tile
    cq, sq = _rope_tables(np.arange(seq))
    dmat, qdec, kdec, sdec = _decay_tables(PROMPT_RET_BLOCK)
    tri, upper = _routing_constants(tile)
    ws = p['w_s'][:, :GM_CHUNK, :GM_CHUNK]
    bst = p['b_s'][:, :GM_CHUNK].T

    n_tiles = bsz * n_t
    cur = lambda s: jnp.minimum(s, n_tiles - 1)
    prev = lambda s: jnp.maximum(s - 1, 0)
    const = lambda shape: pl.BlockSpec(shape, lambda s: (0,) * len(shape))
    pos = lambda: pl.BlockSpec((tile, HEAD_DIM), lambda s: (lax.rem(cur(s), n_t), 0))
    in_specs = [
        pl.BlockSpec((1, tile, D_MODEL), lambda s: (cur(s) // n_t, lax.rem(cur(s), n_t), 0)),
        const((1, D_MODEL)), const((D_MODEL, 6 * RET_WIDTH)),
        pos(), pos(),
        const((N_HEADS, PROMPT_RET_BLOCK, PROMPT_RET_BLOCK)),
        const((N_HEADS, PROMPT_RET_BLOCK, HEAD_DIM)), const((N_HEADS, PROMPT_RET_BLOCK, HEAD_DIM)),
        const((N_HEADS, 1, HEAD_DIM)),
        const((1, GM_WIDTH)), const((1, GM_WIDTH)),
        const((N_HEADS, GM_CHUNK, GM_CHUNK)), const((GM_CHUNK, N_HEADS)),
        const((D_MODEL, D_MODEL)), const((1, D_MODEL)),
        const((D_MODEL, N_EXPERTS)), const((1, N_EXPERTS)),
        const((tile, tile)), const((N_EXPERTS, N_EXPERTS)),
    ]
    out_specs = [
        pl.BlockSpec((tile, D_MODEL), lambda s: (cur(s), 0)),
        pl.BlockSpec((tile * ROW_CHUNKS, LANES), lambda s: (prev(s), 0)),
        pl.BlockSpec((tile, TOP_K), lambda s: (prev(s), 0)),
        pl.BlockSpec((tile, TOP_K), lambda s: (prev(s), 0)),
        pl.BlockSpec((1, 1, N_EXPERTS), lambda s: (prev(s), 0, 0)),
        pl.BlockSpec((1, 1, N_EXPERTS), lambda s: (prev(s), 0, 0)),
        pl.BlockSpec((1, 1, TILE_SLOTS), lambda s: (prev(s), 0, 0)),
        pl.BlockSpec((1, N_HEADS, HEAD_DIM, HEAD_DIM), lambda s: (cur(s) // n_t, 0, 0, 0)),
    ]
    out_shape = _front_out(bsz * seq, n_tiles) + [jax.ShapeDtypeStruct((bsz, N_HEADS, HEAD_DIM, HEAD_DIM), F32)]
    return pl.pallas_call(
        functools.partial(_prompt_kernel, ret_block=PROMPT_RET_BLOCK, tiles_per_seq=n_t, n_tiles=n_tiles),
        grid=(n_tiles + 1,),
        in_specs=in_specs,
        out_specs=out_specs,
        out_shape=out_shape,
        scratch_shapes=[
            pltpu.VMEM((N_HEADS, HEAD_DIM, HEAD_DIM), F32),
            pltpu.VMEM((tile, D_MODEL), BF16),
            pltpu.VMEM((tile, D_MODEL), F32),
        ],
        compiler_params=pltpu.CompilerParams(
            dimension_semantics=("arbitrary",), vmem_limit_bytes=VMEM_LIMIT),
        name="front_prompt",
    )(x, p['norm1_g'], p['w_in'], cq, sq, dmat, qdec, kdec, sdec, p['ln_v_g'], p['ln_v_b'], ws, bst,
      p['w_out'], p['norm2_g'], p['w_router'], p['b_router'], tri, upper)


def _decode_call(x, state, past_len, p):
    n_streams, frames, _ = x.shape
    tile = n_streams * frames
    assert tile == TOKEN_TILE and frames <= RET_CHUNK and frames & (frames - 1) == 0
    cq, sq = [np.tile(a, (n_streams, 1)) for a in _rope_tables(past_len + np.arange(frames))]
    _, qdec, kdec, sdec = _decay_tables(frames)
    qdec = np.tile(qdec, (1, n_streams, 1))
    kdec = np.tile(kdec, (1, n_streams, 1))
    tri, upper = _routing_constants(tile)
    ws = p['w_s'][:, :frames, :frames]
    bst = jnp.tile(p['b_s'][:, :frames].T, (n_streams, 1))

    def whole(a):
        return pl.BlockSpec(a.shape, lambda i, n=a.ndim: (0,) * n, pipeline_mode=pl.Buffered(1))

    args = (x.reshape(tile, D_MODEL), state, p['norm1_g'], p['w_in'], cq, sq, qdec, kdec, sdec,
            p['ln_v_g'], p['ln_v_b'], ws, bst, p['w_out'], p['norm2_g'], p['w_router'], p['b_router'],
            tri, upper)
    out_shape = _front_out(tile, 1) + [
        jax.ShapeDtypeStruct(state.shape, F32),
        jax.ShapeDtypeStruct((tile, GM_WIDTH), F32),
    ]
    out_specs = [pl.BlockSpec(s.shape, lambda i, n=len(s.shape): (0,) * n) for s in out_shape]
    out_specs[7] = pl.BlockSpec(memory_space=pl.ANY)
    return pl.pallas_call(
        functools.partial(_decode_kernel, n_streams=n_streams, frames=frames),
        grid=(1,),
        in_specs=[whole(a) for a in args],
        out_specs=out_specs,
        out_shape=out_shape,
        scratch_shapes=[
            pltpu.VMEM((tile, D_MODEL), BF16),
            pltpu.VMEM((N_HEADS, tile, HEAD_DIM), BF16),
            pltpu.VMEM((N_HEADS, tile, HEAD_DIM), BF16),
            pltpu.VMEM((N_HEADS, tile, HEAD_DIM), BF16),
            pltpu.VMEM((N_HEADS, tile, HEAD_DIM), F32),
            pltpu.VMEM((2, N_HEADS, HEAD_DIM, HEAD_DIM), F32),
            pltpu.SemaphoreType.DMA((2,)),
        ],
        compiler_params=pltpu.CompilerParams(
            dimension_semantics=("arbitrary",), vmem_limit_bytes=VMEM_LIMIT),
        name="front_decode",
    )(*args)


def _plan_kernel(ct_ref, lower_ref, upper_ref, cum_ref, nb_ref, bs_ref, be_ref, nbt_ref, tlo_ref, thi_ref,
                 *, n_blocks):
    counts = ct_ref[...]
    cum = _exact_count_dot(lower_ref[...], counts, counts_on_left=False)
    cum_ref[...] = cum.astype(I32)
    n_tiles = counts.shape[0]
    total = cum[n_tiles:n_tiles + 1]
    nb = jnp.floor((total + (EXPERT_ROWS - 1)) * (1.0 / EXPERT_ROWS))
    bstart = _dot(jnp.broadcast_to(nb, (8, N_EXPERTS)).astype(BF16), upper_ref[...])[0:1]
    bend = bstart + nb
    nb_ref[...] = nb.astype(I32)
    bs_ref[...] = bstart.astype(I32)
    blk = lax.broadcasted_iota(I32, (n_blocks, N_EXPERTS), 0).astype(F32)
    be = jnp.minimum(jnp.sum((bend <= blk).astype(F32), axis=-1, keepdims=True), N_EXPERTS - 1.0)
    be_ref[...] = be.astype(I32)
    nbt_ref[...] = jnp.sum(nb, axis=-1, keepdims=True).astype(I32)
    mine = lax.broadcasted_iota(I32, (n_blocks, N_EXPERTS), 1).astype(F32) == be
    pick = lambda row: jnp.sum(jnp.where(mine, row, 0.0), axis=-1, keepdims=True)
    r0 = (blk[:, 0:1] - pick(bstart)) * EXPERT_ROWS
    tlo = jnp.zeros((n_blocks, 1), F32)
    thi = jnp.zeros((n_blocks, 1), F32)
    seg_lo = pick(cum[0:1])
    for i in range(n_tiles):
        seg_hi = pick(cum[i + 1:i + 2])
        tlo = tlo + (seg_hi <= r0).astype(F32)
        thi = thi + (seg_lo < r0 + EXPERT_ROWS).astype(F32)
        seg_lo = seg_hi
    tlo_ref[...] = tlo.astype(I32)
    thi_ref[...] = thi.astype(I32)


def _plan_call(counts, n_blocks):
    n_tiles = counts.shape[0]
    lower = (jnp.arange(n_tiles + 1)[:, None] > jnp.arange(n_tiles)[None, :]).astype(BF16)
    upper = (jnp.arange(N_EXPERTS)[:, None] < jnp.arange(N_EXPERTS)[None, :]).astype(BF16)
    return pl.pallas_call(
        functools.partial(_plan_kernel, n_blocks=n_blocks),
        out_shape=[
            jax.ShapeDtypeStruct((n_tiles + 1, N_EXPERTS), I32),
            jax.ShapeDtypeStruct((1, N_EXPERTS), I32),
            jax.ShapeDtypeStruct((1, N_EXPERTS), I32),
            jax.ShapeDtypeStruct((n_blocks, 1), I32),
            jax.ShapeDtypeStruct((1, 1), I32),
            jax.ShapeDtypeStruct((n_blocks, 1), I32),
            jax.ShapeDtypeStruct((n_blocks, 1), I32),
        ],
        name="plan",
    )(counts, lower, upper)


def _dispatch_kernel(h2p_ref, h2s_ref, wp_ref, out_ref, *, n_prompt_tiles):
    i = pl.program_id(0)

    def permute(src_ref):
        def body(t, c):
            row = src_ref[_row_tile(t), :]
            for kk in range(TOP_K):
                first = pl.multiple_of(wp_ref[0, 0, kk * TOKEN_TILE + t], ROW_CHUNKS)
                out_ref[pl.ds(first, ROW_CHUNKS), :] = row
            return c

        lax.fori_loop(0, TOKEN_TILE, body, 0, unroll=32)

    @pl.when(i < n_prompt_tiles)
    def _():
        permute(h2p_ref)

    @pl.when(i >= n_prompt_tiles)
    def _():
        permute(h2s_ref)


def _dispatch_call(h2_p, h2_s, word_pos):
    rows = TOKEN_TILE * ROW_CHUNKS
    n_p_tiles = h2_p.shape[0] // rows
    n_s_tiles = h2_s.shape[0] // rows
    n_tiles = n_p_tiles + n_s_tiles
    return pl.pallas_call(
        functools.partial(_dispatch_kernel, n_prompt_tiles=n_p_tiles),
        grid=(n_tiles,),
        in_specs=[
            pl.BlockSpec((rows, LANES), lambda i: (jnp.minimum(i, n_p_tiles - 1), 0)),
            pl.BlockSpec((rows, LANES), lambda i: (jnp.maximum(i - n_p_tiles, 0), 0)),
            pl.BlockSpec((1, 1, TILE_SLOTS), lambda i: (i, 0, 0), memory_space=pltpu.SMEM),
        ],
        out_specs=pl.BlockSpec((TILE_SLOTS * ROW_CHUNKS, LANES), lambda i: (i, 0)),
        out_shape=jax.ShapeDtypeStruct((n_tiles * TILE_SLOTS * ROW_CHUNKS, LANES), U32),
        compiler_params=pltpu.CompilerParams(
            dimension_semantics=("arbitrary",), vmem_limit_bytes=VMEM_LIMIT),
        name="dispatch",
    )(h2_p, h2_s, word_pos)


def _expert_kernel(be_ref, bs_ref, nb_ref, nbt_ref, cum_ref, off_ref, tlo_ref, thi_ref,
                   xb_ref, wg_ref, wu_ref, wd_ref, bg_ref, bu_ref, bd_ref,
                   yb_ref, xbuf, ybuf, wbf, in_sem, out_sem, *, n_tiles):
    b = pl.program_id(0)
    nbt = nbt_ref[0]

    def block_rows(blk):
        e = be_ref[blk]
        r0 = (blk - bs_ref[e]) * EXPERT_ROWS
        return r0, jnp.minimum(cum_ref[n_tiles * N_EXPERTS + e] - r0, EXPERT_ROWS)

    def segments(blk, slot, fn):
        e = be_ref[blk]
        r0 = (blk - bs_ref[e]) * EXPERT_ROWS

        def body(i, c):
            seg_lo = cum_ref[i * N_EXPERTS + e]
            seg_hi = cum_ref[(i + 1) * N_EXPERTS + e]
            lo = jnp.maximum(seg_lo, r0)
            hi = jnp.minimum(seg_hi, r0 + EXPERT_ROWS)

            @pl.when(hi > lo)
            def _():
                bucket_row = i * TILE_SLOTS + off_ref[i * N_EXPERTS + e] + (lo - seg_lo)
                fn(slot, bucket_row, slot * EXPERT_ROWS + (lo - r0), hi - lo)
            return c

        lax.fori_loop(tlo_ref[blk], thi_ref[blk], body, 0)

    def words(row, n):
        return pl.ds(pl.multiple_of(row * ROW_CHUNKS, ROW_CHUNKS), n * ROW_CHUNKS)

    def gather(slot, bucket_row, buffer_row, n):
        return pltpu.make_async_copy(
            xb_ref.at[words(bucket_row, n)], xbuf.at[words(buffer_row, n)], in_sem.at[slot])

    def scatter(slot, bucket_row, buffer_row, n):
        return pltpu.make_async_copy(
            ybuf.at[words(buffer_row, n)], yb_ref.at[words(bucket_row, n)], out_sem.at[slot])

    def start_gather(blk):
        @pl.when(blk < nbt)
        def _():
            segments(blk, lax.rem(blk, 2), lambda *a: gather(*a).start())

    def wait_gather(blk):
        @pl.when(blk < nbt)
        def _():
            slot = lax.rem(blk, 2)
            gather(slot, 0, slot * EXPERT_ROWS, block_rows(blk)[1]).wait()

    def start_scatter(blk):
        @pl.when(blk < nbt)
        def _():
            segments(blk, lax.rem(blk, 2), lambda *a: scatter(*a).start())

    def wait_scatter(blk):
        @pl.when((blk >= 0) & (blk < nbt))
        def _():
            slot = lax.rem(blk, 2)
            scatter(slot, 0, slot * EXPERT_ROWS, block_rows(blk)[1]).wait()

    def mlp(base, n_rows, rows_left):
        valid = lax.broadcasted_iota(I32, (n_rows, 1), 0) < rows_left
        x = jnp.where(valid, _load_rows(xbuf, base, n_rows), 0.0)
        gt = _dot(x, wbf[0]) + bg_ref[0]
        up = _dot(x, wbf[1]) + bu_ref[0]
        gt = jnp.minimum(gt, SWIGLU_LIMIT)
        up = jnp.clip(up, -SWIGLU_LIMIT, SWIGLU_LIMIT)
        act = gt * _sigmoid(gt * SWIGLU_ALPHA) * (up + 1.0)
        _store_rows(ybuf, base, _dot(act.astype(BF16), wbf[2]) + bd_ref[0])

    def block_body(blk, carry):
        wait_gather(blk)
        start_gather(blk + 1)
        wait_scatter(blk - 2)
        rows_left = block_rows(blk)[1]
        base = lax.rem(blk, 2) * EXPERT_ROWS

        lower = 0
        for upper, passes in EXPERT_PASS_PLANS:
            @pl.when((rows_left > lower) & (rows_left <= upper))
            def _(passes=passes):
                first = 0
                for n_rows in passes:
                    mlp(base + first, n_rows, rows_left - first)
                    first += n_rows
            lower = upper

        start_scatter(blk)
        return carry

    first_block = bs_ref[b]
    n_mine = nb_ref[b]

    @pl.when(b == 0)
    def _():
        xbuf[...] = jnp.zeros_like(xbuf)
        start_gather(0)

    @pl.when(n_mine > 0)
    def _():
        wbf[0] = wg_ref[0, 0].astype(BF16)
        wbf[1] = wu_ref[0, 0].astype(BF16)
        wbf[2] = wd_ref[0, 0].astype(BF16)

    lax.fori_loop(first_block, first_block + n_mine, block_body, 0)

    @pl.when(b == N_EXPERTS - 1)
    def _():
        wait_scatter(nbt - 2)
        wait_scatter(nbt - 1)


def _expert_call(be, bstart, nb, nbt, cum_flat, off_flat, tlo, thi, buckets, p, n_tiles):
    wspec = pl.BlockSpec((1, 1, D_MODEL, D_MODEL), lambda e, *_: (0, e, 0, 0))
    bspec = pl.BlockSpec((1, 1, D_MODEL), lambda e, *_: (e, 0, 0))
    any_spec = pl.BlockSpec(memory_space=pl.ANY)
    return pl.pallas_call(
        functools.partial(_expert_kernel, n_tiles=n_tiles),
        grid_spec=pltpu.PrefetchScalarGridSpec(
            num_scalar_prefetch=8,
            grid=(N_EXPERTS,),
            in_specs=[any_spec, wspec, wspec, wspec, bspec, bspec, bspec],
            out_specs=any_spec,
            scratch_shapes=[
                pltpu.VMEM((2 * EXPERT_ROWS * ROW_CHUNKS, LANES), U32),
                pltpu.VMEM((2 * EXPERT_ROWS * ROW_CHUNKS, LANES), U32),
                pltpu.VMEM((3, D_MODEL, D_MODEL), BF16),
                pltpu.SemaphoreType.DMA((2,)),
                pltpu.SemaphoreType.DMA((2,)),
            ],
        ),
        out_shape=jax.ShapeDtypeStruct(buckets.shape, U32),
        compiler_params=pltpu.CompilerParams(
            dimension_semantics=("arbitrary",), vmem_limit_bytes=VMEM_LIMIT),
        name="experts",
    )(be, bstart, nb, nbt, cum_flat, off_flat, tlo, thi, buckets, p['w_gate'], p['w_up'], p['w_down'],
      p['b_gate'][:, None, :], p['b_up'][:, None, :], p['b_down'][:, None, :])


def _combine_kernel(x2_ref, yb_ref, lp_ref, tw_ref, gf_ref, y_ref, ybf, pick):
    for c in range(TILE_SLOTS // COMBINE_CHUNK):
        rows = slice(c * COMBINE_CHUNK, (c + 1) * COMBINE_CHUNK)
        ybf[rows, :] = _load_rows(yb_ref, c * COMBINE_CHUNK, COMBINE_CHUNK)
    slot = lax.broadcasted_iota(I32, (TOKEN_TILE, TILE_SLOTS), 1)
    lp = lp_ref[...]
    tw = tw_ref[...]
    sel = jnp.zeros((TOKEN_TILE, TILE_SLOTS), F32)
    for kk in range(TOP_K):
        sel = jnp.where(slot == lp[:, kk:kk + 1], tw[:, kk:kk + 1], sel)
    pick[...] = sel.astype(BF16)
    y_ref[...] = _rmsnorm(x2_ref[...] + _dot(pick[...], ybf[...]), gf_ref[...])


def _combine_call(x2, ybuckets, lp, tw, gf, tile_off):
    n_tok = x2.shape[0]
    n_tiles = n_tok // TOKEN_TILE
    return pl.pallas_call(
        _combine_kernel,
        grid=(n_tiles,),
        in_specs=[
            pl.BlockSpec((TOKEN_TILE, D_MODEL), lambda i: (i, 0)),
            pl.BlockSpec((TILE_SLOTS * ROW_CHUNKS, LANES), lambda i: (tile_off + i, 0)),
            pl.BlockSpec((TOKEN_TILE, TOP_K), lambda i: (i, 0)),
            pl.BlockSpec((TOKEN_TILE, TOP_K), lambda i: (i, 0)),
            pl.BlockSpec((1, D_MODEL), lambda i: (0, 0)),
        ],
        out_specs=pl.BlockSpec((TOKEN_TILE, D_MODEL), lambda i: (i, 0)),
        scratch_shapes=[
            pltpu.VMEM((TILE_SLOTS, D_MODEL), BF16),
            pltpu.VMEM((TOKEN_TILE, TILE_SLOTS), BF16),
        ],
        out_shape=jax.ShapeDtypeStruct((n_tok, D_MODEL), F32),
        compiler_params=pltpu.CompilerParams(
            dimension_semantics=("arbitrary",), vmem_limit_bytes=VMEM_LIMIT),
        name="combine",
    )(x2, ybuckets, lp, tw, gf)


def kernel(x_prompt, x_sample, state_ret, norm1_g, w_in, ln_v_g, ln_v_b, w_s, b_s, w_out, norm2_g, w_router, b_router, w_gate, b_gate, w_up, b_up, w_down, b_down, norm_f_g):
    bsz, seq, _ = x_prompt.shape
    dbsz, dseq, _ = x_sample.shape
    n_p = bsz * seq
    n_s = dbsz * dseq
    n_total = n_p + n_s
    past_len = 2048
    assert w_in.shape[0] == 1, "single layer"

    p = dict(norm1_g=norm1_g, w_in=w_in[0].astype(BF16), ln_v_g=ln_v_g, ln_v_b=ln_v_b, w_s=w_s[0], b_s=b_s[0],
             w_out=w_out[0].astype(BF16), norm2_g=norm2_g, w_router=w_router[0].astype(BF16), b_router=b_router,
             w_gate=w_gate, b_gate=b_gate[0], w_up=w_up, b_up=b_up[0], w_down=w_down, b_down=b_down[0])

    x2_p, h2_p, lp_p, tw_p, ct_p, of_p, wp_p, st_p = _prompt_call(x_prompt, p)
    x2_s, h2_s, lp_s, tw_s, ct_s, of_s, wp_s, st_s, vn_s = _decode_call(x_sample, state_ret[0], past_len, p)

    n_tiles = n_total // TOKEN_TILE
    word_pos = jnp.concatenate([wp_p, wp_s], axis=0)
    counts = jnp.concatenate([ct_p, ct_s], axis=0).reshape(n_tiles, N_EXPERTS)
    offsets = jnp.concatenate([of_p, of_s], axis=0).reshape(n_tiles * N_EXPERTS).astype(I32)

    n_blocks = -(-(n_total * TOP_K + N_EXPERTS * (EXPERT_ROWS - 1)) // EXPERT_ROWS)
    cum, nb, bstart, be, nbt, tlo, thi = _plan_call(counts, n_blocks)

    buckets = _dispatch_call(h2_p, h2_s, word_pos)
    ybuckets = _expert_call(be.reshape(n_blocks), bstart.reshape(N_EXPERTS), nb.reshape(N_EXPERTS), nbt.reshape(1),
                            cum.reshape((n_tiles + 1) * N_EXPERTS), offsets, tlo.reshape(n_blocks),
                            thi.reshape(n_blocks), buckets, p, n_tiles)

    y_p = _combine_call(x2_p, ybuckets, lp_p, tw_p, norm_f_g[None, :], 0)
    y_s = _combine_call(x2_s, ybuckets, lp_s, tw_s, norm_f_g[None, :], n_p // TOKEN_TILE)

    return (y_p.reshape(bsz, seq, D_MODEL), y_s.reshape(dbsz, dseq, D_MODEL),
            st_p[None], st_s[None], vn_s.reshape(1, dbsz, dseq, GM_WIDTH))
```

```python
import functools

import numpy as np
import jax
import jax.numpy as jnp
from jax import lax
from jax.experimental import pallas as pl
from jax.experimental.pallas import tpu as pltpu

F32 = jnp.float32
BF16 = jnp.bfloat16
I32 = jnp.int32
U32 = jnp.uint32

D_MODEL = 1024
RET_WIDTH = 512
N_HEADS = 4
HEAD_DIM = 128
GM_WIDTH = 512
GM_CHUNK = 128
RET_CHUNK = 64
N_EXPERTS = 32
TOP_K = 4
SWIGLU_LIMIT = 7.0
SWIGLU_ALPHA = 1.702
ROPE_BASE = 10000.0
EPS = 1e-6

TOKEN_TILE = 512
PROMPT_RET_BLOCK = 256
EXPERT_ROWS = 1024
EXPERT_PASS_PLANS = ((256, (256,)), (512, (512,)), (768, (512, 256)), (1024, (512, 512)))
TILE_SLOTS = TOKEN_TILE * TOP_K
COMBINE_CHUNK = 256
VMEM_LIMIT = 56 * 1024 * 1024
LANES = 128
ROW_CHUNKS = D_MODEL // LANES // 2
K_SCALE = HEAD_DIM ** -0.5
SPLIT = 16.0
_LOG_GAMMA = [float(np.log1p(-np.float32(2.0) ** np.float32(-5.0 - hd)).astype(np.float32)) for hd in range(N_HEADS)]


def _load_rows(ref, first_row, n_rows):
    base = first_row * ROW_CHUNKS
    lo, hi = [], []
    for c in range(ROW_CHUNKS):
        w = ref[pl.ds(base + c, n_rows, stride=ROW_CHUNKS), :]
        lo.append(pltpu.bitcast(lax.shift_left(w, jnp.uint32(16)), F32))
        hi.append(pltpu.bitcast(lax.bitwise_and(w, jnp.uint32(0xFFFF0000)), F32))
    return jnp.concatenate(lo + hi, axis=1).astype(BF16)


def _store_rows(ref, first_row, val):
    base = first_row * ROW_CHUNKS
    bits = pltpu.bitcast(val.astype(BF16).astype(F32), U32)
    half = D_MODEL // 2
    for c in range(ROW_CHUNKS):
        low = lax.shift_right_logical(bits[:, c * LANES:(c + 1) * LANES], jnp.uint32(16))
        ref[pl.ds(base + c, val.shape[0], stride=ROW_CHUNKS), :] = lax.bitwise_or(
            bits[:, half + c * LANES:half + (c + 1) * LANES], low)


def _row_tile(row):
    return pl.ds(pl.multiple_of(row * ROW_CHUNKS, ROW_CHUNKS), ROW_CHUNKS)


def _rmsnorm(x, g):
    ms = jnp.mean(x * x, axis=-1, keepdims=True)
    return x * lax.rsqrt(ms + EPS) * g


def _gelu(x):
    c = np.float32(np.sqrt(2.0 / np.pi))
    return x * (0.5 * (1.0 + jnp.tanh(c * (x + 0.044715 * (x * x * x)))))


def _sigmoid(x):
    return 1.0 / (1.0 + jnp.exp(-x))


def _dot(a, b):
    return jnp.dot(a, b, preferred_element_type=F32)


def _dot_nt(a, b):
    return lax.dot_general(a, b, (((1,), (1,)), ((), ())), preferred_element_type=F32)


def _dot_tn(a, b):
    return lax.dot_general(a, b, (((0,), (0,)), ((), ())), preferred_element_type=F32)


def _exact_count_dot(a, b, *, counts_on_left):
    cnt = a if counts_on_left else b
    hi = jnp.floor(cnt * (1.0 / SPLIT))
    lo = cnt - SPLIT * hi
    if counts_on_left:
        return SPLIT * _dot(hi.astype(BF16), b) + _dot(lo.astype(BF16), b)
    return SPLIT * _dot(a, hi.astype(BF16)) + _dot(a, lo.astype(BF16))


def _rotary(xh, cos2, sin2):
    return xh * cos2 + pltpu.roll(xh, HEAD_DIM // 2, axis=1) * sin2


def _gated_head_norm(o, gate):
    mu = jnp.mean(o, axis=-1, keepdims=True)
    oc = o - mu
    var = jnp.mean(oc * oc, axis=-1, keepdims=True)
    return ((gate * _sigmoid(gate)) * (oc * lax.rsqrt(var + EPS))).astype(BF16)


def _gmlp_inputs(u, vg, lng, lnb):
    uf = _gelu(u)
    vf = _gelu(vg)
    mu = jnp.mean(vf, axis=-1, keepdims=True)
    vc = vf - mu
    var = jnp.mean(vc * vc, axis=-1, keepdims=True)
    return uf, vc * lax.rsqrt(var + EPS) * lng + lnb


def _route_and_store(x, mix_scr, wout_ref, g2_ref, wr_ref, br_ref, tri_ref, upper_ref,
                     x2_ref, h2_ref, lp_ref, tw_ref, ct_ref, of_ref, wp_ref):
    h2 = _out_proj(x, mix_scr, wout_ref, g2_ref, x2_ref, h2_ref)
    _route(h2, wr_ref, br_ref, tri_ref, upper_ref, lp_ref, tw_ref, ct_ref, of_ref, wp_ref)


def _out_proj(x, mix_scr, wout_ref, g2_ref, x2_ref, h2_ref):
    x2 = x + _dot(mix_scr[...], wout_ref[...].astype(BF16))
    x2_ref[...] = x2
    h2 = _rmsnorm(x2, g2_ref[...])
    _store_rows(h2_ref, 0, h2)
    return h2


def _route(h2, wr_ref, br_ref, tri_ref, upper_ref, lp_ref, tw_ref, ct_ref, of_ref, wp_ref):
    _route_place(_route_choose(h2, wr_ref, br_ref), tri_ref, upper_ref, lp_ref, tw_ref, ct_ref, of_ref, wp_ref)


def _route_choose(h2, wr_ref, br_ref):
    tile = h2.shape[0]
    logits = _dot(h2.astype(BF16), wr_ref[...]) + br_ref[...]

    iota_e = lax.broadcasted_iota(I32, (tile, N_EXPERTS), 1)
    iota_ef = iota_e.astype(F32)
    lg = logits
    vals = []
    idxs = []
    for _ in range(TOP_K):
        m = jnp.max(lg, axis=-1, keepdims=True)
        idx = jnp.min(jnp.where(lg == m, iota_ef, float(N_EXPERTS)), axis=-1, keepdims=True).astype(I32)
        vals.append(m)
        idxs.append(idx)
        lg = jnp.where(iota_e == idx, -jnp.inf, lg)
    exps = [jnp.exp(vk - vals[0]) for vk in vals]
    denom = exps[0] + exps[1] + exps[2] + exps[3]
    return idxs, exps, denom


def _route_place(choice, tri_ref, upper_ref, lp_ref, tw_ref, ct_ref, of_ref, wp_ref):
    idxs, exps, denom = choice
    tile = denom.shape[0]
    iota_e = lax.broadcasted_iota(I32, (tile, N_EXPERTS), 1)
    iota_k = lax.broadcasted_iota(I32, (tile, TOP_K), 1)
    onehot = jnp.zeros((tile, N_EXPERTS), F32)
    for idx in idxs:
        onehot = onehot + (iota_e == idx).astype(F32)
    rank = _dot(tri_ref[...], onehot.astype(BF16))
    count = jnp.sum(onehot, axis=0, keepdims=True)
    offset = _exact_count_dot(jnp.broadcast_to(count, (8, N_EXPERTS)), upper_ref[...], counts_on_left=True)[0:1]
    where_to = rank + offset
    lp = jnp.zeros((tile, TOP_K), I32)
    tw = jnp.zeros((tile, TOP_K), F32)
    word_rows = []
    for kk in range(TOP_K):
        pos = jnp.sum(jnp.where(iota_e == idxs[kk], where_to, 0.0), axis=-1, keepdims=True)
        lp = jnp.where(iota_k == kk, pos.astype(I32), lp)
        tw = jnp.where(iota_k == kk, exps[kk] / denom, tw)
        word_rows.append(jnp.transpose(jnp.broadcast_to(pos * float(ROW_CHUNKS), (tile, LANES)))[0:1, :])
    lp_ref[...] = lp
    tw_ref[...] = tw
    ct_ref[0] = count
    of_ref[0] = offset
    wp_ref[0] = jnp.concatenate(word_rows, axis=1).astype(I32)


def _prompt_kernel(x_ref, g1_ref, win_ref, cq_ref, sq_ref, dmat_ref, qdec_ref, kdec_ref,
                   sdec_ref, lng_ref, lnb_ref, ws_ref, bst_ref, wout_ref, g2_ref, wr_ref, br_ref,
                   tri_ref, upper_ref,
                   x2_ref, h2_ref, lp_ref, tw_ref, ct_ref, of_ref, wp_ref, st_ref,
                   s_scr, mix_scr, x2_carry, win_bf, wout_bf, *, ret_block, tiles_per_seq, n_tiles):
    step = pl.program_id(0)

    @pl.when(step == 0)
    def _():
        s_scr[...] = jnp.zeros_like(s_scr)
        x2_carry[...] = jnp.zeros_like(x2_carry)
        win_bf[...] = win_ref[...].astype(BF16)
        wout_bf[...] = wout_ref[...].astype(BF16)

    def norm_prev_tile():
        h2_prev = _rmsnorm(x2_carry[...], g2_ref[...])
        _store_rows(h2_ref, 0, h2_prev)
        return _route_choose(h2_prev, wr_ref, br_ref)

    def place_prev_tile(choice):
        _route_place(choice, tri_ref, upper_ref, lp_ref, tw_ref, ct_ref, of_ref, wp_ref)

    @pl.when(step == n_tiles)
    def _():
        place_prev_tile(norm_prev_tile())

    @pl.when(step < n_tiles)
    def _():
        _prompt_step(step, norm_prev_tile, place_prev_tile, x_ref, g1_ref, win_bf, cq_ref, sq_ref,
                     dmat_ref, qdec_ref, kdec_ref, sdec_ref, lng_ref, lnb_ref, ws_ref, bst_ref, wout_bf,
                     x2_ref, st_ref, s_scr, mix_scr, x2_carry, ret_block=ret_block, tiles_per_seq=tiles_per_seq)


def _prompt_step(step, norm_prev_tile, place_prev_tile, x_ref, g1_ref, win_ref, cq_ref, sq_ref,
                 dmat_ref, qdec_ref, kdec_ref, sdec_ref, lng_ref, lnb_ref, ws_ref, bst_ref, wout_ref,
                 x2_ref, st_ref, s_scr, mix_scr, x2_carry, *, ret_block, tiles_per_seq):
    tile = TOKEN_TILE
    t = lax.rem(step, tiles_per_seq)

    choice = norm_prev_tile()
    s_scr[...] = jnp.where(t == 0, 0.0, s_scr[...])

    x = x_ref[0]
    h = _rmsnorm(x, g1_ref[...]).astype(BF16)

    def proj(i):
        return _dot(h, win_ref[:, i * RET_WIDTH:(i + 1) * RET_WIDTH])

    q = proj(0)
    k = proj(1)
    v = proj(2)
    gate = proj(3)
    u = proj(4)
    vg = proj(5)
    place_prev_tile(choice)
    cq = cq_ref[...]
    sq = sq_ref[...]
    ck = cq * K_SCALE
    sk = sq * K_SCALE

    for hd in range(N_HEADS):
        cols = slice(hd * HEAD_DIM, (hd + 1) * HEAD_DIM)
        qr = _rotary(q[:, cols], cq, sq)
        kr = _rotary(k[:, cols], ck, sk)
        vh = v[:, cols].astype(BF16)
        dm = dmat_ref[hd]
        qd = qdec_ref[hd]
        kd = kdec_ref[hd]
        sd = sdec_ref[hd]
        for c in range(tile // ret_block):
            rows = slice(c * ret_block, (c + 1) * ret_block)
            qb = qr[rows].astype(BF16)
            kb = kr[rows]
            vb = vh[rows]
            state = s_scr[hd]
            scores = _dot_nt(qb, kb.astype(BF16)) * dm
            o = _dot(scores.astype(BF16), vb) + qd * _dot(qb, state.astype(BF16))
            s_scr[hd] = sd * state + _dot_tn((kb * kd).astype(BF16), vb)
            mix_scr[rows, cols] = _gated_head_norm(o, gate[rows, cols])

    uf, vn = _gmlp_inputs(u, vg, lng_ref[...], lnb_ref[...])
    vnb = vn.astype(BF16)
    r_i = lax.broadcasted_iota(I32, (GM_CHUNK, GM_CHUNK), 0)
    c_i = lax.broadcasted_iota(I32, (GM_CHUNK, GM_CHUNK), 1)
    for hd in range(N_HEADS):
        cols = slice(hd * HEAD_DIM, (hd + 1) * HEAD_DIM)
        w_tril = jnp.where(r_i >= c_i, ws_ref[hd], 0.0).astype(BF16)
        bias = bst_ref[:, hd:hd + 1]
        for c in range(tile // GM_CHUNK):
            rows = slice(c * GM_CHUNK, (c + 1) * GM_CHUNK)
            mixed = _dot(w_tril, vnb[rows, cols]) + bias
            mix_scr[rows, RET_WIDTH + hd * HEAD_DIM:RET_WIDTH + (hd + 1) * HEAD_DIM] = (
                uf[rows, cols] * mixed).astype(BF16)

    x2 = x + _dot(mix_scr[...], wout_ref[...])
    x2_ref[...] = x2
    x2_carry[...] = x2

    @pl.when(t == tiles_per_seq - 1)
    def _():
        st_ref[0] = s_scr[...]


def _decode_kernel(x_ref, s0_ref, g1_ref, win_ref, cq_ref, sq_ref, qdec_ref, kdec_ref,
                   sdec_ref, lng_ref, lnb_ref, ws_ref, bst_ref, wout_ref, g2_ref, wr_ref, br_ref,
                   tri_ref, upper_ref,
                   x2_ref, h2_ref, lp_ref, tw_ref, ct_ref, of_ref, wp_ref, st_ref, vn_ref,
                   mix_scr, q_scr, k_scr, v_scr, oc_scr, st_stage, st_sem, *, n_streams, frames):
    tile = n_streams * frames
    x = x_ref[...]
    h = _rmsnorm(x, g1_ref[...]).astype(BF16)

    def proj(i):
        return _dot(h, win_ref[:, i * RET_WIDTH:(i + 1) * RET_WIDTH].astype(BF16))

    q = proj(0)
    k = proj(1)
    v = proj(2)
    gate = proj(3)
    u = proj(4)
    vg = proj(5)
    cq = cq_ref[...]
    sq = sq_ref[...]
    ck = cq * K_SCALE
    sk = sq * K_SCALE

    r_i = lax.broadcasted_iota(I32, (tile, tile), 0)
    c_i = lax.broadcasted_iota(I32, (tile, tile), 1)
    shift = frames.bit_length() - 1
    r_frame = lax.bitwise_and(r_i, frames - 1)
    c_frame = lax.bitwise_and(c_i, frames - 1)
    keep = (lax.shift_right_logical(r_i, shift) == lax.shift_right_logical(c_i, shift)) & (r_frame >= c_frame)
    frame_gap = (r_frame - c_frame).astype(F32)

    o_in = []
    for hd in range(N_HEADS):
        cols = slice(hd * HEAD_DIM, (hd + 1) * HEAD_DIM)
        qr = _rotary(q[:, cols], cq, sq)
        kr = _rotary(k[:, cols], ck, sk)
        vh = v[:, cols].astype(BF16)
        qb = qr.astype(BF16)
        decay = jnp.where(keep, jnp.exp(frame_gap * _LOG_GAMMA[hd]), 0.0)
        scores = _dot_nt(qb, kr.astype(BF16)) * decay
        o_in.append(_dot(scores.astype(BF16), vh))
        q_scr[hd] = qb
        k_scr[hd] = (kr * kdec_ref[hd]).astype(BF16)
        v_scr[hd] = vh

    def state_copy(b):
        slot = lax.rem(b, 2)
        return pltpu.make_async_copy(st_stage.at[slot], st_ref.at[b], st_sem.at[slot])

    def stream_body(b, carry):
        rows = pl.ds(pl.multiple_of(b * frames, frames), frames)
        slot = lax.rem(b, 2)

        @pl.when(b >= 2)
        def _():
            state_copy(b - 2).wait()

        for hd in range(N_HEADS):
            state = s0_ref[b, hd]
            oc_scr[hd, rows, :] = _dot(q_scr[hd, rows, :], state.astype(BF16))
            st_stage[slot, hd] = sdec_ref[hd] * state + _dot_tn(k_scr[hd, rows, :], v_scr[hd, rows, :])
        state_copy(b).start()
        return carry

    lax.fori_loop(0, n_streams, stream_body, 0)
    state_copy(n_streams - 2).wait()
    state_copy(n_streams - 1).wait()

    for hd in range(N_HEADS):
        cols = slice(hd * HEAD_DIM, (hd + 1) * HEAD_DIM)
        o = o_in[hd] + qdec_ref[hd] * oc_scr[hd]
        mix_scr[:, cols] = _gated_head_norm(o, gate[:, cols])

    uf, vn = _gmlp_inputs(u, vg, lng_ref[...], lnb_ref[...])
    vn_ref[...] = vn
    vnb = vn.astype(BF16)
    sel = (lax.bitwise_and(lax.broadcasted_iota(I32, (tile, frames), 0), frames - 1)
           == lax.broadcasted_iota(I32, (tile, frames), 1)).astype(BF16)
    for hd in range(N_HEADS):
        cols = slice(hd * HEAD_DIM, (hd + 1) * HEAD_DIM)
        w_rows = _dot(sel, ws_ref[hd].astype(BF16)).astype(BF16)
        w_blk = jnp.where(keep, _dot_nt(w_rows, sel), 0.0).astype(BF16)
        mixed = _dot(w_blk, vnb[:, cols]) + bst_ref[:, hd:hd + 1]
        mix_scr[:, RET_WIDTH + hd * HEAD_DIM:RET_WIDTH + (hd + 1) * HEAD_DIM] = (uf[:, cols] * mixed).astype(BF16)

    _route_and_store(x, mix_scr, wout_ref, g2_ref, wr_ref, br_ref, tri_ref, upper_ref,
                     x2_ref, h2_ref, lp_ref, tw_ref, ct_ref, of_ref, wp_ref)


def _rope_tables(pos):
    half = HEAD_DIM // 2
    inv = np.float32(ROPE_BASE) ** (-np.arange(half, dtype=np.float32) / np.float32(half))
    ang = pos.astype(np.float32)[:, None] * inv[None, :]
    cos = np.cos(ang).astype(np.float32)
    sin = np.sin(ang).astype(np.float32)
    return np.concatenate([cos, cos], axis=-1), np.concatenate([-sin, sin], axis=-1)


def _decay_tables(block):
    log_g = np.asarray(_LOG_GAMMA, np.float32)
    idx = np.arange(block, dtype=np.float32)
    diff = idx[:, None] - idx[None, :]
    dmat = np.where(diff[None] >= 0, np.exp(np.maximum(diff, 0.0)[None] * log_g[:, None, None]), 0.0)
    q_dec = np.exp((idx + 1.0)[None, :] * log_g[:, None])
    k_dec = np.exp((block - 1.0 - idx)[None, :] * log_g[:, None])
    s_dec = np.exp(np.float32(block) * log_g)
    bc = lambda a: np.ascontiguousarray(
        np.broadcast_to(a[:, :, None], (N_HEADS, block, HEAD_DIM)).astype(np.float32))
    s_row = np.ascontiguousarray(np.broadcast_to(s_dec[:, None, None], (N_HEADS, 1, HEAD_DIM)).astype(np.float32))
    return dmat.astype(np.float32), bc(q_dec), bc(k_dec), s_row


def _routing_constants(tile):
    tri = jnp.asarray(np.arange(tile)[:, None] > np.arange(tile)[None, :], BF16)
    upper = jnp.asarray(np.arange(N_EXPERTS)[:, None] < np.arange(N_EXPERTS)[None, :], BF16)
    return tri, upper


def _front_out(n_tok, n_tiles):
    return [
        jax.ShapeDtypeStruct((n_tok, D_MODEL), F32),
        jax.ShapeDtypeStruct((n_tok * ROW_CHUNKS, LANES), U32),
        jax.ShapeDtypeStruct((n_tok, TOP_K), I32),
        jax.ShapeDtypeStruct((n_tok, TOP_K), F32),
        jax.ShapeDtypeStruct((n_tiles, 1, N_EXPERTS), F32),
        jax.ShapeDtypeStruct((n_tiles, 1, N_EXPERTS), F32),
        jax.ShapeDtypeStruct((n_tiles, 1, TILE_SLOTS), I32),
    ]


def _prompt_call(x, p):
    bsz, seq, _ = x.shape
    tile = TOKEN_TILE
    n_t = seq // tile
    cq, sq = _rope_tables(np.arange(seq))
    dmat, qdec, kdec, sdec = _decay_tables(PROMPT_RET_BLOCK)
    tri, upper = _routing_constants(tile)
    ws = p['w_s'][:, :GM_CHUNK, :GM_CHUNK]
    bst = p['b_s'][:, :GM_CHUNK].T

    n_tiles = bsz * n_t
    cur = lambda s: jnp.minimum(s, n_tiles - 1)
    prev = lambda s: jnp.maximum(s - 1, 0)
    const = lambda shape: pl.BlockSpec(shape, lambda s: (0,) * len(shape))
    once = lambda shape: pl.BlockSpec(shape, lambda s: (0,) * len(shape), pipeline_mode=pl.Buffered(1))
    pos = lambda: pl.BlockSpec((tile, HEAD_DIM), lambda s: (lax.rem(cur(s), n_t), 0))
    in_specs = [
        pl.BlockSpec((1, tile, D_MODEL), lambda s: (cur(s) // n_t, lax.rem(cur(s), n_t), 0)),
        const((1, D_MODEL)), once((D_MODEL, 6 * RET_WIDTH)),
        pos(), pos(),
        const((N_HEADS, PROMPT_RET_BLOCK, PROMPT_RET_BLOCK)),
        const((N_HEADS, PROMPT_RET_BLOCK, HEAD_DIM)), const((N_HEADS, PROMPT_RET_BLOCK, HEAD_DIM)),
        const((N_HEADS, 1, HEAD_DIM)),
        const((1, GM_WIDTH)), const((1, GM_WIDTH)),
        const((N_HEADS, GM_CHUNK, GM_CHUNK)), const((GM_CHUNK, N_HEADS)),
        once((D_MODEL, D_MODEL)), const((1, D_MODEL)),
        const((D_MODEL, N_EXPERTS)), const((1, N_EXPERTS)),
        const((tile, tile)), const((N_EXPERTS, N_EXPERTS)),
    ]
    out_specs = [
        pl.BlockSpec((tile, D_MODEL), lambda s: (cur(s), 0)),
        pl.BlockSpec((tile * ROW_CHUNKS, LANES), lambda s: (prev(s), 0)),
        pl.BlockSpec((tile, TOP_K), lambda s: (prev(s), 0)),
        pl.BlockSpec((tile, TOP_K), lambda s: (prev(s), 0)),
        pl.BlockSpec((1, 1, N_EXPERTS), lambda s: (prev(s), 0, 0)),
        pl.BlockSpec((1, 1, N_EXPERTS), lambda s: (prev(s), 0, 0)),
        pl.BlockSpec((1, 1, TILE_SLOTS), lambda s: (prev(s), 0, 0)),
        pl.BlockSpec((1, N_HEADS, HEAD_DIM, HEAD_DIM), lambda s: (cur(s) // n_t, 0, 0, 0)),
    ]
    out_shape = _front_out(bsz * seq, n_tiles) + [jax.ShapeDtypeStruct((bsz, N_HEADS, HEAD_DIM, HEAD_DIM), F32)]
    return pl.pallas_call(
        functools.partial(_prompt_kernel, ret_block=PROMPT_RET_BLOCK, tiles_per_seq=n_t, n_tiles=n_tiles),
        grid=(n_tiles + 1,),
        in_specs=in_specs,
        out_specs=out_specs,
        out_shape=out_shape,
        scratch_shapes=[
            pltpu.VMEM((N_HEADS, HEAD_DIM, HEAD_DIM), F32),
            pltpu.VMEM((tile, D_MODEL), BF16),
            pltpu.VMEM((tile, D_MODEL), F32),
            pltpu.VMEM((D_MODEL, 6 * RET_WIDTH), BF16),
            pltpu.VMEM((D_MODEL, D_MODEL), BF16),
        ],
        compiler_params=pltpu.CompilerParams(
            dimension_semantics=("arbitrary",), vmem_limit_bytes=VMEM_LIMIT),
        name="front_prompt",
    )(x, p['norm1_g'], p['w_in'], cq, sq, dmat, qdec, kdec, sdec, p['ln_v_g'], p['ln_v_b'], ws, bst,
      p['w_out'], p['norm2_g'], p['w_router'], p['b_router'], tri, upper)


def _decode_call(x, state, past_len, p):
    n_streams, frames, _ = x.shape
    tile = n_streams * frames
    assert tile == TOKEN_TILE and frames <= RET_CHUNK and frames & (frames - 1) == 0
    cq, sq = [np.tile(a, (n_streams, 1)) for a in _rope_tables(past_len + np.arange(frames))]
    _, qdec, kdec, sdec = _decay_tables(frames)
    qdec = np.tile(qdec, (1, n_streams, 1))
    kdec = np.tile(kdec, (1, n_streams, 1))
    tri, upper = _routing_constants(tile)
    ws = p['w_s'][:, :frames, :frames]
    bst = jnp.tile(p['b_s'][:, :frames].T, (n_streams, 1))

    def whole(a):
        return pl.BlockSpec(a.shape, lambda i, n=a.ndim: (0,) * n, pipeline_mode=pl.Buffered(1))

    args = (x.reshape(tile, D_MODEL), state, p['norm1_g'], p['w_in'], cq, sq, qdec, kdec, sdec,
            p['ln_v_g'], p['ln_v_b'], ws, bst, p['w_out'], p['norm2_g'], p['w_router'], p['b_router'],
            tri, upper)
    out_shape = _front_out(tile, 1) + [
        jax.ShapeDtypeStruct(state.shape, F32),
        jax.ShapeDtypeStruct((tile, GM_WIDTH), F32),
    ]
    out_specs = [pl.BlockSpec(s.shape, lambda i, n=len(s.shape): (0,) * n) for s in out_shape]
    out_specs[7] = pl.BlockSpec(memory_space=pl.ANY)
    return pl.pallas_call(
        functools.partial(_decode_kernel, n_streams=n_streams, frames=frames),
        grid=(1,),
        in_specs=[whole(a) for a in args],
        out_specs=out_specs,
        out_shape=out_shape,
        scratch_shapes=[
            pltpu.VMEM((tile, D_MODEL), BF16),
            pltpu.VMEM((N_HEADS, tile, HEAD_DIM), BF16),
            pltpu.VMEM((N_HEADS, tile, HEAD_DIM), BF16),
            pltpu.VMEM((N_HEADS, tile, HEAD_DIM), BF16),
            pltpu.VMEM((N_HEADS, tile, HEAD_DIM), F32),
            pltpu.VMEM((2, N_HEADS, HEAD_DIM, HEAD_DIM), F32),
            pltpu.SemaphoreType.DMA((2,)),
        ],
        compiler_params=pltpu.CompilerParams(
            dimension_semantics=("arbitrary",), vmem_limit_bytes=VMEM_LIMIT),
        name="front_decode",
    )(*args)


def _plan_kernel(ct_ref, lower_ref, upper_ref, cum_ref, nb_ref, bs_ref, be_ref, nbt_ref, tlo_ref, thi_ref,
                 *, n_blocks):
    counts = ct_ref[...]
    cum = _exact_count_dot(lower_ref[...], counts, counts_on_left=False)
    cum_ref[...] = cum.astype(I32)
    n_tiles = counts.shape[0]
    total = cum[n_tiles:n_tiles + 1]
    nb = jnp.floor((total + (EXPERT_ROWS - 1)) * (1.0 / EXPERT_ROWS))
    bstart = _dot(jnp.broadcast_to(nb, (8, N_EXPERTS)).astype(BF16), upper_ref[...])[0:1]
    bend = bstart + nb
    nb_ref[...] = nb.astype(I32)
    bs_ref[...] = bstart.astype(I32)
    blk = lax.broadcasted_iota(I32, (n_blocks, N_EXPERTS), 0).astype(F32)
    be = jnp.minimum(jnp.sum((bend <= blk).astype(F32), axis=-1, keepdims=True), N_EXPERTS - 1.0)
    be_ref[...] = be.astype(I32)
    nbt_ref[...] = jnp.sum(nb, axis=-1, keepdims=True).astype(I32)
    mine = lax.broadcasted_iota(I32, (n_blocks, N_EXPERTS), 1).astype(F32) == be
    pick = lambda row: jnp.sum(jnp.where(mine, row, 0.0), axis=-1, keepdims=True)
    r0 = (blk[:, 0:1] - pick(bstart)) * EXPERT_ROWS
    tlo = jnp.zeros((n_blocks, 1), F32)
    thi = jnp.zeros((n_blocks, 1), F32)
    seg_lo = pick(cum[0:1])
    for i in range(n_tiles):
        seg_hi = pick(cum[i + 1:i + 2])
        tlo = tlo + (seg_hi <= r0).astype(F32)
        thi = thi + (seg_lo < r0 + EXPERT_ROWS).astype(F32)
        seg_lo = seg_hi
    tlo_ref[...] = tlo.astype(I32)
    thi_ref[...] = thi.astype(I32)


def _plan_call(counts, n_blocks):
    n_tiles = counts.shape[0]
    lower = (jnp.arange(n_tiles + 1)[:, None] > jnp.arange(n_tiles)[None, :]).astype(BF16)
    upper = (jnp.arange(N_EXPERTS)[:, None] < jnp.arange(N_EXPERTS)[None, :]).astype(BF16)
    return pl.pallas_call(
        functools.partial(_plan_kernel, n_blocks=n_blocks),
        out_shape=[
            jax.ShapeDtypeStruct((n_tiles + 1, N_EXPERTS), I32),
            jax.ShapeDtypeStruct((1, N_EXPERTS), I32),
            jax.ShapeDtypeStruct((1, N_EXPERTS), I32),
            jax.ShapeDtypeStruct((n_blocks, 1), I32),
            jax.ShapeDtypeStruct((1, 1), I32),
            jax.ShapeDtypeStruct((n_blocks, 1), I32),
            jax.ShapeDtypeStruct((n_blocks, 1), I32),
        ],
        name="plan",
    )(counts, lower, upper)


def _dispatch_kernel(h2p_ref, h2s_ref, wp_ref, out_ref, *, n_prompt_tiles):
    i = pl.program_id(0)

    def permute(src_ref):
        def body(t, c):
            row = src_ref[_row_tile(t), :]
            for kk in range(TOP_K):
                first = pl.multiple_of(wp_ref[0, 0, kk * TOKEN_TILE + t], ROW_CHUNKS)
                out_ref[pl.ds(first, ROW_CHUNKS), :] = row
            return c

        lax.fori_loop(0, TOKEN_TILE, body, 0, unroll=32)

    @pl.when(i < n_prompt_tiles)
    def _():
        permute(h2p_ref)

    @pl.when(i >= n_prompt_tiles)
    def _():
        permute(h2s_ref)


def _dispatch_call(h2_p, h2_s, word_pos):
    rows = TOKEN_TILE * ROW_CHUNKS
    n_p_tiles = h2_p.shape[0] // rows
    n_s_tiles = h2_s.shape[0] // rows
    n_tiles = n_p_tiles + n_s_tiles
    return pl.pallas_call(
        functools.partial(_dispatch_kernel, n_prompt_tiles=n_p_tiles),
        grid=(n_tiles,),
        in_specs=[
            pl.BlockSpec((rows, LANES), lambda i: (jnp.minimum(i, n_p_tiles - 1), 0)),
            pl.BlockSpec((rows, LANES), lambda i: (jnp.maximum(i - n_p_tiles, 0), 0)),
            pl.BlockSpec((1, 1, TILE_SLOTS), lambda i: (i, 0, 0), memory_space=pltpu.SMEM),
        ],
        out_specs=pl.BlockSpec((TILE_SLOTS * ROW_CHUNKS, LANES), lambda i: (i, 0)),
        out_shape=jax.ShapeDtypeStruct((n_tiles * TILE_SLOTS * ROW_CHUNKS, LANES), U32),
        compiler_params=pltpu.CompilerParams(
            dimension_semantics=("arbitrary",), vmem_limit_bytes=VMEM_LIMIT),
        name="dispatch",
    )(h2_p, h2_s, word_pos)


def _expert_kernel(be_ref, bs_ref, nb_ref, nbt_ref, cum_ref, off_ref, tlo_ref, thi_ref,
                   xb_ref, wg_ref, wu_ref, wd_ref, bg_ref, bu_ref, bd_ref,
                   yb_ref, xbuf, ybuf, wbf, in_sem, out_sem, *, n_tiles):
    b = pl.program_id(0)
    nbt = nbt_ref[0]

    def block_rows(blk):
        e = be_ref[blk]
        r0 = (blk - bs_ref[e]) * EXPERT_ROWS
        return r0, jnp.minimum(cum_ref[n_tiles * N_EXPERTS + e] - r0, EXPERT_ROWS)

    def segments(blk, slot, fn):
        e = be_ref[blk]
        r0 = (blk - bs_ref[e]) * EXPERT_ROWS

        def body(i, c):
            seg_lo = cum_ref[i * N_EXPERTS + e]
            seg_hi = cum_ref[(i + 1) * N_EXPERTS + e]
            lo = jnp.maximum(seg_lo, r0)
            hi = jnp.minimum(seg_hi, r0 + EXPERT_ROWS)

            @pl.when(hi > lo)
            def _():
                bucket_row = i * TILE_SLOTS + off_ref[i * N_EXPERTS + e] + (lo - seg_lo)
                fn(slot, bucket_row, slot * EXPERT_ROWS + (lo - r0), hi - lo)
            return c

        lax.fori_loop(tlo_ref[blk], thi_ref[blk], body, 0)

    def words(row, n):
        return pl.ds(pl.multiple_of(row * ROW_CHUNKS, ROW_CHUNKS), n * ROW_CHUNKS)

    def gather(slot, bucket_row, buffer_row, n):
        return pltpu.make_async_copy(
            xb_ref.at[words(bucket_row, n)], xbuf.at[words(buffer_row, n)], in_sem.at[slot])

    def scatter(slot, bucket_row, buffer_row, n):
        return pltpu.make_async_copy(
            ybuf.at[words(buffer_row, n)], yb_ref.at[words(bucket_row, n)], out_sem.at[slot])

    def start_gather(blk):
        @pl.when(blk < nbt)
        def _():
            segments(blk, lax.rem(blk, 2), lambda *a: gather(*a).start())

    def wait_gather(blk):
        @pl.when(blk < nbt)
        def _():
            slot = lax.rem(blk, 2)
            gather(slot, 0, slot * EXPERT_ROWS, block_rows(blk)[1]).wait()

    def start_scatter(blk):
        @pl.when(blk < nbt)
        def _():
            segments(blk, lax.rem(blk, 2), lambda *a: scatter(*a).start())

    def wait_scatter(blk):
        @pl.when((blk >= 0) & (blk < nbt))
        def _():
            slot = lax.rem(blk, 2)
            scatter(slot, 0, slot * EXPERT_ROWS, block_rows(blk)[1]).wait()

    def mlp(base, n_rows, rows_left):
        valid = lax.broadcasted_iota(I32, (n_rows, 1), 0) < rows_left
        x = jnp.where(valid, _load_rows(xbuf, base, n_rows), 0.0)
        gt = _dot(x, wbf[0]) + bg_ref[0]
        up = _dot(x, wbf[1]) + bu_ref[0]
        gt = jnp.minimum(gt, SWIGLU_LIMIT)
        up = jnp.clip(up, -SWIGLU_LIMIT, SWIGLU_LIMIT)
        act = gt * _sigmoid(gt * SWIGLU_ALPHA) * (up + 1.0)
        _store_rows(ybuf, base, _dot(act.astype(BF16), wbf[2]) + bd_ref[0])

    def block_body(blk, carry):
        wait_gather(blk)
        start_gather(blk + 1)
        wait_scatter(blk - 2)
        rows_left = block_rows(blk)[1]
        base = lax.rem(blk, 2) * EXPERT_ROWS

        lower = 0
        for upper, passes in EXPERT_PASS_PLANS:
            @pl.when((rows_left > lower) & (rows_left <= upper))
            def _(passes=passes):
                first = 0
                for n_rows in passes:
                    mlp(base + first, n_rows, rows_left - first)
                    first += n_rows
            lower = upper

        start_scatter(blk)
        return carry

    first_block = bs_ref[b]
    n_mine = nb_ref[b]

    @pl.when(b == 0)
    def _():
        xbuf[...] = jnp.zeros_like(xbuf)
        start_gather(0)

    @pl.when(n_mine > 0)
    def _():
        wbf[0] = wg_ref[0, 0].astype(BF16)
        wbf[1] = wu_ref[0, 0].astype(BF16)
        wbf[2] = wd_ref[0, 0].astype(BF16)

    lax.fori_loop(first_block, first_block + n_mine, block_body, 0)

    @pl.when(b == N_EXPERTS - 1)
    def _():
        wait_scatter(nbt - 2)
        wait_scatter(nbt - 1)


def _expert_call(be, bstart, nb, nbt, cum_flat, off_flat, tlo, thi, buckets, p, n_tiles):
    wspec = pl.BlockSpec((1, 1, D_MODEL, D_MODEL), lambda e, *_: (0, e, 0, 0))
    bspec = pl.BlockSpec((1, 1, D_MODEL), lambda e, *_: (e, 0, 0))
    any_spec = pl.BlockSpec(memory_space=pl.ANY)
    return pl.pallas_call(
        functools.partial(_expert_kernel, n_tiles=n_tiles),
        grid_spec=pltpu.PrefetchScalarGridSpec(
            num_scalar_prefetch=8,
            grid=(N_EXPERTS,),
            in_specs=[any_spec, wspec, wspec, wspec, bspec, bspec, bspec],
            out_specs=any_spec,
            scratch_shapes=[
                pltpu.VMEM((2 * EXPERT_ROWS * ROW_CHUNKS, LANES), U32),
                pltpu.VMEM((2 * EXPERT_ROWS * ROW_CHUNKS, LANES), U32),
                pltpu.VMEM((3, D_MODEL, D_MODEL), BF16),
                pltpu.SemaphoreType.DMA((2,)),
                pltpu.SemaphoreType.DMA((2,)),
            ],
        ),
        out_shape=jax.ShapeDtypeStruct(buckets.shape, U32),
        compiler_params=pltpu.CompilerParams(
            dimension_semantics=("arbitrary",), vmem_limit_bytes=VMEM_LIMIT),
        name="experts",
    )(be, bstart, nb, nbt, cum_flat, off_flat, tlo, thi, buckets, p['w_gate'], p['w_up'], p['w_down'],
      p['b_gate'][:, None, :], p['b_up'][:, None, :], p['b_down'][:, None, :])


def _combine_kernel(x2_ref, yb_ref, lp_ref, tw_ref, gf_ref, y_ref, ybf, pick):
    for c in range(TILE_SLOTS // COMBINE_CHUNK):
        rows = slice(c * COMBINE_CHUNK, (c + 1) * COMBINE_CHUNK)
        ybf[rows, :] = _load_rows(yb_ref, c * COMBINE_CHUNK, COMBINE_CHUNK)
    slot = lax.broadcasted_iota(I32, (TOKEN_TILE, TILE_SLOTS), 1)
    lp = lp_ref[...]
    tw = tw_ref[...]
    sel = jnp.zeros((TOKEN_TILE, TILE_SLOTS), F32)
    for kk in range(TOP_K):
        sel = jnp.where(slot == lp[:, kk:kk + 1], tw[:, kk:kk + 1], sel)
    pick[...] = sel.astype(BF16)
    y_ref[...] = _rmsnorm(x2_ref[...] + _dot(pick[...], ybf[...]), gf_ref[...])


def _combine_call(x2, ybuckets, lp, tw, gf, tile_off):
    n_tok = x2.shape[0]
    n_tiles = n_tok // TOKEN_TILE
    return pl.pallas_call(
        _combine_kernel,
        grid=(n_tiles,),
        in_specs=[
            pl.BlockSpec((TOKEN_TILE, D_MODEL), lambda i: (i, 0)),
            pl.BlockSpec((TILE_SLOTS * ROW_CHUNKS, LANES), lambda i: (tile_off + i, 0)),
            pl.BlockSpec((TOKEN_TILE, TOP_K), lambda i: (i, 0)),
            pl.BlockSpec((TOKEN_TILE, TOP_K), lambda i: (i, 0)),
            pl.BlockSpec((1, D_MODEL), lambda i: (0, 0)),
        ],
        out_specs=pl.BlockSpec((TOKEN_TILE, D_MODEL), lambda i: (i, 0)),
        scratch_shapes=[
            pltpu.VMEM((TILE_SLOTS, D_MODEL), BF16),
            pltpu.VMEM((TOKEN_TILE, TILE_SLOTS), BF16),
        ],
        out_shape=jax.ShapeDtypeStruct((n_tok, D_MODEL), F32),
        compiler_params=pltpu.CompilerParams(
            dimension_semantics=("arbitrary",), vmem_limit_bytes=VMEM_LIMIT),
        name="combine",
    )(x2, ybuckets, lp, tw, gf)


def kernel(x_prompt, x_sample, state_ret, norm1_g, w_in, ln_v_g, ln_v_b, w_s, b_s, w_out, norm2_g, w_router, b_router, w_gate, b_gate, w_up, b_up, w_down, b_down, norm_f_g):
    bsz, seq, _ = x_prompt.shape
    dbsz, dseq, _ = x_sample.shape
    n_p = bsz * seq
    n_s = dbsz * dseq
    n_total = n_p + n_s
    past_len = 2048
    assert w_in.shape[0] == 1, "single layer"

    p = dict(norm1_g=norm1_g, w_in=w_in[0], ln_v_g=ln_v_g, ln_v_b=ln_v_b, w_s=w_s[0], b_s=b_s[0],
             w_out=w_out[0], norm2_g=norm2_g, w_router=w_router[0].astype(BF16), b_router=b_router,
             w_gate=w_gate, b_gate=b_gate[0], w_up=w_up, b_up=b_up[0], w_down=w_down, b_down=b_down[0])

    x2_p, h2_p, lp_p, tw_p, ct_p, of_p, wp_p, st_p = _prompt_call(x_prompt, p)
    x2_s, h2_s, lp_s, tw_s, ct_s, of_s, wp_s, st_s, vn_s = _decode_call(x_sample, state_ret[0], past_len, p)

    n_tiles = n_total // TOKEN_TILE
    word_pos = jnp.concatenate([wp_p, wp_s], axis=0)
    counts = jnp.concatenate([ct_p, ct_s], axis=0).reshape(n_tiles, N_EXPERTS)
    offsets = jnp.concatenate([of_p, of_s], axis=0).reshape(n_tiles * N_EXPERTS).astype(I32)

    n_blocks = -(-(n_total * TOP_K + N_EXPERTS * (EXPERT_ROWS - 1)) // EXPERT_ROWS)
    cum, nb, bstart, be, nbt, tlo, thi = _plan_call(counts, n_blocks)

    buckets = _dispatch_call(h2_p, h2_s, word_pos)
    ybuckets = _expert_call(be.reshape(n_blocks), bstart.reshape(N_EXPERTS), nb.reshape(N_EXPERTS), nbt.reshape(1),
                            cum.reshape((n_tiles + 1) * N_EXPERTS), offsets, tlo.reshape(n_blocks),
                            thi.reshape(n_blocks), buckets, p, n_tiles)

    y_p = _combine_call(x2_p, ybuckets, lp_p, tw_p, norm_f_g[None, :], 0)
    y_s = _combine_call(x2_s, ybuckets, lp_s, tw_s, norm_f_g[None, :], n_p // TOKEN_TILE)

    return (y_p.reshape(bsz, seq, D_MODEL), y_s.reshape(dbsz, dseq, D_MODEL),
            st_p[None], st_s[None], vn_s.reshape(1, dbsz, dseq, GM_WIDTH))
```

```python
import functools

import numpy as np
import jax
import jax.numpy as jnp
from jax import lax
from jax.experimental import pallas as pl
from jax.experimental.pallas import tpu as pltpu

F32 = jnp.float32
BF16 = jnp.bfloat16
I32 = jnp.int32
U32 = jnp.uint32

D_MODEL = 1024
RET_WIDTH = 512
N_HEADS = 4
HEAD_DIM = 128
GM_WIDTH = 512
GM_CHUNK = 128
RET_CHUNK = 64
N_EXPERTS = 32
TOP_K = 4
SWIGLU_LIMIT = 7.0
SWIGLU_ALPHA = 1.702
ROPE_BASE = 10000.0
EPS = 1e-6

TOKEN_TILE = 512
PROMPT_RET_BLOCK = 256
EXPERT_ROWS = 1024
EXPERT_PASS_PLANS = ((256, (256,)), (512, (512,)), (768, (512, 256)), (1024, (512, 512)))
TILE_SLOTS = TOKEN_TILE * TOP_K
COMBINE_CHUNK = 256
COMBINE_RING = 3
VMEM_LIMIT = 56 * 1024 * 1024
LANES = 128
ROW_CHUNKS = D_MODEL // LANES // 2
K_SCALE = HEAD_DIM ** -0.5
SPLIT = 16.0
_LOG_GAMMA = [float(np.log1p(-np.float32(2.0) ** np.float32(-5.0 - hd)).astype(np.float32)) for hd in range(N_HEADS)]


def _load_rows(ref, first_row, n_rows):
    base = first_row * ROW_CHUNKS
    lo, hi = [], []
    for c in range(ROW_CHUNKS):
        w = ref[pl.ds(base + c, n_rows, stride=ROW_CHUNKS), :]
        lo.append(pltpu.bitcast(lax.shift_left(w, jnp.uint32(16)), F32))
        hi.append(pltpu.bitcast(lax.bitwise_and(w, jnp.uint32(0xFFFF0000)), F32))
    return jnp.concatenate(lo + hi, axis=1).astype(BF16)


def _store_rows(ref, first_row, val):
    base = first_row * ROW_CHUNKS
    bits = pltpu.bitcast(val.astype(BF16).astype(F32), U32)
    half = D_MODEL // 2
    for c in range(ROW_CHUNKS):
        low = lax.shift_right_logical(bits[:, c * LANES:(c + 1) * LANES], jnp.uint32(16))
        ref[pl.ds(base + c, val.shape[0], stride=ROW_CHUNKS), :] = lax.bitwise_or(
            bits[:, half + c * LANES:half + (c + 1) * LANES], low)


def _row_tile(row):
    return pl.ds(pl.multiple_of(row * ROW_CHUNKS, ROW_CHUNKS), ROW_CHUNKS)


def _rmsnorm(x, g):
    ms = jnp.mean(x * x, axis=-1, keepdims=True)
    return x * lax.rsqrt(ms + EPS) * g


def _gelu(x):
    c = np.float32(np.sqrt(2.0 / np.pi))
    return x * (0.5 * (1.0 + jnp.tanh(c * (x + 0.044715 * (x * x * x)))))


def _sigmoid(x):
    return 1.0 / (1.0 + jnp.exp(-x))


def _dot(a, b):
    return jnp.dot(a, b, preferred_element_type=F32)


def _dot_nt(a, b):
    return lax.dot_general(a, b, (((1,), (1,)), ((), ())), preferred_element_type=F32)


def _dot_tn(a, b):
    return lax.dot_general(a, b, (((0,), (0,)), ((), ())), preferred_element_type=F32)


def _exact_count_dot(a, b, *, counts_on_left):
    cnt = a if counts_on_left else b
    hi = jnp.floor(cnt * (1.0 / SPLIT))
    lo = cnt - SPLIT * hi
    if counts_on_left:
        return SPLIT * _dot(hi.astype(BF16), b) + _dot(lo.astype(BF16), b)
    return SPLIT * _dot(a, hi.astype(BF16)) + _dot(a, lo.astype(BF16))


def _rotary(xh, cos2, sin2):
    return xh * cos2 + pltpu.roll(xh, HEAD_DIM // 2, axis=1) * sin2


def _gated_head_norm(o, gate):
    mu = jnp.mean(o, axis=-1, keepdims=True)
    oc = o - mu
    var = jnp.mean(oc * oc, axis=-1, keepdims=True)
    return ((gate * _sigmoid(gate)) * (oc * lax.rsqrt(var + EPS))).astype(BF16)


def _gmlp_inputs(u, vg, lng, lnb):
    uf = _gelu(u)
    vf = _gelu(vg)
    mu = jnp.mean(vf, axis=-1, keepdims=True)
    vc = vf - mu
    var = jnp.mean(vc * vc, axis=-1, keepdims=True)
    return uf, vc * lax.rsqrt(var + EPS) * lng + lnb


def _route_and_store(x, mix_scr, wout_ref, g2_ref, wr_ref, br_ref, tri_ref, upper_ref,
                     x2_ref, h2_ref, lp_ref, tw_ref, ct_ref, of_ref, wp_ref):
    h2 = _out_proj(x, mix_scr, wout_ref, g2_ref, x2_ref, h2_ref)
    _route(h2, wr_ref, br_ref, tri_ref, upper_ref, lp_ref, tw_ref, ct_ref, of_ref, wp_ref)


def _out_proj(x, mix_scr, wout_ref, g2_ref, x2_ref, h2_ref):
    x2 = x + _dot(mix_scr[...], wout_ref[...].astype(BF16))
    x2_ref[...] = x2
    h2 = _rmsnorm(x2, g2_ref[...])
    _store_rows(h2_ref, 0, h2)
    return h2


def _route(h2, wr_ref, br_ref, tri_ref, upper_ref, lp_ref, tw_ref, ct_ref, of_ref, wp_ref):
    _route_place(_route_choose(h2, wr_ref, br_ref), tri_ref, upper_ref, lp_ref, tw_ref, ct_ref, of_ref, wp_ref)


def _route_choose(h2, wr_ref, br_ref):
    tile = h2.shape[0]
    logits = _dot(h2.astype(BF16), wr_ref[...]) + br_ref[...]

    iota_e = lax.broadcasted_iota(I32, (tile, N_EXPERTS), 1)
    iota_ef = iota_e.astype(F32)
    lg = logits
    vals = []
    idxs = []
    for _ in range(TOP_K):
        m = jnp.max(lg, axis=-1, keepdims=True)
        idx = jnp.min(jnp.where(lg == m, iota_ef, float(N_EXPERTS)), axis=-1, keepdims=True).astype(I32)
        vals.append(m)
        idxs.append(idx)
        lg = jnp.where(iota_e == idx, -jnp.inf, lg)
    exps = [jnp.exp(vk - vals[0]) for vk in vals]
    denom = exps[0] + exps[1] + exps[2] + exps[3]
    return idxs, exps, denom


def _route_place(choice, tri_ref, upper_ref, lp_ref, tw_ref, ct_ref, of_ref, wp_ref):
    idxs, exps, denom = choice
    tile = denom.shape[0]
    iota_e = lax.broadcasted_iota(I32, (tile, N_EXPERTS), 1)
    iota_k = lax.broadcasted_iota(I32, (tile, TOP_K), 1)
    onehot = jnp.zeros((tile, N_EXPERTS), F32)
    for idx in idxs:
        onehot = onehot + (iota_e == idx).astype(F32)
    rank = _dot(tri_ref[...], onehot.astype(BF16))
    count = jnp.sum(onehot, axis=0, keepdims=True)
    offset = _exact_count_dot(jnp.broadcast_to(count, (8, N_EXPERTS)), upper_ref[...], counts_on_left=True)[0:1]
    where_to = rank + offset
    lp = jnp.zeros((tile, TOP_K), I32)
    tw = jnp.zeros((tile, TOP_K), F32)
    word_rows = []
    for kk in range(TOP_K):
        pos = jnp.sum(jnp.where(iota_e == idxs[kk], where_to, 0.0), axis=-1, keepdims=True)
        lp = jnp.where(iota_k == kk, pos.astype(I32), lp)
        tw = jnp.where(iota_k == kk, exps[kk] / denom, tw)
        word_rows.append(jnp.transpose(jnp.broadcast_to(pos * float(ROW_CHUNKS), (tile, LANES)))[0:1, :])
    lp_ref[...] = lp
    tw_ref[...] = tw
    ct_ref[0] = count
    of_ref[0] = offset
    wp_ref[0] = jnp.concatenate(word_rows, axis=1).astype(I32)


def _prompt_kernel(x_ref, g1_ref, win_ref, cq_ref, sq_ref, dmat_ref, qdec_ref, kdec_ref,
                   sdec_ref, lng_ref, lnb_ref, ws_ref, bst_ref, wout_ref, g2_ref, wr_ref, br_ref,
                   tri_ref, upper_ref,
                   x2_ref, h2_ref, lp_ref, tw_ref, ct_ref, of_ref, wp_ref, st_ref,
                   s_scr, mix_scr, x2_carry, win_bf, wout_bf, *, ret_block, tiles_per_seq, n_tiles):
    step = pl.program_id(0)

    @pl.when(step == 0)
    def _():
        s_scr[...] = jnp.zeros_like(s_scr)
        x2_carry[...] = jnp.zeros_like(x2_carry)
        win_bf[...] = win_ref[...].astype(BF16)
        wout_bf[...] = wout_ref[...].astype(BF16)

    def norm_prev_tile():
        h2_prev = _rmsnorm(x2_carry[...], g2_ref[...])
        _store_rows(h2_ref, 0, h2_prev)
        return _route_choose(h2_prev, wr_ref, br_ref)

    def place_prev_tile(choice):
        _route_place(choice, tri_ref, upper_ref, lp_ref, tw_ref, ct_ref, of_ref, wp_ref)

    @pl.when(step == n_tiles)
    def _():
        place_prev_tile(norm_prev_tile())

    @pl.when(step < n_tiles)
    def _():
        _prompt_step(step, norm_prev_tile, place_prev_tile, x_ref, g1_ref, win_bf, cq_ref, sq_ref,
                     dmat_ref, qdec_ref, kdec_ref, sdec_ref, lng_ref, lnb_ref, ws_ref, bst_ref, wout_bf,
                     x2_ref, st_ref, s_scr, mix_scr, x2_carry, ret_block=ret_block, tiles_per_seq=tiles_per_seq)


def _prompt_step(step, norm_prev_tile, place_prev_tile, x_ref, g1_ref, win_ref, cq_ref, sq_ref,
                 dmat_ref, qdec_ref, kdec_ref, sdec_ref, lng_ref, lnb_ref, ws_ref, bst_ref, wout_ref,
                 x2_ref, st_ref, s_scr, mix_scr, x2_carry, *, ret_block, tiles_per_seq):
    tile = TOKEN_TILE
    t = lax.rem(step, tiles_per_seq)

    choice = norm_prev_tile()
    s_scr[...] = jnp.where(t == 0, 0.0, s_scr[...])

    x = x_ref[0]
    h = _rmsnorm(x, g1_ref[...]).astype(BF16)

    def proj(i):
        return _dot(h, win_ref[:, i * RET_WIDTH:(i + 1) * RET_WIDTH])

    q = proj(0)
    k = proj(1)
    v = proj(2)
    gate = proj(3)
    u = proj(4)
    vg = proj(5)
    place_prev_tile(choice)
    cq = cq_ref[...]
    sq = sq_ref[...]
    ck = cq * K_SCALE
    sk = sq * K_SCALE

    for hd in range(N_HEADS):
        cols = slice(hd * HEAD_DIM, (hd + 1) * HEAD_DIM)
        qr = _rotary(q[:, cols], cq, sq)
        kr = _rotary(k[:, cols], ck, sk)
        vh = v[:, cols].astype(BF16)
        dm = dmat_ref[hd]
        qd = qdec_ref[hd]
        kd = kdec_ref[hd]
        sd = sdec_ref[hd]
        for c in range(tile // ret_block):
            rows = slice(c * ret_block, (c + 1) * ret_block)
            qb = qr[rows].astype(BF16)
            kb = kr[rows]
            vb = vh[rows]
            state = s_scr[hd]
            scores = _dot_nt(qb, kb.astype(BF16)) * dm
            o = _dot(scores.astype(BF16), vb) + qd * _dot(qb, state.astype(BF16))
            s_scr[hd] = sd * state + _dot_tn((kb * kd).astype(BF16), vb)
            mix_scr[rows, cols] = _gated_head_norm(o, gate[rows, cols])

    uf, vn = _gmlp_inputs(u, vg, lng_ref[...], lnb_ref[...])
    vnb = vn.astype(BF16)
    r_i = lax.broadcasted_iota(I32, (GM_CHUNK, GM_CHUNK), 0)
    c_i = lax.broadcasted_iota(I32, (GM_CHUNK, GM_CHUNK), 1)
    for hd in range(N_HEADS):
        cols = slice(hd * HEAD_DIM, (hd + 1) * HEAD_DIM)
        w_tril = jnp.where(r_i >= c_i, ws_ref[hd], 0.0).astype(BF16)
        bias = bst_ref[:, hd:hd + 1]
        for c in range(tile // GM_CHUNK):
            rows = slice(c * GM_CHUNK, (c + 1) * GM_CHUNK)
            mixed = _dot(w_tril, vnb[rows, cols]) + bias
            mix_scr[rows, RET_WIDTH + hd * HEAD_DIM:RET_WIDTH + (hd + 1) * HEAD_DIM] = (
                uf[rows, cols] * mixed).astype(BF16)

    x2 = x + _dot(mix_scr[...], wout_ref[...])
    x2_ref[...] = x2
    x2_carry[...] = x2

    @pl.when(t == tiles_per_seq - 1)
    def _():
        st_ref[0] = s_scr[...]


def _decode_kernel(x_ref, s0_ref, g1_ref, win_ref, cq_ref, sq_ref, qdec_ref, kdec_ref,
                   sdec_ref, lng_ref, lnb_ref, ws_ref, bst_ref, wout_ref, g2_ref, wr_ref, br_ref,
                   tri_ref, upper_ref,
                   x2_ref, h2_ref, lp_ref, tw_ref, ct_ref, of_ref, wp_ref, st_ref, vn_ref,
                   mix_scr, q_scr, k_scr, v_scr, oc_scr, st_stage, st_sem, *, n_streams, frames):
    tile = n_streams * frames
    x = x_ref[...]
    h = _rmsnorm(x, g1_ref[...]).astype(BF16)

    def proj(i):
        return _dot(h, win_ref[:, i * RET_WIDTH:(i + 1) * RET_WIDTH].astype(BF16))

    q = proj(0)
    k = proj(1)
    v = proj(2)
    gate = proj(3)
    u = proj(4)
    vg = proj(5)
    cq = cq_ref[...]
    sq = sq_ref[...]
    ck = cq * K_SCALE
    sk = sq * K_SCALE

    r_i = lax.broadcasted_iota(I32, (tile, tile), 0)
    c_i = lax.broadcasted_iota(I32, (tile, tile), 1)
    shift = frames.bit_length() - 1
    r_frame = lax.bitwise_and(r_i, frames - 1)
    c_frame = lax.bitwise_and(c_i, frames - 1)
    keep = (lax.shift_right_logical(r_i, shift) == lax.shift_right_logical(c_i, shift)) & (r_frame >= c_frame)
    frame_gap = (r_frame - c_frame).astype(F32)

    o_in = []
    for hd in range(N_HEADS):
        cols = slice(hd * HEAD_DIM, (hd + 1) * HEAD_DIM)
        qr = _rotary(q[:, cols], cq, sq)
        kr = _rotary(k[:, cols], ck, sk)
        vh = v[:, cols].astype(BF16)
        qb = qr.astype(BF16)
        decay = jnp.where(keep, jnp.exp(frame_gap * _LOG_GAMMA[hd]), 0.0)
        scores = _dot_nt(qb, kr.astype(BF16)) * decay
        o_in.append(_dot(scores.astype(BF16), vh))
        q_scr[hd] = qb
        k_scr[hd] = (kr * kdec_ref[hd]).astype(BF16)
        v_scr[hd] = vh

    def state_copy(b):
        slot = lax.rem(b, 2)
        return pltpu.make_async_copy(st_stage.at[slot], st_ref.at[b], st_sem.at[slot])

    def stream_body(b, carry):
        rows = pl.ds(pl.multiple_of(b * frames, frames), frames)
        slot = lax.rem(b, 2)

        @pl.when(b >= 2)
        def _():
            state_copy(b - 2).wait()

        for hd in range(N_HEADS):
            state = s0_ref[b, hd]
            oc_scr[hd, rows, :] = _dot(q_scr[hd, rows, :], state.astype(BF16))
            st_stage[slot, hd] = sdec_ref[hd] * state + _dot_tn(k_scr[hd, rows, :], v_scr[hd, rows, :])
        state_copy(b).start()
        return carry

    lax.fori_loop(0, n_streams, stream_body, 0)
    state_copy(n_streams - 2).wait()
    state_copy(n_streams - 1).wait()

    for hd in range(N_HEADS):
        cols = slice(hd * HEAD_DIM, (hd + 1) * HEAD_DIM)
        o = o_in[hd] + qdec_ref[hd] * oc_scr[hd]
        mix_scr[:, cols] = _gated_head_norm(o, gate[:, cols])

    uf, vn = _gmlp_inputs(u, vg, lng_ref[...], lnb_ref[...])
    vn_ref[...] = vn
    vnb = vn.astype(BF16)
    sel = (lax.bitwise_and(lax.broadcasted_iota(I32, (tile, frames), 0), frames - 1)
           == lax.broadcasted_iota(I32, (tile, frames), 1)).astype(BF16)
    for hd in range(N_HEADS):
        cols = slice(hd * HEAD_DIM, (hd + 1) * HEAD_DIM)
        w_rows = _dot(sel, ws_ref[hd].astype(BF16)).astype(BF16)
        w_blk = jnp.where(keep, _dot_nt(w_rows, sel), 0.0).astype(BF16)
        mixed = _dot(w_blk, vnb[:, cols]) + bst_ref[:, hd:hd + 1]
        mix_scr[:, RET_WIDTH + hd * HEAD_DIM:RET_WIDTH + (hd + 1) * HEAD_DIM] = (uf[:, cols] * mixed).astype(BF16)

    _route_and_store(x, mix_scr, wout_ref, g2_ref, wr_ref, br_ref, tri_ref, upper_ref,
                     x2_ref, h2_ref, lp_ref, tw_ref, ct_ref, of_ref, wp_ref)


def _rope_tables(pos):
    half = HEAD_DIM // 2
    inv = np.float32(ROPE_BASE) ** (-np.arange(half, dtype=np.float32) / np.float32(half))
    ang = pos.astype(np.float32)[:, None] * inv[None, :]
    cos = np.cos(ang).astype(np.float32)
    sin = np.sin(ang).astype(np.float32)
    return np.concatenate([cos, cos], axis=-1), np.concatenate([-sin, sin], axis=-1)


def _decay_tables(block):
    log_g = np.asarray(_LOG_GAMMA, np.float32)
    idx = np.arange(block, dtype=np.float32)
    diff = idx[:, None] - idx[None, :]
    dmat = np.where(diff[None] >= 0, np.exp(np.maximum(diff, 0.0)[None] * log_g[:, None, None]), 0.0)
    q_dec = np.exp((idx + 1.0)[None, :] * log_g[:, None])
    k_dec = np.exp((block - 1.0 - idx)[None, :] * log_g[:, None])
    s_dec = np.exp(np.float32(block) * log_g)
    bc = lambda a: np.ascontiguousarray(
        np.broadcast_to(a[:, :, None], (N_HEADS, block, HEAD_DIM)).astype(np.float32))
    s_row = np.ascontiguousarray(np.broadcast_to(s_dec[:, None, None], (N_HEADS, 1, HEAD_DIM)).astype(np.float32))
    return dmat.astype(np.float32), bc(q_dec), bc(k_dec), s_row


def _routing_constants(tile):
    tri = jnp.asarray(np.arange(tile)[:, None] > np.arange(tile)[None, :], BF16)
    upper = jnp.asarray(np.arange(N_EXPERTS)[:, None] < np.arange(N_EXPERTS)[None, :], BF16)
    return tri, upper


def _front_out(n_tok, n_tiles):
    return [
        jax.ShapeDtypeStruct((n_tok, D_MODEL), F32),
        jax.ShapeDtypeStruct((n_tok * ROW_CHUNKS, LANES), U32),
        jax.ShapeDtypeStruct((n_tok, TOP_K), I32),
        jax.ShapeDtypeStruct((n_tok, TOP_K), F32),
        jax.ShapeDtypeStruct((n_tiles, 1, N_EXPERTS), F32),
        jax.ShapeDtypeStruct((n_tiles, 1, N_EXPERTS), F32),
        jax.ShapeDtypeStruct((n_tiles, 1, TILE_SLOTS), I32),
    ]


def _prompt_call(x, p):
    bsz, seq, _ = x.shape
    tile = TOKEN_TILE
    n_t = seq // tile
    cq, sq = _rope_tables(np.arange(seq))
    dmat, qdec, kdec, sdec = _decay_tables(PROMPT_RET_BLOCK)
    tri, upper = _routing_constants(tile)
    ws = p['w_s'][:, :GM_CHUNK, :GM_CHUNK]
    bst = p['b_s'][:, :GM_CHUNK].T

    n_tiles = bsz * n_t
    cur = lambda s: jnp.minimum(s, n_tiles - 1)
    prev = lambda s: jnp.maximum(s - 1, 0)
    const = lambda shape: pl.BlockSpec(shape, lambda s: (0,) * len(shape))
    once = lambda shape: pl.BlockSpec(shape, lambda s: (0,) * len(shape), pipeline_mode=pl.Buffered(1))
    pos = lambda: pl.BlockSpec((tile, HEAD_DIM), lambda s: (lax.rem(cur(s), n_t), 0))
    in_specs = [
        pl.BlockSpec((1, tile, D_MODEL), lambda s: (cur(s) // n_t, lax.rem(cur(s), n_t), 0)),
        const((1, D_MODEL)), once((D_MODEL, 6 * RET_WIDTH)),
        pos(), pos(),
        const((N_HEADS, PROMPT_RET_BLOCK, PROMPT_RET_BLOCK)),
        const((N_HEADS, PROMPT_RET_BLOCK, HEAD_DIM)), const((N_HEADS, PROMPT_RET_BLOCK, HEAD_DIM)),
        const((N_HEADS, 1, HEAD_DIM)),
        const((1, GM_WIDTH)), const((1, GM_WIDTH)),
        const((N_HEADS, GM_CHUNK, GM_CHUNK)), const((GM_CHUNK, N_HEADS)),
        once((D_MODEL, D_MODEL)), const((1, D_MODEL)),
        const((D_MODEL, N_EXPERTS)), const((1, N_EXPERTS)),
        const((tile, tile)), const((N_EXPERTS, N_EXPERTS)),
    ]
    out_specs = [
        pl.BlockSpec((tile, D_MODEL), lambda s: (cur(s), 0)),
        pl.BlockSpec((tile * ROW_CHUNKS, LANES), lambda s: (prev(s), 0)),
        pl.BlockSpec((tile, TOP_K), lambda s: (prev(s), 0)),
        pl.BlockSpec((tile, TOP_K), lambda s: (prev(s), 0)),
        pl.BlockSpec((1, 1, N_EXPERTS), lambda s: (prev(s), 0, 0)),
        pl.BlockSpec((1, 1, N_EXPERTS), lambda s: (prev(s), 0, 0)),
        pl.BlockSpec((1, 1, TILE_SLOTS), lambda s: (prev(s), 0, 0)),
        pl.BlockSpec((1, N_HEADS, HEAD_DIM, HEAD_DIM), lambda s: (cur(s) // n_t, 0, 0, 0)),
    ]
    out_shape = _front_out(bsz * seq, n_tiles) + [jax.ShapeDtypeStruct((bsz, N_HEADS, HEAD_DIM, HEAD_DIM), F32)]
    return pl.pallas_call(
        functools.partial(_prompt_kernel, ret_block=PROMPT_RET_BLOCK, tiles_per_seq=n_t, n_tiles=n_tiles),
        grid=(n_tiles + 1,),
        in_specs=in_specs,
        out_specs=out_specs,
        out_shape=out_shape,
        scratch_shapes=[
            pltpu.VMEM((N_HEADS, HEAD_DIM, HEAD_DIM), F32),
            pltpu.VMEM((tile, D_MODEL), BF16),
            pltpu.VMEM((tile, D_MODEL), F32),
            pltpu.VMEM((D_MODEL, 6 * RET_WIDTH), BF16),
            pltpu.VMEM((D_MODEL, D_MODEL), BF16),
        ],
        compiler_params=pltpu.CompilerParams(
            dimension_semantics=("arbitrary",), vmem_limit_bytes=VMEM_LIMIT),
        name="front_prompt",
    )(x, p['norm1_g'], p['w_in'], cq, sq, dmat, qdec, kdec, sdec, p['ln_v_g'], p['ln_v_b'], ws, bst,
      p['w_out'], p['norm2_g'], p['w_router'], p['b_router'], tri, upper)


def _decode_call(x, state, past_len, p):
    n_streams, frames, _ = x.shape
    tile = n_streams * frames
    assert tile == TOKEN_TILE and frames <= RET_CHUNK and frames & (frames - 1) == 0
    cq, sq = [np.tile(a, (n_streams, 1)) for a in _rope_tables(past_len + np.arange(frames))]
    _, qdec, kdec, sdec = _decay_tables(frames)
    qdec = np.tile(qdec, (1, n_streams, 1))
    kdec = np.tile(kdec, (1, n_streams, 1))
    tri, upper = _routing_constants(tile)
    ws = p['w_s'][:, :frames, :frames]
    bst = jnp.tile(p['b_s'][:, :frames].T, (n_streams, 1))

    def whole(a):
        return pl.BlockSpec(a.shape, lambda i, n=a.ndim: (0,) * n, pipeline_mode=pl.Buffered(1))

    args = (x.reshape(tile, D_MODEL), state, p['norm1_g'], p['w_in'], cq, sq, qdec, kdec, sdec,
            p['ln_v_g'], p['ln_v_b'], ws, bst, p['w_out'], p['norm2_g'], p['w_router'], p['b_router'],
            tri, upper)
    out_shape = _front_out(tile, 1) + [
        jax.ShapeDtypeStruct(state.shape, F32),
        jax.ShapeDtypeStruct((tile, GM_WIDTH), F32),
    ]
    out_specs = [pl.BlockSpec(s.shape, lambda i, n=len(s.shape): (0,) * n) for s in out_shape]
    out_specs[7] = pl.BlockSpec(memory_space=pl.ANY)
    return pl.pallas_call(
        functools.partial(_decode_kernel, n_streams=n_streams, frames=frames),
        grid=(1,),
        in_specs=[whole(a) for a in args],
        out_specs=out_specs,
        out_shape=out_shape,
        scratch_shapes=[
            pltpu.VMEM((tile, D_MODEL), BF16),
            pltpu.VMEM((N_HEADS, tile, HEAD_DIM), BF16),
            pltpu.VMEM((N_HEADS, tile, HEAD_DIM), BF16),
            pltpu.VMEM((N_HEADS, tile, HEAD_DIM), BF16),
            pltpu.VMEM((N_HEADS, tile, HEAD_DIM), F32),
            pltpu.VMEM((2, N_HEADS, HEAD_DIM, HEAD_DIM), F32),
            pltpu.SemaphoreType.DMA((2,)),
        ],
        compiler_params=pltpu.CompilerParams(
            dimension_semantics=("arbitrary",), vmem_limit_bytes=VMEM_LIMIT),
        name="front_decode",
    )(*args)


def _plan_kernel(ct_ref, lower_ref, upper_ref, cum_ref, nb_ref, bs_ref, be_ref, nbt_ref, tlo_ref, thi_ref,
                 *, n_blocks):
    counts = ct_ref[...]
    cum = _exact_count_dot(lower_ref[...], counts, counts_on_left=False)
    cum_ref[...] = cum.astype(I32)
    n_tiles = counts.shape[0]
    total = cum[n_tiles:n_tiles + 1]
    nb = jnp.floor((total + (EXPERT_ROWS - 1)) * (1.0 / EXPERT_ROWS))
    bstart = _dot(jnp.broadcast_to(nb, (8, N_EXPERTS)).astype(BF16), upper_ref[...])[0:1]
    bend = bstart + nb
    nb_ref[...] = nb.astype(I32)
    bs_ref[...] = bstart.astype(I32)
    blk = lax.broadcasted_iota(I32, (n_blocks, N_EXPERTS), 0).astype(F32)
    be = jnp.minimum(jnp.sum((bend <= blk).astype(F32), axis=-1, keepdims=True), N_EXPERTS - 1.0)
    be_ref[...] = be.astype(I32)
    nbt_ref[...] = jnp.sum(nb, axis=-1, keepdims=True).astype(I32)
    mine = lax.broadcasted_iota(I32, (n_blocks, N_EXPERTS), 1).astype(F32) == be
    pick = lambda row: jnp.sum(jnp.where(mine, row, 0.0), axis=-1, keepdims=True)
    r0 = (blk[:, 0:1] - pick(bstart)) * EXPERT_ROWS
    tlo = jnp.zeros((n_blocks, 1), F32)
    thi = jnp.zeros((n_blocks, 1), F32)
    seg_lo = pick(cum[0:1])
    for i in range(n_tiles):
        seg_hi = pick(cum[i + 1:i + 2])
        tlo = tlo + (seg_hi <= r0).astype(F32)
        thi = thi + (seg_lo < r0 + EXPERT_ROWS).astype(F32)
        seg_lo = seg_hi
    tlo_ref[...] = tlo.astype(I32)
    thi_ref[...] = thi.astype(I32)


def _plan_call(counts, n_blocks):
    n_tiles = counts.shape[0]
    lower = (jnp.arange(n_tiles + 1)[:, None] > jnp.arange(n_tiles)[None, :]).astype(BF16)
    upper = (jnp.arange(N_EXPERTS)[:, None] < jnp.arange(N_EXPERTS)[None, :]).astype(BF16)
    return pl.pallas_call(
        functools.partial(_plan_kernel, n_blocks=n_blocks),
        out_shape=[
            jax.ShapeDtypeStruct((n_tiles + 1, N_EXPERTS), I32),
            jax.ShapeDtypeStruct((1, N_EXPERTS), I32),
            jax.ShapeDtypeStruct((1, N_EXPERTS), I32),
            jax.ShapeDtypeStruct((n_blocks, 1), I32),
            jax.ShapeDtypeStruct((1, 1), I32),
            jax.ShapeDtypeStruct((n_blocks, 1), I32),
            jax.ShapeDtypeStruct((n_blocks, 1), I32),
        ],
        name="plan",
    )(counts, lower, upper)


def _dispatch_kernel(h2p_ref, h2s_ref, wp_ref, out_ref, *, n_prompt_tiles):
    i = pl.program_id(0)

    def permute(src_ref):
        def body(t, c):
            row = src_ref[_row_tile(t), :]
            for kk in range(TOP_K):
                first = pl.multiple_of(wp_ref[0, 0, kk * TOKEN_TILE + t], ROW_CHUNKS)
                out_ref[pl.ds(first, ROW_CHUNKS), :] = row
            return c

        lax.fori_loop(0, TOKEN_TILE, body, 0, unroll=32)

    @pl.when(i < n_prompt_tiles)
    def _():
        permute(h2p_ref)

    @pl.when(i >= n_prompt_tiles)
    def _():
        permute(h2s_ref)


def _dispatch_call(h2_p, h2_s, word_pos):
    rows = TOKEN_TILE * ROW_CHUNKS
    n_p_tiles = h2_p.shape[0] // rows
    n_s_tiles = h2_s.shape[0] // rows
    n_tiles = n_p_tiles + n_s_tiles
    return pl.pallas_call(
        functools.partial(_dispatch_kernel, n_prompt_tiles=n_p_tiles),
        grid=(n_tiles,),
        in_specs=[
            pl.BlockSpec((rows, LANES), lambda i: (jnp.minimum(i, n_p_tiles - 1), 0)),
            pl.BlockSpec((rows, LANES), lambda i: (jnp.maximum(i - n_p_tiles, 0), 0)),
            pl.BlockSpec((1, 1, TILE_SLOTS), lambda i: (i, 0, 0), memory_space=pltpu.SMEM),
        ],
        out_specs=pl.BlockSpec((TILE_SLOTS * ROW_CHUNKS, LANES), lambda i: (i, 0)),
        out_shape=jax.ShapeDtypeStruct((n_tiles * TILE_SLOTS * ROW_CHUNKS, LANES), U32),
        compiler_params=pltpu.CompilerParams(
            dimension_semantics=("arbitrary",), vmem_limit_bytes=VMEM_LIMIT),
        name="dispatch",
    )(h2_p, h2_s, word_pos)


def _expert_kernel(be_ref, bs_ref, nb_ref, nbt_ref, cum_ref, off_ref, tlo_ref, thi_ref,
                   xb_ref, wg_ref, wu_ref, wd_ref, bg_ref, bu_ref, bd_ref,
                   yb_ref, xbuf, ybuf, wbf, in_sem, out_sem, *, n_tiles):
    b = pl.program_id(0)
    nbt = nbt_ref[0]

    def block_rows(blk):
        e = be_ref[blk]
        r0 = (blk - bs_ref[e]) * EXPERT_ROWS
        return r0, jnp.minimum(cum_ref[n_tiles * N_EXPERTS + e] - r0, EXPERT_ROWS)

    def segments(blk, slot, fn):
        e = be_ref[blk]
        r0 = (blk - bs_ref[e]) * EXPERT_ROWS

        def body(i, c):
            seg_lo = cum_ref[i * N_EXPERTS + e]
            seg_hi = cum_ref[(i + 1) * N_EXPERTS + e]
            lo = jnp.maximum(seg_lo, r0)
            hi = jnp.minimum(seg_hi, r0 + EXPERT_ROWS)

            @pl.when(hi > lo)
            def _():
                bucket_row = i * TILE_SLOTS + off_ref[i * N_EXPERTS + e] + (lo - seg_lo)
                fn(slot, bucket_row, slot * EXPERT_ROWS + (lo - r0), hi - lo)
            return c

        lax.fori_loop(tlo_ref[blk], thi_ref[blk], body, 0)

    def words(row, n):
        return pl.ds(pl.multiple_of(row * ROW_CHUNKS, ROW_CHUNKS), n * ROW_CHUNKS)

    def gather(slot, bucket_row, buffer_row, n):
        return pltpu.make_async_copy(
            xb_ref.at[words(bucket_row, n)], xbuf.at[words(buffer_row, n)], in_sem.at[slot])

    def scatter(slot, bucket_row, buffer_row, n):
        return pltpu.make_async_copy(
            ybuf.at[words(buffer_row, n)], yb_ref.at[words(bucket_row, n)], out_sem.at[slot])

    def start_gather(blk):
        @pl.when(blk < nbt)
        def _():
            segments(blk, lax.rem(blk, 2), lambda *a: gather(*a).start())

    def wait_gather(blk):
        @pl.when(blk < nbt)
        def _():
            slot = lax.rem(blk, 2)
            gather(slot, 0, slot * EXPERT_ROWS, block_rows(blk)[1]).wait()

    def start_scatter(blk):
        @pl.when(blk < nbt)
        def _():
            segments(blk, lax.rem(blk, 2), lambda *a: scatter(*a).start())

    def wait_scatter(blk):
        @pl.when((blk >= 0) & (blk < nbt))
        def _():
            slot = lax.rem(blk, 2)
            scatter(slot, 0, slot * EXPERT_ROWS, block_rows(blk)[1]).wait()

    def mlp(base, n_rows, rows_left):
        valid = lax.broadcasted_iota(I32, (n_rows, 1), 0) < rows_left
        x = jnp.where(valid, _load_rows(xbuf, base, n_rows), 0.0)
        mine = pl.ds(b, 1)
        gt = _dot(x, wbf[0]) + bg_ref[0, mine, :]
        up = _dot(x, wbf[1]) + bu_ref[0, mine, :]
        gt = jnp.minimum(gt, SWIGLU_LIMIT)
        up = jnp.clip(up, -SWIGLU_LIMIT, SWIGLU_LIMIT)
        act = gt * _sigmoid(gt * SWIGLU_ALPHA) * (up + 1.0)
        _store_rows(ybuf, base, _dot(act.astype(BF16), wbf[2]) + bd_ref[0, mine, :])

    def block_body(blk, carry):
        wait_gather(blk)
        start_gather(blk + 1)
        wait_scatter(blk - 2)
        rows_left = block_rows(blk)[1]
        base = lax.rem(blk, 2) * EXPERT_ROWS

        lower = 0
        for upper, passes in EXPERT_PASS_PLANS:
            @pl.when((rows_left > lower) & (rows_left <= upper))
            def _(passes=passes):
                first = 0
                for n_rows in passes:
                    mlp(base + first, n_rows, rows_left - first)
                    first += n_rows
            lower = upper

        start_scatter(blk)
        return carry

    first_block = bs_ref[b]
    n_mine = nb_ref[b]

    @pl.when(b == 0)
    def _():
        xbuf[...] = jnp.zeros_like(xbuf)
        start_gather(0)

    @pl.when(n_mine > 0)
    def _():
        wbf[0] = wg_ref[0, 0].astype(BF16)
        wbf[1] = wu_ref[0, 0].astype(BF16)
        wbf[2] = wd_ref[0, 0].astype(BF16)

    lax.fori_loop(first_block, first_block + n_mine, block_body, 0)

    @pl.when(b == N_EXPERTS - 1)
    def _():
        wait_scatter(nbt - 2)
        wait_scatter(nbt - 1)


def _expert_call(be, bstart, nb, nbt, cum_flat, off_flat, tlo, thi, buckets, p, n_tiles):
    wspec = pl.BlockSpec((1, 1, D_MODEL, D_MODEL), lambda e, *_: (0, e, 0, 0))
    bspec = pl.BlockSpec((1, N_EXPERTS, D_MODEL), lambda e, *_: (0, 0, 0))
    any_spec = pl.BlockSpec(memory_space=pl.ANY)
    return pl.pallas_call(
        functools.partial(_expert_kernel, n_tiles=n_tiles),
        grid_spec=pltpu.PrefetchScalarGridSpec(
            num_scalar_prefetch=8,
            grid=(N_EXPERTS,),
            in_specs=[any_spec, wspec, wspec, wspec, bspec, bspec, bspec],
            out_specs=any_spec,
            scratch_shapes=[
                pltpu.VMEM((2 * EXPERT_ROWS * ROW_CHUNKS, LANES), U32),
                pltpu.VMEM((2 * EXPERT_ROWS * ROW_CHUNKS, LANES), U32),
                pltpu.VMEM((3, D_MODEL, D_MODEL), BF16),
                pltpu.SemaphoreType.DMA((2,)),
                pltpu.SemaphoreType.DMA((2,)),
            ],
        ),
        out_shape=jax.ShapeDtypeStruct(buckets.shape, U32),
        compiler_params=pltpu.CompilerParams(
            dimension_semantics=("arbitrary",), vmem_limit_bytes=VMEM_LIMIT),
        name="experts",
    )(be, bstart, nb, nbt, cum_flat, off_flat, tlo, thi, buckets, p['w_gate'], p['w_up'], p['w_down'],
      p['b_gate'], p['b_up'], p['b_down'])


def _combine_kernel(x2_ref, yb_ref, lp_ref, tw_ref, gf_ref, y_ref, ybf, pick, ring, ring_sem, *, tile_off, n_tiles):
    i = pl.program_id(0)
    tile_words = TILE_SLOTS * ROW_CHUNKS

    def bucket_copy(j):
        slot = lax.rem(j, COMBINE_RING)
        src = yb_ref.at[pl.ds(pl.multiple_of((tile_off + j) * tile_words, tile_words), tile_words)]
        dst = ring.at[pl.ds(pl.multiple_of(slot * tile_words, tile_words), tile_words)]
        return pltpu.make_async_copy(src, dst, ring_sem.at[slot])

    @pl.when(i == 0)
    def _():
        for j in range(min(COMBINE_RING - 1, n_tiles)):
            bucket_copy(j).start()

    bucket_copy(i).wait()

    @pl.when(i + (COMBINE_RING - 1) < n_tiles)
    def _():
        bucket_copy(i + (COMBINE_RING - 1)).start()

    first = lax.rem(i, COMBINE_RING) * TILE_SLOTS
    for c in range(TILE_SLOTS // COMBINE_CHUNK):
        rows = slice(c * COMBINE_CHUNK, (c + 1) * COMBINE_CHUNK)
        ybf[rows, :] = _load_rows(ring, first + c * COMBINE_CHUNK, COMBINE_CHUNK)
    slot = lax.broadcasted_iota(I32, (TOKEN_TILE, TILE_SLOTS), 1)
    lp = lp_ref[...]
    tw = tw_ref[...]
    sel = jnp.zeros((TOKEN_TILE, TILE_SLOTS), F32)
    for kk in range(TOP_K):
        sel = jnp.where(slot == lp[:, kk:kk + 1], tw[:, kk:kk + 1], sel)
    pick[...] = sel.astype(BF16)
    y_ref[...] = _rmsnorm(x2_ref[...] + _dot(pick[...], ybf[...]), gf_ref[...])


def _combine_call(x2, ybuckets, lp, tw, gf, tile_off):
    n_tok = x2.shape[0]
    n_tiles = n_tok // TOKEN_TILE
    return pl.pallas_call(
        functools.partial(_combine_kernel, tile_off=tile_off, n_tiles=n_tiles),
        grid=(n_tiles,),
        in_specs=[
            pl.BlockSpec((TOKEN_TILE, D_MODEL), lambda i: (i, 0)),
            pl.BlockSpec(memory_space=pl.ANY),
            pl.BlockSpec((TOKEN_TILE, TOP_K), lambda i: (i, 0)),
            pl.BlockSpec((TOKEN_TILE, TOP_K), lambda i: (i, 0)),
            pl.BlockSpec((1, D_MODEL), lambda i: (0, 0)),
        ],
        out_specs=pl.BlockSpec((TOKEN_TILE, D_MODEL), lambda i: (i, 0)),
        scratch_shapes=[
            pltpu.VMEM((TILE_SLOTS, D_MODEL), BF16),
            pltpu.VMEM((TOKEN_TILE, TILE_SLOTS), BF16),
            pltpu.VMEM((COMBINE_RING * TILE_SLOTS * ROW_CHUNKS, LANES), U32),
            pltpu.SemaphoreType.DMA((COMBINE_RING,)),
        ],
        out_shape=jax.ShapeDtypeStruct((n_tok, D_MODEL), F32),
        compiler_params=pltpu.CompilerParams(
            dimension_semantics=("arbitrary",), vmem_limit_bytes=VMEM_LIMIT),
        name="combine",
    )(x2, ybuckets, lp, tw, gf)


def kernel(x_prompt, x_sample, state_ret, norm1_g, w_in, ln_v_g, ln_v_b, w_s, b_s, w_out, norm2_g, w_router, b_router, w_gate, b_gate, w_up, b_up, w_down, b_down, norm_f_g):
    bsz, seq, _ = x_prompt.shape
    dbsz, dseq, _ = x_sample.shape
    n_p = bsz * seq
    n_s = dbsz * dseq
    n_total = n_p + n_s
    past_len = 2048
    assert w_in.shape[0] == 1, "single layer"

    p = dict(norm1_g=norm1_g, w_in=w_in[0], ln_v_g=ln_v_g, ln_v_b=ln_v_b, w_s=w_s[0], b_s=b_s[0],
             w_out=w_out[0], norm2_g=norm2_g, w_router=w_router[0].astype(BF16), b_router=b_router,
             w_gate=w_gate, b_gate=b_gate, w_up=w_up, b_up=b_up, w_down=w_down, b_down=b_down)

    x2_p, h2_p, lp_p, tw_p, ct_p, of_p, wp_p, st_p = _prompt_call(x_prompt, p)
    x2_s, h2_s, lp_s, tw_s, ct_s, of_s, wp_s, st_s, vn_s = _decode_call(x_sample, state_ret[0], past_len, p)

    n_tiles = n_total // TOKEN_TILE
    word_pos = jnp.concatenate([wp_p, wp_s], axis=0)
    counts = jnp.concatenate([ct_p, ct_s], axis=0).reshape(n_tiles, N_EXPERTS)
    offsets = jnp.concatenate([of_p, of_s], axis=0).reshape(n_tiles * N_EXPERTS).astype(I32)

    n_blocks = -(-(n_total * TOP_K + N_EXPERTS * (EXPERT_ROWS - 1)) // EXPERT_ROWS)
    cum, nb, bstart, be, nbt, tlo, thi = _plan_call(counts, n_blocks)

    buckets = _dispatch_call(h2_p, h2_s, word_pos)
    ybuckets = _expert_call(be.reshape(n_blocks), bstart.reshape(N_EXPERTS), nb.reshape(N_EXPERTS), nbt.reshape(1),
                            cum.reshape((n_tiles + 1) * N_EXPERTS), offsets, tlo.reshape(n_blocks),
                            thi.reshape(n_blocks), buckets, p, n_tiles)

    y_p = _combine_call(x2_p, ybuckets, lp_p, tw_p, norm_f_g[None, :], 0)
    y_s = _combine_call(x2_s, ybuckets, lp_s, tw_s, norm_f_g[None, :], n_p // TOKEN_TILE)

    return (y_p.reshape(bsz, seq, D_MODEL), y_s.reshape(dbsz, dseq, D_MODEL),
            st_p[None], st_s[None], vn_s.reshape(1, dbsz, dseq, GM_WIDTH))
```

```python
import functools

import numpy as np
import jax
import jax.numpy as jnp
from jax import lax
from jax.experimental import pallas as pl
from jax.experimental.pallas import tpu as pltpu

F32 = jnp.float32
BF16 = jnp.bfloat16
I32 = jnp.int32
U32 = jnp.uint32

D_MODEL = 1024
RET_WIDTH = 512
N_HEADS = 4
HEAD_DIM = 128
GM_WIDTH = 512
GM_CHUNK = 128
RET_CHUNK = 64
N_EXPERTS = 32
TOP_K = 4
SWIGLU_LIMIT = 7.0
SWIGLU_ALPHA = 1.702
ROPE_BASE = 10000.0
EPS = 1e-6

TOKEN_TILE = 512
PROMPT_RET_BLOCK = 256
EXPERT_ROWS = 1024
EXPERT_PASS_PLANS = ((256, (256,)), (512, (512,)), (768, (512, 256)), (1024, (512, 512)))
TILE_SLOTS = TOKEN_TILE * TOP_K
COMBINE_CHUNK = 256
COMBINE_RING = 3
DISPATCH_RING = 3
VMEM_LIMIT = 56 * 1024 * 1024
LANES = 128
ROW_CHUNKS = D_MODEL // LANES // 2
K_SCALE = HEAD_DIM ** -0.5
SPLIT = 16.0
_LOG_GAMMA = [float(np.log1p(-np.float32(2.0) ** np.float32(-5.0 - hd)).astype(np.float32)) for hd in range(N_HEADS)]


def _load_rows(ref, first_row, n_rows):
    base = first_row * ROW_CHUNKS
    lo, hi = [], []
    for c in range(ROW_CHUNKS):
        w = ref[pl.ds(base + c, n_rows, stride=ROW_CHUNKS), :]
        lo.append(pltpu.bitcast(lax.shift_left(w, jnp.uint32(16)), F32))
        hi.append(pltpu.bitcast(lax.bitwise_and(w, jnp.uint32(0xFFFF0000)), F32))
    return jnp.concatenate(lo + hi, axis=1).astype(BF16)


def _store_rows(ref, first_row, val):
    base = first_row * ROW_CHUNKS
    bits = pltpu.bitcast(val.astype(BF16).astype(F32), U32)
    half = D_MODEL // 2
    for c in range(ROW_CHUNKS):
        low = lax.shift_right_logical(bits[:, c * LANES:(c + 1) * LANES], jnp.uint32(16))
        ref[pl.ds(base + c, val.shape[0], stride=ROW_CHUNKS), :] = lax.bitwise_or(
            bits[:, half + c * LANES:half + (c + 1) * LANES], low)


def _row_tile(row):
    return pl.ds(pl.multiple_of(row * ROW_CHUNKS, ROW_CHUNKS), ROW_CHUNKS)


def _rmsnorm(x, g):
    ms = jnp.mean(x * x, axis=-1, keepdims=True)
    return x * lax.rsqrt(ms + EPS) * g


def _gelu(x):
    c = np.float32(np.sqrt(2.0 / np.pi))
    return x * (0.5 * (1.0 + jnp.tanh(c * (x + 0.044715 * (x * x * x)))))


def _sigmoid(x):
    return 1.0 / (1.0 + jnp.exp(-x))


def _dot(a, b):
    return jnp.dot(a, b, preferred_element_type=F32)


def _dot_nt(a, b):
    return lax.dot_general(a, b, (((1,), (1,)), ((), ())), preferred_element_type=F32)


def _dot_tn(a, b):
    return lax.dot_general(a, b, (((0,), (0,)), ((), ())), preferred_element_type=F32)


def _exact_count_dot(a, b, *, counts_on_left):
    cnt = a if counts_on_left else b
    hi = jnp.floor(cnt * (1.0 / SPLIT))
    lo = cnt - SPLIT * hi
    if counts_on_left:
        return SPLIT * _dot(hi.astype(BF16), b) + _dot(lo.astype(BF16), b)
    return SPLIT * _dot(a, hi.astype(BF16)) + _dot(a, lo.astype(BF16))


def _rotary(xh, cos2, sin2):
    return xh * cos2 + pltpu.roll(xh, HEAD_DIM // 2, axis=1) * sin2


def _gated_head_norm(o, gate):
    mu = jnp.mean(o, axis=-1, keepdims=True)
    oc = o - mu
    var = jnp.mean(oc * oc, axis=-1, keepdims=True)
    return ((gate * _sigmoid(gate)) * (oc * lax.rsqrt(var + EPS))).astype(BF16)


def _gmlp_inputs(u, vg, lng, lnb):
    uf = _gelu(u)
    vf = _gelu(vg)
    mu = jnp.mean(vf, axis=-1, keepdims=True)
    vc = vf - mu
    var = jnp.mean(vc * vc, axis=-1, keepdims=True)
    return uf, vc * lax.rsqrt(var + EPS) * lng + lnb


def _route_and_store(x, mix_scr, wout_ref, g2_ref, wr_ref, br_ref, tri_ref, upper_ref,
                     x2_ref, h2_ref, lp_ref, tw_ref, ct_ref, of_ref, wp_ref):
    h2 = _out_proj(x, mix_scr, wout_ref, g2_ref, x2_ref, h2_ref)
    _route(h2, wr_ref, br_ref, tri_ref, upper_ref, lp_ref, tw_ref, ct_ref, of_ref, wp_ref)


def _out_proj(x, mix_scr, wout_ref, g2_ref, x2_ref, h2_ref):
    x2 = x + _dot(mix_scr[...], wout_ref[...].astype(BF16))
    x2_ref[...] = x2
    h2 = _rmsnorm(x2, g2_ref[...])
    _store_rows(h2_ref, 0, h2)
    return h2


def _route(h2, wr_ref, br_ref, tri_ref, upper_ref, lp_ref, tw_ref, ct_ref, of_ref, wp_ref):
    _route_place(_route_choose(h2, wr_ref, br_ref), tri_ref, upper_ref, lp_ref, tw_ref, ct_ref, of_ref, wp_ref)


def _route_choose(h2, wr_ref, br_ref):
    tile = h2.shape[0]
    logits = _dot(h2.astype(BF16), wr_ref[...]) + br_ref[...]

    iota_e = lax.broadcasted_iota(I32, (tile, N_EXPERTS), 1)
    iota_ef = iota_e.astype(F32)
    lg = logits
    vals = []
    idxs = []
    for _ in range(TOP_K):
        m = jnp.max(lg, axis=-1, keepdims=True)
        idx = jnp.min(jnp.where(lg == m, iota_ef, float(N_EXPERTS)), axis=-1, keepdims=True).astype(I32)
        vals.append(m)
        idxs.append(idx)
        lg = jnp.where(iota_e == idx, -jnp.inf, lg)
    exps = [jnp.exp(vk - vals[0]) for vk in vals]
    denom = exps[0] + exps[1] + exps[2] + exps[3]
    return idxs, exps, denom


def _route_place(choice, tri_ref, upper_ref, lp_ref, tw_ref, ct_ref, of_ref, wp_ref):
    idxs, exps, denom = choice
    tile = denom.shape[0]
    iota_e = lax.broadcasted_iota(I32, (tile, N_EXPERTS), 1)
    iota_k = lax.broadcasted_iota(I32, (tile, TOP_K), 1)
    onehot = jnp.zeros((tile, N_EXPERTS), F32)
    for idx in idxs:
        onehot = onehot + (iota_e == idx).astype(F32)
    rank = _dot(tri_ref[...], onehot.astype(BF16))
    count = jnp.sum(onehot, axis=0, keepdims=True)
    offset = _exact_count_dot(jnp.broadcast_to(count, (8, N_EXPERTS)), upper_ref[...], counts_on_left=True)[0:1]
    where_to = rank + offset
    lp = jnp.zeros((tile, TOP_K), I32)
    tw = jnp.zeros((tile, TOP_K), F32)
    word_rows = []
    for kk in range(TOP_K):
        pos = jnp.sum(jnp.where(iota_e == idxs[kk], where_to, 0.0), axis=-1, keepdims=True)
        lp = jnp.where(iota_k == kk, pos.astype(I32), lp)
        tw = jnp.where(iota_k == kk, exps[kk] / denom, tw)
        word_rows.append(jnp.transpose(jnp.broadcast_to(pos * float(ROW_CHUNKS), (tile, LANES)))[0:1, :])
    lp_ref[...] = lp
    tw_ref[...] = tw
    ct_ref[0] = count
    of_ref[0] = offset
    wp_ref[0] = jnp.concatenate(word_rows, axis=1).astype(I32)


def _prompt_kernel(x_ref, g1_ref, win_ref, cq_ref, sq_ref, dmat_ref, qdec_ref, kdec_ref,
                   sdec_ref, lng_ref, lnb_ref, ws_ref, bst_ref, wout_ref, g2_ref, wr_ref, br_ref,
                   tri_ref, upper_ref,
                   x2_ref, h2_ref, lp_ref, tw_ref, ct_ref, of_ref, wp_ref, st_ref,
                   s_scr, mix_scr, x2_carry, win_bf, wout_bf, *, ret_block, tiles_per_seq, n_tiles):
    step = pl.program_id(0)

    @pl.when(step == 0)
    def _():
        s_scr[...] = jnp.zeros_like(s_scr)
        x2_carry[...] = jnp.zeros_like(x2_carry)
        win_bf[...] = win_ref[...].astype(BF16)
        wout_bf[...] = wout_ref[...].astype(BF16)

    def norm_prev_tile():
        h2_prev = _rmsnorm(x2_carry[...], g2_ref[...])
        _store_rows(h2_ref, 0, h2_prev)
        return _route_choose(h2_prev, wr_ref, br_ref)

    def place_prev_tile(choice):
        _route_place(choice, tri_ref, upper_ref, lp_ref, tw_ref, ct_ref, of_ref, wp_ref)

    @pl.when(step == n_tiles)
    def _():
        place_prev_tile(norm_prev_tile())

    @pl.when(step < n_tiles)
    def _():
        _prompt_step(step, norm_prev_tile, place_prev_tile, x_ref, g1_ref, win_bf, cq_ref, sq_ref,
                     dmat_ref, qdec_ref, kdec_ref, sdec_ref, lng_ref, lnb_ref, ws_ref, bst_ref, wout_bf,
                     x2_ref, st_ref, s_scr, mix_scr, x2_carry, ret_block=ret_block, tiles_per_seq=tiles_per_seq)


def _prompt_step(step, norm_prev_tile, place_prev_tile, x_ref, g1_ref, win_ref, cq_ref, sq_ref,
                 dmat_ref, qdec_ref, kdec_ref, sdec_ref, lng_ref, lnb_ref, ws_ref, bst_ref, wout_ref,
                 x2_ref, st_ref, s_scr, mix_scr, x2_carry, *, ret_block, tiles_per_seq):
    tile = TOKEN_TILE
    t = lax.rem(step, tiles_per_seq)

    choice = norm_prev_tile()
    s_scr[...] = jnp.where(t == 0, 0.0, s_scr[...])

    x = x_ref[0]
    h = _rmsnorm(x, g1_ref[...]).astype(BF16)

    def proj(i):
        return _dot(h, win_ref[:, i * RET_WIDTH:(i + 1) * RET_WIDTH])

    q = proj(0)
    k = proj(1)
    v = proj(2)
    gate = proj(3)
    u = proj(4)
    vg = proj(5)
    place_prev_tile(choice)
    cq = cq_ref[...]
    sq = sq_ref[...]
    ck = cq * K_SCALE
    sk = sq * K_SCALE

    for hd in range(N_HEADS):
        cols = slice(hd * HEAD_DIM, (hd + 1) * HEAD_DIM)
        qr = _rotary(q[:, cols], cq, sq)
        kr = _rotary(k[:, cols], ck, sk)
        vh = v[:, cols].astype(BF16)
        dm = dmat_ref[hd]
        qd = qdec_ref[hd]
        kd = kdec_ref[hd]
        sd = sdec_ref[hd]
        for c in range(tile // ret_block):
            rows = slice(c * ret_block, (c + 1) * ret_block)
            qb = qr[rows].astype(BF16)
            kb = kr[rows]
            vb = vh[rows]
            state = s_scr[hd]
            scores = _dot_nt(qb, kb.astype(BF16)) * dm
            o = _dot(scores.astype(BF16), vb) + qd * _dot(qb, state.astype(BF16))
            s_scr[hd] = sd * state + _dot_tn((kb * kd).astype(BF16), vb)
            mix_scr[rows, cols] = _gated_head_norm(o, gate[rows, cols])

    uf, vn = _gmlp_inputs(u, vg, lng_ref[...], lnb_ref[...])
    vnb = vn.astype(BF16)
    r_i = lax.broadcasted_iota(I32, (GM_CHUNK, GM_CHUNK), 0)
    c_i = lax.broadcasted_iota(I32, (GM_CHUNK, GM_CHUNK), 1)
    for hd in range(N_HEADS):
        cols = slice(hd * HEAD_DIM, (hd + 1) * HEAD_DIM)
        w_tril = jnp.where(r_i >= c_i, ws_ref[hd], 0.0).astype(BF16)
        bias = bst_ref[:, hd:hd + 1]
        for c in range(tile // GM_CHUNK):
            rows = slice(c * GM_CHUNK, (c + 1) * GM_CHUNK)
            mixed = _dot(w_tril, vnb[rows, cols]) + bias
            mix_scr[rows, RET_WIDTH + hd * HEAD_DIM:RET_WIDTH + (hd + 1) * HEAD_DIM] = (
                uf[rows, cols] * mixed).astype(BF16)

    x2 = x + _dot(mix_scr[...], wout_ref[...])
    x2_ref[...] = x2
    x2_carry[...] = x2

    @pl.when(t == tiles_per_seq - 1)
    def _():
        st_ref[0] = s_scr[...]


def _decode_kernel(x_ref, s0_ref, g1_ref, win_ref, cq_ref, sq_ref, qdec_ref, kdec_ref,
                   sdec_ref, lng_ref, lnb_ref, ws_ref, bst_ref, wout_ref, g2_ref, wr_ref, br_ref,
                   tri_ref, upper_ref,
                   x2_ref, h2_ref, lp_ref, tw_ref, ct_ref, of_ref, wp_ref, st_ref, vn_ref,
                   mix_scr, q_scr, k_scr, v_scr, oc_scr, st_stage, st_sem, *, n_streams, frames):
    tile = n_streams * frames
    x = x_ref[...]
    h = _rmsnorm(x, g1_ref[...]).astype(BF16)

    def proj(i):
        return _dot(h, win_ref[:, i * RET_WIDTH:(i + 1) * RET_WIDTH].astype(BF16))

    q = proj(0)
    k = proj(1)
    v = proj(2)
    gate = proj(3)
    u = proj(4)
    vg = proj(5)
    cq = cq_ref[...]
    sq = sq_ref[...]
    ck = cq * K_SCALE
    sk = sq * K_SCALE

    r_i = lax.broadcasted_iota(I32, (tile, tile), 0)
    c_i = lax.broadcasted_iota(I32, (tile, tile), 1)
    shift = frames.bit_length() - 1
    r_frame = lax.bitwise_and(r_i, frames - 1)
    c_frame = lax.bitwise_and(c_i, frames - 1)
    keep = (lax.shift_right_logical(r_i, shift) == lax.shift_right_logical(c_i, shift)) & (r_frame >= c_frame)
    frame_gap = (r_frame - c_frame).astype(F32)

    o_in = []
    for hd in range(N_HEADS):
        cols = slice(hd * HEAD_DIM, (hd + 1) * HEAD_DIM)
        qr = _rotary(q[:, cols], cq, sq)
        kr = _rotary(k[:, cols], ck, sk)
        vh = v[:, cols].astype(BF16)
        qb = qr.astype(BF16)
        decay = jnp.where(keep, jnp.exp(frame_gap * _LOG_GAMMA[hd]), 0.0)
        scores = _dot_nt(qb, kr.astype(BF16)) * decay
        o_in.append(_dot(scores.astype(BF16), vh))
        q_scr[hd] = qb
        k_scr[hd] = (kr * kdec_ref[hd]).astype(BF16)
        v_scr[hd] = vh

    def state_copy(b):
        slot = lax.rem(b, 2)
        return pltpu.make_async_copy(st_stage.at[slot], st_ref.at[b], st_sem.at[slot])

    def stream_body(b, carry):
        rows = pl.ds(pl.multiple_of(b * frames, frames), frames)
        slot = lax.rem(b, 2)

        @pl.when(b >= 2)
        def _():
            state_copy(b - 2).wait()

        for hd in range(N_HEADS):
            state = s0_ref[b, hd]
            oc_scr[hd, rows, :] = _dot(q_scr[hd, rows, :], state.astype(BF16))
            st_stage[slot, hd] = sdec_ref[hd] * state + _dot_tn(k_scr[hd, rows, :], v_scr[hd, rows, :])
        state_copy(b).start()
        return carry

    lax.fori_loop(0, n_streams, stream_body, 0)
    state_copy(n_streams - 2).wait()
    state_copy(n_streams - 1).wait()

    for hd in range(N_HEADS):
        cols = slice(hd * HEAD_DIM, (hd + 1) * HEAD_DIM)
        o = o_in[hd] + qdec_ref[hd] * oc_scr[hd]
        mix_scr[:, cols] = _gated_head_norm(o, gate[:, cols])

    uf, vn = _gmlp_inputs(u, vg, lng_ref[...], lnb_ref[...])
    vn_ref[...] = vn
    vnb = vn.astype(BF16)
    sel = (lax.bitwise_and(lax.broadcasted_iota(I32, (tile, frames), 0), frames - 1)
           == lax.broadcasted_iota(I32, (tile, frames), 1)).astype(BF16)
    for hd in range(N_HEADS):
        cols = slice(hd * HEAD_DIM, (hd + 1) * HEAD_DIM)
        w_rows = _dot(sel, ws_ref[hd].astype(BF16)).astype(BF16)
        w_blk = jnp.where(keep, _dot_nt(w_rows, sel), 0.0).astype(BF16)
        mixed = _dot(w_blk, vnb[:, cols]) + bst_ref[:, hd:hd + 1]
        mix_scr[:, RET_WIDTH + hd * HEAD_DIM:RET_WIDTH + (hd + 1) * HEAD_DIM] = (uf[:, cols] * mixed).astype(BF16)

    _route_and_store(x, mix_scr, wout_ref, g2_ref, wr_ref, br_ref, tri_ref, upper_ref,
                     x2_ref, h2_ref, lp_ref, tw_ref, ct_ref, of_ref, wp_ref)


def _rope_tables(pos):
    half = HEAD_DIM // 2
    inv = np.float32(ROPE_BASE) ** (-np.arange(half, dtype=np.float32) / np.float32(half))
    ang = pos.astype(np.float32)[:, None] * inv[None, :]
    cos = np.cos(ang).astype(np.float32)
    sin = np.sin(ang).astype(np.float32)
    return np.concatenate([cos, cos], axis=-1), np.concatenate([-sin, sin], axis=-1)


def _decay_tables(block):
    log_g = np.asarray(_LOG_GAMMA, np.float32)
    idx = np.arange(block, dtype=np.float32)
    diff = idx[:, None] - idx[None, :]
    dmat = np.where(diff[None] >= 0, np.exp(np.maximum(diff, 0.0)[None] * log_g[:, None, None]), 0.0)
    q_dec = np.exp((idx + 1.0)[None, :] * log_g[:, None])
    k_dec = np.exp((block - 1.0 - idx)[None, :] * log_g[:, None])
    s_dec = np.exp(np.float32(block) * log_g)
    bc = lambda a: np.ascontiguousarray(
        np.broadcast_to(a[:, :, None], (N_HEADS, block, HEAD_DIM)).astype(np.float32))
    s_row = np.ascontiguousarray(np.broadcast_to(s_dec[:, None, None], (N_HEADS, 1, HEAD_DIM)).astype(np.float32))
    return dmat.astype(np.float32), bc(q_dec), bc(k_dec), s_row


def _routing_constants(tile):
    tri = jnp.asarray(np.arange(tile)[:, None] > np.arange(tile)[None, :], BF16)
    upper = jnp.asarray(np.arange(N_EXPERTS)[:, None] < np.arange(N_EXPERTS)[None, :], BF16)
    return tri, upper


def _front_out(n_tok, n_tiles):
    return [
        jax.ShapeDtypeStruct((n_tok, D_MODEL), F32),
        jax.ShapeDtypeStruct((n_tok * ROW_CHUNKS, LANES), U32),
        jax.ShapeDtypeStruct((n_tok, TOP_K), I32),
        jax.ShapeDtypeStruct((n_tok, TOP_K), F32),
        jax.ShapeDtypeStruct((n_tiles, 1, N_EXPERTS), F32),
        jax.ShapeDtypeStruct((n_tiles, 1, N_EXPERTS), F32),
        jax.ShapeDtypeStruct((n_tiles, 1, TILE_SLOTS), I32),
    ]


def _prompt_call(x, p):
    bsz, seq, _ = x.shape
    tile = TOKEN_TILE
    n_t = seq // tile
    cq, sq = _rope_tables(np.arange(seq))
    dmat, qdec, kdec, sdec = _decay_tables(PROMPT_RET_BLOCK)
    tri, upper = _routing_constants(tile)
    ws = p['w_s'][:, :GM_CHUNK, :GM_CHUNK]
    bst = p['b_s'][:, :GM_CHUNK].T

    n_tiles = bsz * n_t
    cur = lambda s: jnp.minimum(s, n_tiles - 1)
    prev = lambda s: jnp.maximum(s - 1, 0)
    const = lambda shape: pl.BlockSpec(shape, lambda s: (0,) * len(shape))
    once = lambda shape: pl.BlockSpec(shape, lambda s: (0,) * len(shape), pipeline_mode=pl.Buffered(1))
    pos = lambda: pl.BlockSpec((tile, HEAD_DIM), lambda s: (lax.rem(cur(s), n_t), 0))
    in_specs = [
        pl.BlockSpec((1, tile, D_MODEL), lambda s: (cur(s) // n_t, lax.rem(cur(s), n_t), 0)),
        const((1, D_MODEL)), once((D_MODEL, 6 * RET_WIDTH)),
        pos(), pos(),
        const((N_HEADS, PROMPT_RET_BLOCK, PROMPT_RET_BLOCK)),
        const((N_HEADS, PROMPT_RET_BLOCK, HEAD_DIM)), const((N_HEADS, PROMPT_RET_BLOCK, HEAD_DIM)),
        const((N_HEADS, 1, HEAD_DIM)),
        const((1, GM_WIDTH)), const((1, GM_WIDTH)),
        const((N_HEADS, GM_CHUNK, GM_CHUNK)), const((GM_CHUNK, N_HEADS)),
        once((D_MODEL, D_MODEL)), const((1, D_MODEL)),
        const((D_MODEL, N_EXPERTS)), const((1, N_EXPERTS)),
        const((tile, tile)), const((N_EXPERTS, N_EXPERTS)),
    ]
    out_specs = [
        pl.BlockSpec((tile, D_MODEL), lambda s: (cur(s), 0)),
        pl.BlockSpec((tile * ROW_CHUNKS, LANES), lambda s: (prev(s), 0)),
        pl.BlockSpec((tile, TOP_K), lambda s: (prev(s), 0)),
        pl.BlockSpec((tile, TOP_K), lambda s: (prev(s), 0)),
        pl.BlockSpec((1, 1, N_EXPERTS), lambda s: (prev(s), 0, 0)),
        pl.BlockSpec((1, 1, N_EXPERTS), lambda s: (prev(s), 0, 0)),
        pl.BlockSpec((1, 1, TILE_SLOTS), lambda s: (prev(s), 0, 0)),
        pl.BlockSpec((1, N_HEADS, HEAD_DIM, HEAD_DIM), lambda s: (cur(s) // n_t, 0, 0, 0)),
    ]
    out_shape = _front_out(bsz * seq, n_tiles) + [jax.ShapeDtypeStruct((bsz, N_HEADS, HEAD_DIM, HEAD_DIM), F32)]
    return pl.pallas_call(
        functools.partial(_prompt_kernel, ret_block=PROMPT_RET_BLOCK, tiles_per_seq=n_t, n_tiles=n_tiles),
        grid=(n_tiles + 1,),
        in_specs=in_specs,
        out_specs=out_specs,
        out_shape=out_shape,
        scratch_shapes=[
            pltpu.VMEM((N_HEADS, HEAD_DIM, HEAD_DIM), F32),
            pltpu.VMEM((tile, D_MODEL), BF16),
            pltpu.VMEM((tile, D_MODEL), F32),
            pltpu.VMEM((D_MODEL, 6 * RET_WIDTH), BF16),
            pltpu.VMEM((D_MODEL, D_MODEL), BF16),
        ],
        compiler_params=pltpu.CompilerParams(
            dimension_semantics=("arbitrary",), vmem_limit_bytes=VMEM_LIMIT),
        name="front_prompt",
    )(x, p['norm1_g'], p['w_in'], cq, sq, dmat, qdec, kdec, sdec, p['ln_v_g'], p['ln_v_b'], ws, bst,
      p['w_out'], p['norm2_g'], p['w_router'], p['b_router'], tri, upper)


def _decode_call(x, state, past_len, p):
    n_streams, frames, _ = x.shape
    tile = n_streams * frames
    assert tile == TOKEN_TILE and frames <= RET_CHUNK and frames & (frames - 1) == 0
    cq, sq = [np.tile(a, (n_streams, 1)) for a in _rope_tables(past_len + np.arange(frames))]
    _, qdec, kdec, sdec = _decay_tables(frames)
    qdec = np.tile(qdec, (1, n_streams, 1))
    kdec = np.tile(kdec, (1, n_streams, 1))
    tri, upper = _routing_constants(tile)
    ws = p['w_s'][:, :frames, :frames]
    bst = jnp.tile(p['b_s'][:, :frames].T, (n_streams, 1))

    def whole(a):
        return pl.BlockSpec(a.shape, lambda i, n=a.ndim: (0,) * n, pipeline_mode=pl.Buffered(1))

    args = (x.reshape(tile, D_MODEL), state, p['norm1_g'], p['w_in'], cq, sq, qdec, kdec, sdec,
            p['ln_v_g'], p['ln_v_b'], ws, bst, p['w_out'], p['norm2_g'], p['w_router'], p['b_router'],
            tri, upper)
    out_shape = _front_out(tile, 1) + [
        jax.ShapeDtypeStruct(state.shape, F32),
        jax.ShapeDtypeStruct((tile, GM_WIDTH), F32),
    ]
    out_specs = [pl.BlockSpec(s.shape, lambda i, n=len(s.shape): (0,) * n) for s in out_shape]
    out_specs[7] = pl.BlockSpec(memory_space=pl.ANY)
    return pl.pallas_call(
        functools.partial(_decode_kernel, n_streams=n_streams, frames=frames),
        grid=(1,),
        in_specs=[whole(a) for a in args],
        out_specs=out_specs,
        out_shape=out_shape,
        scratch_shapes=[
            pltpu.VMEM((tile, D_MODEL), BF16),
            pltpu.VMEM((N_HEADS, tile, HEAD_DIM), BF16),
            pltpu.VMEM((N_HEADS, tile, HEAD_DIM), BF16),
            pltpu.VMEM((N_HEADS, tile, HEAD_DIM), BF16),
            pltpu.VMEM((N_HEADS, tile, HEAD_DIM), F32),
            pltpu.VMEM((2, N_HEADS, HEAD_DIM, HEAD_DIM), F32),
            pltpu.SemaphoreType.DMA((2,)),
        ],
        compiler_params=pltpu.CompilerParams(
            dimension_semantics=("arbitrary",), vmem_limit_bytes=VMEM_LIMIT),
        name="front_decode",
    )(*args)


def _plan_kernel(ct_ref, lower_ref, upper_ref, cum_ref, nb_ref, bs_ref, be_ref, nbt_ref, tlo_ref, thi_ref,
                 *, n_blocks):
    counts = ct_ref[...]
    cum = _exact_count_dot(lower_ref[...], counts, counts_on_left=False)
    cum_ref[...] = cum.astype(I32)
    n_tiles = counts.shape[0]
    total = cum[n_tiles:n_tiles + 1]
    nb = jnp.floor((total + (EXPERT_ROWS - 1)) * (1.0 / EXPERT_ROWS))
    bstart = _dot(jnp.broadcast_to(nb, (8, N_EXPERTS)).astype(BF16), upper_ref[...])[0:1]
    bend = bstart + nb
    nb_ref[...] = nb.astype(I32)
    bs_ref[...] = bstart.astype(I32)
    blk = lax.broadcasted_iota(I32, (n_blocks, N_EXPERTS), 0).astype(F32)
    be = jnp.minimum(jnp.sum((bend <= blk).astype(F32), axis=-1, keepdims=True), N_EXPERTS - 1.0)
    be_ref[...] = be.astype(I32)
    nbt_ref[...] = jnp.sum(nb, axis=-1, keepdims=True).astype(I32)
    mine = lax.broadcasted_iota(I32, (n_blocks, N_EXPERTS), 1).astype(F32) == be
    pick = lambda row: jnp.sum(jnp.where(mine, row, 0.0), axis=-1, keepdims=True)
    r0 = (blk[:, 0:1] - pick(bstart)) * EXPERT_ROWS
    tlo = jnp.zeros((n_blocks, 1), F32)
    thi = jnp.zeros((n_blocks, 1), F32)
    seg_lo = pick(cum[0:1])
    for i in range(n_tiles):
        seg_hi = pick(cum[i + 1:i + 2])
        tlo = tlo + (seg_hi <= r0).astype(F32)
        thi = thi + (seg_lo < r0 + EXPERT_ROWS).astype(F32)
        seg_lo = seg_hi
    tlo_ref[...] = tlo.astype(I32)
    thi_ref[...] = thi.astype(I32)


def _plan_call(counts, n_blocks):
    n_tiles = counts.shape[0]
    lower = (jnp.arange(n_tiles + 1)[:, None] > jnp.arange(n_tiles)[None, :]).astype(BF16)
    upper = (jnp.arange(N_EXPERTS)[:, None] < jnp.arange(N_EXPERTS)[None, :]).astype(BF16)
    return pl.pallas_call(
        functools.partial(_plan_kernel, n_blocks=n_blocks),
        out_shape=[
            jax.ShapeDtypeStruct((n_tiles + 1, N_EXPERTS), I32),
            jax.ShapeDtypeStruct((1, N_EXPERTS), I32),
            jax.ShapeDtypeStruct((1, N_EXPERTS), I32),
            jax.ShapeDtypeStruct((n_blocks, 1), I32),
            jax.ShapeDtypeStruct((1, 1), I32),
            jax.ShapeDtypeStruct((n_blocks, 1), I32),
            jax.ShapeDtypeStruct((n_blocks, 1), I32),
        ],
        name="plan",
    )(counts, lower, upper)


def _dispatch_kernel(h2p_ref, h2s_ref, wp_ref, out_ref, ring, ring_sem, *, n_prompt_tiles, n_tiles):
    i = pl.program_id(0)
    tile_words = TILE_SLOTS * ROW_CHUNKS
    base = lax.rem(i, DISPATCH_RING) * tile_words

    def out_copy(j):
        slot = lax.rem(j, DISPATCH_RING)
        src = ring.at[pl.ds(pl.multiple_of(slot * tile_words, tile_words), tile_words)]
        dst = out_ref.at[pl.ds(pl.multiple_of(j * tile_words, tile_words), tile_words)]
        return pltpu.make_async_copy(src, dst, ring_sem.at[slot])

    @pl.when(i >= DISPATCH_RING)
    def _():
        out_copy(i - DISPATCH_RING).wait()

    def permute(src_ref):
        def body(t, c):
            row = src_ref[_row_tile(t), :]
            for kk in range(TOP_K):
                first = pl.multiple_of(base + wp_ref[0, 0, kk * TOKEN_TILE + t], ROW_CHUNKS)
                ring[pl.ds(first, ROW_CHUNKS), :] = row
            return c

        lax.fori_loop(0, TOKEN_TILE, body, 0, unroll=32)

    @pl.when(i < n_prompt_tiles)
    def _():
        permute(h2p_ref)

    @pl.when(i >= n_prompt_tiles)
    def _():
        permute(h2s_ref)

    out_copy(i).start()

    @pl.when(i == n_tiles - 1)
    def _():
        for back in range(min(DISPATCH_RING, n_tiles) - 1, -1, -1):
            out_copy(i - back).wait()


def _dispatch_call(h2_p, h2_s, word_pos):
    rows = TOKEN_TILE * ROW_CHUNKS
    n_p_tiles = h2_p.shape[0] // rows
    n_s_tiles = h2_s.shape[0] // rows
    n_tiles = n_p_tiles + n_s_tiles
    return pl.pallas_call(
        functools.partial(_dispatch_kernel, n_prompt_tiles=n_p_tiles, n_tiles=n_tiles),
        grid=(n_tiles,),
        in_specs=[
            pl.BlockSpec((rows, LANES), lambda i: (jnp.minimum(i, n_p_tiles - 1), 0)),
            pl.BlockSpec((rows, LANES), lambda i: (jnp.maximum(i - n_p_tiles, 0), 0)),
            pl.BlockSpec((1, 1, TILE_SLOTS), lambda i: (i, 0, 0), memory_space=pltpu.SMEM),
        ],
        out_specs=pl.BlockSpec(memory_space=pl.ANY),
        scratch_shapes=[
            pltpu.VMEM((DISPATCH_RING * TILE_SLOTS * ROW_CHUNKS, LANES), U32),
            pltpu.SemaphoreType.DMA((DISPATCH_RING,)),
        ],
        out_shape=jax.ShapeDtypeStruct((n_tiles * TILE_SLOTS * ROW_CHUNKS, LANES), U32),
        compiler_params=pltpu.CompilerParams(
            dimension_semantics=("arbitrary",), vmem_limit_bytes=VMEM_LIMIT),
        name="dispatch",
    )(h2_p, h2_s, word_pos)


def _expert_kernel(be_ref, bs_ref, nb_ref, nbt_ref, cum_ref, off_ref, tlo_ref, thi_ref,
                   xb_ref, wg_ref, wu_ref, wd_ref, bg_ref, bu_ref, bd_ref,
                   yb_ref, xbuf, ybuf, wbf, in_sem, out_sem, *, n_tiles):
    b = pl.program_id(0)
    nbt = nbt_ref[0]

    def block_rows(blk):
        e = be_ref[blk]
        r0 = (blk - bs_ref[e]) * EXPERT_ROWS
        return r0, jnp.minimum(cum_ref[n_tiles * N_EXPERTS + e] - r0, EXPERT_ROWS)

    def segments(blk, slot, fn):
        e = be_ref[blk]
        r0 = (blk - bs_ref[e]) * EXPERT_ROWS

        def body(i, c):
            seg_lo = cum_ref[i * N_EXPERTS + e]
            seg_hi = cum_ref[(i + 1) * N_EXPERTS + e]
            lo = jnp.maximum(seg_lo, r0)
            hi = jnp.minimum(seg_hi, r0 + EXPERT_ROWS)

            @pl.when(hi > lo)
            def _():
                bucket_row = i * TILE_SLOTS + off_ref[i * N_EXPERTS + e] + (lo - seg_lo)
                fn(slot, bucket_row, slot * EXPERT_ROWS + (lo - r0), hi - lo)
            return c

        lax.fori_loop(tlo_ref[blk], thi_ref[blk], body, 0)

    def words(row, n):
        return pl.ds(pl.multiple_of(row * ROW_CHUNKS, ROW_CHUNKS), n * ROW_CHUNKS)

    def gather(slot, bucket_row, buffer_row, n):
        return pltpu.make_async_copy(
            xb_ref.at[words(bucket_row, n)], xbuf.at[words(buffer_row, n)], in_sem.at[slot])

    def scatter(slot, bucket_row, buffer_row, n):
        return pltpu.make_async_copy(
            ybuf.at[words(buffer_row, n)], yb_ref.at[words(bucket_row, n)], out_sem.at[slot])

    def start_gather(blk):
        @pl.when(blk < nbt)
        def _():
            segments(blk, lax.rem(blk, 2), lambda *a: gather(*a).start())

    def wait_gather(blk):
        @pl.when(blk < nbt)
        def _():
            slot = lax.rem(blk, 2)
            gather(slot, 0, slot * EXPERT_ROWS, block_rows(blk)[1]).wait()

    def start_scatter(blk):
        @pl.when(blk < nbt)
        def _():
            segments(blk, lax.rem(blk, 2), lambda *a: scatter(*a).start())

    def wait_scatter(blk):
        @pl.when((blk >= 0) & (blk < nbt))
        def _():
            slot = lax.rem(blk, 2)
            scatter(slot, 0, slot * EXPERT_ROWS, block_rows(blk)[1]).wait()

    def mlp(base, n_rows, rows_left):
        valid = lax.broadcasted_iota(I32, (n_rows, 1), 0) < rows_left
        x = jnp.where(valid, _load_rows(xbuf, base, n_rows), 0.0)
        mine = pl.ds(b, 1)
        gt = _dot(x, wbf[0]) + bg_ref[0, mine, :]
        up = _dot(x, wbf[1]) + bu_ref[0, mine, :]
        gt = jnp.minimum(gt, SWIGLU_LIMIT)
        up = jnp.clip(up, -SWIGLU_LIMIT, SWIGLU_LIMIT)
        act = gt * _sigmoid(gt * SWIGLU_ALPHA) * (up + 1.0)
        _store_rows(ybuf, base, _dot(act.astype(BF16), wbf[2]) + bd_ref[0, mine, :])

    def block_body(blk, carry):
        wait_gather(blk)
        start_gather(blk + 1)
        wait_scatter(blk - 2)
        rows_left = block_rows(blk)[1]
        base = lax.rem(blk, 2) * EXPERT_ROWS

        lower = 0
        for upper, passes in EXPERT_PASS_PLANS:
            @pl.when((rows_left > lower) & (rows_left <= upper))
            def _(passes=passes):
                first = 0
                for n_rows in passes:
                    mlp(base + first, n_rows, rows_left - first)
                    first += n_rows
            lower = upper

        start_scatter(blk)
        return carry

    first_block = bs_ref[b]
    n_mine = nb_ref[b]

    @pl.when(b == 0)
    def _():
        xbuf[...] = jnp.zeros_like(xbuf)
        start_gather(0)

    @pl.when(n_mine > 0)
    def _():
        wbf[0] = wg_ref[0, 0].astype(BF16)
        wbf[1] = wu_ref[0, 0].astype(BF16)
        wbf[2] = wd_ref[0, 0].astype(BF16)

    lax.fori_loop(first_block, first_block + n_mine, block_body, 0)

    @pl.when(b == N_EXPERTS - 1)
    def _():
        wait_scatter(nbt - 2)
        wait_scatter(nbt - 1)


def _expert_call(be, bstart, nb, nbt, cum_flat, off_flat, tlo, thi, buckets, p, n_tiles):
    wspec = pl.BlockSpec((1, 1, D_MODEL, D_MODEL), lambda e, *_: (0, e, 0, 0))
    bspec = pl.BlockSpec((1, N_EXPERTS, D_MODEL), lambda e, *_: (0, 0, 0))
    any_spec = pl.BlockSpec(memory_space=pl.ANY)
    return pl.pallas_call(
        functools.partial(_expert_kernel, n_tiles=n_tiles),
        grid_spec=pltpu.PrefetchScalarGridSpec(
            num_scalar_prefetch=8,
            grid=(N_EXPERTS,),
            in_specs=[any_spec, wspec, wspec, wspec, bspec, bspec, bspec],
            out_specs=any_spec,
            scratch_shapes=[
                pltpu.VMEM((2 * EXPERT_ROWS * ROW_CHUNKS, LANES), U32),
                pltpu.VMEM((2 * EXPERT_ROWS * ROW_CHUNKS, LANES), U32),
                pltpu.VMEM((3, D_MODEL, D_MODEL), BF16),
                pltpu.SemaphoreType.DMA((2,)),
                pltpu.SemaphoreType.DMA((2,)),
            ],
        ),
        out_shape=jax.ShapeDtypeStruct(buckets.shape, U32),
        compiler_params=pltpu.CompilerParams(
            dimension_semantics=("arbitrary",), vmem_limit_bytes=VMEM_LIMIT),
        name="experts",
    )(be, bstart, nb, nbt, cum_flat, off_flat, tlo, thi, buckets, p['w_gate'], p['w_up'], p['w_down'],
      p['b_gate'], p['b_up'], p['b_down'])


def _combine_kernel(x2_ref, yb_ref, lp_ref, tw_ref, gf_ref, y_ref, ybf, pick, ring, ring_sem, *, tile_off, n_tiles):
    i = pl.program_id(0)
    tile_words = TILE_SLOTS * ROW_CHUNKS

    def bucket_copy(j):
        slot = lax.rem(j, COMBINE_RING)
        src = yb_ref.at[pl.ds(pl.multiple_of((tile_off + j) * tile_words, tile_words), tile_words)]
        dst = ring.at[pl.ds(pl.multiple_of(slot * tile_words, tile_words), tile_words)]
        return pltpu.make_async_copy(src, dst, ring_sem.at[slot])

    @pl.when(i == 0)
    def _():
        for j in range(min(COMBINE_RING - 1, n_tiles)):
            bucket_copy(j).start()

    bucket_copy(i).wait()

    @pl.when(i + (COMBINE_RING - 1) < n_tiles)
    def _():
        bucket_copy(i + (COMBINE_RING - 1)).start()

    first = lax.rem(i, COMBINE_RING) * TILE_SLOTS
    for c in range(TILE_SLOTS // COMBINE_CHUNK):
        rows = slice(c * COMBINE_CHUNK, (c + 1) * COMBINE_CHUNK)
        ybf[rows, :] = _load_rows(ring, first + c * COMBINE_CHUNK, COMBINE_CHUNK)
    slot = lax.broadcasted_iota(I32, (TOKEN_TILE, TILE_SLOTS), 1)
    lp = lp_ref[...]
    tw = tw_ref[...]
    sel = jnp.zeros((TOKEN_TILE, TILE_SLOTS), F32)
    for kk in range(TOP_K):
        sel = jnp.where(slot == lp[:, kk:kk + 1], tw[:, kk:kk + 1], sel)
    pick[...] = sel.astype(BF16)
    y_ref[...] = _rmsnorm(x2_ref[...] + _dot(pick[...], ybf[...]), gf_ref[...])


def _combine_call(x2, ybuckets, lp, tw, gf, tile_off):
    n_tok = x2.shape[0]
    n_tiles = n_tok // TOKEN_TILE
    return pl.pallas_call(
        functools.partial(_combine_kernel, tile_off=tile_off, n_tiles=n_tiles),
        grid=(n_tiles,),
        in_specs=[
            pl.BlockSpec((TOKEN_TILE, D_MODEL), lambda i: (i, 0)),
            pl.BlockSpec(memory_space=pl.ANY),
            pl.BlockSpec((TOKEN_TILE, TOP_K), lambda i: (i, 0)),
            pl.BlockSpec((TOKEN_TILE, TOP_K), lambda i: (i, 0)),
            pl.BlockSpec((1, D_MODEL), lambda i: (0, 0)),
        ],
        out_specs=pl.BlockSpec((TOKEN_TILE, D_MODEL), lambda i: (i, 0)),
        scratch_shapes=[
            pltpu.VMEM((TILE_SLOTS, D_MODEL), BF16),
            pltpu.VMEM((TOKEN_TILE, TILE_SLOTS), BF16),
            pltpu.VMEM((COMBINE_RING * TILE_SLOTS * ROW_CHUNKS, LANES), U32),
            pltpu.SemaphoreType.DMA((COMBINE_RING,)),
        ],
        out_shape=jax.ShapeDtypeStruct((n_tok, D_MODEL), F32),
        compiler_params=pltpu.CompilerParams(
            dimension_semantics=("arbitrary",), vmem_limit_bytes=VMEM_LIMIT),
        name="combine",
    )(x2, ybuckets, lp, tw, gf)


def kernel(x_prompt, x_sample, state_ret, norm1_g, w_in, ln_v_g, ln_v_b, w_s, b_s, w_out, norm2_g, w_router, b_router, w_gate, b_gate, w_up, b_up, w_down, b_down, norm_f_g):
    bsz, seq, _ = x_prompt.shape
    dbsz, dseq, _ = x_sample.shape
    n_p = bsz * seq
    n_s = dbsz * dseq
    n_total = n_p + n_s
    past_len = 2048
    assert w_in.shape[0] == 1, "single layer"

    p = dict(norm1_g=norm1_g, w_in=w_in[0], ln_v_g=ln_v_g, ln_v_b=ln_v_b, w_s=w_s[0], b_s=b_s[0],
             w_out=w_out[0], norm2_g=norm2_g, w_router=w_router[0].astype(BF16), b_router=b_router,
             w_gate=w_gate, b_gate=b_gate, w_up=w_up, b_up=b_up, w_down=w_down, b_down=b_down)

    x2_p, h2_p, lp_p, tw_p, ct_p, of_p, wp_p, st_p = _prompt_call(x_prompt, p)
    x2_s, h2_s, lp_s, tw_s, ct_s, of_s, wp_s, st_s, vn_s = _decode_call(x_sample, state_ret[0], past_len, p)

    n_tiles = n_total // TOKEN_TILE
    word_pos = jnp.concatenate([wp_p, wp_s], axis=0)
    counts = jnp.concatenate([ct_p, ct_s], axis=0).reshape(n_tiles, N_EXPERTS)
    offsets = jnp.concatenate([of_p, of_s], axis=0).reshape(n_tiles * N_EXPERTS).astype(I32)

    n_blocks = -(-(n_total * TOP_K + N_EXPERTS * (EXPERT_ROWS - 1)) // EXPERT_ROWS)
    cum, nb, bstart, be, nbt, tlo, thi = _plan_call(counts, n_blocks)

    buckets = _dispatch_call(h2_p, h2_s, word_pos)
    ybuckets = _expert_call(be.reshape(n_blocks), bstart.reshape(N_EXPERTS), nb.reshape(N_EXPERTS), nbt.reshape(1),
                            cum.reshape((n_tiles + 1) * N_EXPERTS), offsets, tlo.reshape(n_blocks),
                            thi.reshape(n_blocks), buckets, p, n_tiles)

    y_p = _combine_call(x2_p, ybuckets, lp_p, tw_p, norm_f_g[None, :], 0)
    y_s = _combine_call(x2_s, ybuckets, lp_s, tw_s, norm_f_g[None, :], n_p // TOKEN_TILE)

    return (y_p.reshape(bsz, seq, D_MODEL), y_s.reshape(dbsz, dseq, D_MODEL),
            st_p[None], st_s[None], vn_s.reshape(1, dbsz, dseq, GM_WIDTH))
```
